```python
import jax
import jax.numpy as jnp
from jax import lax
import numpy as np

D_MODEL = 1024
BATCH = 4
SEQ = 8192
DEPTH = 2

GRID_W = 64
CTX_LEN = 256
D_INNER = 2 * D_MODEL
HG_WIDTH = D_INNER // 2
HG_DK = 128
HG_HEADS = HG_WIDTH // HG_DK
HG_DV = HG_WIDTH // HG_HEADS
HG_CHUNK = 64
POOL_WIDTH = D_INNER - HG_WIDTH
POOL_WINDOWS = (2, 4, 8, 16)
POOL_GROUP = POOL_WIDTH // len(POOL_WINDOWS)
MLA_HEADS = 16
MLA_NOPE = 128
MLA_ROPE = 64
MLA_V = D_INNER // MLA_HEADS
MLA_Q_RANK = D_MODEL // 2
MLA_KV_RANK = D_MODEL // 4
MLA_SCALE = (MLA_NOPE + MLA_ROPE) ** -0.5
ROPE_FREQ = MLA_ROPE // 4
ROPE_BASE = 10000.0
Q_BLOCK = 128
EPS = 1e-6
N_EVEN = (DEPTH + 1) // 2
N_ODD = DEPTH // 2
EVEN_IN = 5 * HG_WIDTH + 2 * POOL_WIDTH
ODD_IN = MLA_Q_RANK + MLA_KV_RANK + MLA_ROPE + D_INNER

kernel_name = 'hybrid_hgrn2_pool_mla_prefix_dit'


def rms_norm(x, g):
    xf = x.astype(jnp.float32)
    y = xf * lax.rsqrt(jnp.mean(xf * xf, axis=-1, keepdims=True) + EPS)
    return (y * g.astype(jnp.float32)).astype(x.dtype)


def ada_modulation(cond, w, b):
    m = (jax.nn.silu(cond) @ w + b)[:, None, :]
    return jnp.split(m, 3, axis=-1)


def axial_rope_tables(n_tokens):
    rows = n_tokens // GRID_W
    pos_r = jnp.repeat(jnp.arange(rows), GRID_W).astype(jnp.float32)
    pos_c = jnp.tile(jnp.arange(GRID_W), rows).astype(jnp.float32)
    inv = ROPE_BASE ** (-2.0 * jnp.arange(ROPE_FREQ, dtype=jnp.float32) / (MLA_ROPE // 2))
    ang = jnp.stack([pos_r[:, None] * inv, pos_c[:, None] * inv], axis=1)
    return jnp.cos(ang), jnp.sin(ang)


def apply_axial_rope(x, cos, sin):
    shp = x.shape
    xr = x.astype(jnp.float32).reshape(shp[:-1] + (2, 2, ROPE_FREQ))
    x1, x2 = xr[..., 0, :], xr[..., 1, :]
    co, si = cos[None, :, None], sin[None, :, None]
    out = jnp.stack([x1 * co - x2 * si, x2 * co + x1 * si], axis=-2)
    return out.reshape(shp).astype(x.dtype)


def hgrn2_scan(q, k, v, logf, s0):
    out_dtype = v.dtype
    Bn, T, H, _ = q.shape
    n = T // HG_CHUNK

    def to_chunks(a):
        a = a.astype(jnp.float32).reshape(Bn, n, HG_CHUNK, H, a.shape[-1])
        return jnp.moveaxis(a, 1, 0)

    lower = jnp.tril(jnp.ones((HG_CHUNK, HG_CHUNK), bool))[None, :, :, None, None]

    def step(s, inp):
        qc, kc, vc, gc = inp
        b = jnp.cumsum(gc, axis=1)
        diff = jnp.where(lower, b[:, :, None] - b[:, None, :], -jnp.inf)
        att = jnp.sum(qc[:, :, None] * kc[:, None] * jnp.exp(diff), axis=-1)
        o = jnp.einsum('btsh,bshv->bthv', att, vc) + jnp.einsum('bthk,bhkv->bthv', qc * jnp.exp(b), s)
        b_last = b[:, -1]
        k_end = kc * jnp.exp(b_last[:, None] - b)
        s_new = jnp.exp(b_last)[..., None] * s + jnp.einsum('bthk,bthv->bhkv', k_end, vc)
        return s_new, o

    s_fin, o = lax.scan(step, s0, (to_chunks(q), to_chunks(k), to_chunks(v), to_chunks(logf)))
    o = jnp.moveaxis(o, 0, 1).reshape(Bn, T, H, v.shape[-1])
    return o.astype(out_dtype), s_fin


def hgrn2_branch(p, lb, s0f, s0b):
    Bn, T, _ = p.shape
    q, ff, fb, i, g = jnp.split(p, 5, axis=-1)
    heads = lambda a: a.reshape(Bn, T, HG_HEADS, HG_DK)
    q = jax.nn.silu(heads(q))
    i = heads(i)

    def decay(f_pre, lbd):
        f = lbd + (1.0 - lbd) * jax.nn.sigmoid(heads(f_pre).astype(jnp.float32))
        return 1.0 - f, jnp.log(f)

    kf, gf = decay(ff, lb[0])
    kb, gb = decay(fb, lb[1])
    o_f, sf = hgrn2_scan(q, kf, i, gf, s0f)
    o_b, sb = hgrn2_scan(q[:, ::-1], kb[:, ::-1], i[:, ::-1], gb[:, ::-1], s0b)
    return o_f + o_b[:, ::-1], g, sf, sb


def multiscale_pool(u):
    T = u.shape[1]
    uf = u.astype(jnp.float32)
    cs = jnp.concatenate([jnp.zeros_like(uf[:, :1]), jnp.cumsum(uf, axis=1)], axis=1)
    t = jnp.arange(T)
    outs = []
    for gi, w in enumerate(POOL_WINDOWS):
        sl = slice(gi * POOL_GROUP, (gi + 1) * POOL_GROUP)
        lo = jnp.clip(t - w // 2, 0, T)
        hi = jnp.clip(t + w // 2, 0, T)
        cnt = (hi - lo).astype(jnp.float32)[None, :, None]
        outs.append((cs[:, hi, sl] - cs[:, lo, sl]) / cnt - uf[..., sl])
    return jnp.concatenate(outs, axis=-1).astype(u.dtype)


def pool_branch(u, g, pool_w, pool_scale):
    Bn, T, _ = u.shape
    y = multiscale_pool(u).reshape(Bn, T, len(POOL_WINDOWS), POOL_GROUP)
    y = jnp.einsum('btgc,gcd->btgd', y, pool_w).reshape(Bn, T, POOL_WIDTH) * pool_scale
    return y * jax.nn.silu(g)


def even_mixer(zl, zc, in_w, lb, hg_norm_g, pool_w, pool_scale, need_ctx):
    def run(z, s0f, s0b, need_out):
        Bn, T, _ = z.shape
        p = z @ in_w
        o, ga, sf, sb = hgrn2_branch(p[..., :5 * HG_WIDTH], lb, s0f, s0b)
        if not need_out:
            return None, sf, sb
        a = rms_norm(o, hg_norm_g.reshape(HG_HEADS, HG_DV)).reshape(Bn, T, HG_WIDTH) * jax.nn.silu(ga)
        u = p[..., 5 * HG_WIDTH:5 * HG_WIDTH + POOL_WIDTH]
        gb = p[..., 5 * HG_WIDTH + POOL_WIDTH:]
        b = pool_branch(u, gb, pool_w, pool_scale)
        return jnp.concatenate([a, b], axis=-1), sf, sb

    s0 = jnp.zeros((zc.shape[0], HG_HEADS, HG_DK, HG_DV), jnp.float32)
    yc, sf, sb = run(zc, s0, s0, need_ctx)
    yl, _, _ = run(zl, sf, sb, True)
    return yl, yc


def mla_attend(qn, qr, kn, kr, v):
    s = jnp.einsum('bqhn,bkhn->bhqk', qn, kn) + jnp.einsum('bqhr,bkr->bhqk', qr, kr)
    p = jax.nn.softmax(s.astype(jnp.float32) * MLA_SCALE, axis=-1).astype(v.dtype)
    return jnp.einsum('bhqk,bkhv->bqhv', p, v)


def odd_mixer(zl, zc, in_w, qa_g, qb_w, kva_g, kvb_w, cos, sin, need_ctx):
    o1 = MLA_Q_RANK
    o2 = o1 + MLA_KV_RANK
    o3 = o2 + MLA_ROPE
    w = kvb_w.reshape(MLA_KV_RANK, MLA_HEADS, MLA_NOPE + MLA_V)
    w_uk, w_uv = w[..., :MLA_NOPE], w[..., MLA_NOPE:]
    Bn, T, _ = zl.shape
    Lc = zc.shape[1]

    pl = zl @ in_w
    cq_l = rms_norm(pl[..., :o1], qa_g)
    ckv_l = rms_norm(pl[..., o1:o2], kva_g)
    kr_l = apply_axial_rope(pl[..., o2:o3][:, :, None], cos, sin)[:, :, 0]
    g_l = pl[..., o3:]
    pc = zc @ in_w[:, o1:o3]
    ckv_c = rms_norm(pc[..., :MLA_KV_RANK], kva_g)
    kr_c = pc[..., MLA_KV_RANK:]

    ckv_all = jnp.concatenate([ckv_c, ckv_l], axis=1)
    kr_all = jnp.concatenate([kr_c, kr_l], axis=1)
    kn_all = jnp.einsum('blc,chn->blhn', ckv_all, w_uk)
    v_all = jnp.einsum('blc,chv->blhv', ckv_all, w_uv)

    def queries(cq):
        q = (cq @ qb_w).reshape(cq.shape[:2] + (MLA_HEADS, MLA_NOPE + MLA_ROPE))
        return q[..., :MLA_NOPE], q[..., MLA_NOPE:]

    nblk = T // Q_BLOCK

    def block(args):
        cq_b, cos_b, sin_b = args
        qn, qr = queries(cq_b)
        return mla_attend(qn, apply_axial_rope(qr, cos_b, sin_b), kn_all, kr_all, v_all)

    xs = (jnp.moveaxis(cq_l.reshape(Bn, nblk, Q_BLOCK, MLA_Q_RANK), 1, 0),
          cos.reshape(nblk, Q_BLOCK, 2, ROPE_FREQ), sin.reshape(nblk, Q_BLOCK, 2, ROPE_FREQ))
    ol = jnp.moveaxis(lax.map(block, xs), 0, 1).reshape(Bn, T, MLA_HEADS * MLA_V)
    yl = ol * jax.nn.silu(g_l)

    yc = None
    if need_ctx:
        pcq = zc @ in_w[:, :o1]
        gc = zc @ in_w[:, o3:]
        qn, qr = queries(rms_norm(pcq, qa_g))
        oc = mla_attend(qn, qr, kn_all[:, :Lc], kr_c, v_all[:, :Lc]).reshape(Bn, Lc, MLA_HEADS * MLA_V)
        yc = oc * jax.nn.silu(gc)
    return yl, yc


def setup_inputs(seed: int = 0) -> dict:
    key = jax.random.key(seed)
    ks = jax.random.split(key, 19)
    nrm = lambda k, shape, scale: jax.random.normal(k, shape, jnp.float32) * scale
    gain = lambda k, shape: 1.0 + 0.02 * jax.random.normal(k, shape, jnp.float32)
    return {
        'x': nrm(ks[0], (BATCH, SEQ, D_MODEL), 1.0),
        'c': nrm(ks[1], (BATCH, D_MODEL), 1.0),
        'ctx': nrm(ks[2], (BATCH, CTX_LEN, D_MODEL), 1.0),
        'c_ctx': nrm(ks[3], (D_MODEL,), 1.0),
        'ada_w': nrm(ks[4], (DEPTH, D_MODEL, 3 * D_MODEL), D_MODEL ** -0.5),
        'ada_b': nrm(ks[5], (DEPTH, 3 * D_MODEL), 0.01),
        'norm_g': gain(ks[6], (DEPTH, D_MODEL)),
        'out_w': nrm(ks[7], (DEPTH, D_INNER, D_MODEL), D_INNER ** -0.5),
        'ev_in_w': nrm(ks[8], (N_EVEN, D_MODEL, EVEN_IN), D_MODEL ** -0.5),
        'hg_lb': nrm(ks[9], (2, DEPTH + 1, HG_WIDTH), 0.1),
        'hg_norm_g': gain(ks[10], (N_EVEN, HG_WIDTH)),
        'pool_w': nrm(ks[11], (N_EVEN, len(POOL_WINDOWS), POOL_GROUP, POOL_GROUP), POOL_GROUP ** -0.5),
        'pool_scale': gain(ks[12], (N_EVEN, POOL_WIDTH)),
        'od_in_w': nrm(ks[13], (N_ODD, D_MODEL, ODD_IN), D_MODEL ** -0.5),
        'qa_norm_g': gain(ks[14], (N_ODD, MLA_Q_RANK)),
        'qb_w': nrm(ks[15], (N_ODD, MLA_Q_RANK, MLA_HEADS * (MLA_NOPE + MLA_ROPE)), MLA_Q_RANK ** -0.5),
        'kva_norm_g': gain(ks[16], (N_ODD, MLA_KV_RANK)),
        'kvb_w': nrm(ks[17], (N_ODD, MLA_KV_RANK, MLA_HEADS * (MLA_NOPE + MLA_V)), MLA_KV_RANK ** -0.5),
        'final_norm_g': gain(ks[18], (D_MODEL,)),
    }


def reference(x, c, ctx, c_ctx, ada_w, ada_b, norm_g, out_w, ev_in_w, hg_lb, hg_norm_g, pool_w,
              pool_scale, od_in_w, qa_norm_g, qb_w, kva_norm_g, kvb_w, final_norm_g):
    cos, sin = axial_rope_tables(x.shape[1])
    lb_all = jnp.cumsum(jax.nn.softmax(hg_lb.astype(jnp.float32), axis=1), axis=1)
    h, hc = x, ctx
    for layer in range(DEPTH):
        need_ctx = layer < DEPTH - 1
        sh_l, sc_l, gt_l = ada_modulation(c, ada_w[layer], ada_b[layer])
        sh_c, sc_c, gt_c = ada_modulation(c_ctx[None, :], ada_w[layer], ada_b[layer])
        zl = rms_norm(h, norm_g[layer]) * (1.0 + sc_l) + sh_l
        zc = rms_norm(hc, norm_g[layer]) * (1.0 + sc_c) + sh_c
        j = layer // 2
        if layer % 2 == 0:
            lb = lb_all[:, layer].reshape(2, HG_HEADS, HG_DK)
            yl, yc = even_mixer(zl, zc, ev_in_w[j], lb, hg_norm_g[j], pool_w[j], pool_scale[j], need_ctx)
        else:
            yl, yc = odd_mixer(zl, zc, od_in_w[j], qa_norm_g[j], qb_w[j], kva_norm_g[j], kvb_w[j],
                               cos, sin, need_ctx)
        h = h + gt_l * (yl @ out_w[layer])
        if need_ctx:
            hc = hc + gt_c * (yc @ out_w[layer])
    return rms_norm(h, final_norm_g)
```

```python
import functools

import numpy as np
import jax
import jax.numpy as jnp
from jax import lax
from jax.experimental import pallas as pl
from jax.experimental.pallas import tpu as pltpu

F32 = jnp.float32
BF16 = jnp.bfloat16

EPS = 1e-6
GRID_W = 64
HG_DK = 128
POOL_WINDOWS = (2, 4, 8, 16)
MLA_HEADS = 16
MLA_NOPE = 128
MLA_ROPE = 64
MLA_V = 128
MLA_QK = MLA_NOPE + MLA_ROPE
MLA_SCALE = MLA_QK ** -0.5
ROPE_FREQ = MLA_ROPE // 4
ROPE_BASE = 10000.0

LANES = 128
SUBLANES = 8
VMEM_LIMIT = 48 * 1024 * 1024

HG_CHUNK = 64
TOK_TILE = 256
KV_CHUNK = 256
POOL_HALO = 8


def _dot(a, b):
    return jnp.dot(a, b, preferred_element_type=F32)


def _dot_nt(a, b):
    return lax.dot_general(a, b, (((1,), (1,)), ((), ())), preferred_element_type=F32)


def _dot_tn(a, b):
    return lax.dot_general(a, b, (((0,), (0,)), ((), ())), preferred_element_type=F32)


def _silu(x):
    return x * jax.nn.sigmoid(x)


def _split_bf16(x):
    hi = x.astype(BF16)
    lo = (x - hi.astype(F32)).astype(BF16)
    return hi, lo


def _params(*sem):
    return pltpu.CompilerParams(dimension_semantics=sem, vmem_limit_bytes=VMEM_LIMIT)


def _ada_kernel(c_ref, w_ref, b_ref, o_ref):
    c = c_ref[...]
    s_hi, s_lo = _split_bf16(_silu(c))
    w_hi, w_lo = _split_bf16(w_ref[...])
    o_ref[...] = _dot(s_hi, w_hi) + _dot(s_lo, w_hi) + _dot(s_hi, w_lo) + b_ref[...]


def _ada(cond, ada_w, ada_b):
    depth, d, _ = ada_w.shape
    r = cond.shape[0]
    return pl.pallas_call(
        _ada_kernel,
        grid=(depth, 3),
        in_specs=[
            pl.BlockSpec((r, d), lambda l, j: (0, 0)),
            pl.BlockSpec((None, d, d), lambda l, j: (l, 0, j)),
            pl.BlockSpec((None, 1, d), lambda l, j: (l, 0, j)),
        ],
        out_specs=pl.BlockSpec((None, r, d), lambda l, j: (l, 0, j)),
        out_shape=jax.ShapeDtypeStruct((depth, r, 3 * d), F32),
        compiler_params=_params("parallel", "parallel"),
        name="ada_modulation",
    )(cond, ada_w, ada_b.reshape(depth, 1, 3 * d))


def _rms(x, g):
    return x * lax.rsqrt(jnp.mean(x * x, axis=-1, keepdims=True) + EPS) * g


def _modnorm_mm_kernel(x_ref, mod_ref, g_ref, w_ref, o_ref, z_ref):
    @pl.when(pl.program_id(2) == 0)
    def _():
        y = _rms(x_ref[...], g_ref[...])
        z_ref[...] = (y * (1.0 + mod_ref[1:2, :]) + mod_ref[0:1, :]).astype(BF16)

    o_ref[...] = _dot(z_ref[...], w_ref[...]).astype(o_ref.dtype)


def _modnorm_mm(x, mod, g, w, tm, tn, name):
    bx, r, d = x.shape
    n = w.shape[1]
    tm = min(tm, r)
    return pl.pallas_call(
        _modnorm_mm_kernel,
        grid=(bx, r // tm, n // tn),
        in_specs=[
            pl.BlockSpec((None, tm, d), lambda b, i, j: (b, i, 0)),
            pl.BlockSpec((None, 3, d), lambda b, i, j: (b, 0, 0)),
            pl.BlockSpec((1, d), lambda b, i, j: (0, 0)),
            pl.BlockSpec((d, tn), lambda b, i, j: (0, j)),
        ],
        out_specs=pl.BlockSpec((None, tm, tn), lambda b, i, j: (b, i, j)),
        out_shape=jax.ShapeDtypeStruct((bx, r, n), F32),
        scratch_shapes=[pltpu.VMEM((tm, d), BF16)],
        compiler_params=_params("parallel", "parallel", "arbitrary"),
        name=name,
    )(x, mod, g, w)


def _hgrn_levels(c):
    w = c // 2
    out = []
    while w >= 1:
        out.append(w)
        w //= 2
    return tuple(out)


def _hgrn_constants(c):
    t = np.arange(c)
    u = t[None, :]
    tt = t[:, None]
    a = [np.tril(np.ones((c, c), np.float32))]
    masks, isk = [], []
    for w in _hgrn_levels(c):
        blk = t // (2 * w)
        first = (t % (2 * w)) < w
        ref = (blk * 2 * w + w - 1)[:, None]
        a.append(np.where(first[:, None], (u > tt) & (u <= ref), (u > ref) & (u <= tt)).astype(np.float32))
        masks.append(((blk[:, None] == blk[None, :]) & (~first[:, None]) & first[None, :]).astype(np.float32))
        isk.append(first.astype(np.float32)[:, None])
    a = np.stack(a)
    masks = np.stack(masks)
    isk = np.stack(isk)
    flip = lambda m: m[:, ::-1, ::-1]
    a2 = np.stack([a, flip(a)]).reshape(2, -1, c)
    m2 = np.stack([masks, flip(masks)])
    k2 = np.stack([isk, isk[:, ::-1]])
    return jnp.asarray(a2, BF16), jnp.asarray(m2, F32), jnp.asarray(k2, F32)


def _hgrn_kernel(qf_ref, ff_ref, vf_ref, qb_ref, fb_ref, vb_ref, lb_ref, a_ref, mk_ref, isk_ref, s0_ref,
                 of_ref, ob_ref, sout_ref, st_ref, d_ref, q_s, k_s, v_s, o_s, *, tb, c, nh):
    nlev = len(_hgrn_levels(c))
    nchunk = tb // c

    @pl.when(pl.program_id(1) == 0)
    def _():
        st_ref[...] = s0_ref[...]

    dirs = ((qf_ref, ff_ref, vf_ref, of_ref), (qb_ref, fb_ref, vb_ref, ob_ref))
    for d, (q_ref, f_ref, v_ref, o_ref) in enumerate(dirs):
        lb = lb_ref[d]
        last = c - 1 if d == 0 else 0
        for cc in range(nchunk):
            r0 = (cc if d == 0 else nchunk - 1 - cc) * c
            qp = q_ref[r0:r0 + c, :]
            f = lb + (1.0 - lb) * jax.nn.sigmoid(f_ref[r0:r0 + c, :])
            g_hi, g_lo = _split_bf16(jnp.log(f))
            a = a_ref[d]
            dd = _dot(a, g_hi) + _dot(a, g_lo)
            q = _silu(qp)
            k = 1.0 - f
            v = v_ref[r0:r0 + c, :]
            for h in range(nh):
                sl = slice(h * HG_DK, (h + 1) * HG_DK)
                d_ref[h] = dd[:, sl]
                q_s[h] = q[:, sl]
                k_s[h] = k[:, sl]
                v_s[h] = v[:, sl]

            def head(h, carry, d=d, last=last):
                qh = q_s[h]
                kh = k_s[h]
                vh = v_s[h]
                st = st_ref[d, h]
                b = d_ref[h, 0:c, :]
                inter = _dot_nt((qh * jnp.exp(b)).astype(BF16), st.astype(BF16))
                att = jnp.zeros((c, c), F32)
                for l in range(nlev):
                    e = jnp.exp(d_ref[h, (l + 1) * c:(l + 2) * c, :])
                    x = (e * jnp.where(isk_ref[d, l] > 0.0, kh, qh)).astype(BF16)
                    att = att + mk_ref[d, l] * _dot_nt(x, x)
                diag = jnp.sum(qh * kh, axis=-1, keepdims=True)
                o_s[h] = inter + _dot(att.astype(BF16), vh.astype(BF16)) + diag * vh
                bl = b[last:last + 1, :]
                kend = (kh * jnp.exp(bl - b)).astype(BF16)
                st_ref[d, h] = jnp.exp(bl) * st + _dot_tn(vh.astype(BF16), kend)
                return carry

            lax.fori_loop(0, nh, head, 0)
            for h in range(nh):
                o_ref[r0:r0 + c, h * HG_DK:(h + 1) * HG_DK] = o_s[h]

    @pl.when(pl.program_id(1) == pl.num_programs(1) - 1)
    def _():
        sout_ref[...] = st_ref[...]


def _hgrn(p, lb, s0, consts):
    bsz, r, _ = p.shape
    w = lb.shape[-1]
    nh = w // HG_DK
    tb = min(TOK_TILE, r)
    c = HG_CHUNK
    nb = r // tb
    a2, m2, k2 = consts
    nrow = a2.shape[1]
    fwd = lambda col: pl.BlockSpec((None, tb, w), lambda b, s: (b, s, col))
    bwd = lambda col: pl.BlockSpec((None, tb, w), lambda b, s: (b, nb - 1 - s, col))
    const = lambda arr: pl.BlockSpec(arr.shape, lambda b, s: (0,) * arr.ndim)
    st_spec = pl.BlockSpec((None, 2, nh, HG_DK, HG_DK), lambda b, s: (b, 0, 0, 0, 0))
    kern = functools.partial(_hgrn_kernel, tb=tb, c=c, nh=nh)
    return pl.pallas_call(
        kern,
        grid=(bsz, nb),
        in_specs=[fwd(0), fwd(1), fwd(3), bwd(0), bwd(2), bwd(3), const(lb), const(a2), const(m2), const(k2),
                  st_spec],
        out_specs=[
            pl.BlockSpec((None, tb, w), lambda b, s: (b, s, 0)),
            pl.BlockSpec((None, tb, w), lambda b, s: (b, nb - 1 - s, 0)),
            st_spec,
        ],
        out_shape=[
            jax.ShapeDtypeStruct((bsz, r, w), F32),
            jax.ShapeDtypeStruct((bsz, r, w), F32),
            jax.ShapeDtypeStruct(s0.shape, F32),
        ],
        scratch_shapes=[
            pltpu.VMEM((2, nh, HG_DK, HG_DK), F32),
            pltpu.VMEM((nh, nrow, HG_DK), F32),
            pltpu.VMEM((nh, c, HG_DK), F32),
            pltpu.VMEM((nh, c, HG_DK), F32),
            pltpu.VMEM((nh, c, HG_DK), F32),
            pltpu.VMEM((nh, c, HG_DK), F32),
        ],
        compiler_params=_params("parallel", "arbitrary"),
        name="hgrn2_scan",
    )(p, p, p, p, p, p, lb, a2, m2, k2, s0)


def _even_post_kernel(of_ref, ob_ref, ga_ref, u_ref, gb_ref, up_ref, un_ref, h_ref, mod_ref, hgn_ref, pw_ref,
                      ps_ref, ow_ref, o_ref, ext_ref, y_ref, *, tb, seq, nh):
    j = pl.program_id(1)
    w = nh * HG_DK
    o = of_ref[...] + ob_ref[...]
    for h in range(nh):
        sl = slice(h * HG_DK, (h + 1) * HG_DK)
        y_ref[:, sl] = (_rms(o[:, sl], hgn_ref[:, sl]) * _silu(ga_ref[:, sl])).astype(BF16)
    u = u_ref[...]
    ext_ref[0:POOL_HALO, :] = jnp.where(j > 0, up_ref[...], 0.0)
    ext_ref[POOL_HALO:POOL_HALO + tb, :] = u
    ext_ref[POOL_HALO + tb:, :] = jnp.where(j < pl.num_programs(1) - 1, un_ref[...], 0.0)
    t = j * tb + lax.broadcasted_iota(jnp.int32, (tb, 1), 0)
    grp = w // len(POOL_WINDOWS)
    for gi, win in enumerate(POOL_WINDOWS):
        sl = slice(gi * grp, (gi + 1) * grp)
        acc = ext_ref[POOL_HALO - win // 2:POOL_HALO - win // 2 + tb, sl]
        for off in range(-win // 2 + 1, win // 2):
            acc = acc + ext_ref[POOL_HALO + off:POOL_HALO + off + tb, sl]
        cnt = (jnp.minimum(t + win // 2, seq) - jnp.maximum(t - win // 2, 0)).astype(F32)
        yp = acc / cnt - u[:, sl]
        yb = _dot(yp.astype(BF16), pw_ref[gi]) * ps_ref[:, sl]
        y_ref[:, w + gi * grp:w + (gi + 1) * grp] = (yb * _silu(gb_ref[:, sl])).astype(BF16)
    o_ref[...] = h_ref[...] + mod_ref[2:3, :] * _dot(y_ref[...], ow_ref[...])


def _even_post(o_f, o_b, p, h, mod, hgn, pool_w, pool_scale, out_w):
    bsz, r, w = o_f.shape
    d = h.shape[-1]
    tb = min(TOK_TILE, r)
    nb = r // tb
    hb = tb // POOL_HALO
    nh = w // HG_DK
    tok = lambda col: pl.BlockSpec((None, tb, w), lambda b, j: (b, j, col))
    const = lambda arr: pl.BlockSpec(arr.shape, lambda b, j: (0,) * arr.ndim)
    kern = functools.partial(_even_post_kernel, tb=tb, seq=r, nh=nh)
    return pl.pallas_call(
        kern,
        grid=(bsz, nb),
        in_specs=[
            tok(0), tok(0), tok(4), tok(5), tok(6),
            pl.BlockSpec((None, POOL_HALO, w), lambda b, j: (b, jnp.maximum(j * hb - 1, 0), 5)),
            pl.BlockSpec((None, POOL_HALO, w), lambda b, j: (b, jnp.minimum((j + 1) * hb, nb * hb - 1), 5)),
            pl.BlockSpec((None, tb, d), lambda b, j: (b, j, 0)),
            pl.BlockSpec((None, 3, d), lambda b, j: (b, 0, 0)),
            const(hgn), const(pool_w), const(pool_scale), const(out_w),
        ],
        out_specs=pl.BlockSpec((None, tb, d), lambda b, j: (b, j, 0)),
        out_shape=jax.ShapeDtypeStruct((bsz, r, d), F32),
        scratch_shapes=[pltpu.VMEM((tb + 2 * POOL_HALO, w), F32), pltpu.VMEM((tb, 2 * w), BF16)],
        compiler_params=_params("parallel", "parallel"),
        name="even_post",
    )(o_f, o_b, p, p, p, p, p, h, mod, hgn, pool_w, pool_scale, out_w)


def _mla_kv_kernel(*refs, rope):
    if rope:
        ckv_ref, kr_ref, g_ref, wuk_ref, wuvt_ref, cos_ref, sin_ref, kcat_ref, vt_ref = refs
    else:
        ckv_ref, kr_ref, g_ref, wuk_ref, wuvt_ref, kcat_ref, vt_ref = refs
    cn = _rms(ckv_ref[...], g_ref[...]).astype(BF16)
    kn = _dot(cn, wuk_ref[...])
    kr = kr_ref[...]
    if rope:
        lane = lax.broadcasted_iota(jnp.int32, kr.shape, 1)
        swapped = jnp.where((lane % (2 * ROPE_FREQ)) < ROPE_FREQ,
                            pltpu.roll(kr, LANES - ROPE_FREQ, 1), pltpu.roll(kr, ROPE_FREQ, 1))
        kr = kr * cos_ref[...] + swapped * sin_ref[...]
    kr = kr[:, 0:MLA_ROPE].astype(BF16)
    for h in range(MLA_HEADS):
        kcat_ref[h, :, 0:MLA_NOPE] = kn[:, h * MLA_NOPE:(h + 1) * MLA_NOPE].astype(BF16)
        kcat_ref[h, :, MLA_NOPE:MLA_QK] = kr
        vt_ref[h] = _dot_nt(wuvt_ref[h], cn).astype(BF16)


def _mla_kv(p, ckv_blk, kr_blk, g, wuk, wuvt, tables):
    bsz, r, _ = p.shape
    rank = g.shape[-1]
    tb = KV_CHUNK
    nb = r // tb
    const = lambda arr: pl.BlockSpec(arr.shape, lambda b, j: (0,) * arr.ndim)
    in_specs = [
        pl.BlockSpec((None, tb, rank), lambda b, j: (b, j, ckv_blk)),
        pl.BlockSpec((None, tb, LANES), lambda b, j: (b, j, kr_blk)),
        const(g), const(wuk), const(wuvt),
    ]
    args = [p, p, g, wuk, wuvt]
    if tables is not None:
        in_specs += [pl.BlockSpec((tb, LANES), lambda b, j: (j, 0))] * 2
        args += list(tables)
    return pl.pallas_call(
        functools.partial(_mla_kv_kernel, rope=tables is not None),
        grid=(bsz, nb),
        in_specs=in_specs,
        out_specs=[
            pl.BlockSpec((None, MLA_HEADS, tb, MLA_QK), lambda b, j: (b, 0, j, 0)),
            pl.BlockSpec((None, MLA_HEADS, None, MLA_V, tb), lambda b, j: (b, 0, j, 0, 0)),
        ],
        out_shape=[
            jax.ShapeDtypeStruct((bsz, MLA_HEADS, r, MLA_QK), BF16),
            jax.ShapeDtypeStruct((bsz, MLA_HEADS, nb, MLA_V, tb), BF16),
        ],
        compiler_params=_params("parallel", "parallel"),
        name="mla_kv_rope" if tables is not None else "mla_kv",
    )(*args)


def _mla_q_kernel(cq_ref, g_ref, wqt_ref, cos_ref, sin_ref, qt_ref):
    cn = _rms(cq_ref[...], g_ref[...]).astype(BF16)
    f = ROPE_FREQ
    for h in range(MLA_HEADS):
        qt = _dot_nt(wqt_ref[h], cn) * MLA_SCALE
        qt_ref[h, 0:MLA_NOPE, :] = qt[0:MLA_NOPE].astype(BF16)
        for ax in range(2):
            r0 = MLA_NOPE + ax * 2 * f
            x1 = qt[r0:r0 + f]
            x2 = qt[r0 + f:r0 + 2 * f]
            co = cos_ref[ax]
            si = sin_ref[ax]
            qt_ref[h, r0:r0 + f, :] = (x1 * co - x2 * si).astype(BF16)
            qt_ref[h, r0 + f:r0 + 2 * f, :] = (x2 * co + x1 * si).astype(BF16)


def _mla_q(p, cq_blk, g, wqt, cos_t, sin_t):
    bsz, t, _ = p.shape
    rank = g.shape[-1]
    tm = min(TOK_TILE, t)
    const = lambda arr: pl.BlockSpec(arr.shape, lambda b, j: (0,) * arr.ndim)
    tab = pl.BlockSpec((2, ROPE_FREQ, tm), lambda b, j: (0, 0, j))
    return pl.pallas_call(
        _mla_q_kernel,
        grid=(bsz, t // tm),
        in_specs=[pl.BlockSpec((None, tm, rank), lambda b, j: (b, j, cq_blk)), const(g), const(wqt), tab, tab],
        out_specs=pl.BlockSpec((None, MLA_HEADS, MLA_QK, tm), lambda b, j: (b, 0, 0, j)),
        out_shape=jax.ShapeDtypeStruct((bsz, MLA_HEADS, MLA_QK, t), BF16),
        compiler_params=_params("parallel", "parallel"),
        name="mla_q",
    )(p, g, wqt, cos_t, sin_t)


def _attn_kernel(qt_ref, kc_ref, vtc_ref, kl_ref, vtl_ref, g_ref, o_ref, *, tq, n_ctx, n_lat):
    qt = qt_ref[...]

    def step(k, vt, carry):
        m, l, acc = carry
        s = _dot(k, qt)
        m_new = jnp.maximum(m, jnp.max(s, axis=0, keepdims=True))
        alpha = jnp.exp(m - m_new)
        p = jnp.exp(s - m_new)
        l = alpha * l + jnp.sum(p, axis=0, keepdims=True)
        acc = alpha * acc + _dot(vt, p.astype(BF16))
        return m_new, l, acc

    carry = (jnp.full((1, tq), -jnp.inf, F32), jnp.zeros((1, tq), F32), jnp.zeros((MLA_V, tq), F32))
    for i in range(n_ctx):
        carry = step(kc_ref[i * KV_CHUNK:(i + 1) * KV_CHUNK, :], vtc_ref[i], carry)

    def body(i, carry):
        r0 = pl.multiple_of(i * KV_CHUNK, KV_CHUNK)
        return step(kl_ref[pl.ds(r0, KV_CHUNK), :], vtl_ref[i], carry)

    m, l, acc = lax.fori_loop(0, n_lat, body, carry)
    o = (acc / l).T
    o_ref[...] = (o * _silu(g_ref[...])).astype(o_ref.dtype)


def _attn(qt, kc, vtc, kl, vtl, p):
    bsz, nh, _, t = qt.shape
    lc = kc.shape[2]
    tq = min(2 * TOK_TILE, t)
    kern = functools.partial(_attn_kernel, tq=tq, n_ctx=lc // KV_CHUNK, n_lat=t // KV_CHUNK)
    full4 = lambda arr: pl.BlockSpec((None, None) + arr.shape[2:], lambda b, h, i: (b, h, 0, 0))
    full5 = lambda arr: pl.BlockSpec((None, None) + arr.shape[2:], lambda b, h, i: (b, h, 0, 0, 0))
    return pl.pallas_call(
        kern,
        grid=(bsz, nh, t // tq),
        in_specs=[
            pl.BlockSpec((None, None, MLA_QK, tq), lambda b, h, i: (b, h, 0, i)),
            full4(kc), full5(vtc), full4(kl), full5(vtl),
            pl.BlockSpec((None, tq, MLA_V), lambda b, h, i: (b, i, h)),
        ],
        out_specs=pl.BlockSpec((None, tq, MLA_V), lambda b, h, i: (b, i, h)),
        out_shape=jax.ShapeDtypeStruct((bsz, t, nh * MLA_V), BF16),
        compiler_params=_params("parallel", "parallel", "arbitrary"),
        name="mla_attention",
    )(qt, kc, vtc, kl, vtl, p)


def _out_final_kernel(y_ref, h_ref, mod_ref, ow_ref, g_ref, o_ref):
    hn = h_ref[...] + mod_ref[2:3, :] * _dot(y_ref[...], ow_ref[...])
    o_ref[...] = _rms(hn, g_ref[...])


def _out_final(y, h, mod, out_w, g):
    bsz, t, d = h.shape
    wi = y.shape[-1]
    tm = min(2 * TOK_TILE, t)
    return pl.pallas_call(
        _out_final_kernel,
        grid=(bsz, t // tm),
        in_specs=[
            pl.BlockSpec((None, tm, wi), lambda b, j: (b, j, 0)),
            pl.BlockSpec((None, tm, d), lambda b, j: (b, j, 0)),
            pl.BlockSpec((None, 3, d), lambda b, j: (b, 0, 0)),
            pl.BlockSpec((wi, d), lambda b, j: (0, 0)),
            pl.BlockSpec((1, d), lambda b, j: (0, 0)),
        ],
        out_specs=pl.BlockSpec((None, tm, d), lambda b, j: (b, j, 0)),
        out_shape=jax.ShapeDtypeStruct((bsz, t, d), F32),
        compiler_params=_params("parallel", "parallel"),
        name="out_final",
    )(y, h, mod, out_w, g)


def _rope_tables(n_tokens):
    rows = n_tokens // GRID_W
    pos_r = jnp.repeat(jnp.arange(rows), GRID_W).astype(F32)
    pos_c = jnp.tile(jnp.arange(GRID_W), rows).astype(F32)
    inv = ROPE_BASE ** (-2.0 * jnp.arange(ROPE_FREQ, dtype=F32) / (MLA_ROPE // 2))
    ang = jnp.stack([pos_r[:, None] * inv, pos_c[:, None] * inv], axis=1)
    cos, sin = jnp.cos(ang), jnp.sin(ang)
    pad = LANES - MLA_ROPE
    cos_k = jnp.pad(jnp.stack([cos, cos], axis=2).reshape(n_tokens, MLA_ROPE), ((0, 0), (0, pad)))
    sin_k = jnp.pad(jnp.stack([-sin, sin], axis=2).reshape(n_tokens, MLA_ROPE), ((0, 0), (0, pad)))
    cos_q = jnp.transpose(cos, (1, 2, 0))
    sin_q = jnp.transpose(sin, (1, 2, 0))
    return (cos_k, sin_k), (cos_q, sin_q)


def kernel(x, c, ctx, c_ctx, ada_w, ada_b, norm_g, out_w, ev_in_w, hg_lb, hg_norm_g, pool_w, pool_scale,
           od_in_w, qa_norm_g, qb_w, kva_norm_g, kvb_w, final_norm_g):
    bsz, t, d = x.shape
    lc = ctx.shape[1]
    depth = ada_w.shape[0]
    assert depth == 2 and t % (2 * TOK_TILE) == 0 and lc % TOK_TILE == 0 and t % GRID_W == 0
    w = hg_norm_g.shape[-1]
    nh = w // HG_DK
    q_rank = qa_norm_g.shape[-1]
    kv_rank = kva_norm_g.shape[-1]
    d_inner = out_w.shape[1]

    n_cond = -(-(bsz + 1) // SUBLANES) * SUBLANES
    cond = jnp.zeros((n_cond, d), F32).at[:bsz].set(c).at[bsz].set(c_ctx)
    mods = _ada(cond, ada_w, ada_b).reshape(depth, n_cond, 3, d)
    mod_l = [mods[l, :bsz] for l in range(depth)]
    mod_c = [mods[l, bsz:bsz + 1] for l in range(depth)]

    lb = jnp.cumsum(jax.nn.softmax(hg_lb.astype(F32), axis=1), axis=1)[:, 0].reshape(2, 1, w)
    w_in0 = ev_in_w[0].astype(BF16)
    g0 = norm_g[0].reshape(1, d)
    ctx_flat = ctx.reshape(1, bsz * lc, d)
    n_in0 = w_in0.shape[1]
    p_c = _modnorm_mm(ctx_flat, mod_c[0], g0, w_in0, 4 * TOK_TILE, n_in0 // 4, "in_proj0_ctx").reshape(bsz, lc, n_in0)
    p_l = _modnorm_mm(x, mod_l[0], g0, w_in0, 4 * TOK_TILE, n_in0 // 4, "in_proj0")
    consts = _hgrn_constants(HG_CHUNK)
    s0 = jnp.zeros((bsz, 2, nh, HG_DK, HG_DK), F32)
    of_c, ob_c, s_c = _hgrn(p_c, lb, s0, consts)
    of_l, ob_l, _ = _hgrn(p_l, lb, s_c, consts)
    hgn = hg_norm_g[0].reshape(1, w)
    pw = pool_w[0].astype(BF16)
    ps = pool_scale[0].reshape(1, w)
    ow0 = out_w[0].astype(BF16)
    mod_c0 = jnp.broadcast_to(mod_c[0], (bsz, 3, d))
    hc1 = _even_post(of_c, ob_c, p_c, ctx, mod_c0, hgn, pw, ps, ow0)
    hl1 = _even_post(of_l, ob_l, p_l, x, mod_l[0], hgn, pw, ps, ow0)

    o1 = q_rank
    o2 = o1 + kv_rank
    o3 = o2 + MLA_ROPE
    w1 = od_in_w[0]
    kr_pad = jnp.zeros((d, LANES - MLA_ROPE), F32)
    w_in1 = jnp.concatenate([w1[:, o3:], w1[:, :o1], w1[:, o1:o2], w1[:, o2:o3], kr_pad], axis=1).astype(BF16)
    n_kv = kv_rank + LANES
    w_in1c = w_in1[:, d_inner + q_rank:]
    g1 = norm_g[1].reshape(1, d)
    p1_c = _modnorm_mm(hc1.reshape(1, bsz * lc, d), mod_c[1], g1, w_in1c, 4 * TOK_TILE, n_kv, "in_proj1_ctx")
    p1_c = p1_c.reshape(bsz, lc, n_kv)
    p1_l = _modnorm_mm(hl1, mod_l[1], g1, w_in1, 2 * TOK_TILE, w_in1.shape[1], "in_proj1")

    kvw = kvb_w[0].reshape(kv_rank, MLA_HEADS, MLA_NOPE + MLA_V)
    wuk = kvw[..., :MLA_NOPE].reshape(kv_rank, MLA_HEADS * MLA_NOPE).astype(BF16)
    wuvt = jnp.transpose(kvw[..., MLA_NOPE:], (1, 2, 0)).astype(BF16)
    wqt = jnp.transpose(qb_w[0].reshape(q_rank, MLA_HEADS, MLA_QK), (1, 2, 0)).astype(BF16)
    kvg = kva_norm_g[0].reshape(1, kv_rank)
    qag = qa_norm_g[0].reshape(1, q_rank)
    tab_k, tab_q = _rope_tables(t)
    kc, vtc = _mla_kv(p1_c, 0, kv_rank // LANES, kvg, wuk, wuvt, None)
    kl, vtl = _mla_kv(p1_l, (d_inner + q_rank) // kv_rank, (d_inner + q_rank + kv_rank) // LANES, kvg, wuk, wuvt,
                      tab_k)
    qt = _mla_q(p1_l, d_inner // q_rank, qag, wqt, *tab_q)
    y = _attn(qt, kc, vtc, kl, vtl, p1_l)
    return _out_final(y, hl1, mod_l[1], out_w[1].astype(BF16), final_norm_g.reshape(1, d))
```

```python
import functools

import numpy as np
import jax
import jax.numpy as jnp
from jax import lax
from jax.experimental import pallas as pl
from jax.experimental.pallas import tpu as pltpu

F32 = jnp.float32
BF16 = jnp.bfloat16

EPS = 1e-6
GRID_W = 64
HG_DK = 128
POOL_WINDOWS = (2, 4, 8, 16)
MLA_HEADS = 16
MLA_NOPE = 128
MLA_ROPE = 64
MLA_V = 128
MLA_QK = MLA_NOPE + MLA_ROPE
MLA_SCALE = MLA_QK ** -0.5
LOG2_E = 1.4426950408889634
ROPE_FREQ = MLA_ROPE // 4
ROPE_BASE = 10000.0

LANES = 128
SUBLANES = 8
VMEM_LIMIT = 48 * 1024 * 1024

HG_CHUNK = 64
TOK_TILE = 256
KV_CHUNK = 256
Q_TILE = 2048
Q_SUB = 256
POOL_HALO = 8


def _dot(a, b):
    return jnp.dot(a, b, preferred_element_type=F32)


def _dot_nt(a, b):
    return lax.dot_general(a, b, (((1,), (1,)), ((), ())), preferred_element_type=F32)


def _dot_tn(a, b):
    return lax.dot_general(a, b, (((0,), (0,)), ((), ())), preferred_element_type=F32)


def _silu(x):
    return x * jax.nn.sigmoid(x)


def _split_bf16(x):
    hi = x.astype(BF16)
    lo = (x - hi.astype(F32)).astype(BF16)
    return hi, lo


def _params(*sem):
    return pltpu.CompilerParams(dimension_semantics=sem, vmem_limit_bytes=VMEM_LIMIT)


def _ada_kernel(c_ref, w_ref, b_ref, o_ref):
    c = c_ref[...]
    s_hi, s_lo = _split_bf16(_silu(c))
    w_hi, w_lo = _split_bf16(w_ref[...])
    o_ref[...] = _dot(s_hi, w_hi) + _dot(s_lo, w_hi) + _dot(s_hi, w_lo) + b_ref[...]


def _ada(cond, ada_w, ada_b):
    depth, d, _ = ada_w.shape
    r = cond.shape[0]
    return pl.pallas_call(
        _ada_kernel,
        grid=(depth, 3),
        in_specs=[
            pl.BlockSpec((r, d), lambda l, j: (0, 0)),
            pl.BlockSpec((None, d, d), lambda l, j: (l, 0, j)),
            pl.BlockSpec((None, 1, d), lambda l, j: (l, 0, j)),
        ],
        out_specs=pl.BlockSpec((None, r, d), lambda l, j: (l, 0, j)),
        out_shape=jax.ShapeDtypeStruct((depth, r, 3 * d), F32),
        compiler_params=_params("parallel", "parallel"),
        name="ada_modulation",
    )(cond, ada_w, ada_b.reshape(depth, 1, 3 * d))


def _rms(x, g):
    return x * lax.rsqrt(jnp.mean(x * x, axis=-1, keepdims=True) + EPS) * g


def _modnorm_mm_kernel(x_ref, mod_ref, g_ref, w_ref, o_ref, z_ref):
    @pl.when(pl.program_id(2) == 0)
    def _():
        y = _rms(x_ref[...], g_ref[...])
        z_ref[...] = (y * (1.0 + mod_ref[1:2, :]) + mod_ref[0:1, :]).astype(BF16)

    o_ref[...] = _dot(z_ref[...], w_ref[...]).astype(o_ref.dtype)


def _modnorm_mm(x, mod, g, w, tm, tn, name):
    bx, r, d = x.shape
    n = w.shape[1]
    tm = min(tm, r)
    return pl.pallas_call(
        _modnorm_mm_kernel,
        grid=(bx, r // tm, n // tn),
        in_specs=[
            pl.BlockSpec((None, tm, d), lambda b, i, j: (b, i, 0)),
            pl.BlockSpec((None, 3, d), lambda b, i, j: (b, 0, 0)),
            pl.BlockSpec((1, d), lambda b, i, j: (0, 0)),
            pl.BlockSpec((d, tn), lambda b, i, j: (0, j)),
        ],
        out_specs=pl.BlockSpec((None, tm, tn), lambda b, i, j: (b, i, j)),
        out_shape=jax.ShapeDtypeStruct((bx, r, n), F32),
        scratch_shapes=[pltpu.VMEM((tm, d), BF16)],
        compiler_params=_params("parallel", "parallel", "arbitrary"),
        name=name,
    )(x, mod, g, w)


def _hgrn_levels(c):
    w = c // 2
    out = []
    while w >= 1:
        out.append(w)
        w //= 2
    return tuple(out)


def _hgrn_constants(c):
    t = np.arange(c)
    u = t[None, :]
    tt = t[:, None]
    a = [np.tril(np.ones((c, c), np.float32))]
    masks, isk = [], []
    for w in _hgrn_levels(c):
        blk = t // (2 * w)
        first = (t % (2 * w)) < w
        ref = (blk * 2 * w + w - 1)[:, None]
        a.append(np.where(first[:, None], (u > tt) & (u <= ref), (u > ref) & (u <= tt)).astype(np.float32))
        masks.append(((blk[:, None] == blk[None, :]) & (~first[:, None]) & first[None, :]).astype(np.float32))
        isk.append(first.astype(np.float32)[:, None])
    a = np.stack(a)
    masks = np.stack(masks)
    isk = np.stack(isk)
    flip = lambda m: m[:, ::-1, ::-1]
    a2 = np.stack([a, flip(a)]).reshape(2, -1, c)
    m2 = np.stack([masks, flip(masks)])
    k2 = np.stack([isk, isk[:, ::-1]])
    return jnp.asarray(a2, BF16), jnp.asarray(m2, F32), jnp.asarray(k2, F32)


def _hgrn_kernel(qf_ref, ff_ref, vf_ref, qb_ref, fb_ref, vb_ref, lb_ref, a_ref, mk_ref, isk_ref, s0_ref,
                 of_ref, ob_ref, sout_ref, st_ref, d_ref, q_s, k_s, v_s, o_s, *, tb, c, nh):
    nlev = len(_hgrn_levels(c))
    nchunk = tb // c

    @pl.when(pl.program_id(1) == 0)
    def _():
        st_ref[...] = s0_ref[...]

    dirs = ((qf_ref, ff_ref, vf_ref, of_ref), (qb_ref, fb_ref, vb_ref, ob_ref))
    for d, (q_ref, f_ref, v_ref, o_ref) in enumerate(dirs):
        lb = lb_ref[d]
        last = c - 1 if d == 0 else 0
        for cc in range(nchunk):
            r0 = (cc if d == 0 else nchunk - 1 - cc) * c
            qp = q_ref[r0:r0 + c, :]
            f = lb + (1.0 - lb) * jax.nn.sigmoid(f_ref[r0:r0 + c, :])
            g_hi, g_lo = _split_bf16(jnp.log(f))
            a = a_ref[d]
            dd = _dot(a, g_hi) + _dot(a, g_lo)
            q = _silu(qp)
            k = 1.0 - f
            v = v_ref[r0:r0 + c, :]
            for h in range(nh):
                sl = slice(h * HG_DK, (h + 1) * HG_DK)
                d_ref[h] = dd[:, sl]
                q_s[h] = q[:, sl]
                k_s[h] = k[:, sl]
                v_s[h] = v[:, sl]

            def head(h, carry, d=d, last=last):
                qh = q_s[h]
                kh = k_s[h]
                vh = v_s[h]
                st = st_ref[d, h]
                b = d_ref[h, 0:c, :]
                inter = _dot_nt((qh * jnp.exp(b)).astype(BF16), st.astype(BF16))
                att = jnp.zeros((c, c), F32)
                for l in range(nlev):
                    e = jnp.exp(d_ref[h, (l + 1) * c:(l + 2) * c, :])
                    x = (e * jnp.where(isk_ref[d, l] > 0.0, kh, qh)).astype(BF16)
                    att = att + mk_ref[d, l] * _dot_nt(x, x)
                diag = jnp.sum(qh * kh, axis=-1, keepdims=True)
                o_s[h] = inter + _dot(att.astype(BF16), vh.astype(BF16)) + diag * vh
                bl = b[last:last + 1, :]
                kend = (kh * jnp.exp(bl - b)).astype(BF16)
                st_ref[d, h] = jnp.exp(bl) * st + _dot_tn(vh.astype(BF16), kend)
                return carry

            lax.fori_loop(0, nh, head, 0)
            for h in range(nh):
                o_ref[r0:r0 + c, h * HG_DK:(h + 1) * HG_DK] = o_s[h]

    @pl.when(pl.program_id(1) == pl.num_programs(1) - 1)
    def _():
        sout_ref[...] = st_ref[...]


def _hgrn(p, lb, s0, consts):
    bsz, r, _ = p.shape
    w = lb.shape[-1]
    nh = w // HG_DK
    tb = min(TOK_TILE, r)
    c = HG_CHUNK
    nb = r // tb
    a2, m2, k2 = consts
    nrow = a2.shape[1]
    fwd = lambda col: pl.BlockSpec((None, tb, w), lambda b, s: (b, s, col))
    bwd = lambda col: pl.BlockSpec((None, tb, w), lambda b, s: (b, nb - 1 - s, col))
    const = lambda arr: pl.BlockSpec(arr.shape, lambda b, s: (0,) * arr.ndim)
    st_spec = pl.BlockSpec((None, 2, nh, HG_DK, HG_DK), lambda b, s: (b, 0, 0, 0, 0))
    kern = functools.partial(_hgrn_kernel, tb=tb, c=c, nh=nh)
    return pl.pallas_call(
        kern,
        grid=(bsz, nb),
        in_specs=[fwd(0), fwd(1), fwd(3), bwd(0), bwd(2), bwd(3), const(lb), const(a2), const(m2), const(k2),
                  st_spec],
        out_specs=[
            pl.BlockSpec((None, tb, w), lambda b, s: (b, s, 0)),
            pl.BlockSpec((None, tb, w), lambda b, s: (b, nb - 1 - s, 0)),
            st_spec,
        ],
        out_shape=[
            jax.ShapeDtypeStruct((bsz, r, w), F32),
            jax.ShapeDtypeStruct((bsz, r, w), F32),
            jax.ShapeDtypeStruct(s0.shape, F32),
        ],
        scratch_shapes=[
            pltpu.VMEM((2, nh, HG_DK, HG_DK), F32),
            pltpu.VMEM((nh, nrow, HG_DK), F32),
            pltpu.VMEM((nh, c, HG_DK), F32),
            pltpu.VMEM((nh, c, HG_DK), F32),
            pltpu.VMEM((nh, c, HG_DK), F32),
            pltpu.VMEM((nh, c, HG_DK), F32),
        ],
        compiler_params=_params("parallel", "arbitrary"),
        name="hgrn2_scan",
    )(p, p, p, p, p, p, lb, a2, m2, k2, s0)


def _even_post_kernel(of_ref, ob_ref, ga_ref, u_ref, gb_ref, up_ref, un_ref, h_ref, mod_ref, hgn_ref, pw_ref,
                      ps_ref, ow_ref, o_ref, ext_ref, y_ref, *, tb, seq, nh):
    j = pl.program_id(1)
    w = nh * HG_DK
    o = of_ref[...] + ob_ref[...]
    for h in range(nh):
        sl = slice(h * HG_DK, (h + 1) * HG_DK)
        y_ref[:, sl] = (_rms(o[:, sl], hgn_ref[:, sl]) * _silu(ga_ref[:, sl])).astype(BF16)
    u = u_ref[...]
    ext_ref[0:POOL_HALO, :] = jnp.where(j > 0, up_ref[...], 0.0)
    ext_ref[POOL_HALO:POOL_HALO + tb, :] = u
    ext_ref[POOL_HALO + tb:, :] = jnp.where(j < pl.num_programs(1) - 1, un_ref[...], 0.0)
    t = j * tb + lax.broadcasted_iota(jnp.int32, (tb, 1), 0)
    grp = w // len(POOL_WINDOWS)
    for gi, win in enumerate(POOL_WINDOWS):
        sl = slice(gi * grp, (gi + 1) * grp)
        acc = ext_ref[POOL_HALO - win // 2:POOL_HALO - win // 2 + tb, sl]
        for off in range(-win // 2 + 1, win // 2):
            acc = acc + ext_ref[POOL_HALO + off:POOL_HALO + off + tb, sl]
        cnt = (jnp.minimum(t + win // 2, seq) - jnp.maximum(t - win // 2, 0)).astype(F32)
        yp = acc / cnt - u[:, sl]
        yb = _dot(yp.astype(BF16), pw_ref[gi]) * ps_ref[:, sl]
        y_ref[:, w + gi * grp:w + (gi + 1) * grp] = (yb * _silu(gb_ref[:, sl])).astype(BF16)
    o_ref[...] = h_ref[...] + mod_ref[2:3, :] * _dot(y_ref[...], ow_ref[...])


def _even_post(o_f, o_b, p, h, mod, hgn, pool_w, pool_scale, out_w):
    bsz, r, w = o_f.shape
    d = h.shape[-1]
    tb = min(TOK_TILE, r)
    nb = r // tb
    hb = tb // POOL_HALO
    nh = w // HG_DK
    tok = lambda col: pl.BlockSpec((None, tb, w), lambda b, j: (b, j, col))
    const = lambda arr: pl.BlockSpec(arr.shape, lambda b, j: (0,) * arr.ndim)
    kern = functools.partial(_even_post_kernel, tb=tb, seq=r, nh=nh)
    return pl.pallas_call(
        kern,
        grid=(bsz, nb),
        in_specs=[
            tok(0), tok(0), tok(4), tok(5), tok(6),
            pl.BlockSpec((None, POOL_HALO, w), lambda b, j: (b, jnp.maximum(j * hb - 1, 0), 5)),
            pl.BlockSpec((None, POOL_HALO, w), lambda b, j: (b, jnp.minimum((j + 1) * hb, nb * hb - 1), 5)),
            pl.BlockSpec((None, tb, d), lambda b, j: (b, j, 0)),
            pl.BlockSpec((None, 3, d), lambda b, j: (b, 0, 0)),
            const(hgn), const(pool_w), const(pool_scale), const(out_w),
        ],
        out_specs=pl.BlockSpec((None, tb, d), lambda b, j: (b, j, 0)),
        out_shape=jax.ShapeDtypeStruct((bsz, r, d), F32),
        scratch_shapes=[pltpu.VMEM((tb + 2 * POOL_HALO, w), F32), pltpu.VMEM((tb, 2 * w), BF16)],
        compiler_params=_params("parallel", "parallel"),
        name="even_post",
    )(o_f, o_b, p, p, p, p, p, h, mod, hgn, pool_w, pool_scale, out_w)


def _mla_kv_kernel(*refs, rope):
    if rope:
        ckv_ref, kr_ref, g_ref, wuk_ref, wuvt_ref, cos_ref, sin_ref, kcat_ref, vt_ref = refs
    else:
        ckv_ref, kr_ref, g_ref, wuk_ref, wuvt_ref, kcat_ref, vt_ref = refs
    cn = _rms(ckv_ref[...], g_ref[...]).astype(BF16)
    kn = _dot(cn, wuk_ref[...])
    kr = kr_ref[...]
    if rope:
        lane = lax.broadcasted_iota(jnp.int32, kr.shape, 1)
        swapped = jnp.where((lane % (2 * ROPE_FREQ)) < ROPE_FREQ,
                            pltpu.roll(kr, LANES - ROPE_FREQ, 1), pltpu.roll(kr, ROPE_FREQ, 1))
        kr = kr * cos_ref[...] + swapped * sin_ref[...]
    kr = kr[:, 0:MLA_ROPE].astype(BF16)
    for h in range(MLA_HEADS):
        kcat_ref[h, :, 0:MLA_NOPE] = kn[:, h * MLA_NOPE:(h + 1) * MLA_NOPE].astype(BF16)
        kcat_ref[h, :, MLA_NOPE:MLA_QK] = kr
        vt_ref[h] = _dot_nt(wuvt_ref[h], cn).astype(BF16)


def _mla_kv(p, ckv_blk, kr_blk, g, wuk, wuvt, tables):
    bsz, r, _ = p.shape
    rank = g.shape[-1]
    tb = KV_CHUNK
    nb = r // tb
    const = lambda arr: pl.BlockSpec(arr.shape, lambda b, j: (0,) * arr.ndim)
    in_specs = [
        pl.BlockSpec((None, tb, rank), lambda b, j: (b, j, ckv_blk)),
        pl.BlockSpec((None, tb, LANES), lambda b, j: (b, j, kr_blk)),
        const(g), const(wuk), const(wuvt),
    ]
    args = [p, p, g, wuk, wuvt]
    if tables is not None:
        in_specs += [pl.BlockSpec((tb, LANES), lambda b, j: (j, 0))] * 2
        args += list(tables)
    return pl.pallas_call(
        functools.partial(_mla_kv_kernel, rope=tables is not None),
        grid=(bsz, nb),
        in_specs=in_specs,
        out_specs=[
            pl.BlockSpec((None, MLA_HEADS, tb, MLA_QK), lambda b, j: (b, 0, j, 0)),
            pl.BlockSpec((None, MLA_HEADS, None, MLA_V, tb), lambda b, j: (b, 0, j, 0, 0)),
        ],
        out_shape=[
            jax.ShapeDtypeStruct((bsz, MLA_HEADS, r, MLA_QK), BF16),
            jax.ShapeDtypeStruct((bsz, MLA_HEADS, nb, MLA_V, tb), BF16),
        ],
        compiler_params=_params("parallel", "parallel"),
        name="mla_kv_rope" if tables is not None else "mla_kv",
    )(*args)


def _mla_q_kernel(cq_ref, g_ref, wqt_ref, cos_ref, sin_ref, qt_ref):
    cn = _rms(cq_ref[...], g_ref[...]).astype(BF16)
    f = ROPE_FREQ
    for h in range(MLA_HEADS):
        qt = _dot_nt(wqt_ref[h], cn) * (MLA_SCALE * LOG2_E)
        qt_ref[h, 0:MLA_NOPE, :] = qt[0:MLA_NOPE].astype(BF16)
        for ax in range(2):
            r0 = MLA_NOPE + ax * 2 * f
            x1 = qt[r0:r0 + f]
            x2 = qt[r0 + f:r0 + 2 * f]
            co = cos_ref[ax]
            si = sin_ref[ax]
            qt_ref[h, r0:r0 + f, :] = (x1 * co - x2 * si).astype(BF16)
            qt_ref[h, r0 + f:r0 + 2 * f, :] = (x2 * co + x1 * si).astype(BF16)


def _mla_q(p, cq_blk, g, wqt, cos_t, sin_t):
    bsz, t, _ = p.shape
    rank = g.shape[-1]
    tm = min(TOK_TILE, t)
    const = lambda arr: pl.BlockSpec(arr.shape, lambda b, j: (0,) * arr.ndim)
    tab = pl.BlockSpec((2, ROPE_FREQ, tm), lambda b, j: (0, 0, j))
    return pl.pallas_call(
        _mla_q_kernel,
        grid=(bsz, t // tm),
        in_specs=[pl.BlockSpec((None, tm, rank), lambda b, j: (b, j, cq_blk)), const(g), const(wqt), tab, tab],
        out_specs=pl.BlockSpec((None, MLA_HEADS, MLA_QK, tm), lambda b, j: (b, 0, 0, j)),
        out_shape=jax.ShapeDtypeStruct((bsz, MLA_HEADS, MLA_QK, t), BF16),
        compiler_params=_params("parallel", "parallel"),
        name="mla_q",
    )(p, g, wqt, cos_t, sin_t)


def _attn_kernel(qt_ref, kc_ref, vtc_ref, kl_ref, vtl_ref, g_ref, o_ref, m_ref, l_ref, acc_ref, s_ref, *, tq, n_lat):
    nsub = tq // Q_SUB
    m_ref[...] = jnp.full(m_ref.shape, -jnp.inf, F32)
    l_ref[...] = jnp.zeros(l_ref.shape, F32)
    acc_ref[...] = jnp.zeros(acc_ref.shape, F32)

    def substep(k_next, vt_cur, cur, nxt):
        for g in range(nsub):
            sl = slice(g * Q_SUB, (g + 1) * Q_SUB)
            s_ref[nxt, g] = _dot(k_next, qt_ref[:, sl])
            s = s_ref[cur, g]
            m_old = m_ref[:, sl]
            m_new = jnp.maximum(m_old, jnp.max(s, axis=0, keepdims=True))
            alpha = jnp.exp2(m_old - m_new)
            p = jnp.exp2(s - m_new)
            l_ref[:, sl] = alpha * l_ref[:, sl] + jnp.sum(p, axis=0, keepdims=True)
            acc_ref[:, sl] = alpha * acc_ref[:, sl] + _dot(vt_cur, p.astype(BF16))
            m_ref[:, sl] = m_new

    kc = kc_ref[...]
    for g in range(nsub):
        s_ref[0, g] = _dot(kc, qt_ref[:, g * Q_SUB:(g + 1) * Q_SUB])
    substep(kl_ref[0:KV_CHUNK, :], vtc_ref[0], 0, 1)

    def body(j, carry):
        a = 2 * j
        r1 = pl.multiple_of((a + 1) * KV_CHUNK, KV_CHUNK)
        substep(kl_ref[pl.ds(r1, KV_CHUNK), :], vtl_ref[a], 1, 0)
        r2 = pl.multiple_of(jnp.minimum(a + 2, n_lat - 1) * KV_CHUNK, KV_CHUNK)
        substep(kl_ref[pl.ds(r2, KV_CHUNK), :], vtl_ref[a + 1], 0, 1)
        return carry

    lax.fori_loop(0, n_lat // 2, body, 0)
    o = (acc_ref[...] * (1.0 / l_ref[...])).T
    o_ref[...] = (o * _silu(g_ref[...])).astype(o_ref.dtype)


def _attn(qt, kc, vtc, kl, vtl, p):
    bsz, nh, _, t = qt.shape
    lc = kc.shape[2]
    tq = min(Q_TILE, t)
    n_lat = t // KV_CHUNK
    assert lc == KV_CHUNK and n_lat % 2 == 0
    kern = functools.partial(_attn_kernel, tq=tq, n_lat=n_lat)
    full4 = lambda arr: pl.BlockSpec((None, None) + arr.shape[2:], lambda b, h, i: (b, h, 0, 0))
    full5 = lambda arr: pl.BlockSpec((None, None) + arr.shape[2:], lambda b, h, i: (b, h, 0, 0, 0))
    return pl.pallas_call(
        kern,
        grid=(bsz, nh, t // tq),
        in_specs=[
            pl.BlockSpec((None, None, MLA_QK, tq), lambda b, h, i: (b, h, 0, i)),
            full4(kc), full5(vtc), full4(kl), full5(vtl),
            pl.BlockSpec((None, tq, MLA_V), lambda b, h, i: (b, i, h)),
        ],
        out_specs=pl.BlockSpec((None, tq, MLA_V), lambda b, h, i: (b, i, h)),
        out_shape=jax.ShapeDtypeStruct((bsz, t, nh * MLA_V), BF16),
        scratch_shapes=[pltpu.VMEM((1, tq), F32), pltpu.VMEM((1, tq), F32), pltpu.VMEM((MLA_V, tq), F32),
                        pltpu.VMEM((2, tq // Q_SUB, KV_CHUNK, Q_SUB), F32)],
        compiler_params=_params("parallel", "parallel", "arbitrary"),
        name="mla_attention",
    )(qt, kc, vtc, kl, vtl, p)


def _out_final_kernel(y_ref, h_ref, mod_ref, ow_ref, g_ref, o_ref):
    hn = h_ref[...] + mod_ref[2:3, :] * _dot(y_ref[...], ow_ref[...])
    o_ref[...] = _rms(hn, g_ref[...])


def _out_final(y, h, mod, out_w, g):
    bsz, t, d = h.shape
    wi = y.shape[-1]
    tm = min(2 * TOK_TILE, t)
    return pl.pallas_call(
        _out_final_kernel,
        grid=(bsz, t // tm),
        in_specs=[
            pl.BlockSpec((None, tm, wi), lambda b, j: (b, j, 0)),
            pl.BlockSpec((None, tm, d), lambda b, j: (b, j, 0)),
            pl.BlockSpec((None, 3, d), lambda b, j: (b, 0, 0)),
            pl.BlockSpec((wi, d), lambda b, j: (0, 0)),
            pl.BlockSpec((1, d), lambda b, j: (0, 0)),
        ],
        out_specs=pl.BlockSpec((None, tm, d), lambda b, j: (b, j, 0)),
        out_shape=jax.ShapeDtypeStruct((bsz, t, d), F32),
        compiler_params=_params("parallel", "parallel"),
        name="out_final",
    )(y, h, mod, out_w, g)


def _rope_tables(n_tokens):
    rows = n_tokens // GRID_W
    pos_r = jnp.repeat(jnp.arange(rows), GRID_W).astype(F32)
    pos_c = jnp.tile(jnp.arange(GRID_W), rows).astype(F32)
    inv = ROPE_BASE ** (-2.0 * jnp.arange(ROPE_FREQ, dtype=F32) / (MLA_ROPE // 2))
    ang = jnp.stack([pos_r[:, None] * inv, pos_c[:, None] * inv], axis=1)
    cos, sin = jnp.cos(ang), jnp.sin(ang)
    pad = LANES - MLA_ROPE
    cos_k = jnp.pad(jnp.stack([cos, cos], axis=2).reshape(n_tokens, MLA_ROPE), ((0, 0), (0, pad)))
    sin_k = jnp.pad(jnp.stack([-sin, sin], axis=2).reshape(n_tokens, MLA_ROPE), ((0, 0), (0, pad)))
    cos_q = jnp.transpose(cos, (1, 2, 0))
    sin_q = jnp.transpose(sin, (1, 2, 0))
    return (cos_k, sin_k), (cos_q, sin_q)


def kernel(x, c, ctx, c_ctx, ada_w, ada_b, norm_g, out_w, ev_in_w, hg_lb, hg_norm_g, pool_w, pool_scale,
           od_in_w, qa_norm_g, qb_w, kva_norm_g, kvb_w, final_norm_g):
    bsz, t, d = x.shape
    lc = ctx.shape[1]
    depth = ada_w.shape[0]
    assert depth == 2 and t % (2 * TOK_TILE) == 0 and lc % TOK_TILE == 0 and t % GRID_W == 0
    w = hg_norm_g.shape[-1]
    nh = w // HG_DK
    q_rank = qa_norm_g.shape[-1]
    kv_rank = kva_norm_g.shape[-1]
    d_inner = out_w.shape[1]

    n_cond = -(-(bsz + 1) // SUBLANES) * SUBLANES
    cond = jnp.zeros((n_cond, d), F32).at[:bsz].set(c).at[bsz].set(c_ctx)
    mods = _ada(cond, ada_w, ada_b).reshape(depth, n_cond, 3, d)
    mod_l = [mods[l, :bsz] for l in range(depth)]
    mod_c = [mods[l, bsz:bsz + 1] for l in range(depth)]

    lb = jnp.cumsum(jax.nn.softmax(hg_lb.astype(F32), axis=1), axis=1)[:, 0].reshape(2, 1, w)
    w_in0 = ev_in_w[0].astype(BF16)
    g0 = norm_g[0].reshape(1, d)
    ctx_flat = ctx.reshape(1, bsz * lc, d)
    n_in0 = w_in0.shape[1]
    p_c = _modnorm_mm(ctx_flat, mod_c[0], g0, w_in0, 4 * TOK_TILE, n_in0 // 4, "in_proj0_ctx").reshape(bsz, lc, n_in0)
    p_l = _modnorm_mm(x, mod_l[0], g0, w_in0, 4 * TOK_TILE, n_in0 // 4, "in_proj0")
    consts = _hgrn_constants(HG_CHUNK)
    s0 = jnp.zeros((bsz, 2, nh, HG_DK, HG_DK), F32)
    of_c, ob_c, s_c = _hgrn(p_c, lb, s0, consts)
    of_l, ob_l, _ = _hgrn(p_l, lb, s_c, consts)
    hgn = hg_norm_g[0].reshape(1, w)
    pw = pool_w[0].astype(BF16)
    ps = pool_scale[0].reshape(1, w)
    ow0 = out_w[0].astype(BF16)
    mod_c0 = jnp.broadcast_to(mod_c[0], (bsz, 3, d))
    hc1 = _even_post(of_c, ob_c, p_c, ctx, mod_c0, hgn, pw, ps, ow0)
    hl1 = _even_post(of_l, ob_l, p_l, x, mod_l[0], hgn, pw, ps, ow0)

    o1 = q_rank
    o2 = o1 + kv_rank
    o3 = o2 + MLA_ROPE
    w1 = od_in_w[0]
    kr_pad = jnp.zeros((d, LANES - MLA_ROPE), F32)
    w_in1 = jnp.concatenate([w1[:, o3:], w1[:, :o1], w1[:, o1:o2], w1[:, o2:o3], kr_pad], axis=1).astype(BF16)
    n_kv = kv_rank + LANES
    w_in1c = w_in1[:, d_inner + q_rank:]
    g1 = norm_g[1].reshape(1, d)
    p1_c = _modnorm_mm(hc1.reshape(1, bsz * lc, d), mod_c[1], g1, w_in1c, 4 * TOK_TILE, n_kv, "in_proj1_ctx")
    p1_c = p1_c.reshape(bsz, lc, n_kv)
    p1_l = _modnorm_mm(hl1, mod_l[1], g1, w_in1, 2 * TOK_TILE, w_in1.shape[1], "in_proj1")

    kvw = kvb_w[0].reshape(kv_rank, MLA_HEADS, MLA_NOPE + MLA_V)
    wuk = kvw[..., :MLA_NOPE].reshape(kv_rank, MLA_HEADS * MLA_NOPE).astype(BF16)
    wuvt = jnp.transpose(kvw[..., MLA_NOPE:], (1, 2, 0)).astype(BF16)
    wqt = jnp.transpose(qb_w[0].reshape(q_rank, MLA_HEADS, MLA_QK), (1, 2, 0)).astype(BF16)
    kvg = kva_norm_g[0].reshape(1, kv_rank)
    qag = qa_norm_g[0].reshape(1, q_rank)
    tab_k, tab_q = _rope_tables(t)
    kc, vtc = _mla_kv(p1_c, 0, kv_rank // LANES, kvg, wuk, wuvt, None)
    kl, vtl = _mla_kv(p1_l, (d_inner + q_rank) // kv_rank, (d_inner + q_rank + kv_rank) // LANES, kvg, wuk, wuvt,
                      tab_k)
    qt = _mla_q(p1_l, d_inner // q_rank, qag, wqt, *tab_q)
    y = _attn(qt, kc, vtc, kl, vtl, p1_l)
    return _out_final(y, hl1, mod_l[1], out_w[1].astype(BF16), final_norm_g.reshape(1, d))
```

```python
import functools

import numpy as np
import jax
import jax.numpy as jnp
from jax import lax
from jax.experimental import pallas as pl
from jax.experimental.pallas import tpu as pltpu

F32 = jnp.float32
BF16 = jnp.bfloat16

EPS = 1e-6
GRID_W = 64
HG_DK = 128
POOL_WINDOWS = (2, 4, 8, 16)
MLA_HEADS = 16
MLA_NOPE = 128
MLA_ROPE = 64
MLA_V = 128
MLA_QK = MLA_NOPE + MLA_ROPE
MLA_SCALE = MLA_QK ** -0.5
LOG2_E = 1.4426950408889634
ROPE_FREQ = MLA_ROPE // 4
ROPE_BASE = 10000.0

LANES = 128
SUBLANES = 8
VMEM_LIMIT = 48 * 1024 * 1024

HG_CHUNK = 64
TOK_TILE = 256
KV_CHUNK = 256
Q_TILE = 2048
Q_SUB = 256
POOL_HALO = 8


def _dot(a, b):
    return jnp.dot(a, b, preferred_element_type=F32)


def _dot_nt(a, b):
    return lax.dot_general(a, b, (((1,), (1,)), ((), ())), preferred_element_type=F32)


def _dot_tn(a, b):
    return lax.dot_general(a, b, (((0,), (0,)), ((), ())), preferred_element_type=F32)


def _silu(x):
    return x * jax.nn.sigmoid(x)


def _split_bf16(x):
    hi = x.astype(BF16)
    lo = (x - hi.astype(F32)).astype(BF16)
    return hi, lo


def _params(*sem):
    return pltpu.CompilerParams(dimension_semantics=sem, vmem_limit_bytes=VMEM_LIMIT)


def _ada_kernel(c_ref, w_ref, b_ref, o_ref):
    c = c_ref[...]
    s_hi, s_lo = _split_bf16(_silu(c))
    w_hi, w_lo = _split_bf16(w_ref[...])
    o_ref[...] = _dot(s_hi, w_hi) + _dot(s_lo, w_hi) + _dot(s_hi, w_lo) + b_ref[...]


def _ada(cond, ada_w, ada_b):
    depth, d, _ = ada_w.shape
    r = cond.shape[0]
    return pl.pallas_call(
        _ada_kernel,
        grid=(depth, 3),
        in_specs=[
            pl.BlockSpec((r, d), lambda l, j: (0, 0)),
            pl.BlockSpec((None, d, d), lambda l, j: (l, 0, j)),
            pl.BlockSpec((None, 1, d), lambda l, j: (l, 0, j)),
        ],
        out_specs=pl.BlockSpec((None, r, d), lambda l, j: (l, 0, j)),
        out_shape=jax.ShapeDtypeStruct((depth, r, 3 * d), F32),
        compiler_params=_params("parallel", "parallel"),
        name="ada_modulation",
    )(cond, ada_w, ada_b.reshape(depth, 1, 3 * d))


def _rms(x, g):
    return x * lax.rsqrt(jnp.mean(x * x, axis=-1, keepdims=True) + EPS) * g


def _modnorm_mm_kernel(x_ref, mod_ref, g_ref, w_ref, o_ref, z_ref):
    @pl.when(pl.program_id(2) == 0)
    def _():
        y = _rms(x_ref[...], g_ref[...])
        z_ref[...] = (y * (1.0 + mod_ref[1:2, :]) + mod_ref[0:1, :]).astype(BF16)

    o_ref[...] = _dot(z_ref[...], w_ref[...]).astype(o_ref.dtype)


def _modnorm_mm(x, mod, g, w, tm, tn, name):
    bx, r, d = x.shape
    n = w.shape[1]
    tm = min(tm, r)
    return pl.pallas_call(
        _modnorm_mm_kernel,
        grid=(bx, r // tm, n // tn),
        in_specs=[
            pl.BlockSpec((None, tm, d), lambda b, i, j: (b, i, 0)),
            pl.BlockSpec((None, 3, d), lambda b, i, j: (b, 0, 0)),
            pl.BlockSpec((1, d), lambda b, i, j: (0, 0)),
            pl.BlockSpec((d, tn), lambda b, i, j: (0, j)),
        ],
        out_specs=pl.BlockSpec((None, tm, tn), lambda b, i, j: (b, i, j)),
        out_shape=jax.ShapeDtypeStruct((bx, r, n), F32),
        scratch_shapes=[pltpu.VMEM((tm, d), BF16)],
        compiler_params=_params("parallel", "parallel", "arbitrary"),
        name=name,
    )(x, mod, g, w)


def _hgrn_levels(c):
    w = c // 2
    out = []
    while w >= 1:
        out.append(w)
        w //= 2
    return tuple(out)


def _hgrn_constants(c):
    t = np.arange(c)
    u = t[None, :]
    tt = t[:, None]
    a = [np.tril(np.ones((c, c), np.float32))]
    masks, isk = [], []
    for w in _hgrn_levels(c):
        blk = t // (2 * w)
        first = (t % (2 * w)) < w
        ref = (blk * 2 * w + w - 1)[:, None]
        a.append(np.where(first[:, None], (u > tt) & (u <= ref), (u > ref) & (u <= tt)).astype(np.float32))
        masks.append(((blk[:, None] == blk[None, :]) & (~first[:, None]) & first[None, :]).astype(np.float32))
        isk.append(first.astype(np.float32)[:, None])
    a = np.stack(a)
    masks = np.stack(masks)
    isk = np.stack(isk)
    flip = lambda m: m[:, ::-1, ::-1]
    a2 = np.stack([a, flip(a)]).reshape(2, -1, c)
    m2 = np.stack([masks, flip(masks)])
    k2 = np.stack([isk, isk[:, ::-1]])
    return jnp.asarray(a2, BF16), jnp.asarray(m2, F32), jnp.asarray(k2, F32)


def _hgrn_chunk(q_ref, f_ref, v_ref, o_ref, lb, a, mk_ref, isk_ref, st_ref, d, r0, *, c, nh):
    nlev = len(_hgrn_levels(c))
    last = c - 1 if d == 0 else 0
    rows = pl.ds(r0, c)
    head = lambda x, h: x[:, h * HG_DK:(h + 1) * HG_DK]
    f = lb + (1.0 - lb) * jax.nn.sigmoid(f_ref[rows, :])
    g_hi, g_lo = _split_bf16(jnp.log(f))
    dd = _dot(a, g_hi) + _dot(a, g_lo)
    q = _silu(q_ref[rows, :])
    k = 1.0 - f
    v = v_ref[rows, :]
    vb = v.astype(BF16)
    b = dd[0:c]
    bl = b[last:last + 1, :]
    qe = (q * jnp.exp(b)).astype(BF16)
    kend = (k * jnp.exp(bl - b)).astype(BF16)
    ebl = jnp.exp(bl)
    st = [st_ref[d, h] for h in range(nh)]
    inter = [_dot_nt(head(qe, h), st[h].astype(BF16)) for h in range(nh)]
    att = [None] * nh
    for l in range(nlev):
        e = jnp.exp(dd[(l + 1) * c:(l + 2) * c])
        x = (e * jnp.where(isk_ref[d, l] > 0.0, k, q)).astype(BF16)
        for h in range(nh):
            t = mk_ref[d, l] * _dot_nt(head(x, h), head(x, h))
            att[h] = t if att[h] is None else att[h] + t
    qk = q * k
    for h in range(nh):
        diag = jnp.sum(head(qk, h), axis=-1, keepdims=True)
        o_ref[rows, h * HG_DK:(h + 1) * HG_DK] = (
            inter[h] + _dot(att[h].astype(BF16), head(vb, h)) + diag * head(v, h))
    for h in range(nh):
        st_ref[d, h] = head(ebl, h) * st[h] + _dot_tn(head(vb, h), head(kend, h))


def _hgrn_kernel(qf_ref, ff_ref, vf_ref, qb_ref, fb_ref, vb_ref, lb_ref, a_ref, mk_ref, isk_ref, s0_ref,
                 of_ref, ob_ref, sout_ref, st_ref, *, tb, c, nh):
    nchunk = tb // c

    @pl.when(pl.program_id(1) == 0)
    def _():
        st_ref[...] = s0_ref[...]

    def body(cc, carry):
        rf = pl.multiple_of(cc * c, c)
        rb = pl.multiple_of((nchunk - 1 - cc) * c, c)
        _hgrn_chunk(qf_ref, ff_ref, vf_ref, of_ref, lb_ref[0], a_ref[0], mk_ref, isk_ref, st_ref, 0, rf, c=c, nh=nh)
        _hgrn_chunk(qb_ref, fb_ref, vb_ref, ob_ref, lb_ref[1], a_ref[1], mk_ref, isk_ref, st_ref, 1, rb, c=c, nh=nh)
        return carry

    lax.fori_loop(0, nchunk, body, 0)

    @pl.when(pl.program_id(1) == pl.num_programs(1) - 1)
    def _():
        sout_ref[...] = st_ref[...]


def _hgrn(p, lb, s0, consts):
    bsz, r, _ = p.shape
    w = lb.shape[-1]
    nh = w // HG_DK
    tb = min(TOK_TILE, r)
    c = HG_CHUNK
    nb = r // tb
    a2, m2, k2 = consts
    nrow = a2.shape[1]
    fwd = lambda col: pl.BlockSpec((None, tb, w), lambda b, s: (b, s, col))
    bwd = lambda col: pl.BlockSpec((None, tb, w), lambda b, s: (b, nb - 1 - s, col))
    const = lambda arr: pl.BlockSpec(arr.shape, lambda b, s: (0,) * arr.ndim)
    st_spec = pl.BlockSpec((None, 2, nh, HG_DK, HG_DK), lambda b, s: (b, 0, 0, 0, 0))
    kern = functools.partial(_hgrn_kernel, tb=tb, c=c, nh=nh)
    return pl.pallas_call(
        kern,
        grid=(bsz, nb),
        in_specs=[fwd(0), fwd(1), fwd(3), bwd(0), bwd(2), bwd(3), const(lb), const(a2), const(m2), const(k2),
                  st_spec],
        out_specs=[
            pl.BlockSpec((None, tb, w), lambda b, s: (b, s, 0)),
            pl.BlockSpec((None, tb, w), lambda b, s: (b, nb - 1 - s, 0)),
            st_spec,
        ],
        out_shape=[
            jax.ShapeDtypeStruct((bsz, r, w), F32),
            jax.ShapeDtypeStruct((bsz, r, w), F32),
            jax.ShapeDtypeStruct(s0.shape, F32),
        ],
        scratch_shapes=[pltpu.VMEM((2, nh, HG_DK, HG_DK), F32)],
        compiler_params=_params("parallel", "arbitrary"),
        name="hgrn2_scan",
    )(p, p, p, p, p, p, lb, a2, m2, k2, s0)


def _even_post_kernel(of_ref, ob_ref, ga_ref, u_ref, gb_ref, up_ref, un_ref, h_ref, mod_ref, hgn_ref, pw_ref,
                      ps_ref, ow_ref, o_ref, ext_ref, y_ref, *, tb, seq, nh):
    j = pl.program_id(1)
    w = nh * HG_DK
    o = of_ref[...] + ob_ref[...]
    for h in range(nh):
        sl = slice(h * HG_DK, (h + 1) * HG_DK)
        y_ref[:, sl] = (_rms(o[:, sl], hgn_ref[:, sl]) * _silu(ga_ref[:, sl])).astype(BF16)
    u = u_ref[...]
    ext_ref[0:POOL_HALO, :] = jnp.where(j > 0, up_ref[...], 0.0)
    ext_ref[POOL_HALO:POOL_HALO + tb, :] = u
    ext_ref[POOL_HALO + tb:, :] = jnp.where(j < pl.num_programs(1) - 1, un_ref[...], 0.0)
    t = j * tb + lax.broadcasted_iota(jnp.int32, (tb, 1), 0)
    grp = w // len(POOL_WINDOWS)
    for gi, win in enumerate(POOL_WINDOWS):
        sl = slice(gi * grp, (gi + 1) * grp)
        acc = ext_ref[POOL_HALO - win // 2:POOL_HALO - win // 2 + tb, sl]
        for off in range(-win // 2 + 1, win // 2):
            acc = acc + ext_ref[POOL_HALO + off:POOL_HALO + off + tb, sl]
        cnt = (jnp.minimum(t + win // 2, seq) - jnp.maximum(t - win // 2, 0)).astype(F32)
        yp = acc / cnt - u[:, sl]
        yb = _dot(yp.astype(BF16), pw_ref[gi]) * ps_ref[:, sl]
        y_ref[:, w + gi * grp:w + (gi + 1) * grp] = (yb * _silu(gb_ref[:, sl])).astype(BF16)
    o_ref[...] = h_ref[...] + mod_ref[2:3, :] * _dot(y_ref[...], ow_ref[...])


def _even_post(o_f, o_b, p, h, mod, hgn, pool_w, pool_scale, out_w):
    bsz, r, w = o_f.shape
    d = h.shape[-1]
    tb = min(TOK_TILE, r)
    nb = r // tb
    hb = tb // POOL_HALO
    nh = w // HG_DK
    tok = lambda col: pl.BlockSpec((None, tb, w), lambda b, j: (b, j, col))
    const = lambda arr: pl.BlockSpec(arr.shape, lambda b, j: (0,) * arr.ndim)
    kern = functools.partial(_even_post_kernel, tb=tb, seq=r, nh=nh)
    return pl.pallas_call(
        kern,
        grid=(bsz, nb),
        in_specs=[
            tok(0), tok(0), tok(4), tok(5), tok(6),
            pl.BlockSpec((None, POOL_HALO, w), lambda b, j: (b, jnp.maximum(j * hb - 1, 0), 5)),
            pl.BlockSpec((None, POOL_HALO, w), lambda b, j: (b, jnp.minimum((j + 1) * hb, nb * hb - 1), 5)),
            pl.BlockSpec((None, tb, d), lambda b, j: (b, j, 0)),
            pl.BlockSpec((None, 3, d), lambda b, j: (b, 0, 0)),
            const(hgn), const(pool_w), const(pool_scale), const(out_w),
        ],
        out_specs=pl.BlockSpec((None, tb, d), lambda b, j: (b, j, 0)),
        out_shape=jax.ShapeDtypeStruct((bsz, r, d), F32),
        scratch_shapes=[pltpu.VMEM((tb + 2 * POOL_HALO, w), F32), pltpu.VMEM((tb, 2 * w), BF16)],
        compiler_params=_params("parallel", "parallel"),
        name="even_post",
    )(o_f, o_b, p, p, p, p, p, h, mod, hgn, pool_w, pool_scale, out_w)


def _mla_kv_kernel(*refs, rope):
    if rope:
        ckv_ref, kr_ref, g_ref, wuk_ref, wuvt_ref, cos_ref, sin_ref, kcat_ref, vt_ref = refs
    else:
        ckv_ref, kr_ref, g_ref, wuk_ref, wuvt_ref, kcat_ref, vt_ref = refs
    cn = _rms(ckv_ref[...], g_ref[...]).astype(BF16)
    kn = _dot(cn, wuk_ref[...])
    kr = kr_ref[...]
    if rope:
        lane = lax.broadcasted_iota(jnp.int32, kr.shape, 1)
        swapped = jnp.where((lane % (2 * ROPE_FREQ)) < ROPE_FREQ,
                            pltpu.roll(kr, LANES - ROPE_FREQ, 1), pltpu.roll(kr, ROPE_FREQ, 1))
        kr = kr * cos_ref[...] + swapped * sin_ref[...]
    kr = kr[:, 0:MLA_ROPE].astype(BF16)
    for h in range(MLA_HEADS):
        kcat_ref[h, :, 0:MLA_NOPE] = kn[:, h * MLA_NOPE:(h + 1) * MLA_NOPE].astype(BF16)
        kcat_ref[h, :, MLA_NOPE:MLA_QK] = kr
        vt_ref[h] = _dot_nt(wuvt_ref[h], cn).astype(BF16)


def _mla_kv(p, ckv_blk, kr_blk, g, wuk, wuvt, tables):
    bsz, r, _ = p.shape
    rank = g.shape[-1]
    tb = KV_CHUNK
    nb = r // tb
    const = lambda arr: pl.BlockSpec(arr.shape, lambda b, j: (0,) * arr.ndim)
    in_specs = [
        pl.BlockSpec((None, tb, rank), lambda b, j: (b, j, ckv_blk)),
        pl.BlockSpec((None, tb, LANES), lambda b, j: (b, j, kr_blk)),
        const(g), const(wuk), const(wuvt),
    ]
    args = [p, p, g, wuk, wuvt]
    if tables is not None:
        in_specs += [pl.BlockSpec((tb, LANES), lambda b, j: (j, 0))] * 2
        args += list(tables)
    return pl.pallas_call(
        functools.partial(_mla_kv_kernel, rope=tables is not None),
        grid=(bsz, nb),
        in_specs=in_specs,
        out_specs=[
            pl.BlockSpec((None, MLA_HEADS, tb, MLA_QK), lambda b, j: (b, 0, j, 0)),
            pl.BlockSpec((None, MLA_HEADS, None, MLA_V, tb), lambda b, j: (b, 0, j, 0, 0)),
        ],
        out_shape=[
            jax.ShapeDtypeStruct((bsz, MLA_HEADS, r, MLA_QK), BF16),
            jax.ShapeDtypeStruct((bsz, MLA_HEADS, nb, MLA_V, tb), BF16),
        ],
        compiler_params=_params("parallel", "parallel"),
        name="mla_kv_rope" if tables is not None else "mla_kv",
    )(*args)


def _mla_q_kernel(cq_ref, g_ref, wqt_ref, cos_ref, sin_ref, qt_ref):
    cn = _rms(cq_ref[...], g_ref[...]).astype(BF16)
    f = ROPE_FREQ
    for h in range(MLA_HEADS):
        qt = _dot_nt(wqt_ref[h], cn) * (MLA_SCALE * LOG2_E)
        qt_ref[h, 0:MLA_NOPE, :] = qt[0:MLA_NOPE].astype(BF16)
        for ax in range(2):
            r0 = MLA_NOPE + ax * 2 * f
            x1 = qt[r0:r0 + f]
            x2 = qt[r0 + f:r0 + 2 * f]
            co = cos_ref[ax]
            si = sin_ref[ax]
            qt_ref[h, r0:r0 + f, :] = (x1 * co - x2 * si).astype(BF16)
            qt_ref[h, r0 + f:r0 + 2 * f, :] = (x2 * co + x1 * si).astype(BF16)


def _mla_q(p, cq_blk, g, wqt, cos_t, sin_t):
    bsz, t, _ = p.shape
    rank = g.shape[-1]
    tm = min(TOK_TILE, t)
    const = lambda arr: pl.BlockSpec(arr.shape, lambda b, j: (0,) * arr.ndim)
    tab = pl.BlockSpec((2, ROPE_FREQ, tm), lambda b, j: (0, 0, j))
    return pl.pallas_call(
        _mla_q_kernel,
        grid=(bsz, t // tm),
        in_specs=[pl.BlockSpec((None, tm, rank), lambda b, j: (b, j, cq_blk)), const(g), const(wqt), tab, tab],
        out_specs=pl.BlockSpec((None, MLA_HEADS, MLA_QK, tm), lambda b, j: (b, 0, 0, j)),
        out_shape=jax.ShapeDtypeStruct((bsz, MLA_HEADS, MLA_QK, t), BF16),
        compiler_params=_params("parallel", "parallel"),
        name="mla_q",
    )(p, g, wqt, cos_t, sin_t)


def _attn_kernel(qt_ref, kc_ref, vtc_ref, kl_ref, vtl_ref, g_ref, o_ref, m_ref, l_ref, acc_ref, s_ref, *, tq, n_lat):
    nsub = tq // Q_SUB
    m_ref[...] = jnp.full(m_ref.shape, -jnp.inf, F32)
    l_ref[...] = jnp.zeros(l_ref.shape, F32)
    acc_ref[...] = jnp.zeros(acc_ref.shape, F32)

    def substep(k_next, vt_cur, cur, nxt):
        for g in range(nsub):
            sl = slice(g * Q_SUB, (g + 1) * Q_SUB)
            s_ref[nxt, g] = _dot(k_next, qt_ref[:, sl])
            s = s_ref[cur, g]
            m_old = m_ref[:, sl]
            m_new = jnp.maximum(m_old, jnp.max(s, axis=0, keepdims=True))
            alpha = jnp.exp2(m_old - m_new)
            p = jnp.exp2(s - m_new)
            l_ref[:, sl] = alpha * l_ref[:, sl] + jnp.sum(p, axis=0, keepdims=True)
            acc_ref[:, sl] = alpha * acc_ref[:, sl] + _dot(vt_cur, p.astype(BF16))
            m_ref[:, sl] = m_new

    kc = kc_ref[...]
    for g in range(nsub):
        s_ref[0, g] = _dot(kc, qt_ref[:, g * Q_SUB:(g + 1) * Q_SUB])
    substep(kl_ref[0:KV_CHUNK, :], vtc_ref[0], 0, 1)

    def body(j, carry):
        a = 2 * j
        r1 = pl.multiple_of((a + 1) * KV_CHUNK, KV_CHUNK)
        substep(kl_ref[pl.ds(r1, KV_CHUNK), :], vtl_ref[a], 1, 0)
        r2 = pl.multiple_of(jnp.minimum(a + 2, n_lat - 1) * KV_CHUNK, KV_CHUNK)
        substep(kl_ref[pl.ds(r2, KV_CHUNK), :], vtl_ref[a + 1], 0, 1)
        return carry

    lax.fori_loop(0, n_lat // 2, body, 0)
    o = (acc_ref[...] * (1.0 / l_ref[...])).T
    o_ref[...] = (o * _silu(g_ref[...])).astype(o_ref.dtype)


def _attn(qt, kc, vtc, kl, vtl, p):
    bsz, nh, _, t = qt.shape
    lc = kc.shape[2]
    tq = min(Q_TILE, t)
    n_lat = t // KV_CHUNK
    assert lc == KV_CHUNK and n_lat % 2 == 0
    kern = functools.partial(_attn_kernel, tq=tq, n_lat=n_lat)
    full4 = lambda arr: pl.BlockSpec((None, None) + arr.shape[2:], lambda b, h, i: (b, h, 0, 0))
    full5 = lambda arr: pl.BlockSpec((None, None) + arr.shape[2:], lambda b, h, i: (b, h, 0, 0, 0))
    return pl.pallas_call(
        kern,
        grid=(bsz, nh, t // tq),
        in_specs=[
            pl.BlockSpec((None, None, MLA_QK, tq), lambda b, h, i: (b, h, 0, i)),
            full4(kc), full5(vtc), full4(kl), full5(vtl),
            pl.BlockSpec((None, tq, MLA_V), lambda b, h, i: (b, i, h)),
        ],
        out_specs=pl.BlockSpec((None, tq, MLA_V), lambda b, h, i: (b, i, h)),
        out_shape=jax.ShapeDtypeStruct((bsz, t, nh * MLA_V), BF16),
        scratch_shapes=[pltpu.VMEM((1, tq), F32), pltpu.VMEM((1, tq), F32), pltpu.VMEM((MLA_V, tq), F32),
                        pltpu.VMEM((2, tq // Q_SUB, KV_CHUNK, Q_SUB), F32)],
        compiler_params=_params("parallel", "parallel", "arbitrary"),
        name="mla_attention",
    )(qt, kc, vtc, kl, vtl, p)


def _out_final_kernel(y_ref, h_ref, mod_ref, ow_ref, g_ref, o_ref):
    hn = h_ref[...] + mod_ref[2:3, :] * _dot(y_ref[...], ow_ref[...])
    o_ref[...] = _rms(hn, g_ref[...])


def _out_final(y, h, mod, out_w, g):
    bsz, t, d = h.shape
    wi = y.shape[-1]
    tm = min(2 * TOK_TILE, t)
    return pl.pallas_call(
        _out_final_kernel,
        grid=(bsz, t // tm),
        in_specs=[
            pl.BlockSpec((None, tm, wi), lambda b, j: (b, j, 0)),
            pl.BlockSpec((None, tm, d), lambda b, j: (b, j, 0)),
            pl.BlockSpec((None, 3, d), lambda b, j: (b, 0, 0)),
            pl.BlockSpec((wi, d), lambda b, j: (0, 0)),
            pl.BlockSpec((1, d), lambda b, j: (0, 0)),
        ],
        out_specs=pl.BlockSpec((None, tm, d), lambda b, j: (b, j, 0)),
        out_shape=jax.ShapeDtypeStruct((bsz, t, d), F32),
        compiler_params=_params("parallel", "parallel"),
        name="out_final",
    )(y, h, mod, out_w, g)


def _rope_tables(n_tokens):
    rows = n_tokens // GRID_W
    pos_r = jnp.repeat(jnp.arange(rows), GRID_W).astype(F32)
    pos_c = jnp.tile(jnp.arange(GRID_W), rows).astype(F32)
    inv = ROPE_BASE ** (-2.0 * jnp.arange(ROPE_FREQ, dtype=F32) / (MLA_ROPE // 2))
    ang = jnp.stack([pos_r[:, None] * inv, pos_c[:, None] * inv], axis=1)
    cos, sin = jnp.cos(ang), jnp.sin(ang)
    pad = LANES - MLA_ROPE
    cos_k = jnp.pad(jnp.stack([cos, cos], axis=2).reshape(n_tokens, MLA_ROPE), ((0, 0), (0, pad)))
    sin_k = jnp.pad(jnp.stack([-sin, sin], axis=2).reshape(n_tokens, MLA_ROPE), ((0, 0), (0, pad)))
    cos_q = jnp.transpose(cos, (1, 2, 0))
    sin_q = jnp.transpose(sin, (1, 2, 0))
    return (cos_k, sin_k), (cos_q, sin_q)


def kernel(x, c, ctx, c_ctx, ada_w, ada_b, norm_g, out_w, ev_in_w, hg_lb, hg_norm_g, pool_w, pool_scale,
           od_in_w, qa_norm_g, qb_w, kva_norm_g, kvb_w, final_norm_g):
    bsz, t, d = x.shape
    lc = ctx.shape[1]
    depth = ada_w.shape[0]
    assert depth == 2 and t % (2 * TOK_TILE) == 0 and lc % TOK_TILE == 0 and t % GRID_W == 0
    w = hg_norm_g.shape[-1]
    nh = w // HG_DK
    q_rank = qa_norm_g.shape[-1]
    kv_rank = kva_norm_g.shape[-1]
    d_inner = out_w.shape[1]

    n_cond = -(-(bsz + 1) // SUBLANES) * SUBLANES
    cond = jnp.zeros((n_cond, d), F32).at[:bsz].set(c).at[bsz].set(c_ctx)
    mods = _ada(cond, ada_w, ada_b).reshape(depth, n_cond, 3, d)
    mod_l = [mods[l, :bsz] for l in range(depth)]
    mod_c = [mods[l, bsz:bsz + 1] for l in range(depth)]

    lb = jnp.cumsum(jax.nn.softmax(hg_lb.astype(F32), axis=1), axis=1)[:, 0].reshape(2, 1, w)
    w_in0 = ev_in_w[0].astype(BF16)
    g0 = norm_g[0].reshape(1, d)
    ctx_flat = ctx.reshape(1, bsz * lc, d)
    n_in0 = w_in0.shape[1]
    p_c = _modnorm_mm(ctx_flat, mod_c[0], g0, w_in0, 4 * TOK_TILE, n_in0 // 4, "in_proj0_ctx").reshape(bsz, lc, n_in0)
    p_l = _modnorm_mm(x, mod_l[0], g0, w_in0, 4 * TOK_TILE, n_in0 // 4, "in_proj0")
    consts = _hgrn_constants(HG_CHUNK)
    s0 = jnp.zeros((bsz, 2, nh, HG_DK, HG_DK), F32)
    of_c, ob_c, s_c = _hgrn(p_c, lb, s0, consts)
    of_l, ob_l, _ = _hgrn(p_l, lb, s_c, consts)
    hgn = hg_norm_g[0].reshape(1, w)
    pw = pool_w[0].astype(BF16)
    ps = pool_scale[0].reshape(1, w)
    ow0 = out_w[0].astype(BF16)
    mod_c0 = jnp.broadcast_to(mod_c[0], (bsz, 3, d))
    hc1 = _even_post(of_c, ob_c, p_c, ctx, mod_c0, hgn, pw, ps, ow0)
    hl1 = _even_post(of_l, ob_l, p_l, x, mod_l[0], hgn, pw, ps, ow0)

    o1 = q_rank
    o2 = o1 + kv_rank
    o3 = o2 + MLA_ROPE
    w1 = od_in_w[0]
    kr_pad = jnp.zeros((d, LANES - MLA_ROPE), F32)
    w_in1 = jnp.concatenate([w1[:, o3:], w1[:, :o1], w1[:, o1:o2], w1[:, o2:o3], kr_pad], axis=1).astype(BF16)
    n_kv = kv_rank + LANES
    w_in1c = w_in1[:, d_inner + q_rank:]
    g1 = norm_g[1].reshape(1, d)
    p1_c = _modnorm_mm(hc1.reshape(1, bsz * lc, d), mod_c[1], g1, w_in1c, 4 * TOK_TILE, n_kv, "in_proj1_ctx")
    p1_c = p1_c.reshape(bsz, lc, n_kv)
    p1_l = _modnorm_mm(hl1, mod_l[1], g1, w_in1, 2 * TOK_TILE, w_in1.shape[1], "in_proj1")

    kvw = kvb_w[0].reshape(kv_rank, MLA_HEADS, MLA_NOPE + MLA_V)
    wuk = kvw[..., :MLA_NOPE].reshape(kv_rank, MLA_HEADS * MLA_NOPE).astype(BF16)
    wuvt = jnp.transpose(kvw[..., MLA_NOPE:], (1, 2, 0)).astype(BF16)
    wqt = jnp.transpose(qb_w[0].reshape(q_rank, MLA_HEADS, MLA_QK), (1, 2, 0)).astype(BF16)
    kvg = kva_norm_g[0].reshape(1, kv_rank)
    qag = qa_norm_g[0].reshape(1, q_rank)
    tab_k, tab_q = _rope_tables(t)
    kc, vtc = _mla_kv(p1_c, 0, kv_rank // LANES, kvg, wuk, wuvt, None)
    kl, vtl = _mla_kv(p1_l, (d_inner + q_rank) // kv_rank, (d_inner + q_rank + kv_rank) // LANES, kvg, wuk, wuvt,
                      tab_k)
    qt = _mla_q(p1_l, d_inner // q_rank, qag, wqt, *tab_q)
    y = _attn(qt, kc, vtc, kl, vtl, p1_l)
    return _out_final(y, hl1, mod_l[1], out_w[1].astype(BF16), final_norm_g.reshape(1, d))
```

```python
import functools

import numpy as np
import jax
import jax.numpy as jnp
from jax import lax
from jax.experimental import pallas as pl
from jax.experimental.pallas import tpu as pltpu

F32 = jnp.float32
BF16 = jnp.bfloat16

EPS = 1e-6
GRID_W = 64
HG_DK = 128
POOL_WINDOWS = (2, 4, 8, 16)
MLA_HEADS = 16
MLA_NOPE = 128
MLA_ROPE = 64
MLA_V = 128
MLA_QK = MLA_NOPE + MLA_ROPE
QK_PAD = 256
VT_ROWS = MLA_V + 16
MLA_SCALE = MLA_QK ** -0.5
LOG2_E = 1.4426950408889634
ROPE_FREQ = MLA_ROPE // 4
ROPE_BASE = 10000.0

LANES = 128
SUBLANES = 8
VMEM_LIMIT = 48 * 1024 * 1024

HG_CHUNK = 64
TOK_TILE = 256
KV_CHUNK = 256
Q_TILE = 2048
Q_SUB = 256
POOL_HALO = 8


def _dot(a, b):
    return jnp.dot(a, b, preferred_element_type=F32)


def _dot_nt(a, b):
    return lax.dot_general(a, b, (((1,), (1,)), ((), ())), preferred_element_type=F32)


def _dot_tn(a, b):
    return lax.dot_general(a, b, (((0,), (0,)), ((), ())), preferred_element_type=F32)


def _silu(x):
    return x * jax.nn.sigmoid(x)


def _split_bf16(x):
    hi = x.astype(BF16)
    lo = (x - hi.astype(F32)).astype(BF16)
    return hi, lo


def _params(*sem):
    return pltpu.CompilerParams(dimension_semantics=sem, vmem_limit_bytes=VMEM_LIMIT)


def _ada_kernel(c_ref, w_ref, b_ref, o_ref):
    c = c_ref[...]
    s_hi, s_lo = _split_bf16(_silu(c))
    w_hi, w_lo = _split_bf16(w_ref[...])
    o_ref[...] = _dot(s_hi, w_hi) + _dot(s_lo, w_hi) + _dot(s_hi, w_lo) + b_ref[...]


def _ada(cond, ada_w, ada_b):
    depth, d, _ = ada_w.shape
    r = cond.shape[0]
    return pl.pallas_call(
        _ada_kernel,
        grid=(depth, 3),
        in_specs=[
            pl.BlockSpec((r, d), lambda l, j: (0, 0)),
            pl.BlockSpec((None, d, d), lambda l, j: (l, 0, j)),
            pl.BlockSpec((None, 1, d), lambda l, j: (l, 0, j)),
        ],
        out_specs=pl.BlockSpec((None, r, d), lambda l, j: (l, 0, j)),
        out_shape=jax.ShapeDtypeStruct((depth, r, 3 * d), F32),
        compiler_params=_params("parallel", "parallel"),
        name="ada_modulation",
    )(cond, ada_w, ada_b.reshape(depth, 1, 3 * d))


def _rms(x, g):
    return x * lax.rsqrt(jnp.mean(x * x, axis=-1, keepdims=True) + EPS) * g


def _modnorm_mm_kernel(x_ref, mod_ref, g_ref, w_ref, o_ref, z_ref):
    @pl.when(pl.program_id(2) == 0)
    def _():
        y = _rms(x_ref[...], g_ref[...])
        z_ref[...] = (y * (1.0 + mod_ref[1:2, :]) + mod_ref[0:1, :]).astype(BF16)

    o_ref[...] = _dot(z_ref[...], w_ref[...]).astype(o_ref.dtype)


def _modnorm_mm(x, mod, g, w, tm, tn, name):
    bx, r, d = x.shape
    n = w.shape[1]
    tm = min(tm, r)
    return pl.pallas_call(
        _modnorm_mm_kernel,
        grid=(bx, r // tm, n // tn),
        in_specs=[
            pl.BlockSpec((None, tm, d), lambda b, i, j: (b, i, 0)),
            pl.BlockSpec((None, 3, d), lambda b, i, j: (b, 0, 0)),
            pl.BlockSpec((1, d), lambda b, i, j: (0, 0)),
            pl.BlockSpec((d, tn), lambda b, i, j: (0, j)),
        ],
        out_specs=pl.BlockSpec((None, tm, tn), lambda b, i, j: (b, i, j)),
        out_shape=jax.ShapeDtypeStruct((bx, r, n), F32),
        scratch_shapes=[pltpu.VMEM((tm, d), BF16)],
        compiler_params=_params("parallel", "parallel", "arbitrary"),
        name=name,
    )(x, mod, g, w)


def _hgrn_levels(c):
    w = c // 2
    out = []
    while w >= 1:
        out.append(w)
        w //= 2
    return tuple(out)


def _hgrn_constants(c):
    t = np.arange(c)
    u = t[None, :]
    tt = t[:, None]
    a = [np.tril(np.ones((c, c), np.float32))]
    masks, isk = [], []
    for w in _hgrn_levels(c):
        blk = t // (2 * w)
        first = (t % (2 * w)) < w
        ref = (blk * 2 * w + w - 1)[:, None]
        a.append(np.where(first[:, None], (u > tt) & (u <= ref), (u > ref) & (u <= tt)).astype(np.float32))
        masks.append(((blk[:, None] == blk[None, :]) & (~first[:, None]) & first[None, :]).astype(np.float32))
        isk.append(first.astype(np.float32)[:, None])
    a = np.stack(a)
    masks = np.stack(masks)
    isk = np.stack(isk)
    flip = lambda m: m[:, ::-1, ::-1]
    a2 = np.stack([a, flip(a)]).reshape(2, -1, c)
    m2 = np.stack([masks, flip(masks)])
    k2 = np.stack([isk, isk[:, ::-1]])
    return jnp.asarray(a2, BF16), jnp.asarray(m2, F32), jnp.asarray(k2, F32)


def _hgrn_chunk(q_ref, f_ref, v_ref, o_ref, lb, a, mk_ref, isk_ref, st_ref, d, r0, *, c, nh):
    nlev = len(_hgrn_levels(c))
    last = c - 1 if d == 0 else 0
    rows = pl.ds(r0, c)
    head = lambda x, h: x[:, h * HG_DK:(h + 1) * HG_DK]
    f = lb + (1.0 - lb) * jax.nn.sigmoid(f_ref[rows, :])
    g_hi, g_lo = _split_bf16(jnp.log(f))
    dd = _dot(a, g_hi) + _dot(a, g_lo)
    q = _silu(q_ref[rows, :])
    k = 1.0 - f
    v = v_ref[rows, :]
    vb = v.astype(BF16)
    b = dd[0:c]
    bl = b[last:last + 1, :]
    qe = (q * jnp.exp(b)).astype(BF16)
    kend = (k * jnp.exp(bl - b)).astype(BF16)
    ebl = jnp.exp(bl)
    st = [st_ref[d, h] for h in range(nh)]
    inter = [_dot_nt(head(qe, h), st[h].astype(BF16)) for h in range(nh)]
    att = [None] * nh
    for l in range(nlev):
        e = jnp.exp(dd[(l + 1) * c:(l + 2) * c])
        x = (e * jnp.where(isk_ref[d, l] > 0.0, k, q)).astype(BF16)
        for h in range(nh):
            t = mk_ref[d, l] * _dot_nt(head(x, h), head(x, h))
            att[h] = t if att[h] is None else att[h] + t
    qk = q * k
    for h in range(nh):
        diag = jnp.sum(head(qk, h), axis=-1, keepdims=True)
        o_ref[rows, h * HG_DK:(h + 1) * HG_DK] = (
            inter[h] + _dot(att[h].astype(BF16), head(vb, h)) + diag * head(v, h))
    for h in range(nh):
        st_ref[d, h] = head(ebl, h) * st[h] + _dot_tn(head(vb, h), head(kend, h))


def _hgrn_kernel(qf_ref, ff_ref, vf_ref, qb_ref, fb_ref, vb_ref, lb_ref, a_ref, mk_ref, isk_ref, s0_ref,
                 of_ref, ob_ref, sout_ref, st_ref, *, tb, c, nh):
    nchunk = tb // c

    @pl.when(pl.program_id(1) == 0)
    def _():
        st_ref[...] = s0_ref[...]

    def body(cc, carry):
        rf = pl.multiple_of(cc * c, c)
        rb = pl.multiple_of((nchunk - 1 - cc) * c, c)
        _hgrn_chunk(qf_ref, ff_ref, vf_ref, of_ref, lb_ref[0], a_ref[0], mk_ref, isk_ref, st_ref, 0, rf, c=c, nh=nh)
        _hgrn_chunk(qb_ref, fb_ref, vb_ref, ob_ref, lb_ref[1], a_ref[1], mk_ref, isk_ref, st_ref, 1, rb, c=c, nh=nh)
        return carry

    lax.fori_loop(0, nchunk, body, 0)

    @pl.when(pl.program_id(1) == pl.num_programs(1) - 1)
    def _():
        sout_ref[...] = st_ref[...]


def _hgrn(p, lb, s0, consts):
    bsz, r, _ = p.shape
    w = lb.shape[-1]
    nh = w // HG_DK
    tb = min(TOK_TILE, r)
    c = HG_CHUNK
    nb = r // tb
    a2, m2, k2 = consts
    nrow = a2.shape[1]
    fwd = lambda col: pl.BlockSpec((None, tb, w), lambda b, s: (b, s, col))
    bwd = lambda col: pl.BlockSpec((None, tb, w), lambda b, s: (b, nb - 1 - s, col))
    const = lambda arr: pl.BlockSpec(arr.shape, lambda b, s: (0,) * arr.ndim)
    st_spec = pl.BlockSpec((None, 2, nh, HG_DK, HG_DK), lambda b, s: (b, 0, 0, 0, 0))
    kern = functools.partial(_hgrn_kernel, tb=tb, c=c, nh=nh)
    return pl.pallas_call(
        kern,
        grid=(bsz, nb),
        in_specs=[fwd(0), fwd(1), fwd(3), bwd(0), bwd(2), bwd(3), const(lb), const(a2), const(m2), const(k2),
                  st_spec],
        out_specs=[
            pl.BlockSpec((None, tb, w), lambda b, s: (b, s, 0)),
            pl.BlockSpec((None, tb, w), lambda b, s: (b, nb - 1 - s, 0)),
            st_spec,
        ],
        out_shape=[
            jax.ShapeDtypeStruct((bsz, r, w), F32),
            jax.ShapeDtypeStruct((bsz, r, w), F32),
            jax.ShapeDtypeStruct(s0.shape, F32),
        ],
        scratch_shapes=[pltpu.VMEM((2, nh, HG_DK, HG_DK), F32)],
        compiler_params=_params("parallel", "arbitrary"),
        name="hgrn2_scan",
    )(p, p, p, p, p, p, lb, a2, m2, k2, s0)


def _even_post_kernel(of_ref, ob_ref, ga_ref, u_ref, gb_ref, up_ref, un_ref, h_ref, mod_ref, hgn_ref, pw_ref,
                      ps_ref, ow_ref, o_ref, ext_ref, y_ref, *, tb, seq, nh):
    j = pl.program_id(1)
    w = nh * HG_DK
    o = of_ref[...] + ob_ref[...]
    for h in range(nh):
        sl = slice(h * HG_DK, (h + 1) * HG_DK)
        y_ref[:, sl] = (_rms(o[:, sl], hgn_ref[:, sl]) * _silu(ga_ref[:, sl])).astype(BF16)
    u = u_ref[...]
    ext_ref[0:POOL_HALO, :] = jnp.where(j > 0, up_ref[...], 0.0)
    ext_ref[POOL_HALO:POOL_HALO + tb, :] = u
    ext_ref[POOL_HALO + tb:, :] = jnp.where(j < pl.num_programs(1) - 1, un_ref[...], 0.0)
    t = j * tb + lax.broadcasted_iota(jnp.int32, (tb, 1), 0)
    grp = w // len(POOL_WINDOWS)
    for gi, win in enumerate(POOL_WINDOWS):
        sl = slice(gi * grp, (gi + 1) * grp)
        acc = ext_ref[POOL_HALO - win // 2:POOL_HALO - win // 2 + tb, sl]
        for off in range(-win // 2 + 1, win // 2):
            acc = acc + ext_ref[POOL_HALO + off:POOL_HALO + off + tb, sl]
        cnt = (jnp.minimum(t + win // 2, seq) - jnp.maximum(t - win // 2, 0)).astype(F32)
        yp = acc / cnt - u[:, sl]
        yb = _dot(yp.astype(BF16), pw_ref[gi]) * ps_ref[:, sl]
        y_ref[:, w + gi * grp:w + (gi + 1) * grp] = (yb * _silu(gb_ref[:, sl])).astype(BF16)
    o_ref[...] = h_ref[...] + mod_ref[2:3, :] * _dot(y_ref[...], ow_ref[...])


def _even_post(o_f, o_b, p, h, mod, hgn, pool_w, pool_scale, out_w):
    bsz, r, w = o_f.shape
    d = h.shape[-1]
    tb = min(TOK_TILE, r)
    nb = r // tb
    hb = tb // POOL_HALO
    nh = w // HG_DK
    tok = lambda col: pl.BlockSpec((None, tb, w), lambda b, j: (b, j, col))
    const = lambda arr: pl.BlockSpec(arr.shape, lambda b, j: (0,) * arr.ndim)
    kern = functools.partial(_even_post_kernel, tb=tb, seq=r, nh=nh)
    return pl.pallas_call(
        kern,
        grid=(bsz, nb),
        in_specs=[
            tok(0), tok(0), tok(4), tok(5), tok(6),
            pl.BlockSpec((None, POOL_HALO, w), lambda b, j: (b, jnp.maximum(j * hb - 1, 0), 5)),
            pl.BlockSpec((None, POOL_HALO, w), lambda b, j: (b, jnp.minimum((j + 1) * hb, nb * hb - 1), 5)),
            pl.BlockSpec((None, tb, d), lambda b, j: (b, j, 0)),
            pl.BlockSpec((None, 3, d), lambda b, j: (b, 0, 0)),
            const(hgn), const(pool_w), const(pool_scale), const(out_w),
        ],
        out_specs=pl.BlockSpec((None, tb, d), lambda b, j: (b, j, 0)),
        out_shape=jax.ShapeDtypeStruct((bsz, r, d), F32),
        scratch_shapes=[pltpu.VMEM((tb + 2 * POOL_HALO, w), F32), pltpu.VMEM((tb, 2 * w), BF16)],
        compiler_params=_params("parallel", "parallel"),
        name="even_post",
    )(o_f, o_b, p, p, p, p, p, h, mod, hgn, pool_w, pool_scale, out_w)


def _mla_kv_kernel(*refs, rope):
    if rope:
        ckv_ref, kr_ref, g_ref, wuk_ref, wuvt_ref, cos_ref, sin_ref, kcat_ref, vt_ref = refs
    else:
        ckv_ref, kr_ref, g_ref, wuk_ref, wuvt_ref, kcat_ref, vt_ref = refs
    cn = _rms(ckv_ref[...], g_ref[...]).astype(BF16)
    kn = _dot(cn, wuk_ref[...])
    kr = kr_ref[...]
    if rope:
        lane = lax.broadcasted_iota(jnp.int32, kr.shape, 1)
        swapped = jnp.where((lane % (2 * ROPE_FREQ)) < ROPE_FREQ,
                            pltpu.roll(kr, LANES - ROPE_FREQ, 1), pltpu.roll(kr, ROPE_FREQ, 1))
        kr = kr * cos_ref[...] + swapped * sin_ref[...]
    kr = kr.astype(BF16)
    ones_rows = (lax.broadcasted_iota(jnp.int32, (VT_ROWS - MLA_V, kr.shape[0]), 0) == 0).astype(BF16)
    for h in range(MLA_HEADS):
        kcat_ref[h, :, 0:MLA_NOPE] = kn[:, h * MLA_NOPE:(h + 1) * MLA_NOPE].astype(BF16)
        kcat_ref[h, :, MLA_NOPE:] = kr
        vt_ref[h, 0:MLA_V, :] = _dot_nt(wuvt_ref[h], cn).astype(BF16)
        vt_ref[h, MLA_V:, :] = ones_rows


def _mla_kv(p, ckv_blk, kr_blk, g, wuk, wuvt, tables):
    bsz, r, _ = p.shape
    rank = g.shape[-1]
    tb = KV_CHUNK
    nb = r // tb
    const = lambda arr: pl.BlockSpec(arr.shape, lambda b, j: (0,) * arr.ndim)
    in_specs = [
        pl.BlockSpec((None, tb, rank), lambda b, j: (b, j, ckv_blk)),
        pl.BlockSpec((None, tb, LANES), lambda b, j: (b, j, kr_blk)),
        const(g), const(wuk), const(wuvt),
    ]
    args = [p, p, g, wuk, wuvt]
    if tables is not None:
        in_specs += [pl.BlockSpec((tb, LANES), lambda b, j: (j, 0))] * 2
        args += list(tables)
    return pl.pallas_call(
        functools.partial(_mla_kv_kernel, rope=tables is not None),
        grid=(bsz, nb),
        in_specs=in_specs,
        out_specs=[
            pl.BlockSpec((None, MLA_HEADS, tb, QK_PAD), lambda b, j: (b, 0, j, 0)),
            pl.BlockSpec((None, MLA_HEADS, None, VT_ROWS, tb), lambda b, j: (b, 0, j, 0, 0)),
        ],
        out_shape=[
            jax.ShapeDtypeStruct((bsz, MLA_HEADS, r, QK_PAD), BF16),
            jax.ShapeDtypeStruct((bsz, MLA_HEADS, nb, VT_ROWS, tb), BF16),
        ],
        compiler_params=_params("parallel", "parallel"),
        name="mla_kv_rope" if tables is not None else "mla_kv",
    )(*args)


def _mla_q_kernel(cq_ref, g_ref, wqt_ref, cos_ref, sin_ref, qt_ref):
    cn = _rms(cq_ref[...], g_ref[...]).astype(BF16)
    f = ROPE_FREQ
    for h in range(MLA_HEADS):
        qt = _dot_nt(wqt_ref[h], cn) * (MLA_SCALE * LOG2_E)
        qt_ref[h, 0:MLA_NOPE, :] = qt[0:MLA_NOPE].astype(BF16)
        for ax in range(2):
            r0 = MLA_NOPE + ax * 2 * f
            x1 = qt[r0:r0 + f]
            x2 = qt[r0 + f:r0 + 2 * f]
            co = cos_ref[ax]
            si = sin_ref[ax]
            qt_ref[h, r0:r0 + f, :] = (x1 * co - x2 * si).astype(BF16)
            qt_ref[h, r0 + f:r0 + 2 * f, :] = (x2 * co + x1 * si).astype(BF16)
        qt_ref[h, MLA_QK:, :] = jnp.zeros((QK_PAD - MLA_QK, cn.shape[0]), BF16)


def _mla_q(p, cq_blk, g, wqt, cos_t, sin_t):
    bsz, t, _ = p.shape
    rank = g.shape[-1]
    tm = min(TOK_TILE, t)
    const = lambda arr: pl.BlockSpec(arr.shape, lambda b, j: (0,) * arr.ndim)
    tab = pl.BlockSpec((2, ROPE_FREQ, tm), lambda b, j: (0, 0, j))
    return pl.pallas_call(
        _mla_q_kernel,
        grid=(bsz, t // tm),
        in_specs=[pl.BlockSpec((None, tm, rank), lambda b, j: (b, j, cq_blk)), const(g), const(wqt), tab, tab],
        out_specs=pl.BlockSpec((None, MLA_HEADS, QK_PAD, tm), lambda b, j: (b, 0, 0, j)),
        out_shape=jax.ShapeDtypeStruct((bsz, MLA_HEADS, QK_PAD, t), BF16),
        compiler_params=_params("parallel", "parallel"),
        name="mla_q",
    )(p, g, wqt, cos_t, sin_t)


def _attn_kernel(qt_ref, kc_ref, vtc_ref, kl_ref, vtl_ref, g_ref, o_ref, m_ref, acc_ref, s_ref, *, tq, n_lat):
    nsub = tq // Q_SUB
    m_ref[...] = jnp.full(m_ref.shape, -jnp.inf, F32)
    acc_ref[...] = jnp.zeros(acc_ref.shape, F32)

    def substep(k_next, vt_cur, cur, nxt):
        for g in range(nsub):
            sl = slice(g * Q_SUB, (g + 1) * Q_SUB)
            s_ref[nxt, g] = _dot(k_next, qt_ref[:, sl])
            s = s_ref[cur, g]
            m_old = m_ref[:, sl]
            m_new = jnp.maximum(m_old, jnp.max(s, axis=0, keepdims=True))
            alpha = jnp.exp2(m_old - m_new)
            p = jnp.exp2(s - m_new)
            acc_ref[:, sl] = alpha * acc_ref[:, sl] + _dot(vt_cur, p.astype(BF16))
            m_ref[:, sl] = m_new

    kc = kc_ref[...]
    for g in range(nsub):
        s_ref[0, g] = _dot(kc, qt_ref[:, g * Q_SUB:(g + 1) * Q_SUB])
    substep(kl_ref[0:KV_CHUNK, :], vtc_ref[0], 0, 1)

    per_trip = 4 if n_lat % 4 == 0 else 2

    def body(j, carry):
        for u in range(per_trip):
            a = per_trip * j + u
            r = pl.multiple_of(jnp.minimum(a + 1, n_lat - 1) * KV_CHUNK, KV_CHUNK)
            substep(kl_ref[pl.ds(r, KV_CHUNK), :], vtl_ref[a], (1 + u) % 2, u % 2)
        return carry

    lax.fori_loop(0, n_lat // per_trip, body, 0)
    o = (acc_ref[0:MLA_V, :] * (1.0 / acc_ref[MLA_V:MLA_V + 1, :])).T
    o_ref[...] = (o * _silu(g_ref[...])).astype(o_ref.dtype)


def _attn(qt, kc, vtc, kl, vtl, p):
    bsz, nh, _, t = qt.shape
    lc = kc.shape[2]
    tq = min(Q_TILE, t)
    n_lat = t // KV_CHUNK
    assert lc == KV_CHUNK and n_lat % 2 == 0
    kern = functools.partial(_attn_kernel, tq=tq, n_lat=n_lat)
    full4 = lambda arr: pl.BlockSpec((None, None) + arr.shape[2:], lambda b, h, i: (b, h, 0, 0))
    full5 = lambda arr: pl.BlockSpec((None, None) + arr.shape[2:], lambda b, h, i: (b, h, 0, 0, 0))
    return pl.pallas_call(
        kern,
        grid=(bsz, nh, t // tq),
        in_specs=[
            pl.BlockSpec((None, None, QK_PAD, tq), lambda b, h, i: (b, h, 0, i)),
            full4(kc), full5(vtc), full4(kl), full5(vtl),
            pl.BlockSpec((None, tq, MLA_V), lambda b, h, i: (b, i, h)),
        ],
        out_specs=pl.BlockSpec((None, tq, MLA_V), lambda b, h, i: (b, i, h)),
        out_shape=jax.ShapeDtypeStruct((bsz, t, nh * MLA_V), BF16),
        scratch_shapes=[pltpu.VMEM((1, tq), F32), pltpu.VMEM((VT_ROWS, tq), F32),
                        pltpu.VMEM((2, tq // Q_SUB, KV_CHUNK, Q_SUB), F32)],
        compiler_params=_params("parallel", "parallel", "arbitrary"),
        name="mla_attention",
    )(qt, kc, vtc, kl, vtl, p)


def _out_final_kernel(y_ref, h_ref, mod_ref, ow_ref, g_ref, o_ref):
    hn = h_ref[...] + mod_ref[2:3, :] * _dot(y_ref[...], ow_ref[...])
    o_ref[...] = _rms(hn, g_ref[...])


def _out_final(y, h, mod, out_w, g):
    bsz, t, d = h.shape
    wi = y.shape[-1]
    tm = min(2 * TOK_TILE, t)
    return pl.pallas_call(
        _out_final_kernel,
        grid=(bsz, t // tm),
        in_specs=[
            pl.BlockSpec((None, tm, wi), lambda b, j: (b, j, 0)),
            pl.BlockSpec((None, tm, d), lambda b, j: (b, j, 0)),
            pl.BlockSpec((None, 3, d), lambda b, j: (b, 0, 0)),
            pl.BlockSpec((wi, d), lambda b, j: (0, 0)),
            pl.BlockSpec((1, d), lambda b, j: (0, 0)),
        ],
        out_specs=pl.BlockSpec((None, tm, d), lambda b, j: (b, j, 0)),
        out_shape=jax.ShapeDtypeStruct((bsz, t, d), F32),
        compiler_params=_params("parallel", "parallel"),
        name="out_final",
    )(y, h, mod, out_w, g)


def _rope_tables(n_tokens):
    rows = n_tokens // GRID_W
    pos_r = jnp.repeat(jnp.arange(rows), GRID_W).astype(F32)
    pos_c = jnp.tile(jnp.arange(GRID_W), rows).astype(F32)
    inv = ROPE_BASE ** (-2.0 * jnp.arange(ROPE_FREQ, dtype=F32) / (MLA_ROPE // 2))
    ang = jnp.stack([pos_r[:, None] * inv, pos_c[:, None] * inv], axis=1)
    cos, sin = jnp.cos(ang), jnp.sin(ang)
    pad = LANES - MLA_ROPE
    cos_k = jnp.pad(jnp.stack([cos, cos], axis=2).reshape(n_tokens, MLA_ROPE), ((0, 0), (0, pad)))
    sin_k = jnp.pad(jnp.stack([-sin, sin], axis=2).reshape(n_tokens, MLA_ROPE), ((0, 0), (0, pad)))
    cos_q = jnp.transpose(cos, (1, 2, 0))
    sin_q = jnp.transpose(sin, (1, 2, 0))
    return (cos_k, sin_k), (cos_q, sin_q)


def kernel(x, c, ctx, c_ctx, ada_w, ada_b, norm_g, out_w, ev_in_w, hg_lb, hg_norm_g, pool_w, pool_scale,
           od_in_w, qa_norm_g, qb_w, kva_norm_g, kvb_w, final_norm_g):
    bsz, t, d = x.shape
    lc = ctx.shape[1]
    depth = ada_w.shape[0]
    assert depth == 2 and t % (2 * TOK_TILE) == 0 and lc % TOK_TILE == 0 and t % GRID_W == 0
    w = hg_norm_g.shape[-1]
    nh = w // HG_DK
    q_rank = qa_norm_g.shape[-1]
    kv_rank = kva_norm_g.shape[-1]
    d_inner = out_w.shape[1]

    n_cond = -(-(bsz + 1) // SUBLANES) * SUBLANES
    cond = jnp.zeros((n_cond, d), F32).at[:bsz].set(c).at[bsz].set(c_ctx)
    mods = _ada(cond, ada_w, ada_b).reshape(depth, n_cond, 3, d)
    mod_l = [mods[l, :bsz] for l in range(depth)]
    mod_c = [mods[l, bsz:bsz + 1] for l in range(depth)]

    lb = jnp.cumsum(jax.nn.softmax(hg_lb.astype(F32), axis=1), axis=1)[:, 0].reshape(2, 1, w)
    w_in0 = ev_in_w[0].astype(BF16)
    g0 = norm_g[0].reshape(1, d)
    ctx_flat = ctx.reshape(1, bsz * lc, d)
    n_in0 = w_in0.shape[1]
    p_c = _modnorm_mm(ctx_flat, mod_c[0], g0, w_in0, 4 * TOK_TILE, n_in0 // 4, "in_proj0_ctx").reshape(bsz, lc, n_in0)
    p_l = _modnorm_mm(x, mod_l[0], g0, w_in0, 4 * TOK_TILE, n_in0 // 4, "in_proj0")
    consts = _hgrn_constants(HG_CHUNK)
    s0 = jnp.zeros((bsz, 2, nh, HG_DK, HG_DK), F32)
    of_c, ob_c, s_c = _hgrn(p_c, lb, s0, consts)
    of_l, ob_l, _ = _hgrn(p_l, lb, s_c, consts)
    hgn = hg_norm_g[0].reshape(1, w)
    pw = pool_w[0].astype(BF16)
    ps = pool_scale[0].reshape(1, w)
    ow0 = out_w[0].astype(BF16)
    mod_c0 = jnp.broadcast_to(mod_c[0], (bsz, 3, d))
    hc1 = _even_post(of_c, ob_c, p_c, ctx, mod_c0, hgn, pw, ps, ow0)
    hl1 = _even_post(of_l, ob_l, p_l, x, mod_l[0], hgn, pw, ps, ow0)

    o1 = q_rank
    o2 = o1 + kv_rank
    o3 = o2 + MLA_ROPE
    w1 = od_in_w[0]
    kr_pad = jnp.zeros((d, LANES - MLA_ROPE), F32)
    w_in1 = jnp.concatenate([w1[:, o3:], w1[:, :o1], w1[:, o1:o2], w1[:, o2:o3], kr_pad], axis=1).astype(BF16)
    n_kv = kv_rank + LANES
    w_in1c = w_in1[:, d_inner + q_rank:]
    g1 = norm_g[1].reshape(1, d)
    p1_c = _modnorm_mm(hc1.reshape(1, bsz * lc, d), mod_c[1], g1, w_in1c, 4 * TOK_TILE, n_kv, "in_proj1_ctx")
    p1_c = p1_c.reshape(bsz, lc, n_kv)
    p1_l = _modnorm_mm(hl1, mod_l[1], g1, w_in1, 2 * TOK_TILE, w_in1.shape[1], "in_proj1")

    kvw = kvb_w[0].reshape(kv_rank, MLA_HEADS, MLA_NOPE + MLA_V)
    wuk = kvw[..., :MLA_NOPE].reshape(kv_rank, MLA_HEADS * MLA_NOPE).astype(BF16)
    wuvt = jnp.transpose(kvw[..., MLA_NOPE:], (1, 2, 0)).astype(BF16)
    wqt = jnp.transpose(qb_w[0].reshape(q_rank, MLA_HEADS, MLA_QK), (1, 2, 0)).astype(BF16)
    kvg = kva_norm_g[0].reshape(1, kv_rank)
    qag = qa_norm_g[0].reshape(1, q_rank)
    tab_k, tab_q = _rope_tables(t)
    kc, vtc = _mla_kv(p1_c, 0, kv_rank // LANES, kvg, wuk, wuvt, None)
    kl, vtl = _mla_kv(p1_l, (d_inner + q_rank) // kv_rank, (d_inner + q_rank + kv_rank) // LANES, kvg, wuk, wuvt,
                      tab_k)
    qt = _mla_q(p1_l, d_inner // q_rank, qag, wqt, *tab_q)
    y = _attn(qt, kc, vtc, kl, vtl, p1_l)
    return _out_final(y, hl1, mod_l[1], out_w[1].astype(BF16), final_norm_g.reshape(1, d))
```

```python
import functools

import numpy as np
import jax
import jax.numpy as jnp
from jax import lax
from jax.experimental import pallas as pl
from jax.experimental.pallas import tpu as pltpu

F32 = jnp.float32
BF16 = jnp.bfloat16

EPS = 1e-6
GRID_W = 64
HG_DK = 128
POOL_WINDOWS = (2, 4, 8, 16)
MLA_HEADS = 16
MLA_NOPE = 128
MLA_ROPE = 64
MLA_V = 128
MLA_QK = MLA_NOPE + MLA_ROPE
QK_PAD = 256
VT_ROWS = MLA_V + 16
MLA_SCALE = MLA_QK ** -0.5
LOG2_E = 1.4426950408889634
ROPE_FREQ = MLA_ROPE // 4
ROPE_BASE = 10000.0

LANES = 128
SUBLANES = 8
VMEM_LIMIT = 48 * 1024 * 1024

HG_CHUNK = 64
TOK_TILE = 256
KV_CHUNK = 512
Q_TILE = 2048
Q_SUB = 256
POOL_HALO = 8


def _dot(a, b):
    return jnp.dot(a, b, preferred_element_type=F32)


def _dot_nt(a, b):
    return lax.dot_general(a, b, (((1,), (1,)), ((), ())), preferred_element_type=F32)


def _dot_tn(a, b):
    return lax.dot_general(a, b, (((0,), (0,)), ((), ())), preferred_element_type=F32)


def _silu(x):
    return x * jax.nn.sigmoid(x)


def _split_bf16(x):
    hi = x.astype(BF16)
    lo = (x - hi.astype(F32)).astype(BF16)
    return hi, lo


def _params(*sem):
    return pltpu.CompilerParams(dimension_semantics=sem, vmem_limit_bytes=VMEM_LIMIT)


def _ada_kernel(c_ref, w_ref, b_ref, o_ref):
    c = c_ref[...]
    s_hi, s_lo = _split_bf16(_silu(c))
    w_hi, w_lo = _split_bf16(w_ref[...])
    o_ref[...] = _dot(s_hi, w_hi) + _dot(s_lo, w_hi) + _dot(s_hi, w_lo) + b_ref[...]


def _ada(cond, ada_w, ada_b):
    depth, d, _ = ada_w.shape
    r = cond.shape[0]
    return pl.pallas_call(
        _ada_kernel,
        grid=(depth, 3),
        in_specs=[
            pl.BlockSpec((r, d), lambda l, j: (0, 0)),
            pl.BlockSpec((None, d, d), lambda l, j: (l, 0, j)),
            pl.BlockSpec((None, 1, d), lambda l, j: (l, 0, j)),
        ],
        out_specs=pl.BlockSpec((None, r, d), lambda l, j: (l, 0, j)),
        out_shape=jax.ShapeDtypeStruct((depth, r, 3 * d), F32),
        compiler_params=_params("parallel", "parallel"),
        name="ada_modulation",
    )(cond, ada_w, ada_b.reshape(depth, 1, 3 * d))


def _rms(x, g):
    return x * lax.rsqrt(jnp.mean(x * x, axis=-1, keepdims=True) + EPS) * g


def _modnorm_mm_kernel(x_ref, mod_ref, g_ref, w_ref, o_ref, z_ref):
    @pl.when(pl.program_id(2) == 0)
    def _():
        y = _rms(x_ref[...], g_ref[...])
        z_ref[...] = (y * (1.0 + mod_ref[1:2, :]) + mod_ref[0:1, :]).astype(BF16)

    o_ref[...] = _dot(z_ref[...], w_ref[...]).astype(o_ref.dtype)


def _modnorm_mm(x, mod, g, w, tm, tn, name):
    bx, r, d = x.shape
    n = w.shape[1]
    tm = min(tm, r)
    return pl.pallas_call(
        _modnorm_mm_kernel,
        grid=(bx, r // tm, n // tn),
        in_specs=[
            pl.BlockSpec((None, tm, d), lambda b, i, j: (b, i, 0)),
            pl.BlockSpec((None, 3, d), lambda b, i, j: (b, 0, 0)),
            pl.BlockSpec((1, d), lambda b, i, j: (0, 0)),
            pl.BlockSpec((d, tn), lambda b, i, j: (0, j)),
        ],
        out_specs=pl.BlockSpec((None, tm, tn), lambda b, i, j: (b, i, j)),
        out_shape=jax.ShapeDtypeStruct((bx, r, n), F32),
        scratch_shapes=[pltpu.VMEM((tm, d), BF16)],
        compiler_params=_params("parallel", "parallel", "arbitrary"),
        name=name,
    )(x, mod, g, w)


def _hgrn_levels(c):
    w = c // 2
    out = []
    while w >= 1:
        out.append(w)
        w //= 2
    return tuple(out)


def _hgrn_constants(c):
    t = np.arange(c)
    u = t[None, :]
    tt = t[:, None]
    a = [np.tril(np.ones((c, c), np.float32))]
    masks, isk = [], []
    for w in _hgrn_levels(c):
        blk = t // (2 * w)
        first = (t % (2 * w)) < w
        ref = (blk * 2 * w + w - 1)[:, None]
        a.append(np.where(first[:, None], (u > tt) & (u <= ref), (u > ref) & (u <= tt)).astype(np.float32))
        masks.append(((blk[:, None] == blk[None, :]) & (~first[:, None]) & first[None, :]).astype(np.float32))
        isk.append(first.astype(np.float32)[:, None])
    a = np.stack(a)
    masks = np.stack(masks)
    isk = np.stack(isk)
    flip = lambda m: m[:, ::-1, ::-1]
    a2 = np.stack([a, flip(a)]).reshape(2, -1, c)
    m2 = np.stack([masks, flip(masks)])
    k2 = np.stack([isk, isk[:, ::-1]])
    return jnp.asarray(a2, BF16), jnp.asarray(m2, F32), jnp.asarray(k2, F32)


def _hgrn_chunk(q_ref, f_ref, v_ref, o_ref, lb, a, mk_ref, isk_ref, st_ref, d, r0, *, c, nh):
    nlev = len(_hgrn_levels(c))
    last = c - 1 if d == 0 else 0
    rows = pl.ds(r0, c)
    head = lambda x, h: x[:, h * HG_DK:(h + 1) * HG_DK]
    f = lb + (1.0 - lb) * jax.nn.sigmoid(f_ref[rows, :])
    g_hi, g_lo = _split_bf16(jnp.log(f))
    dd = _dot(a, g_hi) + _dot(a, g_lo)
    q = _silu(q_ref[rows, :])
    k = 1.0 - f
    v = v_ref[rows, :]
    vb = v.astype(BF16)
    b = dd[0:c]
    bl = b[last:last + 1, :]
    qe = (q * jnp.exp(b)).astype(BF16)
    kend = (k * jnp.exp(bl - b)).astype(BF16)
    ebl = jnp.exp(bl)
    st = [st_ref[d, h] for h in range(nh)]
    inter = [_dot_nt(head(qe, h), st[h].astype(BF16)) for h in range(nh)]
    att = [None] * nh
    for l in range(nlev):
        e = jnp.exp(dd[(l + 1) * c:(l + 2) * c])
        x = (e * jnp.where(isk_ref[d, l] > 0.0, k, q)).astype(BF16)
        for h in range(nh):
            t = mk_ref[d, l] * _dot_nt(head(x, h), head(x, h))
            att[h] = t if att[h] is None else att[h] + t
    qk = q * k
    for h in range(nh):
        diag = jnp.sum(head(qk, h), axis=-1, keepdims=True)
        o_ref[rows, h * HG_DK:(h + 1) * HG_DK] = (
            inter[h] + _dot(att[h].astype(BF16), head(vb, h)) + diag * head(v, h))
    for h in range(nh):
        st_ref[d, h] = head(ebl, h) * st[h] + _dot_tn(head(vb, h), head(kend, h))


def _hgrn_kernel(qf_ref, ff_ref, vf_ref, qb_ref, fb_ref, vb_ref, lb_ref, a_ref, mk_ref, isk_ref, s0_ref,
                 of_ref, ob_ref, sout_ref, st_ref, *, tb, c, nh):
    nchunk = tb // c

    @pl.when(pl.program_id(1) == 0)
    def _():
        st_ref[...] = s0_ref[...]

    def body(cc, carry):
        rf = pl.multiple_of(cc * c, c)
        rb = pl.multiple_of((nchunk - 1 - cc) * c, c)
        _hgrn_chunk(qf_ref, ff_ref, vf_ref, of_ref, lb_ref[0], a_ref[0], mk_ref, isk_ref, st_ref, 0, rf, c=c, nh=nh)
        _hgrn_chunk(qb_ref, fb_ref, vb_ref, ob_ref, lb_ref[1], a_ref[1], mk_ref, isk_ref, st_ref, 1, rb, c=c, nh=nh)
        return carry

    lax.fori_loop(0, nchunk, body, 0)

    @pl.when(pl.program_id(1) == pl.num_programs(1) - 1)
    def _():
        sout_ref[...] = st_ref[...]


def _hgrn(p, lb, s0, consts):
    bsz, r, _ = p.shape
    w = lb.shape[-1]
    nh = w // HG_DK
    tb = min(TOK_TILE, r)
    c = HG_CHUNK
    nb = r // tb
    a2, m2, k2 = consts
    nrow = a2.shape[1]
    fwd = lambda col: pl.BlockSpec((None, tb, w), lambda b, s: (b, s, col))
    bwd = lambda col: pl.BlockSpec((None, tb, w), lambda b, s: (b, nb - 1 - s, col))
    const = lambda arr: pl.BlockSpec(arr.shape, lambda b, s: (0,) * arr.ndim)
    st_spec = pl.BlockSpec((None, 2, nh, HG_DK, HG_DK), lambda b, s: (b, 0, 0, 0, 0))
    kern = functools.partial(_hgrn_kernel, tb=tb, c=c, nh=nh)
    return pl.pallas_call(
        kern,
        grid=(bsz, nb),
        in_specs=[fwd(0), fwd(1), fwd(3), bwd(0), bwd(2), bwd(3), const(lb), const(a2), const(m2), const(k2),
                  st_spec],
        out_specs=[
            pl.BlockSpec((None, tb, w), lambda b, s: (b, s, 0)),
            pl.BlockSpec((None, tb, w), lambda b, s: (b, nb - 1 - s, 0)),
            st_spec,
        ],
        out_shape=[
            jax.ShapeDtypeStruct((bsz, r, w), F32),
            jax.ShapeDtypeStruct((bsz, r, w), F32),
            jax.ShapeDtypeStruct(s0.shape, F32),
        ],
        scratch_shapes=[pltpu.VMEM((2, nh, HG_DK, HG_DK), F32)],
        compiler_params=_params("parallel", "arbitrary"),
        name="hgrn2_scan",
    )(p, p, p, p, p, p, lb, a2, m2, k2, s0)


def _even_post_kernel(of_ref, ob_ref, ga_ref, u_ref, gb_ref, up_ref, un_ref, h_ref, mod_ref, hgn_ref, pw_ref,
                      ps_ref, ow_ref, o_ref, ext_ref, y_ref, *, tb, seq, nh):
    j = pl.program_id(1)
    w = nh * HG_DK
    o = of_ref[...] + ob_ref[...]
    for h in range(nh):
        sl = slice(h * HG_DK, (h + 1) * HG_DK)
        y_ref[:, sl] = (_rms(o[:, sl], hgn_ref[:, sl]) * _silu(ga_ref[:, sl])).astype(BF16)
    u = u_ref[...]
    ext_ref[0:POOL_HALO, :] = jnp.where(j > 0, up_ref[...], 0.0)
    ext_ref[POOL_HALO:POOL_HALO + tb, :] = u
    ext_ref[POOL_HALO + tb:, :] = jnp.where(j < pl.num_programs(1) - 1, un_ref[...], 0.0)
    t = j * tb + lax.broadcasted_iota(jnp.int32, (tb, 1), 0)
    grp = w // len(POOL_WINDOWS)
    for gi, win in enumerate(POOL_WINDOWS):
        sl = slice(gi * grp, (gi + 1) * grp)
        acc = ext_ref[POOL_HALO - win // 2:POOL_HALO - win // 2 + tb, sl]
        for off in range(-win // 2 + 1, win // 2):
            acc = acc + ext_ref[POOL_HALO + off:POOL_HALO + off + tb, sl]
        cnt = (jnp.minimum(t + win // 2, seq) - jnp.maximum(t - win // 2, 0)).astype(F32)
        yp = acc / cnt - u[:, sl]
        yb = _dot(yp.astype(BF16), pw_ref[gi]) * ps_ref[:, sl]
        y_ref[:, w + gi * grp:w + (gi + 1) * grp] = (yb * _silu(gb_ref[:, sl])).astype(BF16)
    o_ref[...] = h_ref[...] + mod_ref[2:3, :] * _dot(y_ref[...], ow_ref[...])


def _even_post(o_f, o_b, p, h, mod, hgn, pool_w, pool_scale, out_w):
    bsz, r, w = o_f.shape
    d = h.shape[-1]
    tb = min(TOK_TILE, r)
    nb = r // tb
    hb = tb // POOL_HALO
    nh = w // HG_DK
    tok = lambda col: pl.BlockSpec((None, tb, w), lambda b, j: (b, j, col))
    const = lambda arr: pl.BlockSpec(arr.shape, lambda b, j: (0,) * arr.ndim)
    kern = functools.partial(_even_post_kernel, tb=tb, seq=r, nh=nh)
    return pl.pallas_call(
        kern,
        grid=(bsz, nb),
        in_specs=[
            tok(0), tok(0), tok(4), tok(5), tok(6),
            pl.BlockSpec((None, POOL_HALO, w), lambda b, j: (b, jnp.maximum(j * hb - 1, 0), 5)),
            pl.BlockSpec((None, POOL_HALO, w), lambda b, j: (b, jnp.minimum((j + 1) * hb, nb * hb - 1), 5)),
            pl.BlockSpec((None, tb, d), lambda b, j: (b, j, 0)),
            pl.BlockSpec((None, 3, d), lambda b, j: (b, 0, 0)),
            const(hgn), const(pool_w), const(pool_scale), const(out_w),
        ],
        out_specs=pl.BlockSpec((None, tb, d), lambda b, j: (b, j, 0)),
        out_shape=jax.ShapeDtypeStruct((bsz, r, d), F32),
        scratch_shapes=[pltpu.VMEM((tb + 2 * POOL_HALO, w), F32), pltpu.VMEM((tb, 2 * w), BF16)],
        compiler_params=_params("parallel", "parallel"),
        name="even_post",
    )(o_f, o_b, p, p, p, p, p, h, mod, hgn, pool_w, pool_scale, out_w)


def _mla_kv_kernel(*refs, rope):
    if rope:
        ckv_ref, kr_ref, g_ref, wuk_ref, wuvt_ref, cos_ref, sin_ref, kcat_ref, vt_ref = refs
    else:
        ckv_ref, kr_ref, g_ref, wuk_ref, wuvt_ref, kcat_ref, vt_ref = refs
    cn = _rms(ckv_ref[...], g_ref[...]).astype(BF16)
    kn = _dot(cn, wuk_ref[...])
    kr = kr_ref[...]
    if rope:
        lane = lax.broadcasted_iota(jnp.int32, kr.shape, 1)
        swapped = jnp.where((lane % (2 * ROPE_FREQ)) < ROPE_FREQ,
                            pltpu.roll(kr, LANES - ROPE_FREQ, 1), pltpu.roll(kr, ROPE_FREQ, 1))
        kr = kr * cos_ref[...] + swapped * sin_ref[...]
    kr = kr.astype(BF16)
    ones_rows = (lax.broadcasted_iota(jnp.int32, (VT_ROWS - MLA_V, kr.shape[0]), 0) == 0).astype(BF16)
    for h in range(MLA_HEADS):
        kcat_ref[h, :, 0:MLA_NOPE] = kn[:, h * MLA_NOPE:(h + 1) * MLA_NOPE].astype(BF16)
        kcat_ref[h, :, MLA_NOPE:] = kr
        vt_ref[h, 0:MLA_V, :] = _dot_nt(wuvt_ref[h], cn).astype(BF16)
        vt_ref[h, MLA_V:, :] = ones_rows


def _mla_kv(p, ckv_blk, kr_blk, g, wuk, wuvt, tables, tb):
    bsz, r, _ = p.shape
    rank = g.shape[-1]
    nb = r // tb
    const = lambda arr: pl.BlockSpec(arr.shape, lambda b, j: (0,) * arr.ndim)
    in_specs = [
        pl.BlockSpec((None, tb, rank), lambda b, j: (b, j, ckv_blk)),
        pl.BlockSpec((None, tb, LANES), lambda b, j: (b, j, kr_blk)),
        const(g), const(wuk), const(wuvt),
    ]
    args = [p, p, g, wuk, wuvt]
    if tables is not None:
        in_specs += [pl.BlockSpec((tb, LANES), lambda b, j: (j, 0))] * 2
        args += list(tables)
    return pl.pallas_call(
        functools.partial(_mla_kv_kernel, rope=tables is not None),
        grid=(bsz, nb),
        in_specs=in_specs,
        out_specs=[
            pl.BlockSpec((None, MLA_HEADS, None, tb, QK_PAD), lambda b, j: (b, 0, j, 0, 0)),
            pl.BlockSpec((None, MLA_HEADS, None, VT_ROWS, tb), lambda b, j: (b, 0, j, 0, 0)),
        ],
        out_shape=[
            jax.ShapeDtypeStruct((bsz, MLA_HEADS, nb, tb, QK_PAD), BF16),
            jax.ShapeDtypeStruct((bsz, MLA_HEADS, nb, VT_ROWS, tb), BF16),
        ],
        compiler_params=_params("parallel", "parallel"),
        name="mla_kv_rope" if tables is not None else "mla_kv",
    )(*args)


def _mla_q_kernel(cq_ref, g_ref, wqt_ref, cos_ref, sin_ref, qt_ref):
    cn = _rms(cq_ref[...], g_ref[...]).astype(BF16)
    f = ROPE_FREQ
    for h in range(MLA_HEADS):
        qt = _dot_nt(wqt_ref[h], cn) * (MLA_SCALE * LOG2_E)
        qt_ref[h, 0:MLA_NOPE, :] = qt[0:MLA_NOPE].astype(BF16)
        for ax in range(2):
            r0 = MLA_NOPE + ax * 2 * f
            x1 = qt[r0:r0 + f]
            x2 = qt[r0 + f:r0 + 2 * f]
            co = cos_ref[ax]
            si = sin_ref[ax]
            qt_ref[h, r0:r0 + f, :] = (x1 * co - x2 * si).astype(BF16)
            qt_ref[h, r0 + f:r0 + 2 * f, :] = (x2 * co + x1 * si).astype(BF16)
        qt_ref[h, MLA_QK:, :] = jnp.zeros((QK_PAD - MLA_QK, cn.shape[0]), BF16)


def _mla_q(p, cq_blk, g, wqt, cos_t, sin_t):
    bsz, t, _ = p.shape
    rank = g.shape[-1]
    tm = min(TOK_TILE, t)
    const = lambda arr: pl.BlockSpec(arr.shape, lambda b, j: (0,) * arr.ndim)
    tab = pl.BlockSpec((2, ROPE_FREQ, tm), lambda b, j: (0, 0, j))
    return pl.pallas_call(
        _mla_q_kernel,
        grid=(bsz, t // tm),
        in_specs=[pl.BlockSpec((None, tm, rank), lambda b, j: (b, j, cq_blk)), const(g), const(wqt), tab, tab],
        out_specs=pl.BlockSpec((None, MLA_HEADS, QK_PAD, tm), lambda b, j: (b, 0, 0, j)),
        out_shape=jax.ShapeDtypeStruct((bsz, MLA_HEADS, QK_PAD, t), BF16),
        compiler_params=_params("parallel", "parallel"),
        name="mla_q",
    )(p, g, wqt, cos_t, sin_t)


def _attn_kernel(qt_ref, kc_ref, vtc_ref, kl_ref, vtl_ref, g_ref, o_ref, m_ref, acc_ref, s_ref, mx_ref, *, tq, n_lat):
    nsub = tq // Q_SUB
    m_ref[...] = jnp.full(m_ref.shape, -jnp.inf, F32)
    acc_ref[...] = jnp.zeros(acc_ref.shape, F32)

    def scores(k, nxt, g):
        s = _dot(k, qt_ref[:, g * Q_SUB:(g + 1) * Q_SUB])
        s_ref[nxt, g, 0:k.shape[0], :] = s
        mx_ref[nxt, g] = jnp.max(s, axis=0, keepdims=True)

    def substep(k_next, vt_cur, cur, nxt):
        rows = vt_cur.shape[1]
        for g in range(nsub):
            sl = slice(g * Q_SUB, (g + 1) * Q_SUB)
            scores(k_next, nxt, g)
            m_old = m_ref[:, sl]
            m_new = jnp.maximum(m_old, mx_ref[cur, g])
            alpha = jnp.exp2(m_old - m_new)
            p = jnp.exp2(s_ref[cur, g, 0:rows, :] - m_new)
            acc_ref[:, sl] = alpha * acc_ref[:, sl] + _dot(vt_cur, p.astype(BF16))
            m_ref[:, sl] = m_new

    kc = kc_ref[...]
    for g in range(nsub):
        scores(kc, 0, g)
    substep(kl_ref[0], vtc_ref[...], 0, 1)

    def body(j, carry):
        for u in range(2):
            a = 2 * j + u
            substep(kl_ref[jnp.minimum(a + 1, n_lat - 1)], vtl_ref[a], (1 + u) % 2, u % 2)
        return carry

    lax.fori_loop(0, n_lat // 2, body, 0)
    o = (acc_ref[0:MLA_V, :] * (1.0 / acc_ref[MLA_V:MLA_V + 1, :])).T
    o_ref[...] = (o * _silu(g_ref[...])).astype(o_ref.dtype)


def _attn(qt, kc, vtc, kl, vtl, p):
    bsz, nh, _, t = qt.shape
    lc = kc.shape[3]
    tq = min(Q_TILE, t)
    n_lat = kl.shape[2]
    kv = kl.shape[3]
    assert kc.shape[2] == 1 and lc <= kv and n_lat % 2 == 0
    kern = functools.partial(_attn_kernel, tq=tq, n_lat=n_lat)
    ctx5 = lambda arr: pl.BlockSpec((None, None, None) + arr.shape[3:], lambda b, h, i: (b, h, 0, 0, 0))
    full5 = lambda arr: pl.BlockSpec((None, None) + arr.shape[2:], lambda b, h, i: (b, h, 0, 0, 0))
    return pl.pallas_call(
        kern,
        grid=(bsz, nh, t // tq),
        in_specs=[
            pl.BlockSpec((None, None, QK_PAD, tq), lambda b, h, i: (b, h, 0, i)),
            ctx5(kc), ctx5(vtc), full5(kl), full5(vtl),
            pl.BlockSpec((None, tq, MLA_V), lambda b, h, i: (b, i, h)),
        ],
        out_specs=pl.BlockSpec((None, tq, MLA_V), lambda b, h, i: (b, i, h)),
        out_shape=jax.ShapeDtypeStruct((bsz, t, nh * MLA_V), BF16),
        scratch_shapes=[pltpu.VMEM((1, tq), F32), pltpu.VMEM((VT_ROWS, tq), F32),
                        pltpu.VMEM((2, tq // Q_SUB, kv, Q_SUB), F32), pltpu.VMEM((2, tq // Q_SUB, 1, Q_SUB), F32)],
        compiler_params=_params("parallel", "parallel", "arbitrary"),
        name="mla_attention",
    )(qt, kc, vtc, kl, vtl, p)


def _out_final_kernel(y_ref, h_ref, mod_ref, ow_ref, g_ref, o_ref):
    hn = h_ref[...] + mod_ref[2:3, :] * _dot(y_ref[...], ow_ref[...])
    o_ref[...] = _rms(hn, g_ref[...])


def _out_final(y, h, mod, out_w, g):
    bsz, t, d = h.shape
    wi = y.shape[-1]
    tm = min(2 * TOK_TILE, t)
    return pl.pallas_call(
        _out_final_kernel,
        grid=(bsz, t // tm),
        in_specs=[
            pl.BlockSpec((None, tm, wi), lambda b, j: (b, j, 0)),
            pl.BlockSpec((None, tm, d), lambda b, j: (b, j, 0)),
            pl.BlockSpec((None, 3, d), lambda b, j: (b, 0, 0)),
            pl.BlockSpec((wi, d), lambda b, j: (0, 0)),
            pl.BlockSpec((1, d), lambda b, j: (0, 0)),
        ],
        out_specs=pl.BlockSpec((None, tm, d), lambda b, j: (b, j, 0)),
        out_shape=jax.ShapeDtypeStruct((bsz, t, d), F32),
        compiler_params=_params("parallel", "parallel"),
        name="out_final",
    )(y, h, mod, out_w, g)


def _rope_tables(n_tokens):
    rows = n_tokens // GRID_W
    pos_r = jnp.repeat(jnp.arange(rows), GRID_W).astype(F32)
    pos_c = jnp.tile(jnp.arange(GRID_W), rows).astype(F32)
    inv = ROPE_BASE ** (-2.0 * jnp.arange(ROPE_FREQ, dtype=F32) / (MLA_ROPE // 2))
    ang = jnp.stack([pos_r[:, None] * inv, pos_c[:, None] * inv], axis=1)
    cos, sin = jnp.cos(ang), jnp.sin(ang)
    pad = LANES - MLA_ROPE
    cos_k = jnp.pad(jnp.stack([cos, cos], axis=2).reshape(n_tokens, MLA_ROPE), ((0, 0), (0, pad)))
    sin_k = jnp.pad(jnp.stack([-sin, sin], axis=2).reshape(n_tokens, MLA_ROPE), ((0, 0), (0, pad)))
    cos_q = jnp.transpose(cos, (1, 2, 0))
    sin_q = jnp.transpose(sin, (1, 2, 0))
    return (cos_k, sin_k), (cos_q, sin_q)


def kernel(x, c, ctx, c_ctx, ada_w, ada_b, norm_g, out_w, ev_in_w, hg_lb, hg_norm_g, pool_w, pool_scale,
           od_in_w, qa_norm_g, qb_w, kva_norm_g, kvb_w, final_norm_g):
    bsz, t, d = x.shape
    lc = ctx.shape[1]
    depth = ada_w.shape[0]
    assert depth == 2 and t % (2 * TOK_TILE) == 0 and lc % TOK_TILE == 0 and t % GRID_W == 0
    w = hg_norm_g.shape[-1]
    nh = w // HG_DK
    q_rank = qa_norm_g.shape[-1]
    kv_rank = kva_norm_g.shape[-1]
    d_inner = out_w.shape[1]

    n_cond = -(-(bsz + 1) // SUBLANES) * SUBLANES
    cond = jnp.zeros((n_cond, d), F32).at[:bsz].set(c).at[bsz].set(c_ctx)
    mods = _ada(cond, ada_w, ada_b).reshape(depth, n_cond, 3, d)
    mod_l = [mods[l, :bsz] for l in range(depth)]
    mod_c = [mods[l, bsz:bsz + 1] for l in range(depth)]

    lb = jnp.cumsum(jax.nn.softmax(hg_lb.astype(F32), axis=1), axis=1)[:, 0].reshape(2, 1, w)
    w_in0 = ev_in_w[0].astype(BF16)
    g0 = norm_g[0].reshape(1, d)
    ctx_flat = ctx.reshape(1, bsz * lc, d)
    n_in0 = w_in0.shape[1]
    p_c = _modnorm_mm(ctx_flat, mod_c[0], g0, w_in0, 4 * TOK_TILE, n_in0 // 4, "in_proj0_ctx").reshape(bsz, lc, n_in0)
    p_l = _modnorm_mm(x, mod_l[0], g0, w_in0, 4 * TOK_TILE, n_in0 // 4, "in_proj0")
    consts = _hgrn_constants(HG_CHUNK)
    s0 = jnp.zeros((bsz, 2, nh, HG_DK, HG_DK), F32)
    of_c, ob_c, s_c = _hgrn(p_c, lb, s0, consts)
    of_l, ob_l, _ = _hgrn(p_l, lb, s_c, consts)
    hgn = hg_norm_g[0].reshape(1, w)
    pw = pool_w[0].astype(BF16)
    ps = pool_scale[0].reshape(1, w)
    ow0 = out_w[0].astype(BF16)
    mod_c0 = jnp.broadcast_to(mod_c[0], (bsz, 3, d))
    hc1 = _even_post(of_c, ob_c, p_c, ctx, mod_c0, hgn, pw, ps, ow0)
    hl1 = _even_post(of_l, ob_l, p_l, x, mod_l[0], hgn, pw, ps, ow0)

    o1 = q_rank
    o2 = o1 + kv_rank
    o3 = o2 + MLA_ROPE
    w1 = od_in_w[0]
    kr_pad = jnp.zeros((d, LANES - MLA_ROPE), F32)
    w_in1 = jnp.concatenate([w1[:, o3:], w1[:, :o1], w1[:, o1:o2], w1[:, o2:o3], kr_pad], axis=1).astype(BF16)
    n_kv = kv_rank + LANES
    w_in1c = w_in1[:, d_inner + q_rank:]
    g1 = norm_g[1].reshape(1, d)
    p1_c = _modnorm_mm(hc1.reshape(1, bsz * lc, d), mod_c[1], g1, w_in1c, 4 * TOK_TILE, n_kv, "in_proj1_ctx")
    p1_c = p1_c.reshape(bsz, lc, n_kv)
    p1_l = _modnorm_mm(hl1, mod_l[1], g1, w_in1, 2 * TOK_TILE, w_in1.shape[1], "in_proj1")

    kvw = kvb_w[0].reshape(kv_rank, MLA_HEADS, MLA_NOPE + MLA_V)
    wuk = kvw[..., :MLA_NOPE].reshape(kv_rank, MLA_HEADS * MLA_NOPE).astype(BF16)
    wuvt = jnp.transpose(kvw[..., MLA_NOPE:], (1, 2, 0)).astype(BF16)
    wqt = jnp.transpose(qb_w[0].reshape(q_rank, MLA_HEADS, MLA_QK), (1, 2, 0)).astype(BF16)
    kvg = kva_norm_g[0].reshape(1, kv_rank)
    qag = qa_norm_g[0].reshape(1, q_rank)
    tab_k, tab_q = _rope_tables(t)
    kc, vtc = _mla_kv(p1_c, 0, kv_rank // LANES, kvg, wuk, wuvt, None, lc)
    kl, vtl = _mla_kv(p1_l, (d_inner + q_rank) // kv_rank, (d_inner + q_rank + kv_rank) // LANES, kvg, wuk, wuvt,
                      tab_k, min(KV_CHUNK, t))
    qt = _mla_q(p1_l, d_inner // q_rank, qag, wqt, *tab_q)
    y = _attn(qt, kc, vtc, kl, vtl, p1_l)
    return _out_final(y, hl1, mod_l[1], out_w[1].astype(BF16), final_norm_g.reshape(1, d))
```

```python
import functools

import numpy as np
import jax
import jax.numpy as jnp
from jax import lax
from jax.experimental import pallas as pl
from jax.experimental.pallas import tpu as pltpu

F32 = jnp.float32
BF16 = jnp.bfloat16

EPS = 1e-6
GRID_W = 64
HG_DK = 128
POOL_WINDOWS = (2, 4, 8, 16)
MLA_HEADS = 16
MLA_NOPE = 128
MLA_ROPE = 64
MLA_V = 128
MLA_QK = MLA_NOPE + MLA_ROPE
QK_PAD = 256
VT_ROWS = MLA_V + 16
MLA_SCALE = MLA_QK ** -0.5
LOG2_E = 1.4426950408889634
ROPE_FREQ = MLA_ROPE // 4
ROPE_BASE = 10000.0

LANES = 128
SUBLANES = 8
VMEM_LIMIT = 48 * 1024 * 1024

HG_CHUNK = 64
TOK_TILE = 256
KV_CHUNK = 512
Q_TILE = 2048
Q_SUB = 256
POOL_HALO = 8


def _dot(a, b):
    return jnp.dot(a, b, preferred_element_type=F32)


def _dot_nt(a, b):
    return lax.dot_general(a, b, (((1,), (1,)), ((), ())), preferred_element_type=F32)


def _dot_tn(a, b):
    return lax.dot_general(a, b, (((0,), (0,)), ((), ())), preferred_element_type=F32)


def _sigmoid(x):
    return 0.5 * jnp.tanh(0.5 * x) + 0.5


def _silu(x):
    h = 0.5 * x
    return h + h * jnp.tanh(h)


def _split_bf16(x):
    hi = x.astype(BF16)
    lo = (x - hi.astype(F32)).astype(BF16)
    return hi, lo


def _params(*sem):
    return pltpu.CompilerParams(dimension_semantics=sem, vmem_limit_bytes=VMEM_LIMIT)


def _ada_kernel(c_ref, w_ref, b_ref, o_ref):
    c = c_ref[...]
    s_hi, s_lo = _split_bf16(_silu(c))
    w_hi, w_lo = _split_bf16(w_ref[...])
    o_ref[...] = _dot(s_hi, w_hi) + _dot(s_lo, w_hi) + _dot(s_hi, w_lo) + b_ref[...]


def _ada(cond, ada_w, ada_b):
    depth, d, _ = ada_w.shape
    r = cond.shape[0]
    return pl.pallas_call(
        _ada_kernel,
        grid=(depth, 3),
        in_specs=[
            pl.BlockSpec((r, d), lambda l, j: (0, 0)),
            pl.BlockSpec((None, d, d), lambda l, j: (l, 0, j)),
            pl.BlockSpec((None, 1, d), lambda l, j: (l, 0, j)),
        ],
        out_specs=pl.BlockSpec((None, r, d), lambda l, j: (l, 0, j)),
        out_shape=jax.ShapeDtypeStruct((depth, r, 3 * d), F32),
        compiler_params=_params("parallel", "parallel"),
        name="ada_modulation",
    )(cond, ada_w, ada_b.reshape(depth, 1, 3 * d))


def _rms(x, g):
    return x * lax.rsqrt(jnp.mean(x * x, axis=-1, keepdims=True) + EPS) * g


def _modnorm_mm_kernel(x_ref, mod_ref, g_ref, w_ref, o_ref, z_ref):
    @pl.when(pl.program_id(2) == 0)
    def _():
        y = _rms(x_ref[...], g_ref[...])
        z_ref[...] = (y * (1.0 + mod_ref[1:2, :]) + mod_ref[0:1, :]).astype(BF16)

    o_ref[...] = _dot(z_ref[...], w_ref[...]).astype(o_ref.dtype)


def _modnorm_mm(x, mod, g, w, tm, tn, name):
    bx, r, d = x.shape
    n = w.shape[1]
    tm = min(tm, r)
    return pl.pallas_call(
        _modnorm_mm_kernel,
        grid=(bx, r // tm, n // tn),
        in_specs=[
            pl.BlockSpec((None, tm, d), lambda b, i, j: (b, i, 0)),
            pl.BlockSpec((None, 3, d), lambda b, i, j: (b, 0, 0)),
            pl.BlockSpec((1, d), lambda b, i, j: (0, 0)),
            pl.BlockSpec((d, tn), lambda b, i, j: (0, j)),
        ],
        out_specs=pl.BlockSpec((None, tm, tn), lambda b, i, j: (b, i, j)),
        out_shape=jax.ShapeDtypeStruct((bx, r, n), F32),
        scratch_shapes=[pltpu.VMEM((tm, d), BF16)],
        compiler_params=_params("parallel", "parallel", "arbitrary"),
        name=name,
    )(x, mod, g, w)


def _hgrn_levels(c):
    w = c // 2
    out = []
    while w >= 1:
        out.append(w)
        w //= 2
    return tuple(out)


def _hgrn_constants(c):
    t = np.arange(c)
    u = t[None, :]
    tt = t[:, None]
    a = [np.tril(np.ones((c, c), np.float32))]
    masks, isk = [], []
    for w in _hgrn_levels(c):
        blk = t // (2 * w)
        first = (t % (2 * w)) < w
        ref = (blk * 2 * w + w - 1)[:, None]
        a.append(np.where(first[:, None], (u > tt) & (u <= ref), (u > ref) & (u <= tt)).astype(np.float32))
        masks.append(((blk[:, None] == blk[None, :]) & (~first[:, None]) & first[None, :]).astype(np.float32))
        isk.append(first.astype(np.float32)[:, None])
    a = np.stack(a)
    masks = np.stack(masks)
    isk = np.stack(isk)
    flip = lambda m: m[:, ::-1, ::-1]
    a2 = np.stack([a, flip(a)]).reshape(2, -1, c)
    m2 = np.stack([masks, flip(masks)])
    k2 = np.stack([isk, isk[:, ::-1]])
    return jnp.asarray(a2, BF16), jnp.asarray(m2, F32), jnp.asarray(k2, F32)


def _hgrn_chunk(q_ref, f_ref, v_ref, o_ref, lb, a, mk_ref, isk_ref, st_ref, d, r0, *, c, nh):
    nlev = len(_hgrn_levels(c))
    last = c - 1 if d == 0 else 0
    rows = pl.ds(r0, c)
    head = lambda x, h: x[:, h * HG_DK:(h + 1) * HG_DK]
    f = lb + (1.0 - lb) * _sigmoid(f_ref[rows, :])
    g_hi, g_lo = _split_bf16(jnp.log(f))
    dd = _dot(a, g_hi) + _dot(a, g_lo)
    q = _silu(q_ref[rows, :])
    k = 1.0 - f
    v = v_ref[rows, :]
    vb = v.astype(BF16)
    b = dd[0:c]
    bl = b[last:last + 1, :]
    qe = (q * jnp.exp(b)).astype(BF16)
    kend = (k * jnp.exp(bl - b)).astype(BF16)
    ebl = jnp.exp(bl)
    st = [st_ref[d, h] for h in range(nh)]
    inter = [_dot_nt(head(qe, h), st[h].astype(BF16)) for h in range(nh)]
    att = [None] * nh
    for l in range(nlev):
        e = jnp.exp(dd[(l + 1) * c:(l + 2) * c])
        x = (e * jnp.where(isk_ref[d, l] > 0.0, k, q)).astype(BF16)
        for h in range(nh):
            t = mk_ref[d, l] * _dot_nt(head(x, h), head(x, h))
            att[h] = t if att[h] is None else att[h] + t
    qk = q * k
    for h in range(nh):
        diag = jnp.sum(head(qk, h), axis=-1, keepdims=True)
        o_ref[rows, h * HG_DK:(h + 1) * HG_DK] = (
            inter[h] + _dot(att[h].astype(BF16), head(vb, h)) + diag * head(v, h))
    for h in range(nh):
        st_ref[d, h] = head(ebl, h) * st[h] + _dot_tn(head(vb, h), head(kend, h))


def _hgrn_kernel(qf_ref, ff_ref, vf_ref, qb_ref, fb_ref, vb_ref, lb_ref, a_ref, mk_ref, isk_ref, s0_ref,
                 of_ref, ob_ref, sout_ref, st_ref, *, tb, c, nh):
    nchunk = tb // c

    @pl.when(pl.program_id(1) == 0)
    def _():
        st_ref[...] = s0_ref[...]

    def body(cc, carry):
        rf = pl.multiple_of(cc * c, c)
        rb = pl.multiple_of((nchunk - 1 - cc) * c, c)
        _hgrn_chunk(qf_ref, ff_ref, vf_ref, of_ref, lb_ref[0], a_ref[0], mk_ref, isk_ref, st_ref, 0, rf, c=c, nh=nh)
        _hgrn_chunk(qb_ref, fb_ref, vb_ref, ob_ref, lb_ref[1], a_ref[1], mk_ref, isk_ref, st_ref, 1, rb, c=c, nh=nh)
        return carry

    lax.fori_loop(0, nchunk, body, 0)

    @pl.when(pl.program_id(1) == pl.num_programs(1) - 1)
    def _():
        sout_ref[...] = st_ref[...]


def _hgrn(p, lb, s0, consts):
    bsz, r, _ = p.shape
    w = lb.shape[-1]
    nh = w // HG_DK
    tb = min(TOK_TILE, r)
    c = HG_CHUNK
    nb = r // tb
    a2, m2, k2 = consts
    nrow = a2.shape[1]
    fwd = lambda col: pl.BlockSpec((None, tb, w), lambda b, s: (b, s, col))
    bwd = lambda col: pl.BlockSpec((None, tb, w), lambda b, s: (b, nb - 1 - s, col))
    const = lambda arr: pl.BlockSpec(arr.shape, lambda b, s: (0,) * arr.ndim)
    st_spec = pl.BlockSpec((None, 2, nh, HG_DK, HG_DK), lambda b, s: (b, 0, 0, 0, 0))
    kern = functools.partial(_hgrn_kernel, tb=tb, c=c, nh=nh)
    return pl.pallas_call(
        kern,
        grid=(bsz, nb),
        in_specs=[fwd(0), fwd(1), fwd(3), bwd(0), bwd(2), bwd(3), const(lb), const(a2), const(m2), const(k2),
                  st_spec],
        out_specs=[
            pl.BlockSpec((None, tb, w), lambda b, s: (b, s, 0)),
            pl.BlockSpec((None, tb, w), lambda b, s: (b, nb - 1 - s, 0)),
            st_spec,
        ],
        out_shape=[
            jax.ShapeDtypeStruct((bsz, r, w), F32),
            jax.ShapeDtypeStruct((bsz, r, w), F32),
            jax.ShapeDtypeStruct(s0.shape, F32),
        ],
        scratch_shapes=[pltpu.VMEM((2, nh, HG_DK, HG_DK), F32)],
        compiler_params=_params("parallel", "arbitrary"),
        name="hgrn2_scan",
    )(p, p, p, p, p, p, lb, a2, m2, k2, s0)


def _even_post_kernel(of_ref, ob_ref, ga_ref, u_ref, gb_ref, up_ref, un_ref, h_ref, mod_ref, hgn_ref, pw_ref,
                      ps_ref, ow_ref, o_ref, ext_ref, y_ref, *, tb, seq, nh):
    j = pl.program_id(1)
    w = nh * HG_DK
    o = of_ref[...] + ob_ref[...]
    for h in range(nh):
        sl = slice(h * HG_DK, (h + 1) * HG_DK)
        y_ref[:, sl] = (_rms(o[:, sl], hgn_ref[:, sl]) * _silu(ga_ref[:, sl])).astype(BF16)
    u = u_ref[...]
    ext_ref[0:POOL_HALO, :] = jnp.where(j > 0, up_ref[...], 0.0)
    ext_ref[POOL_HALO:POOL_HALO + tb, :] = u
    ext_ref[POOL_HALO + tb:, :] = jnp.where(j < pl.num_programs(1) - 1, un_ref[...], 0.0)
    t = j * tb + lax.broadcasted_iota(jnp.int32, (tb, 1), 0)
    grp = w // len(POOL_WINDOWS)
    for gi, win in enumerate(POOL_WINDOWS):
        sl = slice(gi * grp, (gi + 1) * grp)
        acc = ext_ref[POOL_HALO - win // 2:POOL_HALO - win // 2 + tb, sl]
        for off in range(-win // 2 + 1, win // 2):
            acc = acc + ext_ref[POOL_HALO + off:POOL_HALO + off + tb, sl]
        cnt = (jnp.minimum(t + win // 2, seq) - jnp.maximum(t - win // 2, 0)).astype(F32)
        yp = acc * (1.0 / cnt) - u[:, sl]
        yb = _dot(yp.astype(BF16), pw_ref[gi]) * ps_ref[:, sl]
        y_ref[:, w + gi * grp:w + (gi + 1) * grp] = (yb * _silu(gb_ref[:, sl])).astype(BF16)
    o_ref[...] = h_ref[...] + mod_ref[2:3, :] * _dot(y_ref[...], ow_ref[...])


def _even_post(o_f, o_b, p, h, mod, hgn, pool_w, pool_scale, out_w):
    bsz, r, w = o_f.shape
    d = h.shape[-1]
    tb = min(TOK_TILE, r)
    nb = r // tb
    hb = tb // POOL_HALO
    nh = w // HG_DK
    tok = lambda col: pl.BlockSpec((None, tb, w), lambda b, j: (b, j, col))
    const = lambda arr: pl.BlockSpec(arr.shape, lambda b, j: (0,) * arr.ndim)
    kern = functools.partial(_even_post_kernel, tb=tb, seq=r, nh=nh)
    return pl.pallas_call(
        kern,
        grid=(bsz, nb),
        in_specs=[
            tok(0), tok(0), tok(4), tok(5), tok(6),
            pl.BlockSpec((None, POOL_HALO, w), lambda b, j: (b, jnp.maximum(j * hb - 1, 0), 5)),
            pl.BlockSpec((None, POOL_HALO, w), lambda b, j: (b, jnp.minimum((j + 1) * hb, nb * hb - 1), 5)),
            pl.BlockSpec((None, tb, d), lambda b, j: (b, j, 0)),
            pl.BlockSpec((None, 3, d), lambda b, j: (b, 0, 0)),
            const(hgn), const(pool_w), const(pool_scale), const(out_w),
        ],
        out_specs=pl.BlockSpec((None, tb, d), lambda b, j: (b, j, 0)),
        out_shape=jax.ShapeDtypeStruct((bsz, r, d), F32),
        scratch_shapes=[pltpu.VMEM((tb + 2 * POOL_HALO, w), F32), pltpu.VMEM((tb, 2 * w), BF16)],
        compiler_params=_params("parallel", "parallel"),
        name="even_post",
    )(o_f, o_b, p, p, p, p, p, h, mod, hgn, pool_w, pool_scale, out_w)


def _mla_kv_kernel(*refs, rope):
    if rope:
        ckv_ref, kr_ref, g_ref, wuk_ref, wuvt_ref, cos_ref, sin_ref, kcat_ref, vt_ref = refs
    else:
        ckv_ref, kr_ref, g_ref, wuk_ref, wuvt_ref, kcat_ref, vt_ref = refs
    cn = _rms(ckv_ref[...], g_ref[...]).astype(BF16)
    kn = _dot(cn, wuk_ref[...])
    kr = kr_ref[...]
    if rope:
        lane = lax.broadcasted_iota(jnp.int32, kr.shape, 1)
        swapped = jnp.where((lane % (2 * ROPE_FREQ)) < ROPE_FREQ,
                            pltpu.roll(kr, LANES - ROPE_FREQ, 1), pltpu.roll(kr, ROPE_FREQ, 1))
        kr = kr * cos_ref[...] + swapped * sin_ref[...]
    kr = kr.astype(BF16)
    ones_rows = (lax.broadcasted_iota(jnp.int32, (VT_ROWS - MLA_V, kr.shape[0]), 0) == 0).astype(BF16)
    vt = _dot_nt(wuvt_ref[...], cn)
    for h in range(MLA_HEADS):
        kcat_ref[h, :, 0:MLA_NOPE] = kn[:, h * MLA_NOPE:(h + 1) * MLA_NOPE].astype(BF16)
        kcat_ref[h, :, MLA_NOPE:] = kr
        vt_ref[h, 0:MLA_V, :] = vt[h * MLA_V:(h + 1) * MLA_V].astype(BF16)
        vt_ref[h, MLA_V:, :] = ones_rows


def _mla_kv(p, ckv_blk, kr_blk, g, wuk, wuvt, tables, tb):
    bsz, r, _ = p.shape
    rank = g.shape[-1]
    nb = r // tb
    const = lambda arr: pl.BlockSpec(arr.shape, lambda b, j: (0,) * arr.ndim)
    in_specs = [
        pl.BlockSpec((None, tb, rank), lambda b, j: (b, j, ckv_blk)),
        pl.BlockSpec((None, tb, LANES), lambda b, j: (b, j, kr_blk)),
        const(g), const(wuk), const(wuvt),
    ]
    args = [p, p, g, wuk, wuvt]
    if tables is not None:
        in_specs += [pl.BlockSpec((tb, LANES), lambda b, j: (j, 0))] * 2
        args += list(tables)
    return pl.pallas_call(
        functools.partial(_mla_kv_kernel, rope=tables is not None),
        grid=(bsz, nb),
        in_specs=in_specs,
        out_specs=[
            pl.BlockSpec((None, MLA_HEADS, None, tb, QK_PAD), lambda b, j: (b, 0, j, 0, 0)),
            pl.BlockSpec((None, MLA_HEADS, None, VT_ROWS, tb), lambda b, j: (b, 0, j, 0, 0)),
        ],
        out_shape=[
            jax.ShapeDtypeStruct((bsz, MLA_HEADS, nb, tb, QK_PAD), BF16),
            jax.ShapeDtypeStruct((bsz, MLA_HEADS, nb, VT_ROWS, tb), BF16),
        ],
        compiler_params=_params("parallel", "parallel"),
        name="mla_kv_rope" if tables is not None else "mla_kv",
    )(*args)


def _mla_q_kernel(cq_ref, g_ref, wqt_ref, cos_ref, sin_ref, qt_ref):
    cn = _rms(cq_ref[...], g_ref[...]).astype(BF16)
    f = ROPE_FREQ
    qt_all = _dot_nt(wqt_ref[...], cn) * (MLA_SCALE * LOG2_E)
    for h in range(MLA_HEADS):
        qt = qt_all[h * MLA_QK:(h + 1) * MLA_QK]
        qt_ref[h, 0:MLA_NOPE, :] = qt[0:MLA_NOPE].astype(BF16)
        for ax in range(2):
            r0 = MLA_NOPE + ax * 2 * f
            x1 = qt[r0:r0 + f]
            x2 = qt[r0 + f:r0 + 2 * f]
            co = cos_ref[ax]
            si = sin_ref[ax]
            qt_ref[h, r0:r0 + f, :] = (x1 * co - x2 * si).astype(BF16)
            qt_ref[h, r0 + f:r0 + 2 * f, :] = (x2 * co + x1 * si).astype(BF16)
        qt_ref[h, MLA_QK:, :] = jnp.zeros((QK_PAD - MLA_QK, cn.shape[0]), BF16)


def _mla_q(p, cq_blk, g, wqt, cos_t, sin_t):
    bsz, t, _ = p.shape
    rank = g.shape[-1]
    tm = min(TOK_TILE, t)
    const = lambda arr: pl.BlockSpec(arr.shape, lambda b, j: (0,) * arr.ndim)
    tab = pl.BlockSpec((2, ROPE_FREQ, tm), lambda b, j: (0, 0, j))
    return pl.pallas_call(
        _mla_q_kernel,
        grid=(bsz, t // tm),
        in_specs=[pl.BlockSpec((None, tm, rank), lambda b, j: (b, j, cq_blk)), const(g), const(wqt), tab, tab],
        out_specs=pl.BlockSpec((None, MLA_HEADS, QK_PAD, tm), lambda b, j: (b, 0, 0, j)),
        out_shape=jax.ShapeDtypeStruct((bsz, MLA_HEADS, QK_PAD, t), BF16),
        compiler_params=_params("parallel", "parallel"),
        name="mla_q",
    )(p, g, wqt, cos_t, sin_t)


def _attn_kernel(qt_ref, kc_ref, vtc_ref, kl_ref, vtl_ref, g_ref, o_ref, m_ref, acc_ref, s_ref, mx_ref, *, tq, n_lat):
    nsub = tq // Q_SUB
    m_ref[...] = jnp.full(m_ref.shape, -jnp.inf, F32)
    acc_ref[...] = jnp.zeros(acc_ref.shape, F32)

    def scores(k, nxt, g):
        s = _dot(k, qt_ref[:, g * Q_SUB:(g + 1) * Q_SUB])
        s_ref[nxt, g, 0:k.shape[0], :] = s
        mx_ref[nxt, g] = jnp.max(s, axis=0, keepdims=True)

    def substep(k_next, vt_cur, cur, nxt):
        rows = vt_cur.shape[1]
        for g in range(nsub):
            sl = slice(g * Q_SUB, (g + 1) * Q_SUB)
            scores(k_next, nxt, g)
            m_old = m_ref[:, sl]
            m_new = jnp.maximum(m_old, mx_ref[cur, g])
            alpha = jnp.exp2(m_old - m_new)
            p = jnp.exp2(s_ref[cur, g, 0:rows, :] - m_new)
            acc_ref[:, sl] = alpha * acc_ref[:, sl] + _dot(vt_cur, p.astype(BF16))
            m_ref[:, sl] = m_new

    kc = kc_ref[...]
    for g in range(nsub):
        scores(kc, 0, g)
    substep(kl_ref[0], vtc_ref[...], 0, 1)

    def body(j, carry):
        for u in range(2):
            a = 2 * j + u
            substep(kl_ref[jnp.minimum(a + 1, n_lat - 1)], vtl_ref[a], (1 + u) % 2, u % 2)
        return carry

    lax.fori_loop(0, n_lat // 2, body, 0)
    o = (acc_ref[0:MLA_V, :] * (1.0 / acc_ref[MLA_V:MLA_V + 1, :])).T
    o_ref[...] = (o * _silu(g_ref[...])).astype(o_ref.dtype)


def _attn(qt, kc, vtc, kl, vtl, p):
    bsz, nh, _, t = qt.shape
    lc = kc.shape[3]
    tq = min(Q_TILE, t)
    n_lat = kl.shape[2]
    kv = kl.shape[3]
    assert kc.shape[2] == 1 and lc <= kv and n_lat % 2 == 0
    kern = functools.partial(_attn_kernel, tq=tq, n_lat=n_lat)
    ctx5 = lambda arr: pl.BlockSpec((None, None, None) + arr.shape[3:], lambda b, h, i: (b, h, 0, 0, 0))
    full5 = lambda arr: pl.BlockSpec((None, None) + arr.shape[2:], lambda b, h, i: (b, h, 0, 0, 0))
    return pl.pallas_call(
        kern,
        grid=(bsz, nh, t // tq),
        in_specs=[
            pl.BlockSpec((None, None, QK_PAD, tq), lambda b, h, i: (b, h, 0, i)),
            ctx5(kc), ctx5(vtc), full5(kl), full5(vtl),
            pl.BlockSpec((None, tq, MLA_V), lambda b, h, i: (b, i, h)),
        ],
        out_specs=pl.BlockSpec((None, tq, MLA_V), lambda b, h, i: (b, i, h)),
        out_shape=jax.ShapeDtypeStruct((bsz, t, nh * MLA_V), BF16),
        scratch_shapes=[pltpu.VMEM((1, tq), F32), pltpu.VMEM((VT_ROWS, tq), F32),
                        pltpu.VMEM((2, tq // Q_SUB, kv, Q_SUB), F32), pltpu.VMEM((2, tq // Q_SUB, 1, Q_SUB), F32)],
        compiler_params=_params("parallel", "parallel", "arbitrary"),
        name="mla_attention",
    )(qt, kc, vtc, kl, vtl, p)


def _out_final_kernel(y_ref, h_ref, mod_ref, ow_ref, g_ref, o_ref):
    hn = h_ref[...] + mod_ref[2:3, :] * _dot(y_ref[...], ow_ref[...])
    o_ref[...] = _rms(hn, g_ref[...])


def _out_final(y, h, mod, out_w, g):
    bsz, t, d = h.shape
    wi = y.shape[-1]
    tm = min(2 * TOK_TILE, t)
    return pl.pallas_call(
        _out_final_kernel,
        grid=(bsz, t // tm),
        in_specs=[
            pl.BlockSpec((None, tm, wi), lambda b, j: (b, j, 0)),
            pl.BlockSpec((None, tm, d), lambda b, j: (b, j, 0)),
            pl.BlockSpec((None, 3, d), lambda b, j: (b, 0, 0)),
            pl.BlockSpec((wi, d), lambda b, j: (0, 0)),
            pl.BlockSpec((1, d), lambda b, j: (0, 0)),
        ],
        out_specs=pl.BlockSpec((None, tm, d), lambda b, j: (b, j, 0)),
        out_shape=jax.ShapeDtypeStruct((bsz, t, d), F32),
        compiler_params=_params("parallel", "parallel"),
        name="out_final",
    )(y, h, mod, out_w, g)


def _rope_tables(n_tokens):
    rows = n_tokens // GRID_W
    pos_r = jnp.repeat(jnp.arange(rows), GRID_W).astype(F32)
    pos_c = jnp.tile(jnp.arange(GRID_W), rows).astype(F32)
    inv = ROPE_BASE ** (-2.0 * jnp.arange(ROPE_FREQ, dtype=F32) / (MLA_ROPE // 2))
    ang = jnp.stack([pos_r[:, None] * inv, pos_c[:, None] * inv], axis=1)
    cos, sin = jnp.cos(ang), jnp.sin(ang)
    pad = LANES - MLA_ROPE
    cos_k = jnp.pad(jnp.stack([cos, cos], axis=2).reshape(n_tokens, MLA_ROPE), ((0, 0), (0, pad)))
    sin_k = jnp.pad(jnp.stack([-sin, sin], axis=2).reshape(n_tokens, MLA_ROPE), ((0, 0), (0, pad)))
    cos_q = jnp.transpose(cos, (1, 2, 0))
    sin_q = jnp.transpose(sin, (1, 2, 0))
    return (cos_k, sin_k), (cos_q, sin_q)


def kernel(x, c, ctx, c_ctx, ada_w, ada_b, norm_g, out_w, ev_in_w, hg_lb, hg_norm_g, pool_w, pool_scale,
           od_in_w, qa_norm_g, qb_w, kva_norm_g, kvb_w, final_norm_g):
    bsz, t, d = x.shape
    lc = ctx.shape[1]
    depth = ada_w.shape[0]
    assert depth == 2 and t % (2 * TOK_TILE) == 0 and lc % TOK_TILE == 0 and t % GRID_W == 0
    w = hg_norm_g.shape[-1]
    nh = w // HG_DK
    q_rank = qa_norm_g.shape[-1]
    kv_rank = kva_norm_g.shape[-1]
    d_inner = out_w.shape[1]

    n_cond = -(-(bsz + 1) // SUBLANES) * SUBLANES
    cond = jnp.zeros((n_cond, d), F32).at[:bsz].set(c).at[bsz].set(c_ctx)
    mods = _ada(cond, ada_w, ada_b).reshape(depth, n_cond, 3, d)
    mod_l = [mods[l, :bsz] for l in range(depth)]
    mod_c = [mods[l, bsz:bsz + 1] for l in range(depth)]

    lb = jnp.cumsum(jax.nn.softmax(hg_lb.astype(F32), axis=1), axis=1)[:, 0].reshape(2, 1, w)
    w_in0 = ev_in_w[0].astype(BF16)
    g0 = norm_g[0].reshape(1, d)
    ctx_flat = ctx.reshape(1, bsz * lc, d)
    n_in0 = w_in0.shape[1]
    p_c = _modnorm_mm(ctx_flat, mod_c[0], g0, w_in0, 4 * TOK_TILE, n_in0 // 4, "in_proj0_ctx").reshape(bsz, lc, n_in0)
    p_l = _modnorm_mm(x, mod_l[0], g0, w_in0, 4 * TOK_TILE, n_in0 // 4, "in_proj0")
    consts = _hgrn_constants(HG_CHUNK)
    s0 = jnp.zeros((bsz, 2, nh, HG_DK, HG_DK), F32)
    of_c, ob_c, s_c = _hgrn(p_c, lb, s0, consts)
    of_l, ob_l, _ = _hgrn(p_l, lb, s_c, consts)
    hgn = hg_norm_g[0].reshape(1, w)
    pw = pool_w[0].astype(BF16)
    ps = pool_scale[0].reshape(1, w)
    ow0 = out_w[0].astype(BF16)
    mod_c0 = jnp.broadcast_to(mod_c[0], (bsz, 3, d))
    hc1 = _even_post(of_c, ob_c, p_c, ctx, mod_c0, hgn, pw, ps, ow0)
    hl1 = _even_post(of_l, ob_l, p_l, x, mod_l[0], hgn, pw, ps, ow0)

    o1 = q_rank
    o2 = o1 + kv_rank
    o3 = o2 + MLA_ROPE
    w1 = od_in_w[0]
    kr_pad = jnp.zeros((d, LANES - MLA_ROPE), F32)
    w_in1 = jnp.concatenate([w1[:, o3:], w1[:, :o1], w1[:, o1:o2], w1[:, o2:o3], kr_pad], axis=1).astype(BF16)
    n_kv = kv_rank + LANES
    w_in1c = w_in1[:, d_inner + q_rank:]
    g1 = norm_g[1].reshape(1, d)
    p1_c = _modnorm_mm(hc1.reshape(1, bsz * lc, d), mod_c[1], g1, w_in1c, 4 * TOK_TILE, n_kv, "in_proj1_ctx")
    p1_c = p1_c.reshape(bsz, lc, n_kv)
    p1_l = _modnorm_mm(hl1, mod_l[1], g1, w_in1, 2 * TOK_TILE, w_in1.shape[1], "in_proj1")

    kvw = kvb_w[0].reshape(kv_rank, MLA_HEADS, MLA_NOPE + MLA_V)
    wuk = kvw[..., :MLA_NOPE].reshape(kv_rank, MLA_HEADS * MLA_NOPE).astype(BF16)
    wuvt = jnp.transpose(kvw[..., MLA_NOPE:], (1, 2, 0)).reshape(MLA_HEADS * MLA_V, kv_rank).astype(BF16)
    wqt = jnp.transpose(qb_w[0]).astype(BF16)
    kvg = kva_norm_g[0].reshape(1, kv_rank)
    qag = qa_norm_g[0].reshape(1, q_rank)
    tab_k, tab_q = _rope_tables(t)
    kc, vtc = _mla_kv(p1_c, 0, kv_rank // LANES, kvg, wuk, wuvt, None, lc)
    kl, vtl = _mla_kv(p1_l, (d_inner + q_rank) // kv_rank, (d_inner + q_rank + kv_rank) // LANES, kvg, wuk, wuvt,
                      tab_k, min(KV_CHUNK, t))
    qt = _mla_q(p1_l, d_inner // q_rank, qag, wqt, *tab_q)
    y = _attn(qt, kc, vtc, kl, vtl, p1_l)
    return _out_final(y, hl1, mod_l[1], out_w[1].astype(BF16), final_norm_g.reshape(1, d))
```

```python
import functools

import numpy as np
import jax
import jax.numpy as jnp
from jax import lax
from jax.experimental import pallas as pl
from jax.experimental.pallas import tpu as pltpu

F32 = jnp.float32
BF16 = jnp.bfloat16

EPS = 1e-6
GRID_W = 64
HG_DK = 128
POOL_WINDOWS = (2, 4, 8, 16)
MLA_HEADS = 16
MLA_NOPE = 128
MLA_ROPE = 64
MLA_V = 128
MLA_QK = MLA_NOPE + MLA_ROPE
QK_PAD = 256
VT_ROWS = MLA_V + 16
MLA_SCALE = MLA_QK ** -0.5
LOG2_E = 1.4426950408889634
ROPE_FREQ = MLA_ROPE // 4
ROPE_BASE = 10000.0

LANES = 128
SUBLANES = 8
VMEM_LIMIT = 48 * 1024 * 1024

HG_CHUNK = 64
TOK_TILE = 256
KV_CHUNK = 512
Q_TILE = 2048
Q_SUB = 256
POOL_HALO = 8


def _dot(a, b):
    return jnp.dot(a, b, preferred_element_type=F32)


def _dot_nt(a, b):
    return lax.dot_general(a, b, (((1,), (1,)), ((), ())), preferred_element_type=F32)


def _dot_tn(a, b):
    return lax.dot_general(a, b, (((0,), (0,)), ((), ())), preferred_element_type=F32)


def _sigmoid(x):
    return 0.5 * jnp.tanh(0.5 * x) + 0.5


def _silu(x):
    h = 0.5 * x
    return h + h * jnp.tanh(h)


def _split_bf16(x):
    hi = x.astype(BF16)
    lo = (x - hi.astype(F32)).astype(BF16)
    return hi, lo


def _params(*sem):
    return pltpu.CompilerParams(dimension_semantics=sem, vmem_limit_bytes=VMEM_LIMIT)


def _ada_kernel(c_ref, w_ref, b_ref, o_ref):
    c = c_ref[...]
    s_hi, s_lo = _split_bf16(_silu(c))
    w_hi, w_lo = _split_bf16(w_ref[...])
    o_ref[...] = _dot(s_hi, w_hi) + _dot(s_lo, w_hi) + _dot(s_hi, w_lo) + b_ref[...]


def _ada(cond, ada_w, ada_b):
    depth, d, _ = ada_w.shape
    r = cond.shape[0]
    return pl.pallas_call(
        _ada_kernel,
        grid=(depth, 3),
        in_specs=[
            pl.BlockSpec((r, d), lambda l, j: (0, 0)),
            pl.BlockSpec((None, d, d), lambda l, j: (l, 0, j)),
            pl.BlockSpec((None, 1, d), lambda l, j: (l, 0, j)),
        ],
        out_specs=pl.BlockSpec((None, r, d), lambda l, j: (l, 0, j)),
        out_shape=jax.ShapeDtypeStruct((depth, r, 3 * d), F32),
        compiler_params=_params("parallel", "parallel"),
        name="ada_modulation",
    )(cond, ada_w, ada_b.reshape(depth, 1, 3 * d))


def _rms(x, g):
    return x * lax.rsqrt(jnp.mean(x * x, axis=-1, keepdims=True) + EPS) * g


def _modnorm_mm_kernel(x_ref, mod_ref, g_ref, w_ref, o_ref, z_ref):
    @pl.when(pl.program_id(2) == 0)
    def _():
        y = _rms(x_ref[...], g_ref[...])
        z_ref[...] = (y * (1.0 + mod_ref[1:2, :]) + mod_ref[0:1, :]).astype(BF16)

    o_ref[...] = _dot(z_ref[...], w_ref[...]).astype(o_ref.dtype)


def _modnorm_mm(x, mod, g, w, tm, tn, name):
    bx, r, d = x.shape
    n = w.shape[1]
    tm = min(tm, r)
    return pl.pallas_call(
        _modnorm_mm_kernel,
        grid=(bx, r // tm, n // tn),
        in_specs=[
            pl.BlockSpec((None, tm, d), lambda b, i, j: (b, i, 0)),
            pl.BlockSpec((None, 3, d), lambda b, i, j: (b, 0, 0)),
            pl.BlockSpec((1, d), lambda b, i, j: (0, 0)),
            pl.BlockSpec((d, tn), lambda b, i, j: (0, j)),
        ],
        out_specs=pl.BlockSpec((None, tm, tn), lambda b, i, j: (b, i, j)),
        out_shape=jax.ShapeDtypeStruct((bx, r, n), F32),
        scratch_shapes=[pltpu.VMEM((tm, d), BF16)],
        compiler_params=_params("parallel", "parallel", "arbitrary"),
        name=name,
    )(x, mod, g, w)


def _hgrn_levels(c):
    w = c // 2
    out = []
    while w >= 1:
        out.append(w)
        w //= 2
    return tuple(out)


def _hgrn_constants(c):
    t = np.arange(c)
    tri = np.tril(np.ones((c, c), np.float32))
    masks, isk = [], []
    for w in _hgrn_levels(c):
        blk = t // (2 * w)
        first = (t % (2 * w)) < w
        masks.append(((blk[:, None] == blk[None, :]) & (~first[:, None]) & first[None, :]).astype(np.float32))
        isk.append(first.astype(np.float32)[:, None])
    masks = np.stack(masks)
    isk = np.stack(isk)
    tri2 = np.stack([tri, tri[::-1, ::-1]])
    m2 = np.stack([masks, masks[:, ::-1, ::-1]])
    k2 = np.stack([isk, isk[:, ::-1]])
    return jnp.asarray(tri2, BF16), jnp.asarray(m2, F32), jnp.asarray(k2, F32)


def _hgrn_level_decay(b, g, w, d):
    c, width = b.shape
    row = lax.broadcasted_iota(jnp.int32, (c, 1), 0)
    if w == 1:
        return jnp.where((row % 2 == 1) if d == 0 else (row % 2 == 0), g, 0.0)
    ref_off = w - 1 if d == 0 else w
    sub = lax.broadcasted_iota(jnp.int32, (SUBLANES, 1), 0)
    pieces = []
    for r0 in range(0, c, max(2 * w, SUBLANES)):
        if 2 * w >= SUBLANES:
            pieces.append(jnp.broadcast_to(b[r0 + ref_off:r0 + ref_off + 1, :], (2 * w, width)))
        else:
            lo = jnp.broadcast_to(b[r0 + ref_off:r0 + ref_off + 1, :], (SUBLANES, width))
            hi = jnp.broadcast_to(b[r0 + 2 * w + ref_off:r0 + 2 * w + ref_off + 1, :], (SUBLANES, width))
            pieces.append(jnp.where(sub < 2 * w, lo, hi))
    bref = pieces[0] if len(pieces) == 1 else jnp.concatenate(pieces, axis=0)
    before = (row % (2 * w)) < w
    sign = jnp.where(before == (d == 0), -1.0, 1.0)
    return (b - bref) * sign


def _hgrn_chunk(q_ref, f_ref, v_ref, o_ref, lb, tri, mk_ref, isk_ref, st_ref, d, r0, *, c, nh):
    last = c - 1 if d == 0 else 0
    rows = pl.ds(r0, c)
    head = lambda x, h: x[:, h * HG_DK:(h + 1) * HG_DK]
    f = lb + (1.0 - lb) * _sigmoid(f_ref[rows, :])
    g = jnp.log2(f)
    g_hi, g_lo = _split_bf16(g)
    b = _dot(tri, g_hi) + _dot(tri, g_lo)
    q = _silu(q_ref[rows, :])
    k = 1.0 - f
    v = v_ref[rows, :]
    vb = v.astype(BF16)
    bl = b[last:last + 1, :]
    qe = (q * jnp.exp2(b)).astype(BF16)
    kend = (k * jnp.exp2(bl - b)).astype(BF16)
    ebl = jnp.exp2(bl)
    st = [st_ref[d, h] for h in range(nh)]
    inter = [_dot_nt(head(qe, h), st[h].astype(BF16)) for h in range(nh)]
    att = [None] * nh
    for l, w in enumerate(_hgrn_levels(c)):
        e = jnp.exp2(_hgrn_level_decay(b, g, w, d))
        x = (e * jnp.where(isk_ref[d, l] > 0.0, k, q)).astype(BF16)
        for h in range(nh):
            t = mk_ref[d, l] * _dot_nt(head(x, h), head(x, h))
            att[h] = t if att[h] is None else att[h] + t
    qk = q * k
    for h in range(nh):
        diag = jnp.sum(head(qk, h), axis=-1, keepdims=True)
        o_ref[rows, h * HG_DK:(h + 1) * HG_DK] = (
            inter[h] + _dot(att[h].astype(BF16), head(vb, h)) + diag * head(v, h))
    for h in range(nh):
        st_ref[d, h] = head(ebl, h) * st[h] + _dot_tn(head(vb, h), head(kend, h))


def _hgrn_kernel(qf_ref, ff_ref, vf_ref, qb_ref, fb_ref, vb_ref, lb_ref, tri_ref, mk_ref, isk_ref, s0_ref,
                 of_ref, ob_ref, sout_ref, st_ref, *, tb, c, nh):
    nchunk = tb // c

    @pl.when(pl.program_id(1) == 0)
    def _():
        st_ref[...] = s0_ref[...]

    def body(cc, carry):
        rf = pl.multiple_of(cc * c, c)
        rb = pl.multiple_of((nchunk - 1 - cc) * c, c)
        _hgrn_chunk(qf_ref, ff_ref, vf_ref, of_ref, lb_ref[0], tri_ref[0], mk_ref, isk_ref, st_ref, 0, rf, c=c, nh=nh)
        _hgrn_chunk(qb_ref, fb_ref, vb_ref, ob_ref, lb_ref[1], tri_ref[1], mk_ref, isk_ref, st_ref, 1, rb, c=c, nh=nh)
        return carry

    lax.fori_loop(0, nchunk, body, 0)

    @pl.when(pl.program_id(1) == pl.num_programs(1) - 1)
    def _():
        sout_ref[...] = st_ref[...]


def _hgrn(p, lb, s0, consts):
    bsz, r, _ = p.shape
    w = lb.shape[-1]
    nh = w // HG_DK
    tb = min(TOK_TILE, r)
    c = HG_CHUNK
    nb = r // tb
    tri2, m2, k2 = consts
    fwd = lambda col: pl.BlockSpec((None, tb, w), lambda b, s: (b, s, col))
    bwd = lambda col: pl.BlockSpec((None, tb, w), lambda b, s: (b, nb - 1 - s, col))
    const = lambda arr: pl.BlockSpec(arr.shape, lambda b, s: (0,) * arr.ndim)
    st_spec = pl.BlockSpec((None, 2, nh, HG_DK, HG_DK), lambda b, s: (b, 0, 0, 0, 0))
    kern = functools.partial(_hgrn_kernel, tb=tb, c=c, nh=nh)
    return pl.pallas_call(
        kern,
        grid=(bsz, nb),
        in_specs=[fwd(0), fwd(1), fwd(3), bwd(0), bwd(2), bwd(3), const(lb), const(tri2), const(m2), const(k2),
                  st_spec],
        out_specs=[
            pl.BlockSpec((None, tb, w), lambda b, s: (b, s, 0)),
            pl.BlockSpec((None, tb, w), lambda b, s: (b, nb - 1 - s, 0)),
            st_spec,
        ],
        out_shape=[
            jax.ShapeDtypeStruct((bsz, r, w), F32),
            jax.ShapeDtypeStruct((bsz, r, w), F32),
            jax.ShapeDtypeStruct(s0.shape, F32),
        ],
        scratch_shapes=[pltpu.VMEM((2, nh, HG_DK, HG_DK), F32)],
        compiler_params=_params("parallel", "arbitrary"),
        name="hgrn2_scan",
    )(p, p, p, p, p, p, lb, tri2, m2, k2, s0)


def _even_post_kernel(of_ref, ob_ref, ga_ref, u_ref, gb_ref, up_ref, un_ref, h_ref, mod_ref, hgn_ref, pw_ref,
                      ps_ref, ow_ref, o_ref, ext_ref, y_ref, *, tb, seq, nh):
    j = pl.program_id(1)
    w = nh * HG_DK
    o = of_ref[...] + ob_ref[...]
    for h in range(nh):
        sl = slice(h * HG_DK, (h + 1) * HG_DK)
        y_ref[:, sl] = (_rms(o[:, sl], hgn_ref[:, sl]) * _silu(ga_ref[:, sl])).astype(BF16)
    u = u_ref[...]
    ext_ref[0:POOL_HALO, :] = jnp.where(j > 0, up_ref[...], 0.0)
    ext_ref[POOL_HALO:POOL_HALO + tb, :] = u
    ext_ref[POOL_HALO + tb:, :] = jnp.where(j < pl.num_programs(1) - 1, un_ref[...], 0.0)
    t = j * tb + lax.broadcasted_iota(jnp.int32, (tb, 1), 0)
    grp = w // len(POOL_WINDOWS)
    for gi, win in enumerate(POOL_WINDOWS):
        sl = slice(gi * grp, (gi + 1) * grp)
        acc = ext_ref[POOL_HALO - win // 2:POOL_HALO - win // 2 + tb, sl]
        for off in range(-win // 2 + 1, win // 2):
            acc = acc + ext_ref[POOL_HALO + off:POOL_HALO + off + tb, sl]
        cnt = (jnp.minimum(t + win // 2, seq) - jnp.maximum(t - win // 2, 0)).astype(F32)
        yp = acc * (1.0 / cnt) - u[:, sl]
        yb = _dot(yp.astype(BF16), pw_ref[gi]) * ps_ref[:, sl]
        y_ref[:, w + gi * grp:w + (gi + 1) * grp] = (yb * _silu(gb_ref[:, sl])).astype(BF16)
    o_ref[...] = h_ref[...] + mod_ref[2:3, :] * _dot(y_ref[...], ow_ref[...])


def _even_post(o_f, o_b, p, h, mod, hgn, pool_w, pool_scale, out_w):
    bsz, r, w = o_f.shape
    d = h.shape[-1]
    tb = min(TOK_TILE, r)
    nb = r // tb
    hb = tb // POOL_HALO
    nh = w // HG_DK
    tok = lambda col: pl.BlockSpec((None, tb, w), lambda b, j: (b, j, col))
    const = lambda arr: pl.BlockSpec(arr.shape, lambda b, j: (0,) * arr.ndim)
    kern = functools.partial(_even_post_kernel, tb=tb, seq=r, nh=nh)
    return pl.pallas_call(
        kern,
        grid=(bsz, nb),
        in_specs=[
            tok(0), tok(0), tok(4), tok(5), tok(6),
            pl.BlockSpec((None, POOL_HALO, w), lambda b, j: (b, jnp.maximum(j * hb - 1, 0), 5)),
            pl.BlockSpec((None, POOL_HALO, w), lambda b, j: (b, jnp.minimum((j + 1) * hb, nb * hb - 1), 5)),
            pl.BlockSpec((None, tb, d), lambda b, j: (b, j, 0)),
            pl.BlockSpec((None, 3, d), lambda b, j: (b, 0, 0)),
            const(hgn), const(pool_w), const(pool_scale), const(out_w),
        ],
        out_specs=pl.BlockSpec((None, tb, d), lambda b, j: (b, j, 0)),
        out_shape=jax.ShapeDtypeStruct((bsz, r, d), F32),
        scratch_shapes=[pltpu.VMEM((tb + 2 * POOL_HALO, w), F32), pltpu.VMEM((tb, 2 * w), BF16)],
        compiler_params=_params("parallel", "parallel"),
        name="even_post",
    )(o_f, o_b, p, p, p, p, p, h, mod, hgn, pool_w, pool_scale, out_w)


def _mla_kv_kernel(*refs, rope):
    if rope:
        ckv_ref, kr_ref, g_ref, wuk_ref, wuvt_ref, cos_ref, sin_ref, kcat_ref, vt_ref = refs
    else:
        ckv_ref, kr_ref, g_ref, wuk_ref, wuvt_ref, kcat_ref, vt_ref = refs
    cn = _rms(ckv_ref[...], g_ref[...]).astype(BF16)
    kn = _dot(cn, wuk_ref[...])
    kr = kr_ref[...]
    if rope:
        lane = lax.broadcasted_iota(jnp.int32, kr.shape, 1)
        swapped = jnp.where((lane % (2 * ROPE_FREQ)) < ROPE_FREQ,
                            pltpu.roll(kr, LANES - ROPE_FREQ, 1), pltpu.roll(kr, ROPE_FREQ, 1))
        kr = kr * cos_ref[...] + swapped * sin_ref[...]
    kr = kr.astype(BF16)
    ones_rows = (lax.broadcasted_iota(jnp.int32, (VT_ROWS - MLA_V, kr.shape[0]), 0) == 0).astype(BF16)
    vt = _dot_nt(wuvt_ref[...], cn)
    for h in range(MLA_HEADS):
        kcat_ref[h, :, 0:MLA_NOPE] = kn[:, h * MLA_NOPE:(h + 1) * MLA_NOPE].astype(BF16)
        kcat_ref[h, :, MLA_NOPE:] = kr
        vt_ref[h, 0:MLA_V, :] = vt[h * MLA_V:(h + 1) * MLA_V].astype(BF16)
        vt_ref[h, MLA_V:, :] = ones_rows


def _mla_kv(p, ckv_blk, kr_blk, g, wuk, wuvt, tables, tb):
    bsz, r, _ = p.shape
    rank = g.shape[-1]
    nb = r // tb
    const = lambda arr: pl.BlockSpec(arr.shape, lambda b, j: (0,) * arr.ndim)
    in_specs = [
        pl.BlockSpec((None, tb, rank), lambda b, j: (b, j, ckv_blk)),
        pl.BlockSpec((None, tb, LANES), lambda b, j: (b, j, kr_blk)),
        const(g), const(wuk), const(wuvt),
    ]
    args = [p, p, g, wuk, wuvt]
    if tables is not None:
        in_specs += [pl.BlockSpec((tb, LANES), lambda b, j: (j, 0))] * 2
        args += list(tables)
    return pl.pallas_call(
        functools.partial(_mla_kv_kernel, rope=tables is not None),
        grid=(bsz, nb),
        in_specs=in_specs,
        out_specs=[
            pl.BlockSpec((None, MLA_HEADS, None, tb, QK_PAD), lambda b, j: (b, 0, j, 0, 0)),
            pl.BlockSpec((None, MLA_HEADS, None, VT_ROWS, tb), lambda b, j: (b, 0, j, 0, 0)),
        ],
        out_shape=[
            jax.ShapeDtypeStruct((bsz, MLA_HEADS, nb, tb, QK_PAD), BF16),
            jax.ShapeDtypeStruct((bsz, MLA_HEADS, nb, VT_ROWS, tb), BF16),
        ],
        compiler_params=_params("parallel", "parallel"),
        name="mla_kv_rope" if tables is not None else "mla_kv",
    )(*args)


def _mla_q_kernel(cq_ref, g_ref, wqt_ref, cos_ref, sin_ref, qt_ref):
    cn = _rms(cq_ref[...], g_ref[...]).astype(BF16)
    f = ROPE_FREQ
    qt_all = _dot_nt(wqt_ref[...], cn) * (MLA_SCALE * LOG2_E)
    for h in range(MLA_HEADS):
        qt = qt_all[h * MLA_QK:(h + 1) * MLA_QK]
        qt_ref[h, 0:MLA_NOPE, :] = qt[0:MLA_NOPE].astype(BF16)
        for ax in range(2):
            r0 = MLA_NOPE + ax * 2 * f
            x1 = qt[r0:r0 + f]
            x2 = qt[r0 + f:r0 + 2 * f]
            co = cos_ref[ax]
            si = sin_ref[ax]
            qt_ref[h, r0:r0 + f, :] = (x1 * co - x2 * si).astype(BF16)
            qt_ref[h, r0 + f:r0 + 2 * f, :] = (x2 * co + x1 * si).astype(BF16)
        qt_ref[h, MLA_QK:, :] = jnp.zeros((QK_PAD - MLA_QK, cn.shape[0]), BF16)


def _mla_q(p, cq_blk, g, wqt, cos_t, sin_t):
    bsz, t, _ = p.shape
    rank = g.shape[-1]
    tm = min(TOK_TILE, t)
    const = lambda arr: pl.BlockSpec(arr.shape, lambda b, j: (0,) * arr.ndim)
    tab = pl.BlockSpec((2, ROPE_FREQ, tm), lambda b, j: (0, 0, j))
    return pl.pallas_call(
        _mla_q_kernel,
        grid=(bsz, t // tm),
        in_specs=[pl.BlockSpec((None, tm, rank), lambda b, j: (b, j, cq_blk)), const(g), const(wqt), tab, tab],
        out_specs=pl.BlockSpec((None, MLA_HEADS, QK_PAD, tm), lambda b, j: (b, 0, 0, j)),
        out_shape=jax.ShapeDtypeStruct((bsz, MLA_HEADS, QK_PAD, t), BF16),
        compiler_params=_params("parallel", "parallel"),
        name="mla_q",
    )(p, g, wqt, cos_t, sin_t)


def _attn_kernel(qt_ref, kc_ref, vtc_ref, kl_ref, vtl_ref, g_ref, o_ref, m_ref, acc_ref, s_ref, mx_ref, *, tq, n_lat):
    nsub = tq // Q_SUB
    m_ref[...] = jnp.full(m_ref.shape, -jnp.inf, F32)
    acc_ref[...] = jnp.zeros(acc_ref.shape, F32)

    def scores(k, nxt, g):
        s = _dot(k, qt_ref[:, g * Q_SUB:(g + 1) * Q_SUB])
        s_ref[nxt, g, 0:k.shape[0], :] = s
        mx_ref[nxt, g] = jnp.max(s, axis=0, keepdims=True)

    def substep(k_next, vt_cur, cur, nxt):
        rows = vt_cur.shape[1]
        for g in range(nsub):
            sl = slice(g * Q_SUB, (g + 1) * Q_SUB)
            scores(k_next, nxt, g)
            m_old = m_ref[:, sl]
            m_new = jnp.maximum(m_old, mx_ref[cur, g])
            alpha = jnp.exp2(m_old - m_new)
            p = jnp.exp2(s_ref[cur, g, 0:rows, :] - m_new)
            acc_ref[:, sl] = alpha * acc_ref[:, sl] + _dot(vt_cur, p.astype(BF16))
            m_ref[:, sl] = m_new

    kc = kc_ref[...]
    for g in range(nsub):
        scores(kc, 0, g)
    substep(kl_ref[0], vtc_ref[...], 0, 1)

    def body(j, carry):
        for u in range(2):
            a = 2 * j + u
            substep(kl_ref[jnp.minimum(a + 1, n_lat - 1)], vtl_ref[a], (1 + u) % 2, u % 2)
        return carry

    lax.fori_loop(0, n_lat // 2, body, 0)
    o = (acc_ref[0:MLA_V, :] * (1.0 / acc_ref[MLA_V:MLA_V + 1, :])).T
    o_ref[...] = (o * _silu(g_ref[...])).astype(o_ref.dtype)


def _attn(qt, kc, vtc, kl, vtl, p):
    bsz, nh, _, t = qt.shape
    lc = kc.shape[3]
    tq = min(Q_TILE, t)
    n_lat = kl.shape[2]
    kv = kl.shape[3]
    assert kc.shape[2] == 1 and lc <= kv and n_lat % 2 == 0
    kern = functools.partial(_attn_kernel, tq=tq, n_lat=n_lat)
    ctx5 = lambda arr: pl.BlockSpec((None, None, None) + arr.shape[3:], lambda b, h, i: (b, h, 0, 0, 0))
    full5 = lambda arr: pl.BlockSpec((None, None) + arr.shape[2:], lambda b, h, i: (b, h, 0, 0, 0))
    return pl.pallas_call(
        kern,
        grid=(bsz, nh, t // tq),
        in_specs=[
            pl.BlockSpec((None, None, QK_PAD, tq), lambda b, h, i: (b, h, 0, i)),
            ctx5(kc), ctx5(vtc), full5(kl), full5(vtl),
            pl.BlockSpec((None, tq, MLA_V), lambda b, h, i: (b, i, h)),
        ],
        out_specs=pl.BlockSpec((None, tq, MLA_V), lambda b, h, i: (b, i, h)),
        out_shape=jax.ShapeDtypeStruct((bsz, t, nh * MLA_V), BF16),
        scratch_shapes=[pltpu.VMEM((1, tq), F32), pltpu.VMEM((VT_ROWS, tq), F32),
                        pltpu.VMEM((2, tq // Q_SUB, kv, Q_SUB), F32), pltpu.VMEM((2, tq // Q_SUB, 1, Q_SUB), F32)],
        compiler_params=_params("parallel", "parallel", "arbitrary"),
        name="mla_attention",
    )(qt, kc, vtc, kl, vtl, p)


def _out_final_kernel(y_ref, h_ref, mod_ref, ow_ref, g_ref, o_ref):
    hn = h_ref[...] + mod_ref[2:3, :] * _dot(y_ref[...], ow_ref[...])
    o_ref[...] = _rms(hn, g_ref[...])


def _out_final(y, h, mod, out_w, g):
    bsz, t, d = h.shape
    wi = y.shape[-1]
    tm = min(2 * TOK_TILE, t)
    return pl.pallas_call(
        _out_final_kernel,
        grid=(bsz, t // tm),
        in_specs=[
            pl.BlockSpec((None, tm, wi), lambda b, j: (b, j, 0)),
            pl.BlockSpec((None, tm, d), lambda b, j: (b, j, 0)),
            pl.BlockSpec((None, 3, d), lambda b, j: (b, 0, 0)),
            pl.BlockSpec((wi, d), lambda b, j: (0, 0)),
            pl.BlockSpec((1, d), lambda b, j: (0, 0)),
        ],
        out_specs=pl.BlockSpec((None, tm, d), lambda b, j: (b, j, 0)),
        out_shape=jax.ShapeDtypeStruct((bsz, t, d), F32),
        compiler_params=_params("parallel", "parallel"),
        name="out_final",
    )(y, h, mod, out_w, g)


def _rope_tables(n_tokens):
    rows = n_tokens // GRID_W
    pos_r = jnp.repeat(jnp.arange(rows), GRID_W).astype(F32)
    pos_c = jnp.tile(jnp.arange(GRID_W), rows).astype(F32)
    inv = ROPE_BASE ** (-2.0 * jnp.arange(ROPE_FREQ, dtype=F32) / (MLA_ROPE // 2))
    ang = jnp.stack([pos_r[:, None] * inv, pos_c[:, None] * inv], axis=1)
    cos, sin = jnp.cos(ang), jnp.sin(ang)
    pad = LANES - MLA_ROPE
    cos_k = jnp.pad(jnp.stack([cos, cos], axis=2).reshape(n_tokens, MLA_ROPE), ((0, 0), (0, pad)))
    sin_k = jnp.pad(jnp.stack([-sin, sin], axis=2).reshape(n_tokens, MLA_ROPE), ((0, 0), (0, pad)))
    cos_q = jnp.transpose(cos, (1, 2, 0))
    sin_q = jnp.transpose(sin, (1, 2, 0))
    return (cos_k, sin_k), (cos_q, sin_q)


def kernel(x, c, ctx, c_ctx, ada_w, ada_b, norm_g, out_w, ev_in_w, hg_lb, hg_norm_g, pool_w, pool_scale,
           od_in_w, qa_norm_g, qb_w, kva_norm_g, kvb_w, final_norm_g):
    bsz, t, d = x.shape
    lc = ctx.shape[1]
    depth = ada_w.shape[0]
    assert depth == 2 and t % (2 * TOK_TILE) == 0 and lc % TOK_TILE == 0 and t % GRID_W == 0
    w = hg_norm_g.shape[-1]
    nh = w // HG_DK
    q_rank = qa_norm_g.shape[-1]
    kv_rank = kva_norm_g.shape[-1]
    d_inner = out_w.shape[1]

    n_cond = -(-(bsz + 1) // SUBLANES) * SUBLANES
    cond = jnp.zeros((n_cond, d), F32).at[:bsz].set(c).at[bsz].set(c_ctx)
    mods = _ada(cond, ada_w, ada_b).reshape(depth, n_cond, 3, d)
    mod_l = [mods[l, :bsz] for l in range(depth)]
    mod_c = [mods[l, bsz:bsz + 1] for l in range(depth)]

    lb = jnp.cumsum(jax.nn.softmax(hg_lb.astype(F32), axis=1), axis=1)[:, 0].reshape(2, 1, w)
    w_in0 = ev_in_w[0].astype(BF16)
    g0 = norm_g[0].reshape(1, d)
    ctx_flat = ctx.reshape(1, bsz * lc, d)
    n_in0 = w_in0.shape[1]
    p_c = _modnorm_mm(ctx_flat, mod_c[0], g0, w_in0, 4 * TOK_TILE, n_in0 // 4, "in_proj0_ctx").reshape(bsz, lc, n_in0)
    p_l = _modnorm_mm(x, mod_l[0], g0, w_in0, 4 * TOK_TILE, n_in0 // 4, "in_proj0")
    consts = _hgrn_constants(HG_CHUNK)
    s0 = jnp.zeros((bsz, 2, nh, HG_DK, HG_DK), F32)
    of_c, ob_c, s_c = _hgrn(p_c, lb, s0, consts)
    of_l, ob_l, _ = _hgrn(p_l, lb, s_c, consts)
    hgn = hg_norm_g[0].reshape(1, w)
    pw = pool_w[0].astype(BF16)
    ps = pool_scale[0].reshape(1, w)
    ow0 = out_w[0].astype(BF16)
    mod_c0 = jnp.broadcast_to(mod_c[0], (bsz, 3, d))
    hc1 = _even_post(of_c, ob_c, p_c, ctx, mod_c0, hgn, pw, ps, ow0)
    hl1 = _even_post(of_l, ob_l, p_l, x, mod_l[0], hgn, pw, ps, ow0)

    o1 = q_rank
    o2 = o1 + kv_rank
    o3 = o2 + MLA_ROPE
    w1 = od_in_w[0]
    kr_pad = jnp.zeros((d, LANES - MLA_ROPE), F32)
    w_in1 = jnp.concatenate([w1[:, o3:], w1[:, :o1], w1[:, o1:o2], w1[:, o2:o3], kr_pad], axis=1).astype(BF16)
    n_kv = kv_rank + LANES
    w_in1c = w_in1[:, d_inner + q_rank:]
    g1 = norm_g[1].reshape(1, d)
    p1_c = _modnorm_mm(hc1.reshape(1, bsz * lc, d), mod_c[1], g1, w_in1c, 4 * TOK_TILE, n_kv, "in_proj1_ctx")
    p1_c = p1_c.reshape(bsz, lc, n_kv)
    p1_l = _modnorm_mm(hl1, mod_l[1], g1, w_in1, 2 * TOK_TILE, w_in1.shape[1], "in_proj1")

    kvw = kvb_w[0].reshape(kv_rank, MLA_HEADS, MLA_NOPE + MLA_V)
    wuk = kvw[..., :MLA_NOPE].reshape(kv_rank, MLA_HEADS * MLA_NOPE).astype(BF16)
    wuvt = jnp.transpose(kvw[..., MLA_NOPE:], (1, 2, 0)).reshape(MLA_HEADS * MLA_V, kv_rank).astype(BF16)
    wqt = jnp.transpose(qb_w[0]).astype(BF16)
    kvg = kva_norm_g[0].reshape(1, kv_rank)
    qag = qa_norm_g[0].reshape(1, q_rank)
    tab_k, tab_q = _rope_tables(t)
    kc, vtc = _mla_kv(p1_c, 0, kv_rank // LANES, kvg, wuk, wuvt, None, lc)
    kl, vtl = _mla_kv(p1_l, (d_inner + q_rank) // kv_rank, (d_inner + q_rank + kv_rank) // LANES, kvg, wuk, wuvt,
                      tab_k, min(KV_CHUNK, t))
    qt = _mla_q(p1_l, d_inner // q_rank, qag, wqt, *tab_q)
    y = _attn(qt, kc, vtc, kl, vtl, p1_l)
    return _out_final(y, hl1, mod_l[1], out_w[1].astype(BF16), final_norm_g.reshape(1, d))
```

```python
import functools

import numpy as np
import jax
import jax.numpy as jnp
from jax import lax
from jax.experimental import pallas as pl
from jax.experimental.pallas import tpu as pltpu

F32 = jnp.float32
BF16 = jnp.bfloat16

EPS = 1e-6
GRID_W = 64
HG_DK = 128
POOL_WINDOWS = (2, 4, 8, 16)
MLA_HEADS = 16
MLA_NOPE = 128
MLA_ROPE = 64
MLA_V = 128
MLA_QK = MLA_NOPE + MLA_ROPE
QK_PAD = 256
VT_ROWS = MLA_V + 16
MLA_SCALE = MLA_QK ** -0.5
LOG2_E = 1.4426950408889634
ROPE_FREQ = MLA_ROPE // 4
ROPE_BASE = 10000.0

LANES = 128
SUBLANES = 8
VMEM_LIMIT = 48 * 1024 * 1024

HG_CHUNK = 64
TOK_TILE = 256
KV_CHUNK = 512
Q_TILE = 4096
Q_SUB = 256
POOL_HALO = 8


def _dot(a, b):
    return jnp.dot(a, b, preferred_element_type=F32)


def _dot_nt(a, b):
    return lax.dot_general(a, b, (((1,), (1,)), ((), ())), preferred_element_type=F32)


def _dot_tn(a, b):
    return lax.dot_general(a, b, (((0,), (0,)), ((), ())), preferred_element_type=F32)


def _sigmoid(x):
    return 0.5 * jnp.tanh(0.5 * x) + 0.5


def _silu(x):
    h = 0.5 * x
    return h + h * jnp.tanh(h)


def _split_bf16(x):
    hi = x.astype(BF16)
    lo = (x - hi.astype(F32)).astype(BF16)
    return hi, lo


def _params(*sem):
    return pltpu.CompilerParams(dimension_semantics=sem, vmem_limit_bytes=VMEM_LIMIT)


def _ada_kernel(c_ref, w_ref, b_ref, o_ref):
    c = c_ref[...]
    s_hi, s_lo = _split_bf16(_silu(c))
    w_hi, w_lo = _split_bf16(w_ref[...])
    o_ref[...] = _dot(s_hi, w_hi) + _dot(s_lo, w_hi) + _dot(s_hi, w_lo) + b_ref[...]


def _ada(cond, ada_w, ada_b):
    depth, d, _ = ada_w.shape
    r = cond.shape[0]
    return pl.pallas_call(
        _ada_kernel,
        grid=(depth, 3),
        in_specs=[
            pl.BlockSpec((r, d), lambda l, j: (0, 0)),
            pl.BlockSpec((None, d, d), lambda l, j: (l, 0, j)),
            pl.BlockSpec((None, 1, d), lambda l, j: (l, 0, j)),
        ],
        out_specs=pl.BlockSpec((None, r, d), lambda l, j: (l, 0, j)),
        out_shape=jax.ShapeDtypeStruct((depth, r, 3 * d), F32),
        compiler_params=_params("parallel", "parallel"),
        name="ada_modulation",
    )(cond, ada_w, ada_b.reshape(depth, 1, 3 * d))


def _rms(x, g):
    return x * lax.rsqrt(jnp.mean(x * x, axis=-1, keepdims=True) + EPS) * g


def _modnorm_mm_kernel(x_ref, mod_ref, g_ref, w_ref, o_ref, z_ref):
    @pl.when(pl.program_id(2) == 0)
    def _():
        y = _rms(x_ref[...], g_ref[...])
        z_ref[...] = (y * (1.0 + mod_ref[1:2, :]) + mod_ref[0:1, :]).astype(BF16)

    o_ref[...] = _dot(z_ref[...], w_ref[...]).astype(o_ref.dtype)


def _modnorm_mm(x, mod, g, w, tm, tn, name):
    bx, r, d = x.shape
    n = w.shape[1]
    tm = min(tm, r)
    return pl.pallas_call(
        _modnorm_mm_kernel,
        grid=(bx, r // tm, n // tn),
        in_specs=[
            pl.BlockSpec((None, tm, d), lambda b, i, j: (b, i, 0)),
            pl.BlockSpec((None, 3, d), lambda b, i, j: (b, 0, 0)),
            pl.BlockSpec((1, d), lambda b, i, j: (0, 0)),
            pl.BlockSpec((d, tn), lambda b, i, j: (0, j)),
        ],
        out_specs=pl.BlockSpec((None, tm, tn), lambda b, i, j: (b, i, j)),
        out_shape=jax.ShapeDtypeStruct((bx, r, n), F32),
        scratch_shapes=[pltpu.VMEM((tm, d), BF16)],
        compiler_params=_params("parallel", "parallel", "arbitrary"),
        name=name,
    )(x, mod, g, w)


def _hgrn_levels(c):
    w = c // 2
    out = []
    while w >= 1:
        out.append(w)
        w //= 2
    return tuple(out)


def _hgrn_constants(c):
    t = np.arange(c)
    tri = np.tril(np.ones((c, c), np.float32))
    masks, isk = [], []
    for w in _hgrn_levels(c):
        blk = t // (2 * w)
        first = (t % (2 * w)) < w
        masks.append(((blk[:, None] == blk[None, :]) & (~first[:, None]) & first[None, :]).astype(np.float32))
        isk.append(first.astype(np.float32)[:, None])
    masks = np.stack(masks)
    isk = np.stack(isk)
    tri2 = np.stack([tri, tri[::-1, ::-1]])
    m2 = np.stack([masks, masks[:, ::-1, ::-1]])
    k2 = np.stack([isk, isk[:, ::-1]])
    return jnp.asarray(tri2, BF16), jnp.asarray(m2, F32), jnp.asarray(k2, F32)


def _hgrn_level_decay(b, g, w, d):
    c, width = b.shape
    row = lax.broadcasted_iota(jnp.int32, (c, 1), 0)
    if w == 1:
        return jnp.where((row % 2 == 1) if d == 0 else (row % 2 == 0), g, 0.0)
    ref_off = w - 1 if d == 0 else w
    sub = lax.broadcasted_iota(jnp.int32, (SUBLANES, 1), 0)
    pieces = []
    for r0 in range(0, c, max(2 * w, SUBLANES)):
        if 2 * w >= SUBLANES:
            pieces.append(jnp.broadcast_to(b[r0 + ref_off:r0 + ref_off + 1, :], (2 * w, width)))
        else:
            lo = jnp.broadcast_to(b[r0 + ref_off:r0 + ref_off + 1, :], (SUBLANES, width))
            hi = jnp.broadcast_to(b[r0 + 2 * w + ref_off:r0 + 2 * w + ref_off + 1, :], (SUBLANES, width))
            pieces.append(jnp.where(sub < 2 * w, lo, hi))
    bref = pieces[0] if len(pieces) == 1 else jnp.concatenate(pieces, axis=0)
    before = (row % (2 * w)) < w
    sign = jnp.where(before == (d == 0), -1.0, 1.0)
    return (b - bref) * sign


def _hgrn_chunk(q_ref, f_ref, v_ref, o_ref, lb, tri, mk_ref, isk_ref, st_ref, d, r0, *, c, nh):
    last = c - 1 if d == 0 else 0
    rows = pl.ds(r0, c)
    head = lambda x, h: x[:, h * HG_DK:(h + 1) * HG_DK]
    f = lb + (1.0 - lb) * _sigmoid(f_ref[rows, :])
    g = jnp.log2(f)
    g_hi, g_lo = _split_bf16(g)
    b = _dot(tri, g_hi) + _dot(tri, g_lo)
    q = _silu(q_ref[rows, :])
    k = 1.0 - f
    v = v_ref[rows, :]
    vb = v.astype(BF16)
    bl = b[last:last + 1, :]
    qe = (q * jnp.exp2(b)).astype(BF16)
    kend = (k * jnp.exp2(bl - b)).astype(BF16)
    ebl = jnp.exp2(bl)
    st = [st_ref[d, h] for h in range(nh)]
    inter = [_dot_nt(head(qe, h), st[h].astype(BF16)) for h in range(nh)]
    att = [None] * nh
    for l, w in enumerate(_hgrn_levels(c)):
        e = jnp.exp2(_hgrn_level_decay(b, g, w, d))
        x = (e * jnp.where(isk_ref[d, l] > 0.0, k, q)).astype(BF16)
        for h in range(nh):
            t = mk_ref[d, l] * _dot_nt(head(x, h), head(x, h))
            att[h] = t if att[h] is None else att[h] + t
    qk = q * k
    for h in range(nh):
        diag = jnp.sum(head(qk, h), axis=-1, keepdims=True)
        o_ref[rows, h * HG_DK:(h + 1) * HG_DK] = (
            inter[h] + _dot(att[h].astype(BF16), head(vb, h)) + diag * head(v, h))
    for h in range(nh):
        st_ref[d, h] = head(ebl, h) * st[h] + _dot_tn(head(vb, h), head(kend, h))


def _hgrn_kernel(qf_ref, ff_ref, vf_ref, qb_ref, fb_ref, vb_ref, lb_ref, tri_ref, mk_ref, isk_ref, s0_ref,
                 of_ref, ob_ref, sout_ref, st_ref, *, tb, c, nh):
    nchunk = tb // c

    @pl.when(pl.program_id(1) == 0)
    def _():
        st_ref[...] = s0_ref[...]

    def body(cc, carry):
        rf = pl.multiple_of(cc * c, c)
        rb = pl.multiple_of((nchunk - 1 - cc) * c, c)
        _hgrn_chunk(qf_ref, ff_ref, vf_ref, of_ref, lb_ref[0], tri_ref[0], mk_ref, isk_ref, st_ref, 0, rf, c=c, nh=nh)
        _hgrn_chunk(qb_ref, fb_ref, vb_ref, ob_ref, lb_ref[1], tri_ref[1], mk_ref, isk_ref, st_ref, 1, rb, c=c, nh=nh)
        return carry

    lax.fori_loop(0, nchunk, body, 0)

    @pl.when(pl.program_id(1) == pl.num_programs(1) - 1)
    def _():
        sout_ref[...] = st_ref[...]


def _hgrn(p, lb, s0, consts):
    bsz, r, _ = p.shape
    w = lb.shape[-1]
    nh = w // HG_DK
    tb = min(TOK_TILE, r)
    c = HG_CHUNK
    nb = r // tb
    tri2, m2, k2 = consts
    fwd = lambda col: pl.BlockSpec((None, tb, w), lambda b, s: (b, s, col))
    bwd = lambda col: pl.BlockSpec((None, tb, w), lambda b, s: (b, nb - 1 - s, col))
    const = lambda arr: pl.BlockSpec(arr.shape, lambda b, s: (0,) * arr.ndim)
    st_spec = pl.BlockSpec((None, 2, nh, HG_DK, HG_DK), lambda b, s: (b, 0, 0, 0, 0))
    kern = functools.partial(_hgrn_kernel, tb=tb, c=c, nh=nh)
    return pl.pallas_call(
        kern,
        grid=(bsz, nb),
        in_specs=[fwd(0), fwd(1), fwd(3), bwd(0), bwd(2), bwd(3), const(lb), const(tri2), const(m2), const(k2),
                  st_spec],
        out_specs=[
            pl.BlockSpec((None, tb, w), lambda b, s: (b, s, 0)),
            pl.BlockSpec((None, tb, w), lambda b, s: (b, nb - 1 - s, 0)),
            st_spec,
        ],
        out_shape=[
            jax.ShapeDtypeStruct((bsz, r, w), F32),
            jax.ShapeDtypeStruct((bsz, r, w), F32),
            jax.ShapeDtypeStruct(s0.shape, F32),
        ],
        scratch_shapes=[pltpu.VMEM((2, nh, HG_DK, HG_DK), F32)],
        compiler_params=_params("parallel", "arbitrary"),
        name="hgrn2_scan",
    )(p, p, p, p, p, p, lb, tri2, m2, k2, s0)


def _even_post_kernel(of_ref, ob_ref, ga_ref, u_ref, gb_ref, up_ref, un_ref, h_ref, mod_ref, hgn_ref, pw_ref,
                      ps_ref, ow_ref, o_ref, ext_ref, y_ref, *, tb, seq, nh):
    j = pl.program_id(1)
    w = nh * HG_DK
    o = of_ref[...] + ob_ref[...]
    for h in range(nh):
        sl = slice(h * HG_DK, (h + 1) * HG_DK)
        y_ref[:, sl] = (_rms(o[:, sl], hgn_ref[:, sl]) * _silu(ga_ref[:, sl])).astype(BF16)
    u = u_ref[...]
    ext_ref[0:POOL_HALO, :] = jnp.where(j > 0, up_ref[...], 0.0)
    ext_ref[POOL_HALO:POOL_HALO + tb, :] = u
    ext_ref[POOL_HALO + tb:, :] = jnp.where(j < pl.num_programs(1) - 1, un_ref[...], 0.0)
    t = j * tb + lax.broadcasted_iota(jnp.int32, (tb, 1), 0)
    grp = w // len(POOL_WINDOWS)
    for gi, win in enumerate(POOL_WINDOWS):
        sl = slice(gi * grp, (gi + 1) * grp)
        acc = ext_ref[POOL_HALO - win // 2:POOL_HALO - win // 2 + tb, sl]
        for off in range(-win // 2 + 1, win // 2):
            acc = acc + ext_ref[POOL_HALO + off:POOL_HALO + off + tb, sl]
        cnt = (jnp.minimum(t + win // 2, seq) - jnp.maximum(t - win // 2, 0)).astype(F32)
        yp = acc * (1.0 / cnt) - u[:, sl]
        yb = _dot(yp.astype(BF16), pw_ref[gi]) * ps_ref[:, sl]
        y_ref[:, w + gi * grp:w + (gi + 1) * grp] = (yb * _silu(gb_ref[:, sl])).astype(BF16)
    o_ref[...] = h_ref[...] + mod_ref[2:3, :] * _dot(y_ref[...], ow_ref[...])


def _even_post(o_f, o_b, p, h, mod, hgn, pool_w, pool_scale, out_w):
    bsz, r, w = o_f.shape
    d = h.shape[-1]
    tb = min(TOK_TILE, r)
    nb = r // tb
    hb = tb // POOL_HALO
    nh = w // HG_DK
    tok = lambda col: pl.BlockSpec((None, tb, w), lambda b, j: (b, j, col))
    const = lambda arr: pl.BlockSpec(arr.shape, lambda b, j: (0,) * arr.ndim)
    kern = functools.partial(_even_post_kernel, tb=tb, seq=r, nh=nh)
    return pl.pallas_call(
        kern,
        grid=(bsz, nb),
        in_specs=[
            tok(0), tok(0), tok(4), tok(5), tok(6),
            pl.BlockSpec((None, POOL_HALO, w), lambda b, j: (b, jnp.maximum(j * hb - 1, 0), 5)),
            pl.BlockSpec((None, POOL_HALO, w), lambda b, j: (b, jnp.minimum((j + 1) * hb, nb * hb - 1), 5)),
            pl.BlockSpec((None, tb, d), lambda b, j: (b, j, 0)),
            pl.BlockSpec((None, 3, d), lambda b, j: (b, 0, 0)),
            const(hgn), const(pool_w), const(pool_scale), const(out_w),
        ],
        out_specs=pl.BlockSpec((None, tb, d), lambda b, j: (b, j, 0)),
        out_shape=jax.ShapeDtypeStruct((bsz, r, d), F32),
        scratch_shapes=[pltpu.VMEM((tb + 2 * POOL_HALO, w), F32), pltpu.VMEM((tb, 2 * w), BF16)],
        compiler_params=_params("parallel", "parallel"),
        name="even_post",
    )(o_f, o_b, p, p, p, p, p, h, mod, hgn, pool_w, pool_scale, out_w)


def _mla_kv_kernel(*refs, rope):
    if rope:
        ckv_ref, kr_ref, g_ref, wuk_ref, wuvt_ref, cos_ref, sin_ref, kcat_ref, vt_ref = refs
    else:
        ckv_ref, kr_ref, g_ref, wuk_ref, wuvt_ref, kcat_ref, vt_ref = refs
    cn = _rms(ckv_ref[...], g_ref[...]).astype(BF16)
    kn = _dot(cn, wuk_ref[...])
    kr = kr_ref[...]
    if rope:
        lane = lax.broadcasted_iota(jnp.int32, kr.shape, 1)
        swapped = jnp.where((lane % (2 * ROPE_FREQ)) < ROPE_FREQ,
                            pltpu.roll(kr, LANES - ROPE_FREQ, 1), pltpu.roll(kr, ROPE_FREQ, 1))
        kr = kr * cos_ref[...] + swapped * sin_ref[...]
    kr = kr.astype(BF16)
    ones_rows = (lax.broadcasted_iota(jnp.int32, (VT_ROWS - MLA_V, kr.shape[0]), 0) == 0).astype(BF16)
    vt = _dot_nt(wuvt_ref[...], cn)
    for h in range(MLA_HEADS):
        kcat_ref[h, :, 0:MLA_NOPE] = kn[:, h * MLA_NOPE:(h + 1) * MLA_NOPE].astype(BF16)
        kcat_ref[h, :, MLA_NOPE:] = kr
        vt_ref[h, 0:MLA_V, :] = vt[h * MLA_V:(h + 1) * MLA_V].astype(BF16)
        vt_ref[h, MLA_V:, :] = ones_rows


def _mla_kv(p, ckv_blk, kr_blk, g, wuk, wuvt, tables, tb):
    bsz, r, _ = p.shape
    rank = g.shape[-1]
    nb = r // tb
    const = lambda arr: pl.BlockSpec(arr.shape, lambda b, j: (0,) * arr.ndim)
    in_specs = [
        pl.BlockSpec((None, tb, rank), lambda b, j: (b, j, ckv_blk)),
        pl.BlockSpec((None, tb, LANES), lambda b, j: (b, j, kr_blk)),
        const(g), const(wuk), const(wuvt),
    ]
    args = [p, p, g, wuk, wuvt]
    if tables is not None:
        in_specs += [pl.BlockSpec((tb, LANES), lambda b, j: (j, 0))] * 2
        args += list(tables)
    return pl.pallas_call(
        functools.partial(_mla_kv_kernel, rope=tables is not None),
        grid=(bsz, nb),
        in_specs=in_specs,
        out_specs=[
            pl.BlockSpec((None, MLA_HEADS, None, tb, QK_PAD), lambda b, j: (b, 0, j, 0, 0)),
            pl.BlockSpec((None, MLA_HEADS, None, VT_ROWS, tb), lambda b, j: (b, 0, j, 0, 0)),
        ],
        out_shape=[
            jax.ShapeDtypeStruct((bsz, MLA_HEADS, nb, tb, QK_PAD), BF16),
            jax.ShapeDtypeStruct((bsz, MLA_HEADS, nb, VT_ROWS, tb), BF16),
        ],
        compiler_params=_params("parallel", "parallel"),
        name="mla_kv_rope" if tables is not None else "mla_kv",
    )(*args)


def _mla_q_kernel(cq_ref, g_ref, wqt_ref, cos_ref, sin_ref, qt_ref):
    cn = _rms(cq_ref[...], g_ref[...]).astype(BF16)
    f = ROPE_FREQ
    qt_all = _dot_nt(wqt_ref[...], cn) * (MLA_SCALE * LOG2_E)
    for h in range(MLA_HEADS):
        qt = qt_all[h * MLA_QK:(h + 1) * MLA_QK]
        qt_ref[h, 0:MLA_NOPE, :] = qt[0:MLA_NOPE].astype(BF16)
        for ax in range(2):
            r0 = MLA_NOPE + ax * 2 * f
            x1 = qt[r0:r0 + f]
            x2 = qt[r0 + f:r0 + 2 * f]
            co = cos_ref[ax]
            si = sin_ref[ax]
            qt_ref[h, r0:r0 + f, :] = (x1 * co - x2 * si).astype(BF16)
            qt_ref[h, r0 + f:r0 + 2 * f, :] = (x2 * co + x1 * si).astype(BF16)
        qt_ref[h, MLA_QK:, :] = jnp.zeros((QK_PAD - MLA_QK, cn.shape[0]), BF16)


def _mla_q(p, cq_blk, g, wqt, cos_t, sin_t):
    bsz, t, _ = p.shape
    rank = g.shape[-1]
    tm = min(TOK_TILE, t)
    const = lambda arr: pl.BlockSpec(arr.shape, lambda b, j: (0,) * arr.ndim)
    tab = pl.BlockSpec((2, ROPE_FREQ, tm), lambda b, j: (0, 0, j))
    return pl.pallas_call(
        _mla_q_kernel,
        grid=(bsz, t // tm),
        in_specs=[pl.BlockSpec((None, tm, rank), lambda b, j: (b, j, cq_blk)), const(g), const(wqt), tab, tab],
        out_specs=pl.BlockSpec((None, MLA_HEADS, QK_PAD, tm), lambda b, j: (b, 0, 0, j)),
        out_shape=jax.ShapeDtypeStruct((bsz, MLA_HEADS, QK_PAD, t), BF16),
        compiler_params=_params("parallel", "parallel"),
        name="mla_q",
    )(p, g, wqt, cos_t, sin_t)


def _attn_kernel(qt_ref, kc_ref, vtc_ref, kl_ref, vtl_ref, g_ref, o_ref, m_ref, acc_ref, s_ref, mx_ref, *, tq, n_lat):
    nsub = tq // Q_SUB
    m_ref[...] = jnp.full(m_ref.shape, -jnp.inf, F32)
    acc_ref[...] = jnp.zeros(acc_ref.shape, F32)

    def scores(k, nxt, g):
        s = _dot(k, qt_ref[:, g * Q_SUB:(g + 1) * Q_SUB])
        s_ref[nxt, g, 0:k.shape[0], :] = s
        mx_ref[nxt, g] = jnp.max(s, axis=0, keepdims=True)

    def substep(k_next, vt_cur, cur, nxt):
        rows = vt_cur.shape[1]
        for g in range(nsub):
            sl = slice(g * Q_SUB, (g + 1) * Q_SUB)
            scores(k_next, nxt, g)
            m_old = m_ref[:, sl]
            m_new = jnp.maximum(m_old, mx_ref[cur, g])
            alpha = jnp.exp2(m_old - m_new)
            p = jnp.exp2(s_ref[cur, g, 0:rows, :] - m_new)
            acc_ref[:, sl] = alpha * acc_ref[:, sl] + _dot(vt_cur, p.astype(BF16))
            m_ref[:, sl] = m_new

    kc = kc_ref[...]
    for g in range(nsub):
        scores(kc, 0, g)
    substep(kl_ref[0], vtc_ref[...], 0, 1)

    def body(j, carry):
        for u in range(2):
            a = 2 * j + u
            substep(kl_ref[jnp.minimum(a + 1, n_lat - 1)], vtl_ref[a], (1 + u) % 2, u % 2)
        return carry

    lax.fori_loop(0, n_lat // 2, body, 0)
    o = (acc_ref[0:MLA_V, :] * (1.0 / acc_ref[MLA_V:MLA_V + 1, :])).T
    o_ref[...] = (o * _silu(g_ref[...])).astype(o_ref.dtype)


def _attn(qt, kc, vtc, kl, vtl, p):
    bsz, nh, _, t = qt.shape
    lc = kc.shape[3]
    tq = min(Q_TILE, t)
    n_lat = kl.shape[2]
    kv = kl.shape[3]
    assert kc.shape[2] == 1 and lc <= kv and n_lat % 2 == 0
    kern = functools.partial(_attn_kernel, tq=tq, n_lat=n_lat)
    ctx5 = lambda arr: pl.BlockSpec((None, None, None) + arr.shape[3:], lambda b, h, i: (b, h, 0, 0, 0))
    full5 = lambda arr: pl.BlockSpec((None, None) + arr.shape[2:], lambda b, h, i: (b, h, 0, 0, 0))
    return pl.pallas_call(
        kern,
        grid=(bsz, nh, t // tq),
        in_specs=[
            pl.BlockSpec((None, None, QK_PAD, tq), lambda b, h, i: (b, h, 0, i)),
            ctx5(kc), ctx5(vtc), full5(kl), full5(vtl),
            pl.BlockSpec((None, tq, MLA_V), lambda b, h, i: (b, i, h)),
        ],
        out_specs=pl.BlockSpec((None, tq, MLA_V), lambda b, h, i: (b, i, h)),
        out_shape=jax.ShapeDtypeStruct((bsz, t, nh * MLA_V), BF16),
        scratch_shapes=[pltpu.VMEM((1, tq), F32), pltpu.VMEM((VT_ROWS, tq), F32),
                        pltpu.VMEM((2, tq // Q_SUB, kv, Q_SUB), F32), pltpu.VMEM((2, tq // Q_SUB, 1, Q_SUB), F32)],
        compiler_params=_params("parallel", "parallel", "arbitrary"),
        name="mla_attention",
    )(qt, kc, vtc, kl, vtl, p)


def _out_final_kernel(y_ref, h_ref, mod_ref, ow_ref, g_ref, o_ref):
    hn = h_ref[...] + mod_ref[2:3, :] * _dot(y_ref[...], ow_ref[...])
    o_ref[...] = _rms(hn, g_ref[...])


def _out_final(y, h, mod, out_w, g):
    bsz, t, d = h.shape
    wi = y.shape[-1]
    tm = min(2 * TOK_TILE, t)
    return pl.pallas_call(
        _out_final_kernel,
        grid=(bsz, t // tm),
        in_specs=[
            pl.BlockSpec((None, tm, wi), lambda b, j: (b, j, 0)),
            pl.BlockSpec((None, tm, d), lambda b, j: (b, j, 0)),
            pl.BlockSpec((None, 3, d), lambda b, j: (b, 0, 0)),
            pl.BlockSpec((wi, d), lambda b, j: (0, 0)),
            pl.BlockSpec((1, d), lambda b, j: (0, 0)),
        ],
        out_specs=pl.BlockSpec((None, tm, d), lambda b, j: (b, j, 0)),
        out_shape=jax.ShapeDtypeStruct((bsz, t, d), F32),
        compiler_params=_params("parallel", "parallel"),
        name="out_final",
    )(y, h, mod, out_w, g)


def _rope_tables(n_tokens):
    rows = n_tokens // GRID_W
    pos_r = jnp.repeat(jnp.arange(rows), GRID_W).astype(F32)
    pos_c = jnp.tile(jnp.arange(GRID_W), rows).astype(F32)
    inv = ROPE_BASE ** (-2.0 * jnp.arange(ROPE_FREQ, dtype=F32) / (MLA_ROPE // 2))
    ang = jnp.stack([pos_r[:, None] * inv, pos_c[:, None] * inv], axis=1)
    cos, sin = jnp.cos(ang), jnp.sin(ang)
    pad = LANES - MLA_ROPE
    cos_k = jnp.pad(jnp.stack([cos, cos], axis=2).reshape(n_tokens, MLA_ROPE), ((0, 0), (0, pad)))
    sin_k = jnp.pad(jnp.stack([-sin, sin], axis=2).reshape(n_tokens, MLA_ROPE), ((0, 0), (0, pad)))
    cos_q = jnp.transpose(cos, (1, 2, 0))
    sin_q = jnp.transpose(sin, (1, 2, 0))
    return (cos_k, sin_k), (cos_q, sin_q)


def kernel(x, c, ctx, c_ctx, ada_w, ada_b, norm_g, out_w, ev_in_w, hg_lb, hg_norm_g, pool_w, pool_scale,
           od_in_w, qa_norm_g, qb_w, kva_norm_g, kvb_w, final_norm_g):
    bsz, t, d = x.shape
    lc = ctx.shape[1]
    depth = ada_w.shape[0]
    assert depth == 2 and t % (2 * TOK_TILE) == 0 and lc % TOK_TILE == 0 and t % GRID_W == 0
    w = hg_norm_g.shape[-1]
    nh = w // HG_DK
    q_rank = qa_norm_g.shape[-1]
    kv_rank = kva_norm_g.shape[-1]
    d_inner = out_w.shape[1]

    n_cond = -(-(bsz + 1) // SUBLANES) * SUBLANES
    cond = jnp.zeros((n_cond, d), F32).at[:bsz].set(c).at[bsz].set(c_ctx)
    mods = _ada(cond, ada_w, ada_b).reshape(depth, n_cond, 3, d)
    mod_l = [mods[l, :bsz] for l in range(depth)]
    mod_c = [mods[l, bsz:bsz + 1] for l in range(depth)]

    lb = jnp.cumsum(jax.nn.softmax(hg_lb.astype(F32), axis=1), axis=1)[:, 0].reshape(2, 1, w)
    w_in0 = ev_in_w[0].astype(BF16)
    g0 = norm_g[0].reshape(1, d)
    ctx_flat = ctx.reshape(1, bsz * lc, d)
    n_in0 = w_in0.shape[1]
    p_c = _modnorm_mm(ctx_flat, mod_c[0], g0, w_in0, 4 * TOK_TILE, n_in0 // 4, "in_proj0_ctx").reshape(bsz, lc, n_in0)
    p_l = _modnorm_mm(x, mod_l[0], g0, w_in0, 4 * TOK_TILE, n_in0 // 4, "in_proj0")
    consts = _hgrn_constants(HG_CHUNK)
    s0 = jnp.zeros((bsz, 2, nh, HG_DK, HG_DK), F32)
    of_c, ob_c, s_c = _hgrn(p_c, lb, s0, consts)
    of_l, ob_l, _ = _hgrn(p_l, lb, s_c, consts)
    hgn = hg_norm_g[0].reshape(1, w)
    pw = pool_w[0].astype(BF16)
    ps = pool_scale[0].reshape(1, w)
    ow0 = out_w[0].astype(BF16)
    mod_c0 = jnp.broadcast_to(mod_c[0], (bsz, 3, d))
    hc1 = _even_post(of_c, ob_c, p_c, ctx, mod_c0, hgn, pw, ps, ow0)
    hl1 = _even_post(of_l, ob_l, p_l, x, mod_l[0], hgn, pw, ps, ow0)

    o1 = q_rank
    o2 = o1 + kv_rank
    o3 = o2 + MLA_ROPE
    w1 = od_in_w[0]
    kr_pad = jnp.zeros((d, LANES - MLA_ROPE), F32)
    w_in1 = jnp.concatenate([w1[:, o3:], w1[:, :o1], w1[:, o1:o2], w1[:, o2:o3], kr_pad], axis=1).astype(BF16)
    n_kv = kv_rank + LANES
    w_in1c = w_in1[:, d_inner + q_rank:]
    g1 = norm_g[1].reshape(1, d)
    p1_c = _modnorm_mm(hc1.reshape(1, bsz * lc, d), mod_c[1], g1, w_in1c, 4 * TOK_TILE, n_kv, "in_proj1_ctx")
    p1_c = p1_c.reshape(bsz, lc, n_kv)
    p1_l = _modnorm_mm(hl1, mod_l[1], g1, w_in1, 2 * TOK_TILE, w_in1.shape[1], "in_proj1")

    kvw = kvb_w[0].reshape(kv_rank, MLA_HEADS, MLA_NOPE + MLA_V)
    wuk = kvw[..., :MLA_NOPE].reshape(kv_rank, MLA_HEADS * MLA_NOPE).astype(BF16)
    wuvt = jnp.transpose(kvw[..., MLA_NOPE:], (1, 2, 0)).reshape(MLA_HEADS * MLA_V, kv_rank).astype(BF16)
    wqt = jnp.transpose(qb_w[0]).astype(BF16)
    kvg = kva_norm_g[0].reshape(1, kv_rank)
    qag = qa_norm_g[0].reshape(1, q_rank)
    tab_k, tab_q = _rope_tables(t)
    kc, vtc = _mla_kv(p1_c, 0, kv_rank // LANES, kvg, wuk, wuvt, None, lc)
    kl, vtl = _mla_kv(p1_l, (d_inner + q_rank) // kv_rank, (d_inner + q_rank + kv_rank) // LANES, kvg, wuk, wuvt,
                      tab_k, min(KV_CHUNK, t))
    qt = _mla_q(p1_l, d_inner // q_rank, qag, wqt, *tab_q)
    y = _attn(qt, kc, vtc, kl, vtl, p1_l)
    return _out_final(y, hl1, mod_l[1], out_w[1].astype(BF16), final_norm_g.reshape(1, d))
```

```python
import functools

import numpy as np
import jax
import jax.numpy as jnp
from jax import lax
from jax.experimental import pallas as pl
from jax.experimental.pallas import tpu as pltpu

F32 = jnp.float32
BF16 = jnp.bfloat16

EPS = 1e-6
GRID_W = 64
HG_DK = 128
POOL_WINDOWS = (2, 4, 8, 16)
MLA_HEADS = 16
MLA_NOPE = 128
MLA_ROPE = 64
MLA_V = 128
MLA_QK = MLA_NOPE + MLA_ROPE
QK_PAD = 256
VT_ROWS = MLA_V + 16
MLA_SCALE = MLA_QK ** -0.5
LOG2_E = 1.4426950408889634
ROPE_FREQ = MLA_ROPE // 4
ROPE_BASE = 10000.0

LANES = 128
SUBLANES = 8
VMEM_LIMIT = 48 * 1024 * 1024

HG_CHUNK = 64
TOK_TILE = 256
KV_CHUNK = 512
Q_TILE = 4096
Q_SUB = 256
POOL_HALO = 8


def _dot(a, b):
    return jnp.dot(a, b, preferred_element_type=F32)


def _dot_nt(a, b):
    return lax.dot_general(a, b, (((1,), (1,)), ((), ())), preferred_element_type=F32)


def _dot_tn(a, b):
    return lax.dot_general(a, b, (((0,), (0,)), ((), ())), preferred_element_type=F32)


def _sigmoid(x):
    return 0.5 * jnp.tanh(0.5 * x) + 0.5


def _silu(x):
    h = 0.5 * x
    return h + h * jnp.tanh(h)


def _split_bf16(x):
    hi = x.astype(BF16)
    lo = (x - hi.astype(F32)).astype(BF16)
    return hi, lo


def _params(*sem):
    return pltpu.CompilerParams(dimension_semantics=sem, vmem_limit_bytes=VMEM_LIMIT)


def _ada_kernel(c_ref, w_ref, b_ref, o_ref):
    c = c_ref[...]
    s_hi, s_lo = _split_bf16(_silu(c))
    w_hi, w_lo = _split_bf16(w_ref[...])
    o_ref[...] = _dot(s_hi, w_hi) + _dot(s_lo, w_hi) + _dot(s_hi, w_lo) + b_ref[...]


def _ada(cond, ada_w, ada_b):
    depth, d, _ = ada_w.shape
    r = cond.shape[0]
    return pl.pallas_call(
        _ada_kernel,
        grid=(depth, 3),
        in_specs=[
            pl.BlockSpec((r, d), lambda l, j: (0, 0)),
            pl.BlockSpec((None, d, d), lambda l, j: (l, 0, j)),
            pl.BlockSpec((None, 1, d), lambda l, j: (l, 0, j)),
        ],
        out_specs=pl.BlockSpec((None, r, d), lambda l, j: (l, 0, j)),
        out_shape=jax.ShapeDtypeStruct((depth, r, 3 * d), F32),
        compiler_params=_params("parallel", "parallel"),
        name="ada_modulation",
    )(cond, ada_w, ada_b.reshape(depth, 1, 3 * d))


def _rms(x, g):
    return x * lax.rsqrt(jnp.mean(x * x, axis=-1, keepdims=True) + EPS) * g


def _modnorm_mm_kernel(x_ref, mod_ref, g_ref, w_ref, o_ref, z_ref):
    @pl.when(pl.program_id(2) == 0)
    def _():
        y = _rms(x_ref[...], g_ref[...])
        z_ref[...] = (y * (1.0 + mod_ref[1:2, :]) + mod_ref[0:1, :]).astype(BF16)

    o_ref[...] = _dot(z_ref[...], w_ref[...]).astype(o_ref.dtype)


def _modnorm_mm(x, mod, g, w, tm, tn, name):
    bx, r, d = x.shape
    n = w.shape[1]
    tm = min(tm, r)
    return pl.pallas_call(
        _modnorm_mm_kernel,
        grid=(bx, r // tm, n // tn),
        in_specs=[
            pl.BlockSpec((None, tm, d), lambda b, i, j: (b, i, 0)),
            pl.BlockSpec((None, 3, d), lambda b, i, j: (b, 0, 0)),
            pl.BlockSpec((1, d), lambda b, i, j: (0, 0)),
            pl.BlockSpec((d, tn), lambda b, i, j: (0, j)),
        ],
        out_specs=pl.BlockSpec((None, tm, tn), lambda b, i, j: (b, i, j)),
        out_shape=jax.ShapeDtypeStruct((bx, r, n), F32),
        scratch_shapes=[pltpu.VMEM((tm, d), BF16)],
        compiler_params=_params("parallel", "parallel", "arbitrary"),
        name=name,
    )(x, mod, g, w)


def _hgrn_levels(c):
    w = c // 2
    out = []
    while w >= 1:
        out.append(w)
        w //= 2
    return tuple(out)


def _hgrn_constants(c):
    t = np.arange(c)
    tri = np.tril(np.ones((c, c), np.float32))
    masks = []
    for w in _hgrn_levels(c):
        blk = t // (2 * w)
        first = (t % (2 * w)) < w
        masks.append(((blk[:, None] == blk[None, :]) & (~first[:, None]) & first[None, :]).astype(np.float32))
    masks = np.stack(masks)
    tri2 = np.stack([tri, tri[::-1, ::-1]])
    m2 = np.stack([masks, masks[:, ::-1, ::-1]])
    return jnp.asarray(tri2, BF16), jnp.asarray(m2, F32)


def _hgrn_level_operand(b, g, k, q, w, d):
    c, width = b.shape
    row = lax.broadcasted_iota(jnp.int32, (c, 1), 0)
    keys_first = d == 0
    if w >= SUBLANES:
        ref_off = w - 1 if d == 0 else w
        pieces = []
        for r0 in range(0, c, 2 * w):
            bref = jnp.broadcast_to(b[r0 + ref_off:r0 + ref_off + 1, :], (w, width))
            for half in range(2):
                sl = slice(r0 + half * w, r0 + (half + 1) * w)
                if (half == 0) == keys_first:
                    pieces.append(k[sl] * jnp.exp2(bref - b[sl]))
                else:
                    pieces.append(q[sl] * jnp.exp2(b[sl] - bref))
        return jnp.concatenate(pieces, axis=0)
    before = (row % (2 * w)) < w
    kq = jnp.where(before == keys_first, k, q)
    if w == 1:
        moving = (row % 2 == 1) if d == 0 else (row % 2 == 0)
        return kq * jnp.exp2(jnp.where(moving, g, 0.0))
    ref_off = w - 1 if d == 0 else w
    sub = lax.broadcasted_iota(jnp.int32, (SUBLANES, 1), 0)
    pieces = []
    for r0 in range(0, c, SUBLANES):
        lo = jnp.broadcast_to(b[r0 + ref_off:r0 + ref_off + 1, :], (SUBLANES, width))
        if 2 * w == SUBLANES:
            pieces.append(lo)
        else:
            hi = jnp.broadcast_to(b[r0 + 2 * w + ref_off:r0 + 2 * w + ref_off + 1, :], (SUBLANES, width))
            pieces.append(jnp.where(sub < 2 * w, lo, hi))
    bref = jnp.concatenate(pieces, axis=0)
    return kq * jnp.exp2(-jnp.abs(b - bref))


def _hgrn_wide(q_ref, f_ref, v_ref, lb, tri, d, r0, *, c):
    last = c - 1 if d == 0 else 0
    rows = pl.ds(r0, c)
    half = 0.5 * (1.0 - lb)
    f = (lb + half) + half * jnp.tanh(0.5 * f_ref[rows, :])
    g = jnp.log2(f)
    g_hi, g_lo = _split_bf16(g)
    b = _dot(tri, g_hi) + _dot(tri, g_lo)
    q = _silu(q_ref[rows, :])
    k = 1.0 - f
    v = v_ref[rows, :]
    bl = b[last:last + 1, :]
    return dict(d=d, rows=rows, g=g, b=b, q=q, k=k, v=v, vb=v.astype(BF16), qk=q * k,
                qe=(q * jnp.exp2(b)).astype(BF16), kend=(k * jnp.exp2(bl - b)).astype(BF16), ebl=jnp.exp2(bl))


def _head(x, h):
    return x[:, h * HG_DK:(h + 1) * HG_DK]


def _hgrn_pairs(s, h, mk_ref, st_ref, *, c):
    d = s["d"]
    inter = _dot_nt(_head(s["qe"], h), st_ref[d, h].astype(BF16))
    att = None
    for l, w in enumerate(_hgrn_levels(c)):
        x = _hgrn_level_operand(_head(s["b"], h), _head(s["g"], h), _head(s["k"], h), _head(s["q"], h), w, d)
        x = x.astype(BF16)
        t = mk_ref[d, l] * _dot_nt(x, x)
        att = t if att is None else att + t
    return inter, att


def _hgrn_finish(s, h, inter, att, o_ref, st_ref):
    d = s["d"]
    vb = _head(s["vb"], h)
    diag = jnp.sum(_head(s["qk"], h), axis=-1, keepdims=True)
    o_ref[s["rows"], h * HG_DK:(h + 1) * HG_DK] = inter + _dot(att.astype(BF16), vb) + diag * _head(s["v"], h)
    st_ref[d, h] = _head(s["ebl"], h) * st_ref[d, h] + _dot_tn(vb, _head(s["kend"], h))


def _hgrn_kernel(qf_ref, ff_ref, vf_ref, qb_ref, fb_ref, vb_ref, lb_ref, tri_ref, mk_ref, s0_ref,
                 of_ref, ob_ref, sout_ref, st_ref, *, tb, c, nh):
    nchunk = tb // c

    @pl.when(pl.program_id(1) == 0)
    def _():
        st_ref[...] = s0_ref[...]

    def body(cc, carry):
        rf = pl.multiple_of(cc * c, c)
        rb = pl.multiple_of((nchunk - 1 - cc) * c, c)
        sides = ((_hgrn_wide(qf_ref, ff_ref, vf_ref, lb_ref[0], tri_ref[0], 0, rf, c=c), of_ref),
                 (_hgrn_wide(qb_ref, fb_ref, vb_ref, lb_ref[1], tri_ref[1], 1, rb, c=c), ob_ref))
        pending = [None, None]
        for h in range(nh + 1):
            cur = [_hgrn_pairs(s, h, mk_ref, st_ref, c=c) if h < nh else None for s, _ in sides]
            for (s, o_ref), p in zip(sides, pending):
                if p is not None:
                    _hgrn_finish(s, h - 1, *p, o_ref, st_ref)
            pending = cur
        return carry

    lax.fori_loop(0, nchunk, body, 0)

    @pl.when(pl.program_id(1) == pl.num_programs(1) - 1)
    def _():
        sout_ref[...] = st_ref[...]


def _hgrn(p, lb, s0, consts):
    bsz, r, _ = p.shape
    w = lb.shape[-1]
    nh = w // HG_DK
    tb = min(TOK_TILE, r)
    c = HG_CHUNK
    nb = r // tb
    tri2, m2 = consts
    fwd = lambda col: pl.BlockSpec((None, tb, w), lambda b, s: (b, s, col))
    bwd = lambda col: pl.BlockSpec((None, tb, w), lambda b, s: (b, nb - 1 - s, col))
    const = lambda arr: pl.BlockSpec(arr.shape, lambda b, s: (0,) * arr.ndim)
    st_spec = pl.BlockSpec((None, 2, nh, HG_DK, HG_DK), lambda b, s: (b, 0, 0, 0, 0))
    kern = functools.partial(_hgrn_kernel, tb=tb, c=c, nh=nh)
    return pl.pallas_call(
        kern,
        grid=(bsz, nb),
        in_specs=[fwd(0), fwd(1), fwd(3), bwd(0), bwd(2), bwd(3), const(lb), const(tri2), const(m2),
                  st_spec],
        out_specs=[
            pl.BlockSpec((None, tb, w), lambda b, s: (b, s, 0)),
            pl.BlockSpec((None, tb, w), lambda b, s: (b, nb - 1 - s, 0)),
            st_spec,
        ],
        out_shape=[
            jax.ShapeDtypeStruct((bsz, r, w), F32),
            jax.ShapeDtypeStruct((bsz, r, w), F32),
            jax.ShapeDtypeStruct(s0.shape, F32),
        ],
        scratch_shapes=[pltpu.VMEM((2, nh, HG_DK, HG_DK), F32)],
        compiler_params=_params("parallel", "arbitrary"),
        name="hgrn2_scan",
    )(p, p, p, p, p, p, lb, tri2, m2, s0)


def _even_post_kernel(of_ref, ob_ref, ga_ref, u_ref, gb_ref, up_ref, un_ref, h_ref, mod_ref, hgn_ref, pw_ref,
                      ps_ref, ow_ref, o_ref, ext_ref, y_ref, *, tb, seq, nh):
    j = pl.program_id(1)
    w = nh * HG_DK
    o = of_ref[...] + ob_ref[...]
    for h in range(nh):
        sl = slice(h * HG_DK, (h + 1) * HG_DK)
        y_ref[:, sl] = (_rms(o[:, sl], hgn_ref[:, sl]) * _silu(ga_ref[:, sl])).astype(BF16)
    u = u_ref[...]
    ext_ref[0:POOL_HALO, :] = jnp.where(j > 0, up_ref[...], 0.0)
    ext_ref[POOL_HALO:POOL_HALO + tb, :] = u
    ext_ref[POOL_HALO + tb:, :] = jnp.where(j < pl.num_programs(1) - 1, un_ref[...], 0.0)
    t = j * tb + lax.broadcasted_iota(jnp.int32, (tb, 1), 0)
    grp = w // len(POOL_WINDOWS)
    for gi, win in enumerate(POOL_WINDOWS):
        sl = slice(gi * grp, (gi + 1) * grp)
        acc = ext_ref[POOL_HALO - win // 2:POOL_HALO - win // 2 + tb, sl]
        for off in range(-win // 2 + 1, win // 2):
            acc = acc + ext_ref[POOL_HALO + off:POOL_HALO + off + tb, sl]
        cnt = (jnp.minimum(t + win // 2, seq) - jnp.maximum(t - win // 2, 0)).astype(F32)
        yp = acc * (1.0 / cnt) - u[:, sl]
        yb = _dot(yp.astype(BF16), pw_ref[gi]) * ps_ref[:, sl]
        y_ref[:, w + gi * grp:w + (gi + 1) * grp] = (yb * _silu(gb_ref[:, sl])).astype(BF16)
    o_ref[...] = h_ref[...] + mod_ref[2:3, :] * _dot(y_ref[...], ow_ref[...])


def _even_post(o_f, o_b, p, h, mod, hgn, pool_w, pool_scale, out_w):
    bsz, r, w = o_f.shape
    d = h.shape[-1]
    tb = min(TOK_TILE, r)
    nb = r // tb
    hb = tb // POOL_HALO
    nh = w // HG_DK
    tok = lambda col: pl.BlockSpec((None, tb, w), lambda b, j: (b, j, col))
    const = lambda arr: pl.BlockSpec(arr.shape, lambda b, j: (0,) * arr.ndim)
    kern = functools.partial(_even_post_kernel, tb=tb, seq=r, nh=nh)
    return pl.pallas_call(
        kern,
        grid=(bsz, nb),
        in_specs=[
            tok(0), tok(0), tok(4), tok(5), tok(6),
            pl.BlockSpec((None, POOL_HALO, w), lambda b, j: (b, jnp.maximum(j * hb - 1, 0), 5)),
            pl.BlockSpec((None, POOL_HALO, w), lambda b, j: (b, jnp.minimum((j + 1) * hb, nb * hb - 1), 5)),
            pl.BlockSpec((None, tb, d), lambda b, j: (b, j, 0)),
            pl.BlockSpec((None, 3, d), lambda b, j: (b, 0, 0)),
            const(hgn), const(pool_w), const(pool_scale), const(out_w),
        ],
        out_specs=pl.BlockSpec((None, tb, d), lambda b, j: (b, j, 0)),
        out_shape=jax.ShapeDtypeStruct((bsz, r, d), F32),
        scratch_shapes=[pltpu.VMEM((tb + 2 * POOL_HALO, w), F32), pltpu.VMEM((tb, 2 * w), BF16)],
        compiler_params=_params("parallel", "parallel"),
        name="even_post",
    )(o_f, o_b, p, p, p, p, p, h, mod, hgn, pool_w, pool_scale, out_w)


def _mla_kv_kernel(*refs, rope):
    if rope:
        ckv_ref, kr_ref, g_ref, wuk_ref, wuvt_ref, cos_ref, sin_ref, kcat_ref, vt_ref = refs
    else:
        ckv_ref, kr_ref, g_ref, wuk_ref, wuvt_ref, kcat_ref, vt_ref = refs
    cn = _rms(ckv_ref[...], g_ref[...]).astype(BF16)
    kn = _dot(cn, wuk_ref[...])
    kr = kr_ref[...]
    if rope:
        lane = lax.broadcasted_iota(jnp.int32, kr.shape, 1)
        swapped = jnp.where((lane % (2 * ROPE_FREQ)) < ROPE_FREQ,
                            pltpu.roll(kr, LANES - ROPE_FREQ, 1), pltpu.roll(kr, ROPE_FREQ, 1))
        kr = kr * cos_ref[...] + swapped * sin_ref[...]
    kr = kr.astype(BF16)
    ones_rows = (lax.broadcasted_iota(jnp.int32, (VT_ROWS - MLA_V, kr.shape[0]), 0) == 0).astype(BF16)
    vt = _dot_nt(wuvt_ref[...], cn)
    for h in range(MLA_HEADS):
        kcat_ref[h, :, 0:MLA_NOPE] = kn[:, h * MLA_NOPE:(h + 1) * MLA_NOPE].astype(BF16)
        kcat_ref[h, :, MLA_NOPE:] = kr
        vt_ref[h, 0:MLA_V, :] = vt[h * MLA_V:(h + 1) * MLA_V].astype(BF16)
        vt_ref[h, MLA_V:, :] = ones_rows


def _mla_kv(p, ckv_blk, kr_blk, g, wuk, wuvt, tables, tb):
    bsz, r, _ = p.shape
    rank = g.shape[-1]
    nb = r // tb
    const = lambda arr: pl.BlockSpec(arr.shape, lambda b, j: (0,) * arr.ndim)
    in_specs = [
        pl.BlockSpec((None, tb, rank), lambda b, j: (b, j, ckv_blk)),
        pl.BlockSpec((None, tb, LANES), lambda b, j: (b, j, kr_blk)),
        const(g), const(wuk), const(wuvt),
    ]
    args = [p, p, g, wuk, wuvt]
    if tables is not None:
        in_specs += [pl.BlockSpec((tb, LANES), lambda b, j: (j, 0))] * 2
        args += list(tables)
    return pl.pallas_call(
        functools.partial(_mla_kv_kernel, rope=tables is not None),
        grid=(bsz, nb),
        in_specs=in_specs,
        out_specs=[
            pl.BlockSpec((None, MLA_HEADS, None, tb, QK_PAD), lambda b, j: (b, 0, j, 0, 0)),
            pl.BlockSpec((None, MLA_HEADS, None, VT_ROWS, tb), lambda b, j: (b, 0, j, 0, 0)),
        ],
        out_shape=[
            jax.ShapeDtypeStruct((bsz, MLA_HEADS, nb, tb, QK_PAD), BF16),
            jax.ShapeDtypeStruct((bsz, MLA_HEADS, nb, VT_ROWS, tb), BF16),
        ],
        compiler_params=_params("parallel", "parallel"),
        name="mla_kv_rope" if tables is not None else "mla_kv",
    )(*args)


def _mla_q_kernel(cq_ref, g_ref, wqt_ref, cos_ref, sin_ref, qt_ref):
    cn = _rms(cq_ref[...], g_ref[...]).astype(BF16)
    f = ROPE_FREQ
    qt_all = _dot_nt(wqt_ref[...], cn) * (MLA_SCALE * LOG2_E)
    for h in range(MLA_HEADS):
        qt = qt_all[h * MLA_QK:(h + 1) * MLA_QK]
        qt_ref[h, 0:MLA_NOPE, :] = qt[0:MLA_NOPE].astype(BF16)
        for ax in range(2):
            r0 = MLA_NOPE + ax * 2 * f
            x1 = qt[r0:r0 + f]
            x2 = qt[r0 + f:r0 + 2 * f]
            co = cos_ref[ax]
            si = sin_ref[ax]
            qt_ref[h, r0:r0 + f, :] = (x1 * co - x2 * si).astype(BF16)
            qt_ref[h, r0 + f:r0 + 2 * f, :] = (x2 * co + x1 * si).astype(BF16)
        qt_ref[h, MLA_QK:, :] = jnp.zeros((QK_PAD - MLA_QK, cn.shape[0]), BF16)


def _mla_q(p, cq_blk, g, wqt, cos_t, sin_t):
    bsz, t, _ = p.shape
    rank = g.shape[-1]
    tm = min(TOK_TILE, t)
    const = lambda arr: pl.BlockSpec(arr.shape, lambda b, j: (0,) * arr.ndim)
    tab = pl.BlockSpec((2, ROPE_FREQ, tm), lambda b, j: (0, 0, j))
    return pl.pallas_call(
        _mla_q_kernel,
        grid=(bsz, t // tm),
        in_specs=[pl.BlockSpec((None, tm, rank), lambda b, j: (b, j, cq_blk)), const(g), const(wqt), tab, tab],
        out_specs=pl.BlockSpec((None, MLA_HEADS, QK_PAD, tm), lambda b, j: (b, 0, 0, j)),
        out_shape=jax.ShapeDtypeStruct((bsz, MLA_HEADS, QK_PAD, t), BF16),
        compiler_params=_params("parallel", "parallel"),
        name="mla_q",
    )(p, g, wqt, cos_t, sin_t)


def _attn_kernel(qt_ref, kc_ref, vtc_ref, kl_ref, vtl_ref, g_ref, o_ref, m_ref, acc_ref, s_ref, mx_ref, *, tq, n_lat):
    nsub = tq // Q_SUB
    m_ref[...] = jnp.full(m_ref.shape, -jnp.inf, F32)
    acc_ref[...] = jnp.zeros(acc_ref.shape, F32)

    def scores(k, nxt, g):
        s = _dot(k, qt_ref[:, g * Q_SUB:(g + 1) * Q_SUB])
        s_ref[nxt, g, 0:k.shape[0], :] = s
        mx_ref[nxt, g] = jnp.max(s, axis=0, keepdims=True)

    def substep(k_next, vt_cur, cur, nxt):
        rows = vt_cur.shape[1]
        for g in range(nsub):
            sl = slice(g * Q_SUB, (g + 1) * Q_SUB)
            scores(k_next, nxt, g)
            m_old = m_ref[:, sl]
            m_new = jnp.maximum(m_old, mx_ref[cur, g])
            alpha = jnp.exp2(m_old - m_new)
            p = jnp.exp2(s_ref[cur, g, 0:rows, :] - m_new)
            acc_ref[:, sl] = alpha * acc_ref[:, sl] + _dot(vt_cur, p.astype(BF16))
            m_ref[:, sl] = m_new

    kc = kc_ref[...]
    for g in range(nsub):
        scores(kc, 0, g)
    substep(kl_ref[0], vtc_ref[...], 0, 1)

    def body(j, carry):
        for u in range(2):
            a = 2 * j + u
            substep(kl_ref[jnp.minimum(a + 1, n_lat - 1)], vtl_ref[a], (1 + u) % 2, u % 2)
        return carry

    lax.fori_loop(0, n_lat // 2, body, 0)
    o = (acc_ref[0:MLA_V, :] * (1.0 / acc_ref[MLA_V:MLA_V + 1, :])).T
    o_ref[...] = (o * _silu(g_ref[...])).astype(o_ref.dtype)


def _attn(qt, kc, vtc, kl, vtl, p):
    bsz, nh, _, t = qt.shape
    lc = kc.shape[3]
    tq = min(Q_TILE, t)
    n_lat = kl.shape[2]
    kv = kl.shape[3]
    assert kc.shape[2] == 1 and lc <= kv and n_lat % 2 == 0
    kern = functools.partial(_attn_kernel, tq=tq, n_lat=n_lat)
    ctx5 = lambda arr: pl.BlockSpec((None, None, None) + arr.shape[3:], lambda b, h, i: (b, h, 0, 0, 0))
    full5 = lambda arr: pl.BlockSpec((None, None) + arr.shape[2:], lambda b, h, i: (b, h, 0, 0, 0))
    return pl.pallas_call(
        kern,
        grid=(bsz, nh, t // tq),
        in_specs=[
            pl.BlockSpec((None, None, QK_PAD, tq), lambda b, h, i: (b, h, 0, i)),
            ctx5(kc), ctx5(vtc), full5(kl), full5(vtl),
            pl.BlockSpec((None, tq, MLA_V), lambda b, h, i: (b, i, h)),
        ],
        out_specs=pl.BlockSpec((None, tq, MLA_V), lambda b, h, i: (b, i, h)),
        out_shape=jax.ShapeDtypeStruct((bsz, t, nh * MLA_V), BF16),
        scratch_shapes=[pltpu.VMEM((1, tq), F32), pltpu.VMEM((VT_ROWS, tq), F32),
                        pltpu.VMEM((2, tq // Q_SUB, kv, Q_SUB), F32), pltpu.VMEM((2, tq // Q_SUB, 1, Q_SUB), F32)],
        compiler_params=_params("parallel", "parallel", "arbitrary"),
        name="mla_attention",
    )(qt, kc, vtc, kl, vtl, p)


def _out_final_kernel(y_ref, h_ref, mod_ref, ow_ref, g_ref, o_ref):
    hn = h_ref[...] + mod_ref[2:3, :] * _dot(y_ref[...], ow_ref[...])
    o_ref[...] = _rms(hn, g_ref[...])


def _out_final(y, h, mod, out_w, g):
    bsz, t, d = h.shape
    wi = y.shape[-1]
    tm = min(2 * TOK_TILE, t)
    return pl.pallas_call(
        _out_final_kernel,
        grid=(bsz, t // tm),
        in_specs=[
            pl.BlockSpec((None, tm, wi), lambda b, j: (b, j, 0)),
            pl.BlockSpec((None, tm, d), lambda b, j: (b, j, 0)),
            pl.BlockSpec((None, 3, d), lambda b, j: (b, 0, 0)),
            pl.BlockSpec((wi, d), lambda b, j: (0, 0)),
            pl.BlockSpec((1, d), lambda b, j: (0, 0)),
        ],
        out_specs=pl.BlockSpec((None, tm, d), lambda b, j: (b, j, 0)),
        out_shape=jax.ShapeDtypeStruct((bsz, t, d), F32),
        compiler_params=_params("parallel", "parallel"),
        name="out_final",
    )(y, h, mod, out_w, g)


def _rope_tables(n_tokens):
    rows = n_tokens // GRID_W
    pos_r = jnp.repeat(jnp.arange(rows), GRID_W).astype(F32)
    pos_c = jnp.tile(jnp.arange(GRID_W), rows).astype(F32)
    inv = ROPE_BASE ** (-2.0 * jnp.arange(ROPE_FREQ, dtype=F32) / (MLA_ROPE // 2))
    ang = jnp.stack([pos_r[:, None] * inv, pos_c[:, None] * inv], axis=1)
    cos, sin = jnp.cos(ang), jnp.sin(ang)
    pad = LANES - MLA_ROPE
    cos_k = jnp.pad(jnp.stack([cos, cos], axis=2).reshape(n_tokens, MLA_ROPE), ((0, 0), (0, pad)))
    sin_k = jnp.pad(jnp.stack([-sin, sin], axis=2).reshape(n_tokens, MLA_ROPE), ((0, 0), (0, pad)))
    cos_q = jnp.transpose(cos, (1, 2, 0))
    sin_q = jnp.transpose(sin, (1, 2, 0))
    return (cos_k, sin_k), (cos_q, sin_q)


def kernel(x, c, ctx, c_ctx, ada_w, ada_b, norm_g, out_w, ev_in_w, hg_lb, hg_norm_g, pool_w, pool_scale,
           od_in_w, qa_norm_g, qb_w, kva_norm_g, kvb_w, final_norm_g):
    bsz, t, d = x.shape
    lc = ctx.shape[1]
    depth = ada_w.shape[0]
    assert depth == 2 and t % (2 * TOK_TILE) == 0 and lc % TOK_TILE == 0 and t % GRID_W == 0
    w = hg_norm_g.shape[-1]
    nh = w // HG_DK
    q_rank = qa_norm_g.shape[-1]
    kv_rank = kva_norm_g.shape[-1]
    d_inner = out_w.shape[1]

    n_cond = -(-(bsz + 1) // SUBLANES) * SUBLANES
    cond = jnp.zeros((n_cond, d), F32).at[:bsz].set(c).at[bsz].set(c_ctx)
    mods = _ada(cond, ada_w, ada_b).reshape(depth, n_cond, 3, d)
    mod_l = [mods[l, :bsz] for l in range(depth)]
    mod_c = [mods[l, bsz:bsz + 1] for l in range(depth)]

    lb = jnp.cumsum(jax.nn.softmax(hg_lb.astype(F32), axis=1), axis=1)[:, 0].reshape(2, 1, w)
    w_in0 = ev_in_w[0].astype(BF16)
    g0 = norm_g[0].reshape(1, d)
    ctx_flat = ctx.reshape(1, bsz * lc, d)
    n_in0 = w_in0.shape[1]
    p_c = _modnorm_mm(ctx_flat, mod_c[0], g0, w_in0, 4 * TOK_TILE, n_in0 // 4, "in_proj0_ctx").reshape(bsz, lc, n_in0)
    p_l = _modnorm_mm(x, mod_l[0], g0, w_in0, 4 * TOK_TILE, n_in0 // 4, "in_proj0")
    consts = _hgrn_constants(HG_CHUNK)
    s0 = jnp.zeros((bsz, 2, nh, HG_DK, HG_DK), F32)
    of_c, ob_c, s_c = _hgrn(p_c, lb, s0, consts)
    of_l, ob_l, _ = _hgrn(p_l, lb, s_c, consts)
    hgn = hg_norm_g[0].reshape(1, w)
    pw = pool_w[0].astype(BF16)
    ps = pool_scale[0].reshape(1, w)
    ow0 = out_w[0].astype(BF16)
    mod_c0 = jnp.broadcast_to(mod_c[0], (bsz, 3, d))
    hc1 = _even_post(of_c, ob_c, p_c, ctx, mod_c0, hgn, pw, ps, ow0)
    hl1 = _even_post(of_l, ob_l, p_l, x, mod_l[0], hgn, pw, ps, ow0)

    o1 = q_rank
    o2 = o1 + kv_rank
    o3 = o2 + MLA_ROPE
    w1 = od_in_w[0]
    kr_pad = jnp.zeros((d, LANES - MLA_ROPE), F32)
    w_in1 = jnp.concatenate([w1[:, o3:], w1[:, :o1], w1[:, o1:o2], w1[:, o2:o3], kr_pad], axis=1).astype(BF16)
    n_kv = kv_rank + LANES
    w_in1c = w_in1[:, d_inner + q_rank:]
    g1 = norm_g[1].reshape(1, d)
    p1_c = _modnorm_mm(hc1.reshape(1, bsz * lc, d), mod_c[1], g1, w_in1c, 4 * TOK_TILE, n_kv, "in_proj1_ctx")
    p1_c = p1_c.reshape(bsz, lc, n_kv)
    p1_l = _modnorm_mm(hl1, mod_l[1], g1, w_in1, 2 * TOK_TILE, w_in1.shape[1], "in_proj1")

    kvw = kvb_w[0].reshape(kv_rank, MLA_HEADS, MLA_NOPE + MLA_V)
    wuk = kvw[..., :MLA_NOPE].reshape(kv_rank, MLA_HEADS * MLA_NOPE).astype(BF16)
    wuvt = jnp.transpose(kvw[..., MLA_NOPE:], (1, 2, 0)).reshape(MLA_HEADS * MLA_V, kv_rank).astype(BF16)
    wqt = jnp.transpose(qb_w[0]).astype(BF16)
    kvg = kva_norm_g[0].reshape(1, kv_rank)
    qag = qa_norm_g[0].reshape(1, q_rank)
    tab_k, tab_q = _rope_tables(t)
    kc, vtc = _mla_kv(p1_c, 0, kv_rank // LANES, kvg, wuk, wuvt, None, lc)
    kl, vtl = _mla_kv(p1_l, (d_inner + q_rank) // kv_rank, (d_inner + q_rank + kv_rank) // LANES, kvg, wuk, wuvt,
                      tab_k, min(KV_CHUNK, t))
    qt = _mla_q(p1_l, d_inner // q_rank, qag, wqt, *tab_q)
    y = _attn(qt, kc, vtc, kl, vtl, p1_l)
    return _out_final(y, hl1, mod_l[1], out_w[1].astype(BF16), final_norm_g.reshape(1, d))
```

```python
import functools

import numpy as np
import jax
import jax.numpy as jnp
from jax import lax
from jax.experimental import pallas as pl
from jax.experimental.pallas import tpu as pltpu

F32 = jnp.float32
BF16 = jnp.bfloat16

EPS = 1e-6
GRID_W = 64
HG_DK = 128
POOL_WINDOWS = (2, 4, 8, 16)
MLA_HEADS = 16
MLA_NOPE = 128
MLA_ROPE = 64
MLA_V = 128
MLA_QK = MLA_NOPE + MLA_ROPE
QK_PAD = 256
VT_ROWS = MLA_V + 16
MLA_SCALE = MLA_QK ** -0.5
LOG2_E = 1.4426950408889634
ROPE_FREQ = MLA_ROPE // 4
ROPE_BASE = 10000.0

LANES = 128
SUBLANES = 8
VMEM_LIMIT = 48 * 1024 * 1024

HG_CHUNK = 64
TOK_TILE = 256
KV_CHUNK = 512
Q_TILE = 4096
Q_SUB = 256
POOL_HALO = 8


def _dot(a, b):
    return jnp.dot(a, b, preferred_element_type=F32)


def _dot_nt(a, b):
    return lax.dot_general(a, b, (((1,), (1,)), ((), ())), preferred_element_type=F32)


def _dot_tn(a, b):
    return lax.dot_general(a, b, (((0,), (0,)), ((), ())), preferred_element_type=F32)


def _sigmoid(x):
    return 0.5 * jnp.tanh(0.5 * x) + 0.5


def _silu(x):
    h = 0.5 * x
    return h + h * jnp.tanh(h)


def _split_bf16(x):
    hi = x.astype(BF16)
    lo = (x - hi.astype(F32)).astype(BF16)
    return hi, lo


def _params(*sem):
    return pltpu.CompilerParams(dimension_semantics=sem, vmem_limit_bytes=VMEM_LIMIT)


def _ada_kernel(c_ref, w_ref, b_ref, o_ref):
    c = c_ref[...]
    s_hi, s_lo = _split_bf16(_silu(c))
    w_hi, w_lo = _split_bf16(w_ref[...])
    o_ref[...] = _dot(s_hi, w_hi) + _dot(s_lo, w_hi) + _dot(s_hi, w_lo) + b_ref[...]


def _ada(cond, ada_w, ada_b):
    depth, d, _ = ada_w.shape
    r = cond.shape[0]
    return pl.pallas_call(
        _ada_kernel,
        grid=(depth, 3),
        in_specs=[
            pl.BlockSpec((r, d), lambda l, j: (0, 0)),
            pl.BlockSpec((None, d, d), lambda l, j: (l, 0, j)),
            pl.BlockSpec((None, 1, d), lambda l, j: (l, 0, j)),
        ],
        out_specs=pl.BlockSpec((None, r, d), lambda l, j: (l, 0, j)),
        out_shape=jax.ShapeDtypeStruct((depth, r, 3 * d), F32),
        compiler_params=_params("parallel", "parallel"),
        name="ada_modulation",
    )(cond, ada_w, ada_b.reshape(depth, 1, 3 * d))


def _rms(x, g):
    return x * lax.rsqrt(jnp.mean(x * x, axis=-1, keepdims=True) + EPS) * g


def _modnorm_mm_kernel(x_ref, mod_ref, g_ref, w_ref, o_ref, z_ref):
    @pl.when(pl.program_id(2) == 0)
    def _():
        y = _rms(x_ref[...], g_ref[...])
        z_ref[...] = (y * (1.0 + mod_ref[1:2, :]) + mod_ref[0:1, :]).astype(BF16)

    o_ref[...] = _dot(z_ref[...], w_ref[...]).astype(o_ref.dtype)


def _modnorm_mm(x, mod, g, w, tm, tn, name):
    bx, r, d = x.shape
    n = w.shape[1]
    tm = min(tm, r)
    return pl.pallas_call(
        _modnorm_mm_kernel,
        grid=(bx, r // tm, n // tn),
        in_specs=[
            pl.BlockSpec((None, tm, d), lambda b, i, j: (b, i, 0)),
            pl.BlockSpec((None, 3, d), lambda b, i, j: (b, 0, 0)),
            pl.BlockSpec((1, d), lambda b, i, j: (0, 0)),
            pl.BlockSpec((d, tn), lambda b, i, j: (0, j)),
        ],
        out_specs=pl.BlockSpec((None, tm, tn), lambda b, i, j: (b, i, j)),
        out_shape=jax.ShapeDtypeStruct((bx, r, n), F32),
        scratch_shapes=[pltpu.VMEM((tm, d), BF16)],
        compiler_params=_params("parallel", "parallel", "arbitrary"),
        name=name,
    )(x, mod, g, w)


def _hgrn_levels(c):
    w = c // 2
    out = []
    while w >= 1:
        out.append(w)
        w //= 2
    return tuple(out)


def _hgrn_constants(c):
    t = np.arange(c)
    tri = np.tril(np.ones((c, c), np.float32))
    masks = []
    for w in _hgrn_levels(c):
        blk = t // (2 * w)
        first = (t % (2 * w)) < w
        masks.append(((blk[:, None] == blk[None, :]) & (~first[:, None]) & first[None, :]).astype(np.float32))
    masks = np.stack(masks)
    tri2 = np.stack([tri, tri[::-1, ::-1]])
    m2 = np.stack([masks, masks[:, ::-1, ::-1]])
    return jnp.asarray(tri2, BF16), jnp.asarray(m2, F32)


def _hgrn_level_operand(b, g, k, q, w, d):
    c, width = b.shape
    row = lax.broadcasted_iota(jnp.int32, (c, 1), 0)
    keys_first = d == 0
    if w >= SUBLANES:
        ref_off = w - 1 if d == 0 else w
        pieces = []
        for r0 in range(0, c, 2 * w):
            bref = jnp.broadcast_to(b[r0 + ref_off:r0 + ref_off + 1, :], (w, width))
            for half in range(2):
                sl = slice(r0 + half * w, r0 + (half + 1) * w)
                if (half == 0) == keys_first:
                    pieces.append(k[sl] * jnp.exp2(bref - b[sl]))
                else:
                    pieces.append(q[sl] * jnp.exp2(b[sl] - bref))
        return jnp.concatenate(pieces, axis=0)
    before = (row % (2 * w)) < w
    kq = jnp.where(before == keys_first, k, q)
    if w == 1:
        moving = (row % 2 == 1) if d == 0 else (row % 2 == 0)
        return kq * jnp.exp2(jnp.where(moving, g, 0.0))
    ref_off = w - 1 if d == 0 else w
    sub = lax.broadcasted_iota(jnp.int32, (SUBLANES, 1), 0)
    pieces = []
    for r0 in range(0, c, SUBLANES):
        lo = jnp.broadcast_to(b[r0 + ref_off:r0 + ref_off + 1, :], (SUBLANES, width))
        if 2 * w == SUBLANES:
            pieces.append(lo)
        else:
            hi = jnp.broadcast_to(b[r0 + 2 * w + ref_off:r0 + 2 * w + ref_off + 1, :], (SUBLANES, width))
            pieces.append(jnp.where(sub < 2 * w, lo, hi))
    bref = jnp.concatenate(pieces, axis=0)
    return kq * jnp.exp2(-jnp.abs(b - bref))


def _hgrn_wide(q_ref, f_ref, v_ref, lb, tri, d, r0, *, c):
    last = c - 1 if d == 0 else 0
    rows = pl.ds(r0, c)
    half = 0.5 * (1.0 - lb)
    f = (lb + half) + half * jnp.tanh(0.5 * f_ref[rows, :])
    g = jnp.log2(f)
    g_hi, g_lo = _split_bf16(g)
    b = _dot(tri, g_hi) + _dot(tri, g_lo)
    q = _silu(q_ref[rows, :])
    k = 1.0 - f
    v = v_ref[rows, :]
    bl = b[last:last + 1, :]
    return dict(d=d, rows=rows, g=g, b=b, q=q, k=k, v=v, vb=v.astype(BF16), qk=q * k,
                qe=(q * jnp.exp2(b)).astype(BF16), kend=(k * jnp.exp2(bl - b)).astype(BF16), ebl=jnp.exp2(bl))


def _head(x, h):
    return x[:, h * HG_DK:(h + 1) * HG_DK]


def _hgrn_pairs(s, h, mk_ref, st_ref, *, c):
    d = s["d"]
    inter = _dot_nt(_head(s["qe"], h), st_ref[d, h].astype(BF16))
    att = None
    for l, w in enumerate(_hgrn_levels(c)):
        x = _hgrn_level_operand(_head(s["b"], h), _head(s["g"], h), _head(s["k"], h), _head(s["q"], h), w, d)
        x = x.astype(BF16)
        t = mk_ref[d, l] * _dot_nt(x, x)
        att = t if att is None else att + t
    return inter, att


def _hgrn_finish(s, h, inter, att, o_ref, st_ref):
    d = s["d"]
    vb = _head(s["vb"], h)
    diag = jnp.sum(_head(s["qk"], h), axis=-1, keepdims=True)
    o_ref[s["rows"], h * HG_DK:(h + 1) * HG_DK] = inter + _dot(att.astype(BF16), vb) + diag * _head(s["v"], h)
    st_ref[d, h] = _head(s["ebl"], h) * st_ref[d, h] + _dot_tn(vb, _head(s["kend"], h))


def _hgrn_kernel(qf_ref, ff_ref, vf_ref, qb_ref, fb_ref, vb_ref, lb_ref, tri_ref, mk_ref, s0_ref,
                 of_ref, ob_ref, sout_ref, st_ref, *, tb, c, nh):
    nchunk = tb // c

    @pl.when(pl.program_id(1) == 0)
    def _():
        st_ref[...] = s0_ref[...]

    def body(cc, carry):
        rf = pl.multiple_of(cc * c, c)
        rb = pl.multiple_of((nchunk - 1 - cc) * c, c)
        sides = ((_hgrn_wide(qf_ref, ff_ref, vf_ref, lb_ref[0], tri_ref[0], 0, rf, c=c), of_ref),
                 (_hgrn_wide(qb_ref, fb_ref, vb_ref, lb_ref[1], tri_ref[1], 1, rb, c=c), ob_ref))
        pending = [None, None]
        for h in range(nh + 1):
            cur = [_hgrn_pairs(s, h, mk_ref, st_ref, c=c) if h < nh else None for s, _ in sides]
            for (s, o_ref), p in zip(sides, pending):
                if p is not None:
                    _hgrn_finish(s, h - 1, *p, o_ref, st_ref)
            pending = cur
        return carry

    lax.fori_loop(0, nchunk, body, 0)

    @pl.when(pl.program_id(1) == pl.num_programs(1) - 1)
    def _():
        sout_ref[...] = st_ref[...]


def _hgrn(p, lb, s0, consts):
    bsz, r, _ = p.shape
    w = lb.shape[-1]
    nh = w // HG_DK
    tb = min(TOK_TILE, r)
    c = HG_CHUNK
    nb = r // tb
    tri2, m2 = consts
    fwd = lambda col: pl.BlockSpec((None, tb, w), lambda b, s: (b, s, col))
    bwd = lambda col: pl.BlockSpec((None, tb, w), lambda b, s: (b, nb - 1 - s, col))
    const = lambda arr: pl.BlockSpec(arr.shape, lambda b, s: (0,) * arr.ndim)
    st_spec = pl.BlockSpec((None, 2, nh, HG_DK, HG_DK), lambda b, s: (b, 0, 0, 0, 0))
    kern = functools.partial(_hgrn_kernel, tb=tb, c=c, nh=nh)
    return pl.pallas_call(
        kern,
        grid=(bsz, nb),
        in_specs=[fwd(0), fwd(1), fwd(3), bwd(0), bwd(2), bwd(3), const(lb), const(tri2), const(m2),
                  st_spec],
        out_specs=[
            pl.BlockSpec((None, tb, w), lambda b, s: (b, s, 0)),
            pl.BlockSpec((None, tb, w), lambda b, s: (b, nb - 1 - s, 0)),
            st_spec,
        ],
        out_shape=[
            jax.ShapeDtypeStruct((bsz, r, w), F32),
            jax.ShapeDtypeStruct((bsz, r, w), F32),
            jax.ShapeDtypeStruct(s0.shape, F32),
        ],
        scratch_shapes=[pltpu.VMEM((2, nh, HG_DK, HG_DK), F32)],
        compiler_params=_params("parallel", "arbitrary"),
        name="hgrn2_scan",
    )(p, p, p, p, p, p, lb, tri2, m2, s0)


def _even_post_kernel(of_ref, ob_ref, ga_ref, u_ref, gb_ref, up_ref, un_ref, h_ref, mod_ref, hgn_ref, pw_ref,
                      ps_ref, ow_ref, modn_ref, gn_ref, wn_ref, o_ref, pn_ref, ext_ref, y_ref, z_ref, *, tb, seq, nh):
    step = pl.program_id(1)
    last = pl.num_programs(1) - 2
    j = jnp.minimum(step, last)
    w = nh * HG_DK

    @pl.when(step == 0)
    def _():
        z_ref[...] = jnp.zeros(z_ref.shape, BF16)

    n_parts = nh + len(POOL_WINDOWS)
    n_blk = pn_ref.shape[1] // LANES
    per = -(-n_blk // n_parts)

    def in_proj_part(i):
        c0 = min(i * per, n_blk) * LANES
        c1 = min((i + 1) * per, n_blk) * LANES
        if c0 < c1:
            pn_ref[:, c0:c1] = _dot(z_ref[...], wn_ref[:, c0:c1])

    o = of_ref[...] + ob_ref[...]
    for h in range(nh):
        in_proj_part(h)
        sl = slice(h * HG_DK, (h + 1) * HG_DK)
        y_ref[:, sl] = (_rms(o[:, sl], hgn_ref[:, sl]) * _silu(ga_ref[:, sl])).astype(BF16)
    u = u_ref[...]
    ext_ref[0:POOL_HALO, :] = jnp.where(j > 0, up_ref[...], 0.0)
    ext_ref[POOL_HALO:POOL_HALO + tb, :] = u
    ext_ref[POOL_HALO + tb:, :] = jnp.where(j < last, un_ref[...], 0.0)
    t = j * tb + lax.broadcasted_iota(jnp.int32, (tb, 1), 0)
    grp = w // len(POOL_WINDOWS)
    for gi, win in enumerate(POOL_WINDOWS):
        in_proj_part(nh + gi)
        sl = slice(gi * grp, (gi + 1) * grp)
        acc = ext_ref[POOL_HALO - win // 2:POOL_HALO - win // 2 + tb, sl]
        for off in range(-win // 2 + 1, win // 2):
            acc = acc + ext_ref[POOL_HALO + off:POOL_HALO + off + tb, sl]
        cnt = (jnp.minimum(t + win // 2, seq) - jnp.maximum(t - win // 2, 0)).astype(F32)
        yp = acc * (1.0 / cnt) - u[:, sl]
        yb = _dot(yp.astype(BF16), pw_ref[gi]) * ps_ref[:, sl]
        y_ref[:, w + gi * grp:w + (gi + 1) * grp] = (yb * _silu(gb_ref[:, sl])).astype(BF16)
    hn = h_ref[...] + mod_ref[2:3, :] * _dot(y_ref[...], ow_ref[...])
    o_ref[...] = hn
    z_ref[...] = (_rms(hn, gn_ref[...]) * (1.0 + modn_ref[1:2, :]) + modn_ref[0:1, :]).astype(BF16)


def _even_post(o_f, o_b, p, h, mod, hgn, pool_w, pool_scale, out_w, mod_n, g_n, w_n):
    bsz, r, w = o_f.shape
    d = h.shape[-1]
    tb = min(TOK_TILE, r)
    nb = r // tb
    hb = tb // POOL_HALO
    nh = w // HG_DK
    n_next = w_n.shape[1]
    cur = lambda s: jnp.minimum(s, nb - 1)
    tok = lambda col: pl.BlockSpec((None, tb, w), lambda b, s: (b, cur(s), col))
    const = lambda arr: pl.BlockSpec(arr.shape, lambda b, s: (0,) * arr.ndim)
    kern = functools.partial(_even_post_kernel, tb=tb, seq=r, nh=nh)
    return pl.pallas_call(
        kern,
        grid=(bsz, nb + 1),
        in_specs=[
            tok(0), tok(0), tok(4), tok(5), tok(6),
            pl.BlockSpec((None, POOL_HALO, w), lambda b, s: (b, jnp.maximum(cur(s) * hb - 1, 0), 5)),
            pl.BlockSpec((None, POOL_HALO, w), lambda b, s: (b, jnp.minimum((cur(s) + 1) * hb, nb * hb - 1), 5)),
            pl.BlockSpec((None, tb, d), lambda b, s: (b, cur(s), 0)),
            pl.BlockSpec((None, 3, d), lambda b, s: (b, 0, 0)),
            const(hgn), const(pool_w), const(pool_scale), const(out_w),
            pl.BlockSpec((None, 3, d), lambda b, s: (b, 0, 0)), const(g_n), const(w_n),
        ],
        out_specs=[pl.BlockSpec((None, tb, d), lambda b, s: (b, cur(s), 0)),
                   pl.BlockSpec((None, tb, n_next), lambda b, s: (b, jnp.maximum(s - 1, 0), 0))],
        out_shape=[jax.ShapeDtypeStruct((bsz, r, d), F32), jax.ShapeDtypeStruct((bsz, r, n_next), F32)],
        scratch_shapes=[pltpu.VMEM((tb + 2 * POOL_HALO, w), F32), pltpu.VMEM((tb, 2 * w), BF16),
                        pltpu.VMEM((tb, d), BF16)],
        compiler_params=_params("parallel", "arbitrary"),
        name="even_post",
    )(o_f, o_b, p, p, p, p, p, h, mod, hgn, pool_w, pool_scale, out_w, mod_n, g_n, w_n)


def _mla_kv_kernel(*refs, rope):
    if rope:
        ckv_ref, kr_ref, g_ref, wuk_ref, wuvt_ref, cos_ref, sin_ref, kcat_ref, vt_ref = refs
    else:
        ckv_ref, kr_ref, g_ref, wuk_ref, wuvt_ref, kcat_ref, vt_ref = refs
    cn = _rms(ckv_ref[...], g_ref[...]).astype(BF16)
    kn = _dot(cn, wuk_ref[...])
    kr = kr_ref[...]
    if rope:
        lane = lax.broadcasted_iota(jnp.int32, kr.shape, 1)
        swapped = jnp.where((lane % (2 * ROPE_FREQ)) < ROPE_FREQ,
                            pltpu.roll(kr, LANES - ROPE_FREQ, 1), pltpu.roll(kr, ROPE_FREQ, 1))
        kr = kr * cos_ref[...] + swapped * sin_ref[...]
    kr = kr.astype(BF16)
    ones_rows = (lax.broadcasted_iota(jnp.int32, (VT_ROWS - MLA_V, kr.shape[0]), 0) == 0).astype(BF16)
    vt = _dot_nt(wuvt_ref[...], cn)
    for h in range(MLA_HEADS):
        kcat_ref[h, :, 0:MLA_NOPE] = kn[:, h * MLA_NOPE:(h + 1) * MLA_NOPE].astype(BF16)
        kcat_ref[h, :, MLA_NOPE:] = kr
        vt_ref[h, 0:MLA_V, :] = vt[h * MLA_V:(h + 1) * MLA_V].astype(BF16)
        vt_ref[h, MLA_V:, :] = ones_rows


def _mla_kv(p, ckv_blk, kr_blk, g, wuk, wuvt, tables, tb):
    bsz, r, _ = p.shape
    rank = g.shape[-1]
    nb = r // tb
    const = lambda arr: pl.BlockSpec(arr.shape, lambda b, j: (0,) * arr.ndim)
    in_specs = [
        pl.BlockSpec((None, tb, rank), lambda b, j: (b, j, ckv_blk)),
        pl.BlockSpec((None, tb, LANES), lambda b, j: (b, j, kr_blk)),
        const(g), const(wuk), const(wuvt),
    ]
    args = [p, p, g, wuk, wuvt]
    if tables is not None:
        in_specs += [pl.BlockSpec((tb, LANES), lambda b, j: (j, 0))] * 2
        args += list(tables)
    return pl.pallas_call(
        functools.partial(_mla_kv_kernel, rope=tables is not None),
        grid=(bsz, nb),
        in_specs=in_specs,
        out_specs=[
            pl.BlockSpec((None, MLA_HEADS, None, tb, QK_PAD), lambda b, j: (b, 0, j, 0, 0)),
            pl.BlockSpec((None, MLA_HEADS, None, VT_ROWS, tb), lambda b, j: (b, 0, j, 0, 0)),
        ],
        out_shape=[
            jax.ShapeDtypeStruct((bsz, MLA_HEADS, nb, tb, QK_PAD), BF16),
            jax.ShapeDtypeStruct((bsz, MLA_HEADS, nb, VT_ROWS, tb), BF16),
        ],
        compiler_params=_params("parallel", "parallel"),
        name="mla_kv_rope" if tables is not None else "mla_kv",
    )(*args)


def _mla_q_kernel(cq_ref, g_ref, wqt_ref, cos_ref, sin_ref, qt_ref):
    cn = _rms(cq_ref[...], g_ref[...]).astype(BF16)
    f = ROPE_FREQ
    qt_all = _dot_nt(wqt_ref[...], cn) * (MLA_SCALE * LOG2_E)
    for h in range(MLA_HEADS):
        qt = qt_all[h * MLA_QK:(h + 1) * MLA_QK]
        qt_ref[h, 0:MLA_NOPE, :] = qt[0:MLA_NOPE].astype(BF16)
        for ax in range(2):
            r0 = MLA_NOPE + ax * 2 * f
            x1 = qt[r0:r0 + f]
            x2 = qt[r0 + f:r0 + 2 * f]
            co = cos_ref[ax]
            si = sin_ref[ax]
            qt_ref[h, r0:r0 + f, :] = (x1 * co - x2 * si).astype(BF16)
            qt_ref[h, r0 + f:r0 + 2 * f, :] = (x2 * co + x1 * si).astype(BF16)
        qt_ref[h, MLA_QK:, :] = jnp.zeros((QK_PAD - MLA_QK, cn.shape[0]), BF16)


def _mla_q(p, cq_blk, g, wqt, cos_t, sin_t):
    bsz, t, _ = p.shape
    rank = g.shape[-1]
    tm = min(TOK_TILE, t)
    const = lambda arr: pl.BlockSpec(arr.shape, lambda b, j: (0,) * arr.ndim)
    tab = pl.BlockSpec((2, ROPE_FREQ, tm), lambda b, j: (0, 0, j))
    return pl.pallas_call(
        _mla_q_kernel,
        grid=(bsz, t // tm),
        in_specs=[pl.BlockSpec((None, tm, rank), lambda b, j: (b, j, cq_blk)), const(g), const(wqt), tab, tab],
        out_specs=pl.BlockSpec((None, MLA_HEADS, QK_PAD, tm), lambda b, j: (b, 0, 0, j)),
        out_shape=jax.ShapeDtypeStruct((bsz, MLA_HEADS, QK_PAD, t), BF16),
        compiler_params=_params("parallel", "parallel"),
        name="mla_q",
    )(p, g, wqt, cos_t, sin_t)


def _attn_kernel(qt_ref, kc_ref, vtc_ref, kl_ref, vtl_ref, g_ref, o_ref, m_ref, acc_ref, s_ref, mx_ref, *, tq, n_lat):
    nsub = tq // Q_SUB
    m_ref[...] = jnp.full(m_ref.shape, -jnp.inf, F32)
    acc_ref[...] = jnp.zeros(acc_ref.shape, F32)

    def scores(k, nxt, g):
        s = _dot(k, qt_ref[:, g * Q_SUB:(g + 1) * Q_SUB])
        s_ref[nxt, g, 0:k.shape[0], :] = s
        mx_ref[nxt, g] = jnp.max(s, axis=0, keepdims=True)

    def substep(k_next, vt_cur, cur, nxt):
        rows = vt_cur.shape[1]
        for g in range(nsub):
            sl = slice(g * Q_SUB, (g + 1) * Q_SUB)
            scores(k_next, nxt, g)
            m_old = m_ref[:, sl]
            m_new = jnp.maximum(m_old, mx_ref[cur, g])
            alpha = jnp.exp2(m_old - m_new)
            p = jnp.exp2(s_ref[cur, g, 0:rows, :] - m_new)
            acc_ref[:, sl] = alpha * acc_ref[:, sl] + _dot(vt_cur, p.astype(BF16))
            m_ref[:, sl] = m_new

    kc = kc_ref[...]
    for g in range(nsub):
        scores(kc, 0, g)
    substep(kl_ref[0], vtc_ref[...], 0, 1)

    def body(j, carry):
        for u in range(2):
            a = 2 * j + u
            substep(kl_ref[jnp.minimum(a + 1, n_lat - 1)], vtl_ref[a], (1 + u) % 2, u % 2)
        return carry

    lax.fori_loop(0, n_lat // 2, body, 0)
    o = (acc_ref[0:MLA_V, :] * (1.0 / acc_ref[MLA_V:MLA_V + 1, :])).T
    o_ref[...] = (o * _silu(g_ref[...])).astype(o_ref.dtype)


def _attn(qt, kc, vtc, kl, vtl, p):
    bsz, nh, _, t = qt.shape
    lc = kc.shape[3]
    tq = min(Q_TILE, t)
    n_lat = kl.shape[2]
    kv = kl.shape[3]
    assert kc.shape[2] == 1 and lc <= kv and n_lat % 2 == 0
    kern = functools.partial(_attn_kernel, tq=tq, n_lat=n_lat)
    ctx5 = lambda arr: pl.BlockSpec((None, None, None) + arr.shape[3:], lambda b, h, i: (b, h, 0, 0, 0))
    full5 = lambda arr: pl.BlockSpec((None, None) + arr.shape[2:], lambda b, h, i: (b, h, 0, 0, 0))
    return pl.pallas_call(
        kern,
        grid=(bsz, nh, t // tq),
        in_specs=[
            pl.BlockSpec((None, None, QK_PAD, tq), lambda b, h, i: (b, h, 0, i)),
            ctx5(kc), ctx5(vtc), full5(kl), full5(vtl),
            pl.BlockSpec((None, tq, MLA_V), lambda b, h, i: (b, i, h)),
        ],
        out_specs=pl.BlockSpec((None, tq, MLA_V), lambda b, h, i: (b, i, h)),
        out_shape=jax.ShapeDtypeStruct((bsz, t, nh * MLA_V), BF16),
        scratch_shapes=[pltpu.VMEM((1, tq), F32), pltpu.VMEM((VT_ROWS, tq), F32),
                        pltpu.VMEM((2, tq // Q_SUB, kv, Q_SUB), F32), pltpu.VMEM((2, tq // Q_SUB, 1, Q_SUB), F32)],
        compiler_params=_params("parallel", "parallel", "arbitrary"),
        name="mla_attention",
    )(qt, kc, vtc, kl, vtl, p)


def _out_final_kernel(y_ref, h_ref, mod_ref, ow_ref, g_ref, o_ref):
    hn = h_ref[...] + mod_ref[2:3, :] * _dot(y_ref[...], ow_ref[...])
    o_ref[...] = _rms(hn, g_ref[...])


def _out_final(y, h, mod, out_w, g):
    bsz, t, d = h.shape
    wi = y.shape[-1]
    tm = min(2 * TOK_TILE, t)
    return pl.pallas_call(
        _out_final_kernel,
        grid=(bsz, t // tm),
        in_specs=[
            pl.BlockSpec((None, tm, wi), lambda b, j: (b, j, 0)),
            pl.BlockSpec((None, tm, d), lambda b, j: (b, j, 0)),
            pl.BlockSpec((None, 3, d), lambda b, j: (b, 0, 0)),
            pl.BlockSpec((wi, d), lambda b, j: (0, 0)),
            pl.BlockSpec((1, d), lambda b, j: (0, 0)),
        ],
        out_specs=pl.BlockSpec((None, tm, d), lambda b, j: (b, j, 0)),
        out_shape=jax.ShapeDtypeStruct((bsz, t, d), F32),
        compiler_params=_params("parallel", "parallel"),
        name="out_final",
    )(y, h, mod, out_w, g)


def _rope_tables(n_tokens):
    rows = n_tokens // GRID_W
    pos_r = jnp.repeat(jnp.arange(rows), GRID_W).astype(F32)
    pos_c = jnp.tile(jnp.arange(GRID_W), rows).astype(F32)
    inv = ROPE_BASE ** (-2.0 * jnp.arange(ROPE_FREQ, dtype=F32) / (MLA_ROPE // 2))
    ang = jnp.stack([pos_r[:, None] * inv, pos_c[:, None] * inv], axis=1)
    cos, sin = jnp.cos(ang), jnp.sin(ang)
    pad = LANES - MLA_ROPE
    cos_k = jnp.pad(jnp.stack([cos, cos], axis=2).reshape(n_tokens, MLA_ROPE), ((0, 0), (0, pad)))
    sin_k = jnp.pad(jnp.stack([-sin, sin], axis=2).reshape(n_tokens, MLA_ROPE), ((0, 0), (0, pad)))
    cos_q = jnp.transpose(cos, (1, 2, 0))
    sin_q = jnp.transpose(sin, (1, 2, 0))
    return (cos_k, sin_k), (cos_q, sin_q)


def kernel(x, c, ctx, c_ctx, ada_w, ada_b, norm_g, out_w, ev_in_w, hg_lb, hg_norm_g, pool_w, pool_scale,
           od_in_w, qa_norm_g, qb_w, kva_norm_g, kvb_w, final_norm_g):
    bsz, t, d = x.shape
    lc = ctx.shape[1]
    depth = ada_w.shape[0]
    assert depth == 2 and t % (2 * TOK_TILE) == 0 and lc % TOK_TILE == 0 and t % GRID_W == 0
    w = hg_norm_g.shape[-1]
    nh = w // HG_DK
    q_rank = qa_norm_g.shape[-1]
    kv_rank = kva_norm_g.shape[-1]
    d_inner = out_w.shape[1]

    n_cond = -(-(bsz + 1) // SUBLANES) * SUBLANES
    cond = jnp.zeros((n_cond, d), F32).at[:bsz].set(c).at[bsz].set(c_ctx)
    mods = _ada(cond, ada_w, ada_b).reshape(depth, n_cond, 3, d)
    mod_l = [mods[l, :bsz] for l in range(depth)]
    mod_c = [mods[l, bsz:bsz + 1] for l in range(depth)]

    lb = jnp.cumsum(jax.nn.softmax(hg_lb.astype(F32), axis=1), axis=1)[:, 0].reshape(2, 1, w)
    w_in0 = ev_in_w[0].astype(BF16)
    g0 = norm_g[0].reshape(1, d)
    ctx_flat = ctx.reshape(1, bsz * lc, d)
    n_in0 = w_in0.shape[1]
    p_c = _modnorm_mm(ctx_flat, mod_c[0], g0, w_in0, 4 * TOK_TILE, n_in0 // 4, "in_proj0_ctx").reshape(bsz, lc, n_in0)
    p_l = _modnorm_mm(x, mod_l[0], g0, w_in0, 4 * TOK_TILE, n_in0 // 4, "in_proj0")
    consts = _hgrn_constants(HG_CHUNK)
    s0 = jnp.zeros((bsz, 2, nh, HG_DK, HG_DK), F32)
    of_c, ob_c, s_c = _hgrn(p_c, lb, s0, consts)
    of_l, ob_l, _ = _hgrn(p_l, lb, s_c, consts)
    hgn = hg_norm_g[0].reshape(1, w)
    pw = pool_w[0].astype(BF16)
    ps = pool_scale[0].reshape(1, w)
    ow0 = out_w[0].astype(BF16)
    o1 = q_rank
    o2 = o1 + kv_rank
    o3 = o2 + MLA_ROPE
    w1 = od_in_w[0]
    kr_pad = jnp.zeros((d, LANES - MLA_ROPE), F32)
    w_in1 = jnp.concatenate([w1[:, o3:], w1[:, :o1], w1[:, o1:o2], w1[:, o2:o3], kr_pad], axis=1).astype(BF16)
    w_in1c = w_in1[:, d_inner + q_rank:]
    g1 = norm_g[1].reshape(1, d)
    bcast = lambda m: jnp.broadcast_to(m, (bsz, 3, d))
    _, p1_c = _even_post(of_c, ob_c, p_c, ctx, bcast(mod_c[0]), hgn, pw, ps, ow0, bcast(mod_c[1]), g1, w_in1c)
    hl1, p1_l = _even_post(of_l, ob_l, p_l, x, mod_l[0], hgn, pw, ps, ow0, mod_l[1], g1, w_in1)

    kvw = kvb_w[0].reshape(kv_rank, MLA_HEADS, MLA_NOPE + MLA_V)
    wuk = kvw[..., :MLA_NOPE].reshape(kv_rank, MLA_HEADS * MLA_NOPE).astype(BF16)
    wuvt = jnp.transpose(kvw[..., MLA_NOPE:], (1, 2, 0)).reshape(MLA_HEADS * MLA_V, kv_rank).astype(BF16)
    wqt = jnp.transpose(qb_w[0]).astype(BF16)
    kvg = kva_norm_g[0].reshape(1, kv_rank)
    qag = qa_norm_g[0].reshape(1, q_rank)
    tab_k, tab_q = _rope_tables(t)
    kc, vtc = _mla_kv(p1_c, 0, kv_rank // LANES, kvg, wuk, wuvt, None, lc)
    kl, vtl = _mla_kv(p1_l, (d_inner + q_rank) // kv_rank, (d_inner + q_rank + kv_rank) // LANES, kvg, wuk, wuvt,
                      tab_k, min(KV_CHUNK, t))
    qt = _mla_q(p1_l, d_inner // q_rank, qag, wqt, *tab_q)
    y = _attn(qt, kc, vtc, kl, vtl, p1_l)
    return _out_final(y, hl1, mod_l[1], out_w[1].astype(BF16), final_norm_g.reshape(1, d))
```

```python
import functools

import numpy as np
import jax
import jax.numpy as jnp
from jax import lax
from jax.experimental import pallas as pl
from jax.experimental.pallas import tpu as pltpu

F32 = jnp.float32
BF16 = jnp.bfloat16

EPS = 1e-6
GRID_W = 64
HG_DK = 128
POOL_WINDOWS = (2, 4, 8, 16)
MLA_HEADS = 16
MLA_NOPE = 128
MLA_ROPE = 64
MLA_V = 128
MLA_QK = MLA_NOPE + MLA_ROPE
QK_PAD = 256
VT_ROWS = MLA_V + 16
MLA_SCALE = MLA_QK ** -0.5
LOG2_E = 1.4426950408889634
ROPE_FREQ = MLA_ROPE // 4
ROPE_BASE = 10000.0

LANES = 128
SUBLANES = 8
VMEM_LIMIT = 48 * 1024 * 1024

HG_CHUNK = 64
TOK_TILE = 256
KV_CHUNK = 512
Q_TILE = 4096
Q_SUB = 256
POOL_HALO = 8


def _dot(a, b):
    return jnp.dot(a, b, preferred_element_type=F32)


def _dot_nt(a, b):
    return lax.dot_general(a, b, (((1,), (1,)), ((), ())), preferred_element_type=F32)


def _dot_tn(a, b):
    return lax.dot_general(a, b, (((0,), (0,)), ((), ())), preferred_element_type=F32)


def _sigmoid(x):
    return 0.5 * jnp.tanh(0.5 * x) + 0.5


def _silu(x):
    h = 0.5 * x
    return h + h * jnp.tanh(h)


def _split_bf16(x):
    hi = x.astype(BF16)
    lo = (x - hi.astype(F32)).astype(BF16)
    return hi, lo


def _params(*sem):
    return pltpu.CompilerParams(dimension_semantics=sem, vmem_limit_bytes=VMEM_LIMIT)


def _ada_kernel(c_ref, w_ref, b_ref, o_ref):
    c = c_ref[...]
    s_hi, s_lo = _split_bf16(_silu(c))
    w_hi, w_lo = _split_bf16(w_ref[...])
    o_ref[...] = _dot(s_hi, w_hi) + _dot(s_lo, w_hi) + _dot(s_hi, w_lo) + b_ref[...]


def _ada(cond, ada_w, ada_b):
    depth, d, _ = ada_w.shape
    r = cond.shape[0]
    return pl.pallas_call(
        _ada_kernel,
        grid=(depth, 3),
        in_specs=[
            pl.BlockSpec((r, d), lambda l, j: (0, 0)),
            pl.BlockSpec((None, d, d), lambda l, j: (l, 0, j)),
            pl.BlockSpec((None, 1, d), lambda l, j: (l, 0, j)),
        ],
        out_specs=pl.BlockSpec((None, r, d), lambda l, j: (l, 0, j)),
        out_shape=jax.ShapeDtypeStruct((depth, r, 3 * d), F32),
        compiler_params=_params("parallel", "parallel"),
        name="ada_modulation",
    )(cond, ada_w, ada_b.reshape(depth, 1, 3 * d))


def _rms(x, g):
    return x * lax.rsqrt(jnp.mean(x * x, axis=-1, keepdims=True) + EPS) * g


def _modnorm_mm_kernel(x_ref, mod_ref, g_ref, w_ref, o_ref, z_ref):
    @pl.when(pl.program_id(2) == 0)
    def _():
        y = _rms(x_ref[...], g_ref[...])
        z_ref[...] = (y * (1.0 + mod_ref[1:2, :]) + mod_ref[0:1, :]).astype(BF16)

    o_ref[...] = _dot(z_ref[...], w_ref[...]).astype(o_ref.dtype)


def _modnorm_mm(x, mod, g, w, tm, tn, name):
    bx, r, d = x.shape
    n = w.shape[1]
    tm = min(tm, r)
    return pl.pallas_call(
        _modnorm_mm_kernel,
        grid=(bx, r // tm, n // tn),
        in_specs=[
            pl.BlockSpec((None, tm, d), lambda b, i, j: (b, i, 0)),
            pl.BlockSpec((None, 3, d), lambda b, i, j: (b, 0, 0)),
            pl.BlockSpec((1, d), lambda b, i, j: (0, 0)),
            pl.BlockSpec((d, tn), lambda b, i, j: (0, j)),
        ],
        out_specs=pl.BlockSpec((None, tm, tn), lambda b, i, j: (b, i, j)),
        out_shape=jax.ShapeDtypeStruct((bx, r, n), F32),
        scratch_shapes=[pltpu.VMEM((tm, d), BF16)],
        compiler_params=_params("parallel", "parallel", "arbitrary"),
        name=name,
    )(x, mod, g, w)


def _hgrn_levels(c):
    w = c // 2
    out = []
    while w >= 1:
        out.append(w)
        w //= 2
    return tuple(out)


def _hgrn_constants(c):
    t = np.arange(c)
    tri = np.tril(np.ones((c, c), np.float32))
    masks = []
    for w in _hgrn_levels(c):
        blk = t // (2 * w)
        first = (t % (2 * w)) < w
        masks.append(((blk[:, None] == blk[None, :]) & (~first[:, None]) & first[None, :]).astype(np.float32))
    masks = np.stack(masks)
    tri2 = np.stack([tri, tri[::-1, ::-1]])
    m2 = np.stack([masks, masks[:, ::-1, ::-1]])
    return jnp.asarray(tri2, BF16), jnp.asarray(m2, F32)


def _hgrn_level_operand(b, g, k, q, w, d):
    c, width = b.shape
    row = lax.broadcasted_iota(jnp.int32, (c, 1), 0)
    keys_first = d == 0
    if w >= SUBLANES:
        ref_off = w - 1 if d == 0 else w
        pieces = []
        for r0 in range(0, c, 2 * w):
            bref = jnp.broadcast_to(b[r0 + ref_off:r0 + ref_off + 1, :], (w, width))
            for half in range(2):
                sl = slice(r0 + half * w, r0 + (half + 1) * w)
                if (half == 0) == keys_first:
                    pieces.append(k[sl] * jnp.exp2(bref - b[sl]))
                else:
                    pieces.append(q[sl] * jnp.exp2(b[sl] - bref))
        return jnp.concatenate(pieces, axis=0)
    before = (row % (2 * w)) < w
    kq = jnp.where(before == keys_first, k, q)
    if w == 1:
        moving = (row % 2 == 1) if d == 0 else (row % 2 == 0)
        return kq * jnp.exp2(jnp.where(moving, g, 0.0))
    ref_off = w - 1 if d == 0 else w
    sub = lax.broadcasted_iota(jnp.int32, (SUBLANES, 1), 0)
    pieces = []
    for r0 in range(0, c, SUBLANES):
        lo = jnp.broadcast_to(b[r0 + ref_off:r0 + ref_off + 1, :], (SUBLANES, width))
        if 2 * w == SUBLANES:
            pieces.append(lo)
        else:
            hi = jnp.broadcast_to(b[r0 + 2 * w + ref_off:r0 + 2 * w + ref_off + 1, :], (SUBLANES, width))
            pieces.append(jnp.where(sub < 2 * w, lo, hi))
    bref = jnp.concatenate(pieces, axis=0)
    return kq * jnp.exp2(-jnp.abs(b - bref))


def _hgrn_wide(q_ref, f_ref, v_ref, lb, tri, d, r0, *, c):
    last = c - 1 if d == 0 else 0
    rows = pl.ds(r0, c)
    half = 0.5 * (1.0 - lb)
    f = (lb + half) + half * jnp.tanh(0.5 * f_ref[rows, :])
    g = jnp.log2(f)
    g_hi, g_lo = _split_bf16(g)
    b = _dot(tri, g_hi) + _dot(tri, g_lo)
    q = _silu(q_ref[rows, :])
    k = 1.0 - f
    v = v_ref[rows, :]
    bl = b[last:last + 1, :]
    return dict(d=d, rows=rows, g=g, b=b, q=q, k=k, v=v, vb=v.astype(BF16), qk=q * k,
                qe=(q * jnp.exp2(b)).astype(BF16), kend=(k * jnp.exp2(bl - b)).astype(BF16), ebl=jnp.exp2(bl))


def _head(x, h):
    return x[:, h * HG_DK:(h + 1) * HG_DK]


def _hgrn_pairs(s, h, mk_ref, *, c):
    d = s["d"]
    att = None
    for l, w in enumerate(_hgrn_levels(c)):
        x = _hgrn_level_operand(_head(s["b"], h), _head(s["g"], h), _head(s["k"], h), _head(s["q"], h), w, d)
        x = x.astype(BF16)
        t = mk_ref[d, l] * _dot_nt(x, x)
        att = t if att is None else att + t
    return att


def _hgrn_finish(s, h, att, o_ref, st_ref):
    d = s["d"]
    vb = _head(s["vb"], h)
    inter = _dot_nt(_head(s["qe"], h), st_ref[d, h].astype(BF16))
    diag = jnp.sum(_head(s["qk"], h), axis=-1, keepdims=True)
    o_ref[s["rows"], h * HG_DK:(h + 1) * HG_DK] = inter + _dot(att.astype(BF16), vb) + diag * _head(s["v"], h)
    st_ref[d, h] = _head(s["ebl"], h) * st_ref[d, h] + _dot_tn(vb, _head(s["kend"], h))


def _hgrn_kernel(qf_ref, ff_ref, vf_ref, qb_ref, fb_ref, vb_ref, lb_ref, tri_ref, mk_ref, s0_ref,
                 of_ref, ob_ref, sout_ref, st_ref, *, tb, c, nh):
    nchunk = tb // c

    @pl.when(pl.program_id(1) == 0)
    def _():
        st_ref[...] = s0_ref[...]

    def body(cc, carry):
        rf = pl.multiple_of(cc * c, c)
        rb = pl.multiple_of((nchunk - 1 - cc) * c, c)
        sides = ((_hgrn_wide(qf_ref, ff_ref, vf_ref, lb_ref[0], tri_ref[0], 0, rf, c=c), of_ref),
                 (_hgrn_wide(qb_ref, fb_ref, vb_ref, lb_ref[1], tri_ref[1], 1, rb, c=c), ob_ref))
        pending = [None, None]
        for h in range(nh + 1):
            cur = [_hgrn_pairs(s, h, mk_ref, c=c) if h < nh else None for s, _ in sides]
            for (s, o_ref), p in zip(sides, pending):
                if p is not None:
                    _hgrn_finish(s, h - 1, p, o_ref, st_ref)
            pending = cur
        return carry

    lax.fori_loop(0, nchunk, body, 0, unroll=True)

    @pl.when(pl.program_id(1) == pl.num_programs(1) - 1)
    def _():
        sout_ref[...] = st_ref[...]


def _hgrn(p, lb, s0, consts):
    bsz, r, _ = p.shape
    w = lb.shape[-1]
    nh = w // HG_DK
    tb = min(TOK_TILE, r)
    c = HG_CHUNK
    nb = r // tb
    tri2, m2 = consts
    fwd = lambda col: pl.BlockSpec((None, tb, w), lambda b, s: (b, s, col))
    bwd = lambda col: pl.BlockSpec((None, tb, w), lambda b, s: (b, nb - 1 - s, col))
    const = lambda arr: pl.BlockSpec(arr.shape, lambda b, s: (0,) * arr.ndim)
    st_spec = pl.BlockSpec((None, 2, nh, HG_DK, HG_DK), lambda b, s: (b, 0, 0, 0, 0))
    kern = functools.partial(_hgrn_kernel, tb=tb, c=c, nh=nh)
    return pl.pallas_call(
        kern,
        grid=(bsz, nb),
        in_specs=[fwd(0), fwd(1), fwd(3), bwd(0), bwd(2), bwd(3), const(lb), const(tri2), const(m2),
                  st_spec],
        out_specs=[
            pl.BlockSpec((None, tb, w), lambda b, s: (b, s, 0)),
            pl.BlockSpec((None, tb, w), lambda b, s: (b, nb - 1 - s, 0)),
            st_spec,
        ],
        out_shape=[
            jax.ShapeDtypeStruct((bsz, r, w), F32),
            jax.ShapeDtypeStruct((bsz, r, w), F32),
            jax.ShapeDtypeStruct(s0.shape, F32),
        ],
        scratch_shapes=[pltpu.VMEM((2, nh, HG_DK, HG_DK), F32)],
        compiler_params=_params("parallel", "arbitrary"),
        name="hgrn2_scan",
    )(p, p, p, p, p, p, lb, tri2, m2, s0)


def _even_post_kernel(of_ref, ob_ref, ga_ref, u_ref, gb_ref, up_ref, un_ref, h_ref, mod_ref, hgn_ref, pw_ref,
                      ps_ref, ow_ref, modn_ref, gn_ref, wn_ref, o_ref, pn_ref, ext_ref, y_ref, z_ref, *, tb, seq, nh):
    step = pl.program_id(1)
    last = pl.num_programs(1) - 2
    j = jnp.minimum(step, last)
    w = nh * HG_DK

    @pl.when(step == 0)
    def _():
        z_ref[...] = jnp.zeros(z_ref.shape, BF16)

    n_parts = nh + len(POOL_WINDOWS)
    n_blk = pn_ref.shape[1] // LANES
    per = -(-n_blk // n_parts)

    def in_proj_part(i):
        c0 = min(i * per, n_blk) * LANES
        c1 = min((i + 1) * per, n_blk) * LANES
        if c0 < c1:
            pn_ref[:, c0:c1] = _dot(z_ref[...], wn_ref[:, c0:c1])

    o = of_ref[...] + ob_ref[...]
    for h in range(nh):
        in_proj_part(h)
        sl = slice(h * HG_DK, (h + 1) * HG_DK)
        y_ref[:, sl] = (_rms(o[:, sl], hgn_ref[:, sl]) * _silu(ga_ref[:, sl])).astype(BF16)
    u = u_ref[...]
    ext_ref[0:POOL_HALO, :] = jnp.where(j > 0, up_ref[...], 0.0)
    ext_ref[POOL_HALO:POOL_HALO + tb, :] = u
    ext_ref[POOL_HALO + tb:, :] = jnp.where(j < last, un_ref[...], 0.0)
    t = j * tb + lax.broadcasted_iota(jnp.int32, (tb, 1), 0)
    grp = w // len(POOL_WINDOWS)
    for gi, win in enumerate(POOL_WINDOWS):
        in_proj_part(nh + gi)
        sl = slice(gi * grp, (gi + 1) * grp)
        acc = ext_ref[POOL_HALO - win // 2:POOL_HALO - win // 2 + tb, sl]
        for off in range(-win // 2 + 1, win // 2):
            acc = acc + ext_ref[POOL_HALO + off:POOL_HALO + off + tb, sl]
        cnt = (jnp.minimum(t + win // 2, seq) - jnp.maximum(t - win // 2, 0)).astype(F32)
        yp = acc * (1.0 / cnt) - u[:, sl]
        yb = _dot(yp.astype(BF16), pw_ref[gi]) * ps_ref[:, sl]
        y_ref[:, w + gi * grp:w + (gi + 1) * grp] = (yb * _silu(gb_ref[:, sl])).astype(BF16)
    hn = h_ref[...] + mod_ref[2:3, :] * _dot(y_ref[...], ow_ref[...])
    o_ref[...] = hn
    z_ref[...] = (_rms(hn, gn_ref[...]) * (1.0 + modn_ref[1:2, :]) + modn_ref[0:1, :]).astype(BF16)


def _even_post(o_f, o_b, p, h, mod, hgn, pool_w, pool_scale, out_w, mod_n, g_n, w_n):
    bsz, r, w = o_f.shape
    d = h.shape[-1]
    tb = min(TOK_TILE, r)
    nb = r // tb
    hb = tb // POOL_HALO
    nh = w // HG_DK
    n_next = w_n.shape[1]
    cur = lambda s: jnp.minimum(s, nb - 1)
    tok = lambda col: pl.BlockSpec((None, tb, w), lambda b, s: (b, cur(s), col))
    const = lambda arr: pl.BlockSpec(arr.shape, lambda b, s: (0,) * arr.ndim)
    kern = functools.partial(_even_post_kernel, tb=tb, seq=r, nh=nh)
    return pl.pallas_call(
        kern,
        grid=(bsz, nb + 1),
        in_specs=[
            tok(0), tok(0), tok(4), tok(5), tok(6),
            pl.BlockSpec((None, POOL_HALO, w), lambda b, s: (b, jnp.maximum(cur(s) * hb - 1, 0), 5)),
            pl.BlockSpec((None, POOL_HALO, w), lambda b, s: (b, jnp.minimum((cur(s) + 1) * hb, nb * hb - 1), 5)),
            pl.BlockSpec((None, tb, d), lambda b, s: (b, cur(s), 0)),
            pl.BlockSpec((None, 3, d), lambda b, s: (b, 0, 0)),
            const(hgn), const(pool_w), const(pool_scale), const(out_w),
            pl.BlockSpec((None, 3, d), lambda b, s: (b, 0, 0)), const(g_n), const(w_n),
        ],
        out_specs=[pl.BlockSpec((None, tb, d), lambda b, s: (b, cur(s), 0)),
                   pl.BlockSpec((None, tb, n_next), lambda b, s: (b, jnp.maximum(s - 1, 0), 0))],
        out_shape=[jax.ShapeDtypeStruct((bsz, r, d), F32), jax.ShapeDtypeStruct((bsz, r, n_next), F32)],
        scratch_shapes=[pltpu.VMEM((tb + 2 * POOL_HALO, w), F32), pltpu.VMEM((tb, 2 * w), BF16),
                        pltpu.VMEM((tb, d), BF16)],
        compiler_params=_params("parallel", "arbitrary"),
        name="even_post",
    )(o_f, o_b, p, p, p, p, p, h, mod, hgn, pool_w, pool_scale, out_w, mod_n, g_n, w_n)


def _mla_kv_kernel(*refs, rope):
    if rope:
        ckv_ref, kr_ref, g_ref, wuk_ref, wuvt_ref, cos_ref, sin_ref, kcat_ref, vt_ref = refs
    else:
        ckv_ref, kr_ref, g_ref, wuk_ref, wuvt_ref, kcat_ref, vt_ref = refs
    cn = _rms(ckv_ref[...], g_ref[...]).astype(BF16)
    kn = _dot(cn, wuk_ref[...])
    kr = kr_ref[...]
    if rope:
        lane = lax.broadcasted_iota(jnp.int32, kr.shape, 1)
        swapped = jnp.where((lane % (2 * ROPE_FREQ)) < ROPE_FREQ,
                            pltpu.roll(kr, LANES - ROPE_FREQ, 1), pltpu.roll(kr, ROPE_FREQ, 1))
        kr = kr * cos_ref[...] + swapped * sin_ref[...]
    kr = kr.astype(BF16)
    ones_rows = (lax.broadcasted_iota(jnp.int32, (VT_ROWS - MLA_V, kr.shape[0]), 0) == 0).astype(BF16)
    vt = _dot_nt(wuvt_ref[...], cn)
    for h in range(MLA_HEADS):
        kcat_ref[h, :, 0:MLA_NOPE] = kn[:, h * MLA_NOPE:(h + 1) * MLA_NOPE].astype(BF16)
        kcat_ref[h, :, MLA_NOPE:] = kr
        vt_ref[h, 0:MLA_V, :] = vt[h * MLA_V:(h + 1) * MLA_V].astype(BF16)
        vt_ref[h, MLA_V:, :] = ones_rows


def _mla_kv(p, ckv_blk, kr_blk, g, wuk, wuvt, tables, tb):
    bsz, r, _ = p.shape
    rank = g.shape[-1]
    nb = r // tb
    const = lambda arr: pl.BlockSpec(arr.shape, lambda b, j: (0,) * arr.ndim)
    in_specs = [
        pl.BlockSpec((None, tb, rank), lambda b, j: (b, j, ckv_blk)),
        pl.BlockSpec((None, tb, LANES), lambda b, j: (b, j, kr_blk)),
        const(g), const(wuk), const(wuvt),
    ]
    args = [p, p, g, wuk, wuvt]
    if tables is not None:
        in_specs += [pl.BlockSpec((tb, LANES), lambda b, j: (j, 0))] * 2
        args += list(tables)
    return pl.pallas_call(
        functools.partial(_mla_kv_kernel, rope=tables is not None),
        grid=(bsz, nb),
        in_specs=in_specs,
        out_specs=[
            pl.BlockSpec((None, MLA_HEADS, None, tb, QK_PAD), lambda b, j: (b, 0, j, 0, 0)),
            pl.BlockSpec((None, MLA_HEADS, None, VT_ROWS, tb), lambda b, j: (b, 0, j, 0, 0)),
        ],
        out_shape=[
            jax.ShapeDtypeStruct((bsz, MLA_HEADS, nb, tb, QK_PAD), BF16),
            jax.ShapeDtypeStruct((bsz, MLA_HEADS, nb, VT_ROWS, tb), BF16),
        ],
        compiler_params=_params("parallel", "parallel"),
        name="mla_kv_rope" if tables is not None else "mla_kv",
    )(*args)


def _mla_q_kernel(cq_ref, g_ref, wqt_ref, cos_ref, sin_ref, qt_ref):
    cn = _rms(cq_ref[...], g_ref[...]).astype(BF16)
    f = ROPE_FREQ
    qt_all = _dot_nt(wqt_ref[...], cn) * (MLA_SCALE * LOG2_E)
    for h in range(MLA_HEADS):
        qt = qt_all[h * MLA_QK:(h + 1) * MLA_QK]
        qt_ref[h, 0:MLA_NOPE, :] = qt[0:MLA_NOPE].astype(BF16)
        for ax in range(2):
            r0 = MLA_NOPE + ax * 2 * f
            x1 = qt[r0:r0 + f]
            x2 = qt[r0 + f:r0 + 2 * f]
            co = cos_ref[ax]
            si = sin_ref[ax]
            qt_ref[h, r0:r0 + f, :] = (x1 * co - x2 * si).astype(BF16)
            qt_ref[h, r0 + f:r0 + 2 * f, :] = (x2 * co + x1 * si).astype(BF16)
        qt_ref[h, MLA_QK:, :] = jnp.zeros((QK_PAD - MLA_QK, cn.shape[0]), BF16)


def _mla_q(p, cq_blk, g, wqt, cos_t, sin_t):
    bsz, t, _ = p.shape
    rank = g.shape[-1]
    tm = min(TOK_TILE, t)
    const = lambda arr: pl.BlockSpec(arr.shape, lambda b, j: (0,) * arr.ndim)
    tab = pl.BlockSpec((2, ROPE_FREQ, tm), lambda b, j: (0, 0, j))
    return pl.pallas_call(
        _mla_q_kernel,
        grid=(bsz, t // tm),
        in_specs=[pl.BlockSpec((None, tm, rank), lambda b, j: (b, j, cq_blk)), const(g), const(wqt), tab, tab],
        out_specs=pl.BlockSpec((None, MLA_HEADS, QK_PAD, tm), lambda b, j: (b, 0, 0, j)),
        out_shape=jax.ShapeDtypeStruct((bsz, MLA_HEADS, QK_PAD, t), BF16),
        compiler_params=_params("parallel", "parallel"),
        name="mla_q",
    )(p, g, wqt, cos_t, sin_t)


def _attn_kernel(qt_ref, kc_ref, vtc_ref, kl_ref, vtl_ref, g_ref, o_ref, m_ref, acc_ref, s_ref, mx_ref, *, tq, n_lat):
    nsub = tq // Q_SUB
    m_ref[...] = jnp.full(m_ref.shape, -jnp.inf, F32)
    acc_ref[...] = jnp.zeros(acc_ref.shape, F32)

    def scores(k, nxt, g):
        s = _dot(k, qt_ref[:, g * Q_SUB:(g + 1) * Q_SUB])
        s_ref[nxt, g, 0:k.shape[0], :] = s
        mx_ref[nxt, g] = jnp.max(s, axis=0, keepdims=True)

    def substep(k_next, vt_cur, cur, nxt):
        rows = vt_cur.shape[1]
        for g in range(nsub):
            sl = slice(g * Q_SUB, (g + 1) * Q_SUB)
            scores(k_next, nxt, g)
            m_old = m_ref[:, sl]
            m_new = jnp.maximum(m_old, mx_ref[cur, g])
            alpha = jnp.exp2(m_old - m_new)
            p = jnp.exp2(s_ref[cur, g, 0:rows, :] - m_new)
            acc_ref[:, sl] = alpha * acc_ref[:, sl] + _dot(vt_cur, p.astype(BF16))
            m_ref[:, sl] = m_new

    kc = kc_ref[...]
    for g in range(nsub):
        scores(kc, 0, g)
    substep(kl_ref[0], vtc_ref[...], 0, 1)

    def body(j, carry):
        for u in range(2):
            a = 2 * j + u
            substep(kl_ref[jnp.minimum(a + 1, n_lat - 1)], vtl_ref[a], (1 + u) % 2, u % 2)
        return carry

    lax.fori_loop(0, n_lat // 2, body, 0)
    o = (acc_ref[0:MLA_V, :] * (1.0 / acc_ref[MLA_V:MLA_V + 1, :])).T
    o_ref[...] = (o * _silu(g_ref[...])).astype(o_ref.dtype)


def _attn(qt, kc, vtc, kl, vtl, p):
    bsz, nh, _, t = qt.shape
    lc = kc.shape[3]
    tq = min(Q_TILE, t)
    n_lat = kl.shape[2]
    kv = kl.shape[3]
    assert kc.shape[2] == 1 and lc <= kv and n_lat % 2 == 0
    kern = functools.partial(_attn_kernel, tq=tq, n_lat=n_lat)
    ctx5 = lambda arr: pl.BlockSpec((None, None, None) + arr.shape[3:], lambda b, h, i: (b, h, 0, 0, 0))
    full5 = lambda arr: pl.BlockSpec((None, None) + arr.shape[2:], lambda b, h, i: (b, h, 0, 0, 0))
    return pl.pallas_call(
        kern,
        grid=(bsz, nh, t // tq),
        in_specs=[
            pl.BlockSpec((None, None, QK_PAD, tq), lambda b, h, i: (b, h, 0, i)),
            ctx5(kc), ctx5(vtc), full5(kl), full5(vtl),
            pl.BlockSpec((None, tq, MLA_V), lambda b, h, i: (b, i, h)),
        ],
        out_specs=pl.BlockSpec((None, tq, MLA_V), lambda b, h, i: (b, i, h)),
        out_shape=jax.ShapeDtypeStruct((bsz, t, nh * MLA_V), BF16),
        scratch_shapes=[pltpu.VMEM((1, tq), F32), pltpu.VMEM((VT_ROWS, tq), F32),
                        pltpu.VMEM((2, tq // Q_SUB, kv, Q_SUB), F32), pltpu.VMEM((2, tq // Q_SUB, 1, Q_SUB), F32)],
        compiler_params=_params("parallel", "parallel", "arbitrary"),
        name="mla_attention",
    )(qt, kc, vtc, kl, vtl, p)


def _out_final_kernel(y_ref, h_ref, mod_ref, ow_ref, g_ref, o_ref):
    hn = h_ref[...] + mod_ref[2:3, :] * _dot(y_ref[...], ow_ref[...])
    o_ref[...] = _rms(hn, g_ref[...])


def _out_final(y, h, mod, out_w, g):
    bsz, t, d = h.shape
    wi = y.shape[-1]
    tm = min(2 * TOK_TILE, t)
    return pl.pallas_call(
        _out_final_kernel,
        grid=(bsz, t // tm),
        in_specs=[
            pl.BlockSpec((None, tm, wi), lambda b, j: (b, j, 0)),
            pl.BlockSpec((None, tm, d), lambda b, j: (b, j, 0)),
            pl.BlockSpec((None, 3, d), lambda b, j: (b, 0, 0)),
            pl.BlockSpec((wi, d), lambda b, j: (0, 0)),
            pl.BlockSpec((1, d), lambda b, j: (0, 0)),
        ],
        out_specs=pl.BlockSpec((None, tm, d), lambda b, j: (b, j, 0)),
        out_shape=jax.ShapeDtypeStruct((bsz, t, d), F32),
        compiler_params=_params("parallel", "parallel"),
        name="out_final",
    )(y, h, mod, out_w, g)


def _rope_tables(n_tokens):
    rows = n_tokens // GRID_W
    pos_r = jnp.repeat(jnp.arange(rows), GRID_W).astype(F32)
    pos_c = jnp.tile(jnp.arange(GRID_W), rows).astype(F32)
    inv = ROPE_BASE ** (-2.0 * jnp.arange(ROPE_FREQ, dtype=F32) / (MLA_ROPE // 2))
    ang = jnp.stack([pos_r[:, None] * inv, pos_c[:, None] * inv], axis=1)
    cos, sin = jnp.cos(ang), jnp.sin(ang)
    pad = LANES - MLA_ROPE
    cos_k = jnp.pad(jnp.stack([cos, cos], axis=2).reshape(n_tokens, MLA_ROPE), ((0, 0), (0, pad)))
    sin_k = jnp.pad(jnp.stack([-sin, sin], axis=2).reshape(n_tokens, MLA_ROPE), ((0, 0), (0, pad)))
    cos_q = jnp.transpose(cos, (1, 2, 0))
    sin_q = jnp.transpose(sin, (1, 2, 0))
    return (cos_k, sin_k), (cos_q, sin_q)


def kernel(x, c, ctx, c_ctx, ada_w, ada_b, norm_g, out_w, ev_in_w, hg_lb, hg_norm_g, pool_w, pool_scale,
           od_in_w, qa_norm_g, qb_w, kva_norm_g, kvb_w, final_norm_g):
    bsz, t, d = x.shape
    lc = ctx.shape[1]
    depth = ada_w.shape[0]
    assert depth == 2 and t % (2 * TOK_TILE) == 0 and lc % TOK_TILE == 0 and t % GRID_W == 0
    w = hg_norm_g.shape[-1]
    nh = w // HG_DK
    q_rank = qa_norm_g.shape[-1]
    kv_rank = kva_norm_g.shape[-1]
    d_inner = out_w.shape[1]

    n_cond = -(-(bsz + 1) // SUBLANES) * SUBLANES
    cond = jnp.zeros((n_cond, d), F32).at[:bsz].set(c).at[bsz].set(c_ctx)
    mods = _ada(cond, ada_w, ada_b).reshape(depth, n_cond, 3, d)
    mod_l = [mods[l, :bsz] for l in range(depth)]
    mod_c = [mods[l, bsz:bsz + 1] for l in range(depth)]

    lb = jnp.cumsum(jax.nn.softmax(hg_lb.astype(F32), axis=1), axis=1)[:, 0].reshape(2, 1, w)
    w_in0 = ev_in_w[0].astype(BF16)
    g0 = norm_g[0].reshape(1, d)
    ctx_flat = ctx.reshape(1, bsz * lc, d)
    n_in0 = w_in0.shape[1]
    p_c = _modnorm_mm(ctx_flat, mod_c[0], g0, w_in0, 4 * TOK_TILE, n_in0 // 4, "in_proj0_ctx").reshape(bsz, lc, n_in0)
    p_l = _modnorm_mm(x, mod_l[0], g0, w_in0, 4 * TOK_TILE, n_in0 // 4, "in_proj0")
    consts = _hgrn_constants(HG_CHUNK)
    s0 = jnp.zeros((bsz, 2, nh, HG_DK, HG_DK), F32)
    of_c, ob_c, s_c = _hgrn(p_c, lb, s0, consts)
    of_l, ob_l, _ = _hgrn(p_l, lb, s_c, consts)
    hgn = hg_norm_g[0].reshape(1, w)
    pw = pool_w[0].astype(BF16)
    ps = pool_scale[0].reshape(1, w)
    ow0 = out_w[0].astype(BF16)
    o1 = q_rank
    o2 = o1 + kv_rank
    o3 = o2 + MLA_ROPE
    w1 = od_in_w[0]
    kr_pad = jnp.zeros((d, LANES - MLA_ROPE), F32)
    w_in1 = jnp.concatenate([w1[:, o3:], w1[:, :o1], w1[:, o1:o2], w1[:, o2:o3], kr_pad], axis=1).astype(BF16)
    w_in1c = w_in1[:, d_inner + q_rank:]
    g1 = norm_g[1].reshape(1, d)
    bcast = lambda m: jnp.broadcast_to(m, (bsz, 3, d))
    _, p1_c = _even_post(of_c, ob_c, p_c, ctx, bcast(mod_c[0]), hgn, pw, ps, ow0, bcast(mod_c[1]), g1, w_in1c)
    hl1, p1_l = _even_post(of_l, ob_l, p_l, x, mod_l[0], hgn, pw, ps, ow0, mod_l[1], g1, w_in1)

    kvw = kvb_w[0].reshape(kv_rank, MLA_HEADS, MLA_NOPE + MLA_V)
    wuk = kvw[..., :MLA_NOPE].reshape(kv_rank, MLA_HEADS * MLA_NOPE).astype(BF16)
    wuvt = jnp.transpose(kvw[..., MLA_NOPE:], (1, 2, 0)).reshape(MLA_HEADS * MLA_V, kv_rank).astype(BF16)
    wqt = jnp.transpose(qb_w[0]).astype(BF16)
    kvg = kva_norm_g[0].reshape(1, kv_rank)
    qag = qa_norm_g[0].reshape(1, q_rank)
    tab_k, tab_q = _rope_tables(t)
    kc, vtc = _mla_kv(p1_c, 0, kv_rank // LANES, kvg, wuk, wuvt, None, lc)
    kl, vtl = _mla_kv(p1_l, (d_inner + q_rank) // kv_rank, (d_inner + q_rank + kv_rank) // LANES, kvg, wuk, wuvt,
                      tab_k, min(KV_CHUNK, t))
    qt = _mla_q(p1_l, d_inner // q_rank, qag, wqt, *tab_q)
    y = _attn(qt, kc, vtc, kl, vtl, p1_l)
    return _out_final(y, hl1, mod_l[1], out_w[1].astype(BF16), final_norm_g.reshape(1, d))
```

```python
import functools

import numpy as np
import jax
import jax.numpy as jnp
from jax import lax
from jax.experimental import pallas as pl
from jax.experimental.pallas import tpu as pltpu

F32 = jnp.float32
BF16 = jnp.bfloat16

EPS = 1e-6
GRID_W = 64
HG_DK = 128
POOL_WINDOWS = (2, 4, 8, 16)
MLA_HEADS = 16
MLA_NOPE = 128
MLA_ROPE = 64
MLA_V = 128
MLA_QK = MLA_NOPE + MLA_ROPE
QK_PAD = 256
VT_ROWS = MLA_V + 16
MLA_SCALE = MLA_QK ** -0.5
LOG2_E = 1.4426950408889634
ROPE_FREQ = MLA_ROPE // 4
ROPE_BASE = 10000.0

LANES = 128
SUBLANES = 8
VMEM_LIMIT = 48 * 1024 * 1024

HG_CHUNK = 64
TOK_TILE = 256
KV_CHUNK = 512
Q_TILE = 4096
Q_SUB = 256
POOL_HALO = 8


def _dot(a, b):
    return jnp.dot(a, b, preferred_element_type=F32)


def _dot_nt(a, b):
    return lax.dot_general(a, b, (((1,), (1,)), ((), ())), preferred_element_type=F32)


def _dot_tn(a, b):
    return lax.dot_general(a, b, (((0,), (0,)), ((), ())), preferred_element_type=F32)


def _sigmoid(x):
    return 0.5 * jnp.tanh(0.5 * x) + 0.5


def _silu(x):
    h = 0.5 * x
    return h + h * jnp.tanh(h)


def _split_bf16(x):
    hi = x.astype(BF16)
    lo = (x - hi.astype(F32)).astype(BF16)
    return hi, lo


def _params(*sem):
    return pltpu.CompilerParams(dimension_semantics=sem, vmem_limit_bytes=VMEM_LIMIT)


def _ada_kernel(c_ref, w_ref, b_ref, o_ref):
    c = c_ref[...]
    s_hi, s_lo = _split_bf16(_silu(c))
    w_hi, w_lo = _split_bf16(w_ref[...])
    o_ref[...] = _dot(s_hi, w_hi) + _dot(s_lo, w_hi) + _dot(s_hi, w_lo) + b_ref[...]


def _ada(cond, ada_w, ada_b):
    depth, d, _ = ada_w.shape
    r = cond.shape[0]
    return pl.pallas_call(
        _ada_kernel,
        grid=(depth, 3),
        in_specs=[
            pl.BlockSpec((r, d), lambda l, j: (0, 0)),
            pl.BlockSpec((None, d, d), lambda l, j: (l, 0, j)),
            pl.BlockSpec((None, 1, d), lambda l, j: (l, 0, j)),
        ],
        out_specs=pl.BlockSpec((None, r, d), lambda l, j: (l, 0, j)),
        out_shape=jax.ShapeDtypeStruct((depth, r, 3 * d), F32),
        compiler_params=_params("parallel", "parallel"),
        name="ada_modulation",
    )(cond, ada_w, ada_b.reshape(depth, 1, 3 * d))


def _rms(x, g):
    return x * lax.rsqrt(jnp.mean(x * x, axis=-1, keepdims=True) + EPS) * g


def _modnorm_mm_kernel(x_ref, mod_ref, g_ref, w_ref, o_ref, z_ref):
    @pl.when(pl.program_id(2) == 0)
    def _():
        y = _rms(x_ref[...], g_ref[...])
        z_ref[...] = (y * (1.0 + mod_ref[1:2, :]) + mod_ref[0:1, :]).astype(BF16)

    o_ref[...] = _dot(z_ref[...], w_ref[...]).astype(o_ref.dtype)


def _modnorm_mm(x, mod, g, w, tm, tn, name):
    bx, r, d = x.shape
    n = w.shape[1]
    tm = min(tm, r)
    return pl.pallas_call(
        _modnorm_mm_kernel,
        grid=(bx, r // tm, n // tn),
        in_specs=[
            pl.BlockSpec((None, tm, d), lambda b, i, j: (b, i, 0)),
            pl.BlockSpec((None, 3, d), lambda b, i, j: (b, 0, 0)),
            pl.BlockSpec((1, d), lambda b, i, j: (0, 0)),
            pl.BlockSpec((d, tn), lambda b, i, j: (0, j)),
        ],
        out_specs=pl.BlockSpec((None, tm, tn), lambda b, i, j: (b, i, j)),
        out_shape=jax.ShapeDtypeStruct((bx, r, n), F32),
        scratch_shapes=[pltpu.VMEM((tm, d), BF16)],
        compiler_params=_params("parallel", "parallel", "arbitrary"),
        name=name,
    )(x, mod, g, w)


def _hgrn_levels(c):
    w = c // 2
    out = []
    while w >= 1:
        out.append(w)
        w //= 2
    return tuple(out)


def _hgrn_constants(c):
    t = np.arange(c)
    tri = np.tril(np.ones((c, c), np.float32))
    masks = []
    for w in _hgrn_levels(c):
        blk = t // (2 * w)
        first = (t % (2 * w)) < w
        masks.append(((blk[:, None] == blk[None, :]) & (~first[:, None]) & first[None, :]).astype(np.float32))
    masks = np.stack(masks)
    tri2 = np.stack([tri, tri[::-1, ::-1]])
    m2 = np.stack([masks, masks[:, ::-1, ::-1]])
    return jnp.asarray(tri2, BF16), jnp.asarray(m2, F32)


def _hgrn_level_operand(b, g, k, q, w, d):
    c, width = b.shape
    row = lax.broadcasted_iota(jnp.int32, (c, 1), 0)
    keys_first = d == 0
    if w >= SUBLANES:
        ref_off = w - 1 if d == 0 else w
        pieces = []
        for r0 in range(0, c, 2 * w):
            bref = jnp.broadcast_to(b[r0 + ref_off:r0 + ref_off + 1, :], (w, width))
            for half in range(2):
                sl = slice(r0 + half * w, r0 + (half + 1) * w)
                if (half == 0) == keys_first:
                    pieces.append(k[sl] * jnp.exp2(bref - b[sl]))
                else:
                    pieces.append(q[sl] * jnp.exp2(b[sl] - bref))
        return jnp.concatenate(pieces, axis=0)
    before = (row % (2 * w)) < w
    kq = jnp.where(before == keys_first, k, q)
    if w == 1:
        moving = (row % 2 == 1) if d == 0 else (row % 2 == 0)
        return kq * jnp.exp2(jnp.where(moving, g, 0.0))
    ref_off = w - 1 if d == 0 else w
    sub = lax.broadcasted_iota(jnp.int32, (SUBLANES, 1), 0)
    pieces = []
    for r0 in range(0, c, SUBLANES):
        lo = jnp.broadcast_to(b[r0 + ref_off:r0 + ref_off + 1, :], (SUBLANES, width))
        if 2 * w == SUBLANES:
            pieces.append(lo)
        else:
            hi = jnp.broadcast_to(b[r0 + 2 * w + ref_off:r0 + 2 * w + ref_off + 1, :], (SUBLANES, width))
            pieces.append(jnp.where(sub < 2 * w, lo, hi))
    bref = jnp.concatenate(pieces, axis=0)
    return kq * jnp.exp2(-jnp.abs(b - bref))


def _hgrn_wide(q_ref, f_ref, v_ref, lb, tri, d, r0, *, c):
    last = c - 1 if d == 0 else 0
    rows = pl.ds(r0, c)
    half = 0.5 * (1.0 - lb)
    f = (lb + half) + half * jnp.tanh(0.5 * f_ref[rows, :])
    g = jnp.log2(f)
    g_hi, g_lo = _split_bf16(g)
    b = _dot(tri, g_hi) + _dot(tri, g_lo)
    q = _silu(q_ref[rows, :])
    k = 1.0 - f
    v = v_ref[rows, :]
    bl = b[last:last + 1, :]
    return dict(d=d, rows=rows, g=g, b=b, q=q, k=k, v=v, vb=v.astype(BF16), qk=q * k,
                qe=(q * jnp.exp2(b)).astype(BF16), kend=(k * jnp.exp2(bl - b)).astype(BF16), ebl=jnp.exp2(bl))


def _head(x, h):
    return x[:, h * HG_DK:(h + 1) * HG_DK]


def _hgrn_pairs(s, h, mk_ref, *, c):
    d = s["d"]
    att = None
    for l, w in enumerate(_hgrn_levels(c)):
        x = _hgrn_level_operand(_head(s["b"], h), _head(s["g"], h), _head(s["k"], h), _head(s["q"], h), w, d)
        x = x.astype(BF16)
        t = mk_ref[d, l] * _dot_nt(x, x)
        att = t if att is None else att + t
    return att


def _hgrn_finish(s, h, att, o_ref, st_ref):
    d = s["d"]
    vb = _head(s["vb"], h)
    inter = _dot_nt(_head(s["qe"], h), st_ref[d, h].astype(BF16))
    diag = jnp.sum(_head(s["qk"], h), axis=-1, keepdims=True)
    o_ref[s["rows"], h * HG_DK:(h + 1) * HG_DK] = inter + _dot(att.astype(BF16), vb) + diag * _head(s["v"], h)
    st_ref[d, h] = _head(s["ebl"], h) * st_ref[d, h] + _dot_tn(vb, _head(s["kend"], h))


def _hgrn_kernel(qf_ref, ff_ref, vf_ref, qb_ref, fb_ref, vb_ref, lb_ref, tri_ref, mk_ref, s0_ref,
                 of_ref, ob_ref, sout_ref, st_ref, *, tb, c, nh):
    nchunk = tb // c

    @pl.when(pl.program_id(1) == 0)
    def _():
        st_ref[...] = s0_ref[...]

    def body(cc, carry):
        rf = pl.multiple_of(cc * c, c)
        rb = pl.multiple_of((nchunk - 1 - cc) * c, c)
        sides = ((_hgrn_wide(qf_ref, ff_ref, vf_ref, lb_ref[0], tri_ref[0], 0, rf, c=c), of_ref),
                 (_hgrn_wide(qb_ref, fb_ref, vb_ref, lb_ref[1], tri_ref[1], 1, rb, c=c), ob_ref))
        pending = [None, None]
        for h in range(nh + 1):
            cur = [_hgrn_pairs(s, h, mk_ref, c=c) if h < nh else None for s, _ in sides]
            for (s, o_ref), p in zip(sides, pending):
                if p is not None:
                    _hgrn_finish(s, h - 1, p, o_ref, st_ref)
            pending = cur
        return carry

    lax.fori_loop(0, nchunk, body, 0, unroll=True)

    @pl.when(pl.program_id(1) == pl.num_programs(1) - 1)
    def _():
        sout_ref[...] = st_ref[...]


def _hgrn(p, lb, s0, consts):
    bsz, r, _ = p.shape
    w = lb.shape[-1]
    nh = w // HG_DK
    tb = min(TOK_TILE, r)
    c = HG_CHUNK
    nb = r // tb
    tri2, m2 = consts
    fwd = lambda col: pl.BlockSpec((None, tb, w), lambda b, s: (b, s, col))
    bwd = lambda col: pl.BlockSpec((None, tb, w), lambda b, s: (b, nb - 1 - s, col))
    const = lambda arr: pl.BlockSpec(arr.shape, lambda b, s: (0,) * arr.ndim)
    st_spec = pl.BlockSpec((None, 2, nh, HG_DK, HG_DK), lambda b, s: (b, 0, 0, 0, 0))
    kern = functools.partial(_hgrn_kernel, tb=tb, c=c, nh=nh)
    return pl.pallas_call(
        kern,
        grid=(bsz, nb),
        in_specs=[fwd(0), fwd(1), fwd(3), bwd(0), bwd(2), bwd(3), const(lb), const(tri2), const(m2),
                  st_spec],
        out_specs=[
            pl.BlockSpec((None, tb, w), lambda b, s: (b, s, 0)),
            pl.BlockSpec((None, tb, w), lambda b, s: (b, nb - 1 - s, 0)),
            st_spec,
        ],
        out_shape=[
            jax.ShapeDtypeStruct((bsz, r, w), F32),
            jax.ShapeDtypeStruct((bsz, r, w), F32),
            jax.ShapeDtypeStruct(s0.shape, F32),
        ],
        scratch_shapes=[pltpu.VMEM((2, nh, HG_DK, HG_DK), F32)],
        compiler_params=_params("parallel", "arbitrary"),
        name="hgrn2_scan",
    )(p, p, p, p, p, p, lb, tri2, m2, s0)


def _even_post_kernel(of_ref, ob_ref, ga_ref, u_ref, gb_ref, up_ref, un_ref, h_ref, mod_ref, hgn_ref, pw_ref,
                      ps_ref, ow_ref, modn_ref, gn_ref, wn_ref, o_ref, pn_ref, ext_ref, y_ref, z_ref, *, tb, seq, nh):
    step = pl.program_id(1)
    last = pl.num_programs(1) - 2
    j = jnp.minimum(step, last)
    w = nh * HG_DK

    @pl.when(step == 0)
    def _():
        z_ref[...] = jnp.zeros(z_ref.shape, BF16)

    n_parts = nh + len(POOL_WINDOWS)
    n_blk = pn_ref.shape[1] // LANES
    per = -(-n_blk // n_parts)

    def in_proj_part(i):
        c0 = min(i * per, n_blk) * LANES
        c1 = min((i + 1) * per, n_blk) * LANES
        if c0 < c1:
            pn_ref[:, c0:c1] = _dot(z_ref[...], wn_ref[:, c0:c1])

    o = of_ref[...] + ob_ref[...]
    for h in range(nh):
        in_proj_part(h)
        sl = slice(h * HG_DK, (h + 1) * HG_DK)
        y_ref[:, sl] = (_rms(o[:, sl], hgn_ref[:, sl]) * _silu(ga_ref[:, sl])).astype(BF16)
    u = u_ref[...]
    ext_ref[0:POOL_HALO, :] = jnp.where(j > 0, up_ref[...], 0.0)
    ext_ref[POOL_HALO:POOL_HALO + tb, :] = u
    ext_ref[POOL_HALO + tb:, :] = jnp.where(j < last, un_ref[...], 0.0)
    t = j * tb + lax.broadcasted_iota(jnp.int32, (tb, 1), 0)
    grp = w // len(POOL_WINDOWS)
    for gi, win in enumerate(POOL_WINDOWS):
        in_proj_part(nh + gi)
        sl = slice(gi * grp, (gi + 1) * grp)
        acc = ext_ref[POOL_HALO - win // 2:POOL_HALO - win // 2 + tb, sl]
        for off in range(-win // 2 + 1, win // 2):
            acc = acc + ext_ref[POOL_HALO + off:POOL_HALO + off + tb, sl]
        cnt = (jnp.minimum(t + win // 2, seq) - jnp.maximum(t - win // 2, 0)).astype(F32)
        yp = acc * (1.0 / cnt) - u[:, sl]
        yb = _dot(yp.astype(BF16), pw_ref[gi]) * ps_ref[:, sl]
        y_ref[:, w + gi * grp:w + (gi + 1) * grp] = (yb * _silu(gb_ref[:, sl])).astype(BF16)
    hn = h_ref[...] + mod_ref[2:3, :] * _dot(y_ref[...], ow_ref[...])
    o_ref[...] = hn
    z_ref[...] = (_rms(hn, gn_ref[...]) * (1.0 + modn_ref[1:2, :]) + modn_ref[0:1, :]).astype(BF16)


def _even_post(o_f, o_b, p, h, mod, hgn, pool_w, pool_scale, out_w, mod_n, g_n, w_n):
    bsz, r, w = o_f.shape
    d = h.shape[-1]
    tb = min(TOK_TILE, r)
    nb = r // tb
    hb = tb // POOL_HALO
    nh = w // HG_DK
    n_next = w_n.shape[1]
    cur = lambda s: jnp.minimum(s, nb - 1)
    tok = lambda col: pl.BlockSpec((None, tb, w), lambda b, s: (b, cur(s), col))
    const = lambda arr: pl.BlockSpec(arr.shape, lambda b, s: (0,) * arr.ndim)
    kern = functools.partial(_even_post_kernel, tb=tb, seq=r, nh=nh)
    return pl.pallas_call(
        kern,
        grid=(bsz, nb + 1),
        in_specs=[
            tok(0), tok(0), tok(4), tok(5), tok(6),
            pl.BlockSpec((None, POOL_HALO, w), lambda b, s: (b, jnp.maximum(cur(s) * hb - 1, 0), 5)),
            pl.BlockSpec((None, POOL_HALO, w), lambda b, s: (b, jnp.minimum((cur(s) + 1) * hb, nb * hb - 1), 5)),
            pl.BlockSpec((None, tb, d), lambda b, s: (b, cur(s), 0)),
            pl.BlockSpec((None, 3, d), lambda b, s: (b, 0, 0)),
            const(hgn), const(pool_w), const(pool_scale), const(out_w),
            pl.BlockSpec((None, 3, d), lambda b, s: (b, 0, 0)), const(g_n), const(w_n),
        ],
        out_specs=[pl.BlockSpec((None, tb, d), lambda b, s: (b, cur(s), 0)),
                   pl.BlockSpec((None, tb, n_next), lambda b, s: (b, jnp.maximum(s - 1, 0), 0))],
        out_shape=[jax.ShapeDtypeStruct((bsz, r, d), F32), jax.ShapeDtypeStruct((bsz, r, n_next), F32)],
        scratch_shapes=[pltpu.VMEM((tb + 2 * POOL_HALO, w), F32), pltpu.VMEM((tb, 2 * w), BF16),
                        pltpu.VMEM((tb, d), BF16)],
        compiler_params=_params("parallel", "arbitrary"),
        name="even_post",
    )(o_f, o_b, p, p, p, p, p, h, mod, hgn, pool_w, pool_scale, out_w, mod_n, g_n, w_n)


def _mla_kv_kernel(*refs, rope):
    if rope:
        ckv_ref, kr_ref, g_ref, wuk_ref, wuvt_ref, cos_ref, sin_ref, kcat_ref, vt_ref = refs
    else:
        ckv_ref, kr_ref, g_ref, wuk_ref, wuvt_ref, kcat_ref, vt_ref = refs
    cn = _rms(ckv_ref[...], g_ref[...]).astype(BF16)
    kn = _dot(cn, wuk_ref[...])
    kr = kr_ref[...]
    if rope:
        lane = lax.broadcasted_iota(jnp.int32, kr.shape, 1)
        swapped = jnp.where((lane % (2 * ROPE_FREQ)) < ROPE_FREQ,
                            pltpu.roll(kr, LANES - ROPE_FREQ, 1), pltpu.roll(kr, ROPE_FREQ, 1))
        kr = kr * cos_ref[...] + swapped * sin_ref[...]
    kr = kr.astype(BF16)
    ones_rows = (lax.broadcasted_iota(jnp.int32, (VT_ROWS - MLA_V, kr.shape[0]), 0) == 0).astype(BF16)
    vt = _dot_nt(wuvt_ref[...], cn)
    for h in range(MLA_HEADS):
        kcat_ref[h, :, 0:MLA_NOPE] = kn[:, h * MLA_NOPE:(h + 1) * MLA_NOPE].astype(BF16)
        kcat_ref[h, :, MLA_NOPE:] = kr
        vt_ref[h, 0:MLA_V, :] = vt[h * MLA_V:(h + 1) * MLA_V].astype(BF16)
        vt_ref[h, MLA_V:, :] = ones_rows


def _mla_kv(p, ckv_blk, kr_blk, g, wuk, wuvt, tables, tb):
    bsz, r, _ = p.shape
    rank = g.shape[-1]
    nb = r // tb
    const = lambda arr: pl.BlockSpec(arr.shape, lambda b, j: (0,) * arr.ndim)
    in_specs = [
        pl.BlockSpec((None, tb, rank), lambda b, j: (b, j, ckv_blk)),
        pl.BlockSpec((None, tb, LANES), lambda b, j: (b, j, kr_blk)),
        const(g), const(wuk), const(wuvt),
    ]
    args = [p, p, g, wuk, wuvt]
    if tables is not None:
        in_specs += [pl.BlockSpec((tb, LANES), lambda b, j: (j, 0))] * 2
        args += list(tables)
    return pl.pallas_call(
        functools.partial(_mla_kv_kernel, rope=tables is not None),
        grid=(bsz, nb),
        in_specs=in_specs,
        out_specs=[
            pl.BlockSpec((None, MLA_HEADS, None, tb, QK_PAD), lambda b, j: (b, 0, j, 0, 0)),
            pl.BlockSpec((None, MLA_HEADS, None, VT_ROWS, tb), lambda b, j: (b, 0, j, 0, 0)),
        ],
        out_shape=[
            jax.ShapeDtypeStruct((bsz, MLA_HEADS, nb, tb, QK_PAD), BF16),
            jax.ShapeDtypeStruct((bsz, MLA_HEADS, nb, VT_ROWS, tb), BF16),
        ],
        compiler_params=_params("parallel", "parallel"),
        name="mla_kv_rope" if tables is not None else "mla_kv",
    )(*args)


def _mla_q_kernel(cq_ref, g_ref, wqt_ref, cos_ref, sin_ref, qt_ref):
    cn = _rms(cq_ref[...], g_ref[...]).astype(BF16)
    f = ROPE_FREQ
    qt_all = _dot_nt(wqt_ref[...], cn) * (MLA_SCALE * LOG2_E)
    for h in range(MLA_HEADS):
        qt = qt_all[h * MLA_QK:(h + 1) * MLA_QK]
        qt_ref[h, 0:MLA_NOPE, :] = qt[0:MLA_NOPE].astype(BF16)
        for ax in range(2):
            r0 = MLA_NOPE + ax * 2 * f
            x1 = qt[r0:r0 + f]
            x2 = qt[r0 + f:r0 + 2 * f]
            co = cos_ref[ax]
            si = sin_ref[ax]
            qt_ref[h, r0:r0 + f, :] = (x1 * co - x2 * si).astype(BF16)
            qt_ref[h, r0 + f:r0 + 2 * f, :] = (x2 * co + x1 * si).astype(BF16)
        qt_ref[h, MLA_QK:, :] = jnp.zeros((QK_PAD - MLA_QK, cn.shape[0]), BF16)


def _mla_q(p, cq_blk, g, wqt, cos_t, sin_t):
    bsz, t, _ = p.shape
    rank = g.shape[-1]
    tm = min(2 * TOK_TILE, t)
    const = lambda arr: pl.BlockSpec(arr.shape, lambda b, j: (0,) * arr.ndim)
    tab = pl.BlockSpec((2, ROPE_FREQ, tm), lambda b, j: (0, 0, j))
    return pl.pallas_call(
        _mla_q_kernel,
        grid=(bsz, t // tm),
        in_specs=[pl.BlockSpec((None, tm, rank), lambda b, j: (b, j, cq_blk)), const(g), const(wqt), tab, tab],
        out_specs=pl.BlockSpec((None, MLA_HEADS, QK_PAD, tm), lambda b, j: (b, 0, 0, j)),
        out_shape=jax.ShapeDtypeStruct((bsz, MLA_HEADS, QK_PAD, t), BF16),
        compiler_params=_params("parallel", "parallel"),
        name="mla_q",
    )(p, g, wqt, cos_t, sin_t)


def _attn_kernel(qt_ref, kc_ref, vtc_ref, kl_ref, vtl_ref, g_ref, o_ref, m_ref, acc_ref, s_ref, mx_ref, *, tq, n_lat):
    nsub = tq // Q_SUB
    m_ref[...] = jnp.full(m_ref.shape, -jnp.inf, F32)
    acc_ref[...] = jnp.zeros(acc_ref.shape, F32)

    def scores(k, nxt, g):
        s = _dot(k, qt_ref[:, g * Q_SUB:(g + 1) * Q_SUB])
        s_ref[nxt, g, 0:k.shape[0], :] = s
        mx_ref[nxt, g] = jnp.max(s, axis=0, keepdims=True)

    def substep(k_next, vt_cur, cur, nxt):
        rows = vt_cur.shape[1]
        for g in range(nsub):
            sl = slice(g * Q_SUB, (g + 1) * Q_SUB)
            scores(k_next, nxt, g)
            m_old = m_ref[:, sl]
            m_new = jnp.maximum(m_old, mx_ref[cur, g])
            alpha = jnp.exp2(m_old - m_new)
            p = jnp.exp2(s_ref[cur, g, 0:rows, :] - m_new)
            acc_ref[:, sl] = alpha * acc_ref[:, sl] + _dot(vt_cur, p.astype(BF16))
            m_ref[:, sl] = m_new

    kc = kc_ref[...]
    for g in range(nsub):
        scores(kc, 0, g)
    substep(kl_ref[0], vtc_ref[...], 0, 1)

    def body(j, carry):
        for u in range(2):
            a = 2 * j + u
            substep(kl_ref[jnp.minimum(a + 1, n_lat - 1)], vtl_ref[a], (1 + u) % 2, u % 2)
        return carry

    lax.fori_loop(0, n_lat // 2, body, 0)
    o = (acc_ref[0:MLA_V, :] * (1.0 / acc_ref[MLA_V:MLA_V + 1, :])).T
    o_ref[...] = (o * _silu(g_ref[...])).astype(o_ref.dtype)


def _attn(qt, kc, vtc, kl, vtl, p):
    bsz, nh, _, t = qt.shape
    lc = kc.shape[3]
    tq = min(Q_TILE, t)
    n_lat = kl.shape[2]
    kv = kl.shape[3]
    assert kc.shape[2] == 1 and lc <= kv and n_lat % 2 == 0
    kern = functools.partial(_attn_kernel, tq=tq, n_lat=n_lat)
    ctx5 = lambda arr: pl.BlockSpec((None, None, None) + arr.shape[3:], lambda b, h, i: (b, h, 0, 0, 0))
    full5 = lambda arr: pl.BlockSpec((None, None) + arr.shape[2:], lambda b, h, i: (b, h, 0, 0, 0))
    return pl.pallas_call(
        kern,
        grid=(bsz, nh, t // tq),
        in_specs=[
            pl.BlockSpec((None, None, QK_PAD, tq), lambda b, h, i: (b, h, 0, i)),
            ctx5(kc), ctx5(vtc), full5(kl), full5(vtl),
            pl.BlockSpec((None, tq, MLA_V), lambda b, h, i: (b, i, h)),
        ],
        out_specs=pl.BlockSpec((None, tq, MLA_V), lambda b, h, i: (b, i, h)),
        out_shape=jax.ShapeDtypeStruct((bsz, t, nh * MLA_V), BF16),
        scratch_shapes=[pltpu.VMEM((1, tq), F32), pltpu.VMEM((VT_ROWS, tq), F32),
                        pltpu.VMEM((2, tq // Q_SUB, kv, Q_SUB), F32), pltpu.VMEM((2, tq // Q_SUB, 1, Q_SUB), F32)],
        compiler_params=_params("parallel", "parallel", "arbitrary"),
        name="mla_attention",
    )(qt, kc, vtc, kl, vtl, p)


def _out_final_kernel(y_ref, h_ref, mod_ref, ow_ref, g_ref, o_ref):
    hn = h_ref[...] + mod_ref[2:3, :] * _dot(y_ref[...], ow_ref[...])
    o_ref[...] = _rms(hn, g_ref[...])


def _out_final(y, h, mod, out_w, g):
    bsz, t, d = h.shape
    wi = y.shape[-1]
    tm = min(2 * TOK_TILE, t)
    return pl.pallas_call(
        _out_final_kernel,
        grid=(bsz, t // tm),
        in_specs=[
            pl.BlockSpec((None, tm, wi), lambda b, j: (b, j, 0)),
            pl.BlockSpec((None, tm, d), lambda b, j: (b, j, 0)),
            pl.BlockSpec((None, 3, d), lambda b, j: (b, 0, 0)),
            pl.BlockSpec((wi, d), lambda b, j: (0, 0)),
            pl.BlockSpec((1, d), lambda b, j: (0, 0)),
        ],
        out_specs=pl.BlockSpec((None, tm, d), lambda b, j: (b, j, 0)),
        out_shape=jax.ShapeDtypeStruct((bsz, t, d), F32),
        compiler_params=_params("parallel", "parallel"),
        name="out_final",
    )(y, h, mod, out_w, g)


def _rope_tables(n_tokens):
    rows = n_tokens // GRID_W
    pos_r = jnp.repeat(jnp.arange(rows), GRID_W).astype(F32)
    pos_c = jnp.tile(jnp.arange(GRID_W), rows).astype(F32)
    inv = ROPE_BASE ** (-2.0 * jnp.arange(ROPE_FREQ, dtype=F32) / (MLA_ROPE // 2))
    ang = jnp.stack([pos_r[:, None] * inv, pos_c[:, None] * inv], axis=1)
    cos, sin = jnp.cos(ang), jnp.sin(ang)
    pad = LANES - MLA_ROPE
    cos_k = jnp.pad(jnp.stack([cos, cos], axis=2).reshape(n_tokens, MLA_ROPE), ((0, 0), (0, pad)))
    sin_k = jnp.pad(jnp.stack([-sin, sin], axis=2).reshape(n_tokens, MLA_ROPE), ((0, 0), (0, pad)))
    cos_q = jnp.transpose(cos, (1, 2, 0))
    sin_q = jnp.transpose(sin, (1, 2, 0))
    return (cos_k, sin_k), (cos_q, sin_q)


def kernel(x, c, ctx, c_ctx, ada_w, ada_b, norm_g, out_w, ev_in_w, hg_lb, hg_norm_g, pool_w, pool_scale,
           od_in_w, qa_norm_g, qb_w, kva_norm_g, kvb_w, final_norm_g):
    bsz, t, d = x.shape
    lc = ctx.shape[1]
    depth = ada_w.shape[0]
    assert depth == 2 and t % (2 * TOK_TILE) == 0 and lc % TOK_TILE == 0 and t % GRID_W == 0
    w = hg_norm_g.shape[-1]
    nh = w // HG_DK
    q_rank = qa_norm_g.shape[-1]
    kv_rank = kva_norm_g.shape[-1]
    d_inner = out_w.shape[1]

    n_cond = -(-(bsz + 1) // SUBLANES) * SUBLANES
    cond = jnp.zeros((n_cond, d), F32).at[:bsz].set(c).at[bsz].set(c_ctx)
    mods = _ada(cond, ada_w, ada_b).reshape(depth, n_cond, 3, d)
    mod_l = [mods[l, :bsz] for l in range(depth)]
    mod_c = [mods[l, bsz:bsz + 1] for l in range(depth)]

    lb = jnp.cumsum(jax.nn.softmax(hg_lb.astype(F32), axis=1), axis=1)[:, 0].reshape(2, 1, w)
    w_in0 = ev_in_w[0].astype(BF16)
    g0 = norm_g[0].reshape(1, d)
    ctx_flat = ctx.reshape(1, bsz * lc, d)
    n_in0 = w_in0.shape[1]
    p_c = _modnorm_mm(ctx_flat, mod_c[0], g0, w_in0, 4 * TOK_TILE, n_in0 // 4, "in_proj0_ctx").reshape(bsz, lc, n_in0)
    p_l = _modnorm_mm(x, mod_l[0], g0, w_in0, 4 * TOK_TILE, n_in0 // 4, "in_proj0")
    consts = _hgrn_constants(HG_CHUNK)
    s0 = jnp.zeros((bsz, 2, nh, HG_DK, HG_DK), F32)
    of_c, ob_c, s_c = _hgrn(p_c, lb, s0, consts)
    of_l, ob_l, _ = _hgrn(p_l, lb, s_c, consts)
    hgn = hg_norm_g[0].reshape(1, w)
    pw = pool_w[0].astype(BF16)
    ps = pool_scale[0].reshape(1, w)
    ow0 = out_w[0].astype(BF16)
    o1 = q_rank
    o2 = o1 + kv_rank
    o3 = o2 + MLA_ROPE
    w1 = od_in_w[0]
    kr_pad = jnp.zeros((d, LANES - MLA_ROPE), F32)
    w_in1 = jnp.concatenate([w1[:, o3:], w1[:, :o1], w1[:, o1:o2], w1[:, o2:o3], kr_pad], axis=1).astype(BF16)
    w_in1c = w_in1[:, d_inner + q_rank:]
    g1 = norm_g[1].reshape(1, d)
    bcast = lambda m: jnp.broadcast_to(m, (bsz, 3, d))
    _, p1_c = _even_post(of_c, ob_c, p_c, ctx, bcast(mod_c[0]), hgn, pw, ps, ow0, bcast(mod_c[1]), g1, w_in1c)
    hl1, p1_l = _even_post(of_l, ob_l, p_l, x, mod_l[0], hgn, pw, ps, ow0, mod_l[1], g1, w_in1)

    kvw = kvb_w[0].reshape(kv_rank, MLA_HEADS, MLA_NOPE + MLA_V)
    wuk = kvw[..., :MLA_NOPE].reshape(kv_rank, MLA_HEADS * MLA_NOPE).astype(BF16)
    wuvt = jnp.transpose(kvw[..., MLA_NOPE:], (1, 2, 0)).reshape(MLA_HEADS * MLA_V, kv_rank).astype(BF16)
    wqt = jnp.transpose(qb_w[0]).astype(BF16)
    kvg = kva_norm_g[0].reshape(1, kv_rank)
    qag = qa_norm_g[0].reshape(1, q_rank)
    tab_k, tab_q = _rope_tables(t)
    kc, vtc = _mla_kv(p1_c, 0, kv_rank // LANES, kvg, wuk, wuvt, None, lc)
    kl, vtl = _mla_kv(p1_l, (d_inner + q_rank) // kv_rank, (d_inner + q_rank + kv_rank) // LANES, kvg, wuk, wuvt,
                      tab_k, min(KV_CHUNK, t))
    qt = _mla_q(p1_l, d_inner // q_rank, qag, wqt, *tab_q)
    y = _attn(qt, kc, vtc, kl, vtl, p1_l)
    return _out_final(y, hl1, mod_l[1], out_w[1].astype(BF16), final_norm_g.reshape(1, d))
```

```python
import functools

import numpy as np
import jax
import jax.numpy as jnp
from jax import lax
from jax.experimental import pallas as pl
from jax.experimental.pallas import tpu as pltpu

F32 = jnp.float32
BF16 = jnp.bfloat16

EPS = 1e-6
GRID_W = 64
HG_DK = 128
POOL_WINDOWS = (2, 4, 8, 16)
MLA_HEADS = 16
MLA_NOPE = 128
MLA_ROPE = 64
MLA_V = 128
MLA_QK = MLA_NOPE + MLA_ROPE
QK_PAD = 256
VT_ROWS = MLA_V + 16
MLA_SCALE = MLA_QK ** -0.5
LOG2_E = 1.4426950408889634
ROPE_FREQ = MLA_ROPE // 4
ROPE_BASE = 10000.0

LANES = 128
SUBLANES = 8
VMEM_LIMIT = 48 * 1024 * 1024

HG_CHUNK = 64
TOK_TILE = 256
KV_CHUNK = 512
Q_TILE = 4096
Q_SUB = 256
POOL_HALO = 8


def _dot(a, b):
    return jnp.dot(a, b, preferred_element_type=F32)


def _dot_nt(a, b):
    return lax.dot_general(a, b, (((1,), (1,)), ((), ())), preferred_element_type=F32)


def _dot_tn(a, b):
    return lax.dot_general(a, b, (((0,), (0,)), ((), ())), preferred_element_type=F32)


def _silu(x):
    h = 0.5 * x
    return h + h * jnp.tanh(h)


def _split_bf16(x):
    hi = x.astype(BF16)
    lo = (x - hi.astype(F32)).astype(BF16)
    return hi, lo


def _params(*sem):
    return pltpu.CompilerParams(dimension_semantics=sem, vmem_limit_bytes=VMEM_LIMIT)


def _ada_kernel(c_ref, w_ref, b_ref, o_ref):
    c = c_ref[...]
    s_hi, s_lo = _split_bf16(_silu(c))
    w_hi, w_lo = _split_bf16(w_ref[...])
    o_ref[...] = _dot(s_hi, w_hi) + _dot(s_lo, w_hi) + _dot(s_hi, w_lo) + b_ref[...]


def _ada(cond, ada_w, ada_b):
    depth, d, _ = ada_w.shape
    r = cond.shape[0]
    return pl.pallas_call(
        _ada_kernel,
        grid=(depth, 3),
        in_specs=[
            pl.BlockSpec((r, d), lambda l, j: (0, 0)),
            pl.BlockSpec((None, d, d), lambda l, j: (l, 0, j)),
            pl.BlockSpec((None, 1, d), lambda l, j: (l, 0, j)),
        ],
        out_specs=pl.BlockSpec((None, r, d), lambda l, j: (l, 0, j)),
        out_shape=jax.ShapeDtypeStruct((depth, r, 3 * d), F32),
        compiler_params=_params("parallel", "parallel"),
        name="ada_modulation",
    )(cond, ada_w, ada_b.reshape(depth, 1, 3 * d))


def _rms(x, g):
    return x * lax.rsqrt(jnp.mean(x * x, axis=-1, keepdims=True) + EPS) * g


def _modnorm_mm_kernel(x_ref, mod_ref, g_ref, w_ref, o_ref, z_ref):
    @pl.when(pl.program_id(2) == 0)
    def _():
        y = _rms(x_ref[...], g_ref[...])
        z_ref[...] = (y * (1.0 + mod_ref[1:2, :]) + mod_ref[0:1, :]).astype(BF16)

    o_ref[...] = _dot(z_ref[...], w_ref[...]).astype(o_ref.dtype)


def _modnorm_mm(x, mod, g, w, tm, tn, name):
    bx, r, d = x.shape
    n = w.shape[1]
    tm = min(tm, r)
    return pl.pallas_call(
        _modnorm_mm_kernel,
        grid=(bx, r // tm, n // tn),
        in_specs=[
            pl.BlockSpec((None, tm, d), lambda b, i, j: (b, i, 0)),
            pl.BlockSpec((None, 3, d), lambda b, i, j: (b, 0, 0)),
            pl.BlockSpec((1, d), lambda b, i, j: (0, 0)),
            pl.BlockSpec((d, tn), lambda b, i, j: (0, j)),
        ],
        out_specs=pl.BlockSpec((None, tm, tn), lambda b, i, j: (b, i, j)),
        out_shape=jax.ShapeDtypeStruct((bx, r, n), F32),
        scratch_shapes=[pltpu.VMEM((tm, d), BF16)],
        compiler_params=_params("parallel", "parallel", "arbitrary"),
        name=name,
    )(x, mod, g, w)


def _hgrn_levels(c):
    w = c // 2
    out = []
    while w >= 1:
        out.append(w)
        w //= 2
    return tuple(out)


def _hgrn_constants(c):
    t = np.arange(c)
    tri = np.tril(np.ones((c, c), np.float32))
    masks = []
    for w in _hgrn_levels(c):
        blk = t // (2 * w)
        first = (t % (2 * w)) < w
        masks.append(((blk[:, None] == blk[None, :]) & (~first[:, None]) & first[None, :]).astype(np.float32))
    masks = np.stack(masks)
    tri2 = np.stack([tri, tri[::-1, ::-1]])
    m2 = np.stack([masks, masks[:, ::-1, ::-1]])
    return jnp.asarray(tri2, BF16), jnp.asarray(m2, F32)


def _hgrn_level_operand(b, g, k, q, w, d):
    c, width = b.shape
    row = lax.broadcasted_iota(jnp.int32, (c, 1), 0)
    keys_first = d == 0
    if w >= SUBLANES:
        ref_off = w - 1 if d == 0 else w
        pieces = []
        for r0 in range(0, c, 2 * w):
            bref = jnp.broadcast_to(b[r0 + ref_off:r0 + ref_off + 1, :], (w, width))
            for half in range(2):
                sl = slice(r0 + half * w, r0 + (half + 1) * w)
                if (half == 0) == keys_first:
                    pieces.append(k[sl] * jnp.exp2(bref - b[sl]))
                else:
                    pieces.append(q[sl] * jnp.exp2(b[sl] - bref))
        return jnp.concatenate(pieces, axis=0)
    before = (row % (2 * w)) < w
    kq = jnp.where(before == keys_first, k, q)
    if w == 1:
        moving = (row % 2 == 1) if d == 0 else (row % 2 == 0)
        return kq * jnp.exp2(jnp.where(moving, g, 0.0))
    ref_off = w - 1 if d == 0 else w
    sub = lax.broadcasted_iota(jnp.int32, (SUBLANES, 1), 0)
    pieces = []
    for r0 in range(0, c, SUBLANES):
        lo = jnp.broadcast_to(b[r0 + ref_off:r0 + ref_off + 1, :], (SUBLANES, width))
        if 2 * w == SUBLANES:
            pieces.append(lo)
        else:
            hi = jnp.broadcast_to(b[r0 + 2 * w + ref_off:r0 + 2 * w + ref_off + 1, :], (SUBLANES, width))
            pieces.append(jnp.where(sub < 2 * w, lo, hi))
    bref = jnp.concatenate(pieces, axis=0)
    return kq * jnp.exp2(-jnp.abs(b - bref))


def _hgrn_wide(q_ref, f_ref, v_ref, lb, tri, d, r0, *, c):
    last = c - 1 if d == 0 else 0
    rows = pl.ds(r0, c)
    half = 0.5 * (1.0 - lb)
    f = (lb + half) + half * jnp.tanh(0.5 * f_ref[rows, :])
    g = jnp.log2(f)
    g_hi, g_lo = _split_bf16(g)
    b = _dot(tri, g_hi) + _dot(tri, g_lo)
    q = _silu(q_ref[rows, :])
    k = 1.0 - f
    v = v_ref[rows, :]
    bl = b[last:last + 1, :]
    return dict(d=d, rows=rows, g=g, b=b, q=q, k=k, v=v, vb=v.astype(BF16), qk=q * k,
                qe=(q * jnp.exp2(b)).astype(BF16), kend=(k * jnp.exp2(bl - b)).astype(BF16), ebl=jnp.exp2(bl))


def _head(x, h):
    return x[:, h * HG_DK:(h + 1) * HG_DK]


def _hgrn_pairs(s, h, mk_ref, *, c):
    d = s["d"]
    att = None
    for l, w in enumerate(_hgrn_levels(c)):
        x = _hgrn_level_operand(_head(s["b"], h), _head(s["g"], h), _head(s["k"], h), _head(s["q"], h), w, d)
        x = x.astype(BF16)
        t = mk_ref[d, l] * _dot_nt(x, x)
        att = t if att is None else att + t
    return att


def _hgrn_finish(s, h, att, o_ref, st_ref):
    d = s["d"]
    vb = _head(s["vb"], h)
    inter = _dot_nt(_head(s["qe"], h), st_ref[d, h].astype(BF16))
    diag = jnp.sum(_head(s["qk"], h), axis=-1, keepdims=True)
    o_ref[s["rows"], h * HG_DK:(h + 1) * HG_DK] = inter + _dot(att.astype(BF16), vb) + diag * _head(s["v"], h)
    st_ref[d, h] = _head(s["ebl"], h) * st_ref[d, h] + _dot_tn(vb, _head(s["kend"], h))


def _hgrn_kernel(qf_ref, ff_ref, vf_ref, qb_ref, fb_ref, vb_ref, lb_ref, tri_ref, mk_ref, s0_ref,
                 of_ref, ob_ref, sout_ref, st_ref, *, tb, c, nh, layer):
    nchunk = tb // c
    slots = lb_ref[...]
    e = jnp.exp(slots - jnp.max(slots, axis=1, keepdims=True))
    lb = jnp.sum(e[:, :layer + 1], axis=1) / jnp.sum(e, axis=1)

    @pl.when(pl.program_id(1) == 0)
    def _():
        st_ref[...] = s0_ref[...]

    def body(cc, carry):
        rf = pl.multiple_of(cc * c, c)
        rb = pl.multiple_of((nchunk - 1 - cc) * c, c)
        sides = ((_hgrn_wide(qf_ref, ff_ref, vf_ref, lb[0], tri_ref[0], 0, rf, c=c), of_ref),
                 (_hgrn_wide(qb_ref, fb_ref, vb_ref, lb[1], tri_ref[1], 1, rb, c=c), ob_ref))
        pending = [None, None]
        for h in range(nh + 1):
            cur = [_hgrn_pairs(s, h, mk_ref, c=c) if h < nh else None for s, _ in sides]
            for (s, o_ref), p in zip(sides, pending):
                if p is not None:
                    _hgrn_finish(s, h - 1, p, o_ref, st_ref)
            pending = cur
        return carry

    lax.fori_loop(0, nchunk, body, 0, unroll=True)

    @pl.when(pl.program_id(1) == pl.num_programs(1) - 1)
    def _():
        sout_ref[...] = st_ref[...]


def _hgrn(p, lb, layer, s0, consts):
    bsz, r, _ = p.shape
    w = lb.shape[-1]
    nh = w // HG_DK
    tb = min(TOK_TILE, r)
    c = HG_CHUNK
    nb = r // tb
    tri2, m2 = consts
    fwd = lambda col: pl.BlockSpec((None, tb, w), lambda b, s: (b, s, col))
    bwd = lambda col: pl.BlockSpec((None, tb, w), lambda b, s: (b, nb - 1 - s, col))
    const = lambda arr: pl.BlockSpec(arr.shape, lambda b, s: (0,) * arr.ndim)
    st_spec = pl.BlockSpec((None, 2, nh, HG_DK, HG_DK), lambda b, s: (b, 0, 0, 0, 0))
    kern = functools.partial(_hgrn_kernel, tb=tb, c=c, nh=nh, layer=layer)
    return pl.pallas_call(
        kern,
        grid=(bsz, nb),
        in_specs=[fwd(0), fwd(1), fwd(3), bwd(0), bwd(2), bwd(3), const(lb), const(tri2), const(m2),
                  st_spec],
        out_specs=[
            pl.BlockSpec((None, tb, w), lambda b, s: (b, s, 0)),
            pl.BlockSpec((None, tb, w), lambda b, s: (b, nb - 1 - s, 0)),
            st_spec,
        ],
        out_shape=[
            jax.ShapeDtypeStruct((bsz, r, w), F32),
            jax.ShapeDtypeStruct((bsz, r, w), F32),
            jax.ShapeDtypeStruct(s0.shape, F32),
        ],
        scratch_shapes=[pltpu.VMEM((2, nh, HG_DK, HG_DK), F32)],
        compiler_params=_params("parallel", "arbitrary"),
        name="hgrn2_scan",
    )(p, p, p, p, p, p, lb, tri2, m2, s0)


def _even_post_kernel(of_ref, ob_ref, ga_ref, u_ref, gb_ref, up_ref, un_ref, h_ref, mod_ref, hgn_ref, pw_ref,
                      ps_ref, ow_ref, modn_ref, gn_ref, wn_ref, o_ref, pn_ref, ext_ref, y_ref, z_ref, *, tb, seq, nh):
    step = pl.program_id(1)
    last = pl.num_programs(1) - 2
    j = jnp.minimum(step, last)
    w = nh * HG_DK

    @pl.when(step == 0)
    def _():
        z_ref[...] = jnp.zeros(z_ref.shape, BF16)

    n_parts = nh + len(POOL_WINDOWS)
    n_blk = pn_ref.shape[1] // LANES
    per = -(-n_blk // n_parts)

    def in_proj_part(i):
        c0 = min(i * per, n_blk) * LANES
        c1 = min((i + 1) * per, n_blk) * LANES
        if c0 < c1:
            pn_ref[:, c0:c1] = _dot(z_ref[...], wn_ref[:, c0:c1])

    o = of_ref[...] + ob_ref[...]
    for h in range(nh):
        in_proj_part(h)
        sl = slice(h * HG_DK, (h + 1) * HG_DK)
        y_ref[:, sl] = (_rms(o[:, sl], hgn_ref[:, sl]) * _silu(ga_ref[:, sl])).astype(BF16)
    u = u_ref[...]
    ext_ref[0:POOL_HALO, :] = jnp.where(j > 0, up_ref[...], 0.0)
    ext_ref[POOL_HALO:POOL_HALO + tb, :] = u
    ext_ref[POOL_HALO + tb:, :] = jnp.where(j < last, un_ref[...], 0.0)
    t = j * tb + lax.broadcasted_iota(jnp.int32, (tb, 1), 0)
    grp = w // len(POOL_WINDOWS)
    for gi, win in enumerate(POOL_WINDOWS):
        in_proj_part(nh + gi)
        sl = slice(gi * grp, (gi + 1) * grp)
        acc = ext_ref[POOL_HALO - win // 2:POOL_HALO - win // 2 + tb, sl]
        for off in range(-win // 2 + 1, win // 2):
            acc = acc + ext_ref[POOL_HALO + off:POOL_HALO + off + tb, sl]
        cnt = (jnp.minimum(t + win // 2, seq) - jnp.maximum(t - win // 2, 0)).astype(F32)
        yp = acc * (1.0 / cnt) - u[:, sl]
        yb = _dot(yp.astype(BF16), pw_ref[gi]) * ps_ref[:, sl]
        y_ref[:, w + gi * grp:w + (gi + 1) * grp] = (yb * _silu(gb_ref[:, sl])).astype(BF16)
    hn = h_ref[...] + mod_ref[2:3, :] * _dot(y_ref[...], ow_ref[...])
    o_ref[...] = hn
    z_ref[...] = (_rms(hn, gn_ref[...]) * (1.0 + modn_ref[1:2, :]) + modn_ref[0:1, :]).astype(BF16)


def _even_post(o_f, o_b, p, h, mod, hgn, pool_w, pool_scale, out_w, mod_n, g_n, w_n):
    bsz, r, w = o_f.shape
    d = h.shape[-1]
    tb = min(TOK_TILE, r)
    nb = r // tb
    hb = tb // POOL_HALO
    nh = w // HG_DK
    n_next = w_n.shape[1]
    cur = lambda s: jnp.minimum(s, nb - 1)
    tok = lambda col: pl.BlockSpec((None, tb, w), lambda b, s: (b, cur(s), col))
    const = lambda arr: pl.BlockSpec(arr.shape, lambda b, s: (0,) * arr.ndim)
    kern = functools.partial(_even_post_kernel, tb=tb, seq=r, nh=nh)
    return pl.pallas_call(
        kern,
        grid=(bsz, nb + 1),
        in_specs=[
            tok(0), tok(0), tok(4), tok(5), tok(6),
            pl.BlockSpec((None, POOL_HALO, w), lambda b, s: (b, jnp.maximum(cur(s) * hb - 1, 0), 5)),
            pl.BlockSpec((None, POOL_HALO, w), lambda b, s: (b, jnp.minimum((cur(s) + 1) * hb, nb * hb - 1), 5)),
            pl.BlockSpec((None, tb, d), lambda b, s: (b, cur(s), 0)),
            pl.BlockSpec((None, 3, d), lambda b, s: (b, 0, 0)),
            const(hgn), const(pool_w), const(pool_scale), const(out_w),
            pl.BlockSpec((None, 3, d), lambda b, s: (b, 0, 0)), const(g_n), const(w_n),
        ],
        out_specs=[pl.BlockSpec((None, tb, d), lambda b, s: (b, cur(s), 0)),
                   pl.BlockSpec((None, tb, n_next), lambda b, s: (b, jnp.maximum(s - 1, 0), 0))],
        out_shape=[jax.ShapeDtypeStruct((bsz, r, d), F32), jax.ShapeDtypeStruct((bsz, r, n_next), F32)],
        scratch_shapes=[pltpu.VMEM((tb + 2 * POOL_HALO, w), F32), pltpu.VMEM((tb, 2 * w), BF16),
                        pltpu.VMEM((tb, d), BF16)],
        compiler_params=_params("parallel", "arbitrary"),
        name="even_post",
    )(o_f, o_b, p, p, p, p, p, h, mod, hgn, pool_w, pool_scale, out_w, mod_n, g_n, w_n)


def _mla_kv_kernel(*refs, rope):
    if rope:
        ckv_ref, kr_ref, g_ref, wuk_ref, wuvt_ref, cos_ref, sin_ref, kcat_ref, vt_ref = refs
    else:
        ckv_ref, kr_ref, g_ref, wuk_ref, wuvt_ref, kcat_ref, vt_ref = refs
    cn = _rms(ckv_ref[...], g_ref[...]).astype(BF16)
    kn = _dot(cn, wuk_ref[...])
    kr = kr_ref[...]
    if rope:
        lane = lax.broadcasted_iota(jnp.int32, kr.shape, 1)
        swapped = jnp.where((lane % (2 * ROPE_FREQ)) < ROPE_FREQ,
                            pltpu.roll(kr, LANES - ROPE_FREQ, 1), pltpu.roll(kr, ROPE_FREQ, 1))
        kr = kr * cos_ref[...] + swapped * sin_ref[...]
    kr = kr.astype(BF16)
    ones_rows = (lax.broadcasted_iota(jnp.int32, (VT_ROWS - MLA_V, kr.shape[0]), 0) == 0).astype(BF16)
    vt = _dot_nt(wuvt_ref[...], cn)
    for h in range(MLA_HEADS):
        kcat_ref[h, :, 0:MLA_NOPE] = kn[:, h * MLA_NOPE:(h + 1) * MLA_NOPE].astype(BF16)
        kcat_ref[h, :, MLA_NOPE:] = kr
        vt_ref[h, 0:MLA_V, :] = vt[h * MLA_V:(h + 1) * MLA_V].astype(BF16)
        vt_ref[h, MLA_V:, :] = ones_rows


def _mla_kv(p, ckv_blk, kr_blk, g, wuk, wuvt, tables, tb):
    bsz, r, _ = p.shape
    rank = g.shape[-1]
    nb = r // tb
    const = lambda arr: pl.BlockSpec(arr.shape, lambda b, j: (0,) * arr.ndim)
    in_specs = [
        pl.BlockSpec((None, tb, rank), lambda b, j: (b, j, ckv_blk)),
        pl.BlockSpec((None, tb, LANES), lambda b, j: (b, j, kr_blk)),
        const(g), const(wuk), const(wuvt),
    ]
    args = [p, p, g, wuk, wuvt]
    if tables is not None:
        in_specs += [pl.BlockSpec((tb, LANES), lambda b, j: (j, 0))] * 2
        args += list(tables)
    return pl.pallas_call(
        functools.partial(_mla_kv_kernel, rope=tables is not None),
        grid=(bsz, nb),
        in_specs=in_specs,
        out_specs=[
            pl.BlockSpec((None, MLA_HEADS, None, tb, QK_PAD), lambda b, j: (b, 0, j, 0, 0)),
            pl.BlockSpec((None, MLA_HEADS, None, VT_ROWS, tb), lambda b, j: (b, 0, j, 0, 0)),
        ],
        out_shape=[
            jax.ShapeDtypeStruct((bsz, MLA_HEADS, nb, tb, QK_PAD), BF16),
            jax.ShapeDtypeStruct((bsz, MLA_HEADS, nb, VT_ROWS, tb), BF16),
        ],
        compiler_params=_params("parallel", "parallel"),
        name="mla_kv_rope" if tables is not None else "mla_kv",
    )(*args)


def _mla_q_kernel(cq_ref, g_ref, wqt_ref, cos_ref, sin_ref, qt_ref):
    cn = _rms(cq_ref[...], g_ref[...]).astype(BF16)
    f = ROPE_FREQ
    qt_all = _dot_nt(wqt_ref[...], cn) * (MLA_SCALE * LOG2_E)
    for h in range(MLA_HEADS):
        qt = qt_all[h * MLA_QK:(h + 1) * MLA_QK]
        qt_ref[h, 0:MLA_NOPE, :] = qt[0:MLA_NOPE].astype(BF16)
        for ax in range(2):
            r0 = MLA_NOPE + ax * 2 * f
            x1 = qt[r0:r0 + f]
            x2 = qt[r0 + f:r0 + 2 * f]
            co = cos_ref[ax]
            si = sin_ref[ax]
            qt_ref[h, r0:r0 + f, :] = (x1 * co - x2 * si).astype(BF16)
            qt_ref[h, r0 + f:r0 + 2 * f, :] = (x2 * co + x1 * si).astype(BF16)
        qt_ref[h, MLA_QK:, :] = jnp.zeros((QK_PAD - MLA_QK, cn.shape[0]), BF16)


def _mla_q(p, cq_blk, g, wqt, cos_t, sin_t):
    bsz, t, _ = p.shape
    rank = g.shape[-1]
    tm = min(2 * TOK_TILE, t)
    const = lambda arr: pl.BlockSpec(arr.shape, lambda b, j: (0,) * arr.ndim)
    tab = pl.BlockSpec((2, ROPE_FREQ, tm), lambda b, j: (0, 0, j))
    return pl.pallas_call(
        _mla_q_kernel,
        grid=(bsz, t // tm),
        in_specs=[pl.BlockSpec((None, tm, rank), lambda b, j: (b, j, cq_blk)), const(g), const(wqt), tab, tab],
        out_specs=pl.BlockSpec((None, MLA_HEADS, QK_PAD, tm), lambda b, j: (b, 0, 0, j)),
        out_shape=jax.ShapeDtypeStruct((bsz, MLA_HEADS, QK_PAD, t), BF16),
        compiler_params=_params("parallel", "parallel"),
        name="mla_q",
    )(p, g, wqt, cos_t, sin_t)


def _attn_kernel(qt_ref, kc_ref, vtc_ref, kl_ref, vtl_ref, g_ref, o_ref, m_ref, acc_ref, s_ref, mx_ref, *, tq, n_lat):
    nsub = tq // Q_SUB
    m_ref[...] = jnp.full(m_ref.shape, -jnp.inf, F32)
    acc_ref[...] = jnp.zeros(acc_ref.shape, F32)

    def scores(k, nxt, g):
        s = _dot(k, qt_ref[:, g * Q_SUB:(g + 1) * Q_SUB])
        s_ref[nxt, g, 0:k.shape[0], :] = s
        mx_ref[nxt, g] = jnp.max(s, axis=0, keepdims=True)

    def substep(k_next, vt_cur, cur, nxt):
        rows = vt_cur.shape[1]
        for g in range(nsub):
            sl = slice(g * Q_SUB, (g + 1) * Q_SUB)
            scores(k_next, nxt, g)
            m_old = m_ref[:, sl]
            m_new = jnp.maximum(m_old, mx_ref[cur, g])
            alpha = jnp.exp2(m_old - m_new)
            p = jnp.exp2(s_ref[cur, g, 0:rows, :] - m_new)
            acc_ref[:, sl] = alpha * acc_ref[:, sl] + _dot(vt_cur, p.astype(BF16))
            m_ref[:, sl] = m_new

    kc = kc_ref[...]
    for g in range(nsub):
        scores(kc, 0, g)
    substep(kl_ref[0], vtc_ref[...], 0, 1)

    def body(j, carry):
        for u in range(2):
            a = 2 * j + u
            substep(kl_ref[jnp.minimum(a + 1, n_lat - 1)], vtl_ref[a], (1 + u) % 2, u % 2)
        return carry

    lax.fori_loop(0, n_lat // 2, body, 0)
    o = (acc_ref[0:MLA_V, :] * (1.0 / acc_ref[MLA_V:MLA_V + 1, :])).T
    o_ref[...] = (o * _silu(g_ref[...])).astype(o_ref.dtype)


def _attn(qt, kc, vtc, kl, vtl, p):
    bsz, nh, _, t = qt.shape
    lc = kc.shape[3]
    tq = min(Q_TILE, t)
    n_lat = kl.shape[2]
    kv = kl.shape[3]
    assert kc.shape[2] == 1 and lc <= kv and n_lat % 2 == 0
    kern = functools.partial(_attn_kernel, tq=tq, n_lat=n_lat)
    ctx5 = lambda arr: pl.BlockSpec((None, None, None) + arr.shape[3:], lambda b, h, i: (b, h, 0, 0, 0))
    full5 = lambda arr: pl.BlockSpec((None, None) + arr.shape[2:], lambda b, h, i: (b, h, 0, 0, 0))
    return pl.pallas_call(
        kern,
        grid=(bsz, nh, t // tq),
        in_specs=[
            pl.BlockSpec((None, None, QK_PAD, tq), lambda b, h, i: (b, h, 0, i)),
            ctx5(kc), ctx5(vtc), full5(kl), full5(vtl),
            pl.BlockSpec((None, tq, MLA_V), lambda b, h, i: (b, i, h)),
        ],
        out_specs=pl.BlockSpec((None, tq, MLA_V), lambda b, h, i: (b, i, h)),
        out_shape=jax.ShapeDtypeStruct((bsz, t, nh * MLA_V), BF16),
        scratch_shapes=[pltpu.VMEM((1, tq), F32), pltpu.VMEM((VT_ROWS, tq), F32),
                        pltpu.VMEM((2, tq // Q_SUB, kv, Q_SUB), F32), pltpu.VMEM((2, tq // Q_SUB, 1, Q_SUB), F32)],
        compiler_params=_params("parallel", "parallel", "arbitrary"),
        name="mla_attention",
    )(qt, kc, vtc, kl, vtl, p)


def _out_final_kernel(y_ref, h_ref, mod_ref, ow_ref, g_ref, o_ref):
    hn = h_ref[...] + mod_ref[2:3, :] * _dot(y_ref[...], ow_ref[...])
    o_ref[...] = _rms(hn, g_ref[...])


def _out_final(y, h, mod, out_w, g):
    bsz, t, d = h.shape
    wi = y.shape[-1]
    tm = min(2 * TOK_TILE, t)
    return pl.pallas_call(
        _out_final_kernel,
        grid=(bsz, t // tm),
        in_specs=[
            pl.BlockSpec((None, tm, wi), lambda b, j: (b, j, 0)),
            pl.BlockSpec((None, tm, d), lambda b, j: (b, j, 0)),
            pl.BlockSpec((None, 3, d), lambda b, j: (b, 0, 0)),
            pl.BlockSpec((wi, d), lambda b, j: (0, 0)),
            pl.BlockSpec((1, d), lambda b, j: (0, 0)),
        ],
        out_specs=pl.BlockSpec((None, tm, d), lambda b, j: (b, j, 0)),
        out_shape=jax.ShapeDtypeStruct((bsz, t, d), F32),
        compiler_params=_params("parallel", "parallel"),
        name="out_final",
    )(y, h, mod, out_w, g)


def _rope_tables(n_tokens):
    rows = n_tokens // GRID_W
    pos_r = jnp.repeat(jnp.arange(rows), GRID_W).astype(F32)
    pos_c = jnp.tile(jnp.arange(GRID_W), rows).astype(F32)
    inv = ROPE_BASE ** (-2.0 * jnp.arange(ROPE_FREQ, dtype=F32) / (MLA_ROPE // 2))
    ang = jnp.stack([pos_r[:, None] * inv, pos_c[:, None] * inv], axis=1)
    cos, sin = jnp.cos(ang), jnp.sin(ang)
    pad = LANES - MLA_ROPE
    cos_k = jnp.pad(jnp.stack([cos, cos], axis=2).reshape(n_tokens, MLA_ROPE), ((0, 0), (0, pad)))
    sin_k = jnp.pad(jnp.stack([-sin, sin], axis=2).reshape(n_tokens, MLA_ROPE), ((0, 0), (0, pad)))
    cos_q = jnp.transpose(cos, (1, 2, 0))
    sin_q = jnp.transpose(sin, (1, 2, 0))
    return (cos_k, sin_k), (cos_q, sin_q)


def kernel(x, c, ctx, c_ctx, ada_w, ada_b, norm_g, out_w, ev_in_w, hg_lb, hg_norm_g, pool_w, pool_scale,
           od_in_w, qa_norm_g, qb_w, kva_norm_g, kvb_w, final_norm_g):
    bsz, t, d = x.shape
    lc = ctx.shape[1]
    depth = ada_w.shape[0]
    assert depth == 2 and t % (2 * TOK_TILE) == 0 and lc % TOK_TILE == 0 and t % GRID_W == 0
    w = hg_norm_g.shape[-1]
    nh = w // HG_DK
    q_rank = qa_norm_g.shape[-1]
    kv_rank = kva_norm_g.shape[-1]
    d_inner = out_w.shape[1]

    n_cond = -(-(bsz + 1) // SUBLANES) * SUBLANES
    cond = jnp.zeros((n_cond, d), F32).at[:bsz].set(c).at[bsz].set(c_ctx)
    mods = _ada(cond, ada_w, ada_b).reshape(depth, n_cond, 3, d)
    mod_l = [mods[l, :bsz] for l in range(depth)]
    mod_c = [mods[l, bsz:bsz + 1] for l in range(depth)]

    lb = hg_lb.reshape(2, depth + 1, 1, w)
    w_in0 = ev_in_w[0].astype(BF16)
    g0 = norm_g[0].reshape(1, d)
    ctx_flat = ctx.reshape(1, bsz * lc, d)
    n_in0 = w_in0.shape[1]
    p_c = _modnorm_mm(ctx_flat, mod_c[0], g0, w_in0, 4 * TOK_TILE, n_in0 // 4, "in_proj0_ctx").reshape(bsz, lc, n_in0)
    p_l = _modnorm_mm(x, mod_l[0], g0, w_in0, 4 * TOK_TILE, n_in0 // 4, "in_proj0")
    consts = _hgrn_constants(HG_CHUNK)
    s0 = jnp.zeros((bsz, 2, nh, HG_DK, HG_DK), F32)
    of_c, ob_c, s_c = _hgrn(p_c, lb, 0, s0, consts)
    of_l, ob_l, _ = _hgrn(p_l, lb, 0, s_c, consts)
    hgn = hg_norm_g[0].reshape(1, w)
    pw = pool_w[0].astype(BF16)
    ps = pool_scale[0].reshape(1, w)
    ow0 = out_w[0].astype(BF16)
    o1 = q_rank
    o2 = o1 + kv_rank
    o3 = o2 + MLA_ROPE
    w1 = od_in_w[0]
    kr_pad = jnp.zeros((d, LANES - MLA_ROPE), F32)
    w_in1 = jnp.concatenate([w1[:, o3:], w1[:, :o1], w1[:, o1:o2], w1[:, o2:o3], kr_pad], axis=1).astype(BF16)
    w_in1c = w_in1[:, d_inner + q_rank:]
    g1 = norm_g[1].reshape(1, d)
    bcast = lambda m: jnp.broadcast_to(m, (bsz, 3, d))
    _, p1_c = _even_post(of_c, ob_c, p_c, ctx, bcast(mod_c[0]), hgn, pw, ps, ow0, bcast(mod_c[1]), g1, w_in1c)
    hl1, p1_l = _even_post(of_l, ob_l, p_l, x, mod_l[0], hgn, pw, ps, ow0, mod_l[1], g1, w_in1)

    kvw = kvb_w[0].reshape(kv_rank, MLA_HEADS, MLA_NOPE + MLA_V)
    wuk = kvw[..., :MLA_NOPE].reshape(kv_rank, MLA_HEADS * MLA_NOPE).astype(BF16)
    wuvt = jnp.transpose(kvw[..., MLA_NOPE:], (1, 2, 0)).reshape(MLA_HEADS * MLA_V, kv_rank).astype(BF16)
    wqt = jnp.transpose(qb_w[0]).astype(BF16)
    kvg = kva_norm_g[0].reshape(1, kv_rank)
    qag = qa_norm_g[0].reshape(1, q_rank)
    tab_k, tab_q = _rope_tables(t)
    kc, vtc = _mla_kv(p1_c, 0, kv_rank // LANES, kvg, wuk, wuvt, None, lc)
    kl, vtl = _mla_kv(p1_l, (d_inner + q_rank) // kv_rank, (d_inner + q_rank + kv_rank) // LANES, kvg, wuk, wuvt,
                      tab_k, min(KV_CHUNK, t))
    qt = _mla_q(p1_l, d_inner // q_rank, qag, wqt, *tab_q)
    y = _attn(qt, kc, vtc, kl, vtl, p1_l)
    return _out_final(y, hl1, mod_l[1], out_w[1].astype(BF16), final_norm_g.reshape(1, d))
```

```python
import functools

import numpy as np
import jax
import jax.numpy as jnp
from jax import lax
from jax.experimental import pallas as pl
from jax.experimental.pallas import tpu as pltpu

F32 = jnp.float32
BF16 = jnp.bfloat16

EPS = 1e-6
GRID_W = 64
HG_DK = 128
POOL_WINDOWS = (2, 4, 8, 16)
MLA_HEADS = 16
MLA_NOPE = 128
MLA_ROPE = 64
MLA_V = 128
MLA_QK = MLA_NOPE + MLA_ROPE
QK_PAD = 256
VT_ROWS = MLA_V + 16
MLA_SCALE = MLA_QK ** -0.5
LOG2_E = 1.4426950408889634
ROPE_FREQ = MLA_ROPE // 4
ROPE_BASE = 10000.0

LANES = 128
SUBLANES = 8
VMEM_LIMIT = 48 * 1024 * 1024

HG_CHUNK = 64
TOK_TILE = 256
KV_CHUNK = 512
Q_TILE = 4096
Q_SUB = 256
POOL_HALO = 8


def _dot(a, b):
    return jnp.dot(a, b, preferred_element_type=F32)


def _dot_nt(a, b):
    return lax.dot_general(a, b, (((1,), (1,)), ((), ())), preferred_element_type=F32)


def _dot_tn(a, b):
    return lax.dot_general(a, b, (((0,), (0,)), ((), ())), preferred_element_type=F32)


def _silu(x):
    h = 0.5 * x
    return h + h * jnp.tanh(h)


def _split_bf16(x):
    hi = x.astype(BF16)
    lo = (x - hi.astype(F32)).astype(BF16)
    return hi, lo


def _params(*sem):
    return pltpu.CompilerParams(dimension_semantics=sem, vmem_limit_bytes=VMEM_LIMIT)


def _ada_kernel(c_ref, w_ref, b_ref, o_ref):
    c = c_ref[...]
    s_hi, s_lo = _split_bf16(_silu(c))
    w_hi, w_lo = _split_bf16(w_ref[...])
    o_ref[...] = _dot(s_hi, w_hi) + _dot(s_lo, w_hi) + _dot(s_hi, w_lo) + b_ref[...]


def _ada(cond, ada_w, ada_b):
    depth, d, _ = ada_w.shape
    r = cond.shape[0]
    return pl.pallas_call(
        _ada_kernel,
        grid=(depth, 3),
        in_specs=[
            pl.BlockSpec((r, d), lambda l, j: (0, 0)),
            pl.BlockSpec((None, d, d), lambda l, j: (l, 0, j)),
            pl.BlockSpec((None, 1, d), lambda l, j: (l, 0, j)),
        ],
        out_specs=pl.BlockSpec((None, r, d), lambda l, j: (l, 0, j)),
        out_shape=jax.ShapeDtypeStruct((depth, r, 3 * d), F32),
        compiler_params=_params("parallel", "parallel"),
        name="ada_modulation",
    )(cond, ada_w, ada_b.reshape(depth, 1, 3 * d))


def _rms(x, g):
    return x * lax.rsqrt(jnp.mean(x * x, axis=-1, keepdims=True) + EPS) * g


def _modnorm_mm_kernel(x_ref, mod_ref, g_ref, w_ref, o_ref, z_ref):
    @pl.when(pl.program_id(2) == 0)
    def _():
        y = _rms(x_ref[...], g_ref[...])
        z_ref[...] = (y * (1.0 + mod_ref[1:2, :]) + mod_ref[0:1, :]).astype(BF16)

    o_ref[...] = _dot(z_ref[...], w_ref[...]).astype(o_ref.dtype)


def _modnorm_mm(x, mod, g, w, tm, tn, name):
    bx, r, d = x.shape
    n = w.shape[1]
    tm = min(tm, r)
    return pl.pallas_call(
        _modnorm_mm_kernel,
        grid=(bx, r // tm, n // tn),
        in_specs=[
            pl.BlockSpec((None, tm, d), lambda b, i, j: (b, i, 0)),
            pl.BlockSpec((None, 3, d), lambda b, i, j: (b, 0, 0)),
            pl.BlockSpec((1, d), lambda b, i, j: (0, 0)),
            pl.BlockSpec((d, tn), lambda b, i, j: (0, j)),
        ],
        out_specs=pl.BlockSpec((None, tm, tn), lambda b, i, j: (b, i, j)),
        out_shape=jax.ShapeDtypeStruct((bx, r, n), F32),
        scratch_shapes=[pltpu.VMEM((tm, d), BF16)],
        compiler_params=_params("parallel", "parallel", "arbitrary"),
        name=name,
    )(x, mod, g, w)


def _hgrn_levels(c):
    w = c // 2
    out = []
    while w >= 1:
        out.append(w)
        w //= 2
    return tuple(out)


def _hgrn_constants(c):
    t = np.arange(c)
    tri = np.tril(np.ones((c, c), np.float32))
    masks = []
    for w in _hgrn_levels(c):
        blk = t // (2 * w)
        first = (t % (2 * w)) < w
        masks.append(((blk[:, None] == blk[None, :]) & (~first[:, None]) & first[None, :]).astype(np.float32))
    masks = np.stack(masks)
    tri2 = np.stack([tri, tri[::-1, ::-1]])
    m2 = np.stack([masks, masks[:, ::-1, ::-1]])
    return jnp.asarray(tri2, BF16), jnp.asarray(m2, F32)


def _hgrn_level_operand(b, g, k, q, w, d):
    c, width = b.shape
    row = lax.broadcasted_iota(jnp.int32, (c, 1), 0)
    keys_first = d == 0
    if w >= SUBLANES:
        ref_off = w - 1 if d == 0 else w
        pieces = []
        for r0 in range(0, c, 2 * w):
            bref = jnp.broadcast_to(b[r0 + ref_off:r0 + ref_off + 1, :], (w, width))
            for half in range(2):
                sl = slice(r0 + half * w, r0 + (half + 1) * w)
                if (half == 0) == keys_first:
                    pieces.append(k[sl] * jnp.exp2(bref - b[sl]))
                else:
                    pieces.append(q[sl] * jnp.exp2(b[sl] - bref))
        return jnp.concatenate(pieces, axis=0)
    before = (row % (2 * w)) < w
    kq = jnp.where(before == keys_first, k, q)
    if w == 1:
        moving = (row % 2 == 1) if d == 0 else (row % 2 == 0)
        return kq * jnp.exp2(jnp.where(moving, g, 0.0))
    ref_off = w - 1 if d == 0 else w
    sub = lax.broadcasted_iota(jnp.int32, (SUBLANES, 1), 0)
    pieces = []
    for r0 in range(0, c, SUBLANES):
        lo = jnp.broadcast_to(b[r0 + ref_off:r0 + ref_off + 1, :], (SUBLANES, width))
        if 2 * w == SUBLANES:
            pieces.append(lo)
        else:
            hi = jnp.broadcast_to(b[r0 + 2 * w + ref_off:r0 + 2 * w + ref_off + 1, :], (SUBLANES, width))
            pieces.append(jnp.where(sub < 2 * w, lo, hi))
    bref = jnp.concatenate(pieces, axis=0)
    return kq * jnp.exp2(-jnp.abs(b - bref))


def _hgrn_wide(q_ref, f_ref, v_ref, lb, tri, d, r0, *, c):
    last = c - 1 if d == 0 else 0
    rows = pl.ds(r0, c)
    half = 0.5 * (1.0 - lb)
    f = (lb + half) + half * jnp.tanh(0.5 * f_ref[rows, :])
    g = jnp.log2(f)
    g_hi, g_lo = _split_bf16(g)
    b = _dot(tri, g_hi) + _dot(tri, g_lo)
    q = _silu(q_ref[rows, :])
    k = 1.0 - f
    v = v_ref[rows, :]
    bl = b[last:last + 1, :]
    return dict(d=d, rows=rows, g=g, b=b, q=q, k=k, v=v, vb=v.astype(BF16), qk=q * k,
                qe=(q * jnp.exp2(b)).astype(BF16), kend=(k * jnp.exp2(bl - b)).astype(BF16), ebl=jnp.exp2(bl))


def _head(x, h):
    return x[:, h * HG_DK:(h + 1) * HG_DK]


def _hgrn_pairs(s, h, mk_ref, *, c):
    d = s["d"]
    att = None
    for l, w in enumerate(_hgrn_levels(c)):
        x = _hgrn_level_operand(_head(s["b"], h), _head(s["g"], h), _head(s["k"], h), _head(s["q"], h), w, d)
        x = x.astype(BF16)
        t = mk_ref[d, l] * _dot_nt(x, x)
        att = t if att is None else att + t
    return att


def _hgrn_finish(s, h, att, o_ref, st_ref):
    d = s["d"]
    vb = _head(s["vb"], h)
    inter = _dot_nt(_head(s["qe"], h), st_ref[d, h].astype(BF16))
    diag = jnp.sum(_head(s["qk"], h), axis=-1, keepdims=True)
    o_ref[s["rows"], h * HG_DK:(h + 1) * HG_DK] = inter + _dot(att.astype(BF16), vb) + diag * _head(s["v"], h)
    st_ref[d, h] = _head(s["ebl"], h) * st_ref[d, h] + _dot_tn(vb, _head(s["kend"], h))


def _hgrn_kernel(qf_ref, ff_ref, vf_ref, qb_ref, fb_ref, vb_ref, lb_ref, tri_ref, mk_ref, s0_ref,
                 of_ref, ob_ref, sout_ref, st_ref, lbs_ref, *, tb, c, nh, layer):
    nchunk = tb // c

    @pl.when(pl.program_id(1) == 0)
    def _():
        st_ref[...] = s0_ref[...]
        slots = lb_ref[...]
        e = jnp.exp(slots - jnp.max(slots, axis=1, keepdims=True))
        lbs_ref[...] = jnp.sum(e[:, :layer + 1], axis=1) / jnp.sum(e, axis=1)

    def body(cc, carry):
        rf = pl.multiple_of(cc * c, c)
        rb = pl.multiple_of((nchunk - 1 - cc) * c, c)
        sides = ((_hgrn_wide(qf_ref, ff_ref, vf_ref, lbs_ref[0], tri_ref[0], 0, rf, c=c), of_ref),
                 (_hgrn_wide(qb_ref, fb_ref, vb_ref, lbs_ref[1], tri_ref[1], 1, rb, c=c), ob_ref))
        pending = [None, None]
        for h in range(nh + 1):
            cur = [_hgrn_pairs(s, h, mk_ref, c=c) if h < nh else None for s, _ in sides]
            for (s, o_ref), p in zip(sides, pending):
                if p is not None:
                    _hgrn_finish(s, h - 1, p, o_ref, st_ref)
            pending = cur
        return carry

    lax.fori_loop(0, nchunk, body, 0, unroll=True)

    @pl.when(pl.program_id(1) == pl.num_programs(1) - 1)
    def _():
        sout_ref[...] = st_ref[...]


def _hgrn(p, lb, layer, s0, consts):
    bsz, r, _ = p.shape
    w = lb.shape[-1]
    nh = w // HG_DK
    tb = min(TOK_TILE, r)
    c = HG_CHUNK
    nb = r // tb
    tri2, m2 = consts
    fwd = lambda col: pl.BlockSpec((None, tb, w), lambda b, s: (b, s, col))
    bwd = lambda col: pl.BlockSpec((None, tb, w), lambda b, s: (b, nb - 1 - s, col))
    const = lambda arr: pl.BlockSpec(arr.shape, lambda b, s: (0,) * arr.ndim)
    st_spec = pl.BlockSpec((None, 2, nh, HG_DK, HG_DK), lambda b, s: (b, 0, 0, 0, 0))
    kern = functools.partial(_hgrn_kernel, tb=tb, c=c, nh=nh, layer=layer)
    return pl.pallas_call(
        kern,
        grid=(bsz, nb),
        in_specs=[fwd(0), fwd(1), fwd(3), bwd(0), bwd(2), bwd(3), const(lb), const(tri2), const(m2),
                  st_spec],
        out_specs=[
            pl.BlockSpec((None, tb, w), lambda b, s: (b, s, 0)),
            pl.BlockSpec((None, tb, w), lambda b, s: (b, nb - 1 - s, 0)),
            st_spec,
        ],
        out_shape=[
            jax.ShapeDtypeStruct((bsz, r, w), F32),
            jax.ShapeDtypeStruct((bsz, r, w), F32),
            jax.ShapeDtypeStruct(s0.shape, F32),
        ],
        scratch_shapes=[pltpu.VMEM((2, nh, HG_DK, HG_DK), F32), pltpu.VMEM((2, 1, w), F32)],
        compiler_params=_params("parallel", "arbitrary"),
        name="hgrn2_scan",
    )(p, p, p, p, p, p, lb, tri2, m2, s0)


def _even_post_kernel(of_ref, ob_ref, ga_ref, u_ref, gb_ref, up_ref, un_ref, h_ref, mod_ref, hgn_ref, pw_ref,
                      ps_ref, ow_ref, modn_ref, gn_ref, wn_ref, o_ref, pn_ref, ext_ref, y_ref, z_ref, *, tb, seq, nh):
    step = pl.program_id(1)
    last = pl.num_programs(1) - 2
    j = jnp.minimum(step, last)
    w = nh * HG_DK

    @pl.when(step == 0)
    def _():
        z_ref[...] = jnp.zeros(z_ref.shape, BF16)

    n_parts = nh + len(POOL_WINDOWS)
    n_blk = pn_ref.shape[1] // LANES
    per = -(-n_blk // n_parts)

    def in_proj_part(i):
        c0 = min(i * per, n_blk) * LANES
        c1 = min((i + 1) * per, n_blk) * LANES
        if c0 < c1:
            pn_ref[:, c0:c1] = _dot(z_ref[...], wn_ref[:, c0:c1])

    o = of_ref[...] + ob_ref[...]
    for h in range(nh):
        in_proj_part(h)
        sl = slice(h * HG_DK, (h + 1) * HG_DK)
        y_ref[:, sl] = (_rms(o[:, sl], hgn_ref[:, sl]) * _silu(ga_ref[:, sl])).astype(BF16)
    u = u_ref[...]
    ext_ref[0:POOL_HALO, :] = jnp.where(j > 0, up_ref[...], 0.0)
    ext_ref[POOL_HALO:POOL_HALO + tb, :] = u
    ext_ref[POOL_HALO + tb:, :] = jnp.where(j < last, un_ref[...], 0.0)
    t = j * tb + lax.broadcasted_iota(jnp.int32, (tb, 1), 0)
    grp = w // len(POOL_WINDOWS)
    for gi, win in enumerate(POOL_WINDOWS):
        in_proj_part(nh + gi)
        sl = slice(gi * grp, (gi + 1) * grp)
        acc = ext_ref[POOL_HALO - win // 2:POOL_HALO - win // 2 + tb, sl]
        for off in range(-win // 2 + 1, win // 2):
            acc = acc + ext_ref[POOL_HALO + off:POOL_HALO + off + tb, sl]
        cnt = (jnp.minimum(t + win // 2, seq) - jnp.maximum(t - win // 2, 0)).astype(F32)
        yp = acc * (1.0 / cnt) - u[:, sl]
        yb = _dot(yp.astype(BF16), pw_ref[gi]) * ps_ref[:, sl]
        y_ref[:, w + gi * grp:w + (gi + 1) * grp] = (yb * _silu(gb_ref[:, sl])).astype(BF16)
    hn = h_ref[...] + mod_ref[2:3, :] * _dot(y_ref[...], ow_ref[...])
    o_ref[...] = hn
    z_ref[...] = (_rms(hn, gn_ref[...]) * (1.0 + modn_ref[1:2, :]) + modn_ref[0:1, :]).astype(BF16)


def _even_post(o_f, o_b, p, h, mod, hgn, pool_w, pool_scale, out_w, mod_n, g_n, w_n):
    bsz, r, w = o_f.shape
    d = h.shape[-1]
    tb = min(TOK_TILE, r)
    nb = r // tb
    hb = tb // POOL_HALO
    nh = w // HG_DK
    n_next = w_n.shape[1]
    cur = lambda s: jnp.minimum(s, nb - 1)
    tok = lambda col: pl.BlockSpec((None, tb, w), lambda b, s: (b, cur(s), col))
    const = lambda arr: pl.BlockSpec(arr.shape, lambda b, s: (0,) * arr.ndim)
    kern = functools.partial(_even_post_kernel, tb=tb, seq=r, nh=nh)
    return pl.pallas_call(
        kern,
        grid=(bsz, nb + 1),
        in_specs=[
            tok(0), tok(0), tok(4), tok(5), tok(6),
            pl.BlockSpec((None, POOL_HALO, w), lambda b, s: (b, jnp.maximum(cur(s) * hb - 1, 0), 5)),
            pl.BlockSpec((None, POOL_HALO, w), lambda b, s: (b, jnp.minimum((cur(s) + 1) * hb, nb * hb - 1), 5)),
            pl.BlockSpec((None, tb, d), lambda b, s: (b, cur(s), 0)),
            pl.BlockSpec((None, 3, d), lambda b, s: (b, 0, 0)),
            const(hgn), const(pool_w), const(pool_scale), const(out_w),
            pl.BlockSpec((None, 3, d), lambda b, s: (b, 0, 0)), const(g_n), const(w_n),
        ],
        out_specs=[pl.BlockSpec((None, tb, d), lambda b, s: (b, cur(s), 0)),
                   pl.BlockSpec((None, tb, n_next), lambda b, s: (b, jnp.maximum(s - 1, 0), 0))],
        out_shape=[jax.ShapeDtypeStruct((bsz, r, d), F32), jax.ShapeDtypeStruct((bsz, r, n_next), F32)],
        scratch_shapes=[pltpu.VMEM((tb + 2 * POOL_HALO, w), F32), pltpu.VMEM((tb, 2 * w), BF16),
                        pltpu.VMEM((tb, d), BF16)],
        compiler_params=_params("parallel", "arbitrary"),
        name="even_post",
    )(o_f, o_b, p, p, p, p, p, h, mod, hgn, pool_w, pool_scale, out_w, mod_n, g_n, w_n)


def _mla_kv_kernel(*refs, rope):
    if rope:
        ckv_ref, kr_ref, g_ref, wuk_ref, wuvt_ref, cos_ref, sin_ref, kcat_ref, vt_ref = refs
    else:
        ckv_ref, kr_ref, g_ref, wuk_ref, wuvt_ref, kcat_ref, vt_ref = refs
    cn = _rms(ckv_ref[...], g_ref[...]).astype(BF16)
    kn = _dot(cn, wuk_ref[...])
    kr = kr_ref[...]
    if rope:
        lane = lax.broadcasted_iota(jnp.int32, kr.shape, 1)
        swapped = jnp.where((lane % (2 * ROPE_FREQ)) < ROPE_FREQ,
                            pltpu.roll(kr, LANES - ROPE_FREQ, 1), pltpu.roll(kr, ROPE_FREQ, 1))
        kr = kr * cos_ref[...] + swapped * sin_ref[...]
    kr = kr.astype(BF16)
    ones_rows = (lax.broadcasted_iota(jnp.int32, (VT_ROWS - MLA_V, kr.shape[0]), 0) == 0).astype(BF16)
    vt = _dot_nt(wuvt_ref[...], cn)
    for h in range(MLA_HEADS):
        kcat_ref[h, :, 0:MLA_NOPE] = kn[:, h * MLA_NOPE:(h + 1) * MLA_NOPE].astype(BF16)
        kcat_ref[h, :, MLA_NOPE:] = kr
        vt_ref[h, 0:MLA_V, :] = vt[h * MLA_V:(h + 1) * MLA_V].astype(BF16)
        vt_ref[h, MLA_V:, :] = ones_rows


def _mla_kv(p, ckv_blk, kr_blk, g, wuk, wuvt, tables, tb):
    bsz, r, _ = p.shape
    rank = g.shape[-1]
    nb = r // tb
    const = lambda arr: pl.BlockSpec(arr.shape, lambda b, j: (0,) * arr.ndim)
    in_specs = [
        pl.BlockSpec((None, tb, rank), lambda b, j: (b, j, ckv_blk)),
        pl.BlockSpec((None, tb, LANES), lambda b, j: (b, j, kr_blk)),
        const(g), const(wuk), const(wuvt),
    ]
    args = [p, p, g, wuk, wuvt]
    if tables is not None:
        in_specs += [pl.BlockSpec((tb, LANES), lambda b, j: (j, 0))] * 2
        args += list(tables)
    return pl.pallas_call(
        functools.partial(_mla_kv_kernel, rope=tables is not None),
        grid=(bsz, nb),
        in_specs=in_specs,
        out_specs=[
            pl.BlockSpec((None, MLA_HEADS, None, tb, QK_PAD), lambda b, j: (b, 0, j, 0, 0)),
            pl.BlockSpec((None, MLA_HEADS, None, VT_ROWS, tb), lambda b, j: (b, 0, j, 0, 0)),
        ],
        out_shape=[
            jax.ShapeDtypeStruct((bsz, MLA_HEADS, nb, tb, QK_PAD), BF16),
            jax.ShapeDtypeStruct((bsz, MLA_HEADS, nb, VT_ROWS, tb), BF16),
        ],
        compiler_params=_params("parallel", "parallel"),
        name="mla_kv_rope" if tables is not None else "mla_kv",
    )(*args)


def _mla_q_kernel(cq_ref, g_ref, wqt_ref, cos_ref, sin_ref, qt_ref):
    cn = _rms(cq_ref[...], g_ref[...]).astype(BF16)
    f = ROPE_FREQ
    qt_all = _dot_nt(wqt_ref[...], cn) * (MLA_SCALE * LOG2_E)
    for h in range(MLA_HEADS):
        qt = qt_all[h * MLA_QK:(h + 1) * MLA_QK]
        qt_ref[h, 0:MLA_NOPE, :] = qt[0:MLA_NOPE].astype(BF16)
        for ax in range(2):
            r0 = MLA_NOPE + ax * 2 * f
            x1 = qt[r0:r0 + f]
            x2 = qt[r0 + f:r0 + 2 * f]
            co = cos_ref[ax]
            si = sin_ref[ax]
            qt_ref[h, r0:r0 + f, :] = (x1 * co - x2 * si).astype(BF16)
            qt_ref[h, r0 + f:r0 + 2 * f, :] = (x2 * co + x1 * si).astype(BF16)
        qt_ref[h, MLA_QK:, :] = jnp.zeros((QK_PAD - MLA_QK, cn.shape[0]), BF16)


def _mla_q(p, cq_blk, g, wqt, cos_t, sin_t):
    bsz, t, _ = p.shape
    rank = g.shape[-1]
    tm = min(2 * TOK_TILE, t)
    const = lambda arr: pl.BlockSpec(arr.shape, lambda b, j: (0,) * arr.ndim)
    tab = pl.BlockSpec((2, ROPE_FREQ, tm), lambda b, j: (0, 0, j))
    return pl.pallas_call(
        _mla_q_kernel,
        grid=(bsz, t // tm),
        in_specs=[pl.BlockSpec((None, tm, rank), lambda b, j: (b, j, cq_blk)), const(g), const(wqt), tab, tab],
        out_specs=pl.BlockSpec((None, MLA_HEADS, QK_PAD, tm), lambda b, j: (b, 0, 0, j)),
        out_shape=jax.ShapeDtypeStruct((bsz, MLA_HEADS, QK_PAD, t), BF16),
        compiler_params=_params("parallel", "parallel"),
        name="mla_q",
    )(p, g, wqt, cos_t, sin_t)


def _attn_kernel(qt_ref, kc_ref, vtc_ref, kl_ref, vtl_ref, g_ref, o_ref, m_ref, acc_ref, s_ref, mx_ref, *, tq, n_lat):
    nsub = tq // Q_SUB
    m_ref[...] = jnp.full(m_ref.shape, -jnp.inf, F32)
    acc_ref[...] = jnp.zeros(acc_ref.shape, F32)

    def scores(k, nxt, g):
        s = _dot(k, qt_ref[:, g * Q_SUB:(g + 1) * Q_SUB])
        s_ref[nxt, g, 0:k.shape[0], :] = s
        mx_ref[nxt, g] = jnp.max(s, axis=0, keepdims=True)

    def substep(k_next, vt_cur, cur, nxt):
        rows = vt_cur.shape[1]
        for g in range(nsub):
            sl = slice(g * Q_SUB, (g + 1) * Q_SUB)
            scores(k_next, nxt, g)
            m_old = m_ref[:, sl]
            m_new = jnp.maximum(m_old, mx_ref[cur, g])
            alpha = jnp.exp2(m_old - m_new)
            p = jnp.exp2(s_ref[cur, g, 0:rows, :] - m_new)
            acc_ref[:, sl] = alpha * acc_ref[:, sl] + _dot(vt_cur, p.astype(BF16))
            m_ref[:, sl] = m_new

    kc = kc_ref[...]
    for g in range(nsub):
        scores(kc, 0, g)
    substep(kl_ref[0], vtc_ref[...], 0, 1)

    def body(j, carry):
        for u in range(2):
            a = 2 * j + u
            substep(kl_ref[jnp.minimum(a + 1, n_lat - 1)], vtl_ref[a], (1 + u) % 2, u % 2)
        return carry

    lax.fori_loop(0, n_lat // 2, body, 0)
    o = (acc_ref[0:MLA_V, :] * (1.0 / acc_ref[MLA_V:MLA_V + 1, :])).T
    o_ref[...] = (o * _silu(g_ref[...])).astype(o_ref.dtype)


def _attn(qt, kc, vtc, kl, vtl, p):
    bsz, nh, _, t = qt.shape
    lc = kc.shape[3]
    tq = min(Q_TILE, t)
    n_lat = kl.shape[2]
    kv = kl.shape[3]
    assert kc.shape[2] == 1 and lc <= kv and n_lat % 2 == 0
    kern = functools.partial(_attn_kernel, tq=tq, n_lat=n_lat)
    ctx5 = lambda arr: pl.BlockSpec((None, None, None) + arr.shape[3:], lambda b, h, i: (b, h, 0, 0, 0))
    full5 = lambda arr: pl.BlockSpec((None, None) + arr.shape[2:], lambda b, h, i: (b, h, 0, 0, 0))
    return pl.pallas_call(
        kern,
        grid=(bsz, nh, t // tq),
        in_specs=[
            pl.BlockSpec((None, None, QK_PAD, tq), lambda b, h, i: (b, h, 0, i)),
            ctx5(kc), ctx5(vtc), full5(kl), full5(vtl),
            pl.BlockSpec((None, tq, MLA_V), lambda b, h, i: (b, i, h)),
        ],
        out_specs=pl.BlockSpec((None, tq, MLA_V), lambda b, h, i: (b, i, h)),
        out_shape=jax.ShapeDtypeStruct((bsz, t, nh * MLA_V), BF16),
        scratch_shapes=[pltpu.VMEM((1, tq), F32), pltpu.VMEM((VT_ROWS, tq), F32),
                        pltpu.VMEM((2, tq // Q_SUB, kv, Q_SUB), F32), pltpu.VMEM((2, tq // Q_SUB, 1, Q_SUB), F32)],
        compiler_params=_params("parallel", "parallel", "arbitrary"),
        name="mla_attention",
    )(qt, kc, vtc, kl, vtl, p)


def _out_final_kernel(y_ref, h_ref, mod_ref, ow_ref, g_ref, o_ref):
    hn = h_ref[...] + mod_ref[2:3, :] * _dot(y_ref[...], ow_ref[...])
    o_ref[...] = _rms(hn, g_ref[...])


def _out_final(y, h, mod, out_w, g):
    bsz, t, d = h.shape
    wi = y.shape[-1]
    tm = min(2 * TOK_TILE, t)
    return pl.pallas_call(
        _out_final_kernel,
        grid=(bsz, t // tm),
        in_specs=[
            pl.BlockSpec((None, tm, wi), lambda b, j: (b, j, 0)),
            pl.BlockSpec((None, tm, d), lambda b, j: (b, j, 0)),
            pl.BlockSpec((None, 3, d), lambda b, j: (b, 0, 0)),
            pl.BlockSpec((wi, d), lambda b, j: (0, 0)),
            pl.BlockSpec((1, d), lambda b, j: (0, 0)),
        ],
        out_specs=pl.BlockSpec((None, tm, d), lambda b, j: (b, j, 0)),
        out_shape=jax.ShapeDtypeStruct((bsz, t, d), F32),
        compiler_params=_params("parallel", "parallel"),
        name="out_final",
    )(y, h, mod, out_w, g)


def _rope_tables(n_tokens):
    rows = n_tokens // GRID_W
    pos_r = jnp.repeat(jnp.arange(rows), GRID_W).astype(F32)
    pos_c = jnp.tile(jnp.arange(GRID_W), rows).astype(F32)
    inv = ROPE_BASE ** (-2.0 * jnp.arange(ROPE_FREQ, dtype=F32) / (MLA_ROPE // 2))
    ang = jnp.stack([pos_r[:, None] * inv, pos_c[:, None] * inv], axis=1)
    cos, sin = jnp.cos(ang), jnp.sin(ang)
    pad = LANES - MLA_ROPE
    cos_k = jnp.pad(jnp.stack([cos, cos], axis=2).reshape(n_tokens, MLA_ROPE), ((0, 0), (0, pad)))
    sin_k = jnp.pad(jnp.stack([-sin, sin], axis=2).reshape(n_tokens, MLA_ROPE), ((0, 0), (0, pad)))
    cos_q = jnp.transpose(cos, (1, 2, 0))
    sin_q = jnp.transpose(sin, (1, 2, 0))
    return (cos_k, sin_k), (cos_q, sin_q)


def kernel(x, c, ctx, c_ctx, ada_w, ada_b, norm_g, out_w, ev_in_w, hg_lb, hg_norm_g, pool_w, pool_scale,
           od_in_w, qa_norm_g, qb_w, kva_norm_g, kvb_w, final_norm_g):
    bsz, t, d = x.shape
    lc = ctx.shape[1]
    depth = ada_w.shape[0]
    assert depth == 2 and t % (2 * TOK_TILE) == 0 and lc % TOK_TILE == 0 and t % GRID_W == 0
    w = hg_norm_g.shape[-1]
    nh = w // HG_DK
    q_rank = qa_norm_g.shape[-1]
    kv_rank = kva_norm_g.shape[-1]
    d_inner = out_w.shape[1]

    n_cond = -(-(bsz + 1) // SUBLANES) * SUBLANES
    cond = jnp.zeros((n_cond, d), F32).at[:bsz].set(c).at[bsz].set(c_ctx)
    mods = _ada(cond, ada_w, ada_b).reshape(depth, n_cond, 3, d)
    mod_l = [mods[l, :bsz] for l in range(depth)]
    mod_c = [mods[l, bsz:bsz + 1] for l in range(depth)]

    lb = hg_lb.reshape(2, depth + 1, 1, w)
    w_in0 = ev_in_w[0].astype(BF16)
    g0 = norm_g[0].reshape(1, d)
    ctx_flat = ctx.reshape(1, bsz * lc, d)
    n_in0 = w_in0.shape[1]
    p_c = _modnorm_mm(ctx_flat, mod_c[0], g0, w_in0, 4 * TOK_TILE, n_in0 // 4, "in_proj0_ctx").reshape(bsz, lc, n_in0)
    p_l = _modnorm_mm(x, mod_l[0], g0, w_in0, 4 * TOK_TILE, n_in0 // 4, "in_proj0")
    consts = _hgrn_constants(HG_CHUNK)
    s0 = jnp.zeros((bsz, 2, nh, HG_DK, HG_DK), F32)
    of_c, ob_c, s_c = _hgrn(p_c, lb, 0, s0, consts)
    of_l, ob_l, _ = _hgrn(p_l, lb, 0, s_c, consts)
    hgn = hg_norm_g[0].reshape(1, w)
    pw = pool_w[0].astype(BF16)
    ps = pool_scale[0].reshape(1, w)
    ow0 = out_w[0].astype(BF16)
    o1 = q_rank
    o2 = o1 + kv_rank
    o3 = o2 + MLA_ROPE
    w1 = od_in_w[0]
    kr_pad = jnp.zeros((d, LANES - MLA_ROPE), F32)
    w_in1 = jnp.concatenate([w1[:, o3:], w1[:, :o1], w1[:, o1:o2], w1[:, o2:o3], kr_pad], axis=1).astype(BF16)
    w_in1c = w_in1[:, d_inner + q_rank:]
    g1 = norm_g[1].reshape(1, d)
    bcast = lambda m: jnp.broadcast_to(m, (bsz, 3, d))
    _, p1_c = _even_post(of_c, ob_c, p_c, ctx, bcast(mod_c[0]), hgn, pw, ps, ow0, bcast(mod_c[1]), g1, w_in1c)
    hl1, p1_l = _even_post(of_l, ob_l, p_l, x, mod_l[0], hgn, pw, ps, ow0, mod_l[1], g1, w_in1)

    kvw = kvb_w[0].reshape(kv_rank, MLA_HEADS, MLA_NOPE + MLA_V)
    wuk = kvw[..., :MLA_NOPE].reshape(kv_rank, MLA_HEADS * MLA_NOPE).astype(BF16)
    wuvt = jnp.transpose(kvw[..., MLA_NOPE:], (1, 2, 0)).reshape(MLA_HEADS * MLA_V, kv_rank).astype(BF16)
    wqt = jnp.transpose(qb_w[0]).astype(BF16)
    kvg = kva_norm_g[0].reshape(1, kv_rank)
    qag = qa_norm_g[0].reshape(1, q_rank)
    tab_k, tab_q = _rope_tables(t)
    kc, vtc = _mla_kv(p1_c, 0, kv_rank // LANES, kvg, wuk, wuvt, None, lc)
    kl, vtl = _mla_kv(p1_l, (d_inner + q_rank) // kv_rank, (d_inner + q_rank + kv_rank) // LANES, kvg, wuk, wuvt,
                      tab_k, min(KV_CHUNK, t))
    qt = _mla_q(p1_l, d_inner // q_rank, qag, wqt, *tab_q)
    y = _attn(qt, kc, vtc, kl, vtl, p1_l)
    return _out_final(y, hl1, mod_l[1], out_w[1].astype(BF16), final_norm_g.reshape(1, d))
```

```python
import functools

import numpy as np
import jax
import jax.numpy as jnp
from jax import lax
from jax.experimental import pallas as pl
from jax.experimental.pallas import tpu as pltpu

F32 = jnp.float32
BF16 = jnp.bfloat16

EPS = 1e-6
GRID_W = 64
HG_DK = 128
POOL_WINDOWS = (2, 4, 8, 16)
MLA_HEADS = 16
MLA_NOPE = 128
MLA_ROPE = 64
MLA_V = 128
MLA_QK = MLA_NOPE + MLA_ROPE
QK_PAD = 256
VT_ROWS = MLA_V + 16
MLA_SCALE = MLA_QK ** -0.5
LOG2_E = 1.4426950408889634
ROPE_FREQ = MLA_ROPE // 4
ROPE_BASE = 10000.0

LANES = 128
SUBLANES = 8
VMEM_LIMIT = 48 * 1024 * 1024

HG_CHUNK = 64
TOK_TILE = 256
KV_CHUNK = 512
Q_TILE = 4096
Q_SUB = 256
POOL_HALO = 8


def _dot(a, b):
    return jnp.dot(a, b, preferred_element_type=F32)


def _dot_nt(a, b):
    return lax.dot_general(a, b, (((1,), (1,)), ((), ())), preferred_element_type=F32)


def _dot_tn(a, b):
    return lax.dot_general(a, b, (((0,), (0,)), ((), ())), preferred_element_type=F32)


def _silu(x):
    h = 0.5 * x
    return h + h * jnp.tanh(h)


def _split_bf16(x):
    hi = x.astype(BF16)
    lo = (x - hi.astype(F32)).astype(BF16)
    return hi, lo


def _params(*sem):
    return pltpu.CompilerParams(dimension_semantics=sem, vmem_limit_bytes=VMEM_LIMIT)


def _ada_kernel(c_ref, w_ref, b_ref, o_ref):
    c = c_ref[...]
    s_hi, s_lo = _split_bf16(_silu(c))
    w_hi, w_lo = _split_bf16(w_ref[...])
    o_ref[...] = _dot(s_hi, w_hi) + _dot(s_lo, w_hi) + _dot(s_hi, w_lo) + b_ref[...]


def _ada(cond, ada_w, ada_b):
    depth, d, _ = ada_w.shape
    r = cond.shape[0]
    return pl.pallas_call(
        _ada_kernel,
        grid=(depth, 3),
        in_specs=[
            pl.BlockSpec((r, d), lambda l, j: (0, 0)),
            pl.BlockSpec((None, d, d), lambda l, j: (l, 0, j)),
            pl.BlockSpec((None, 1, d), lambda l, j: (l, 0, j)),
        ],
        out_specs=pl.BlockSpec((None, r, d), lambda l, j: (l, 0, j)),
        out_shape=jax.ShapeDtypeStruct((depth, r, 3 * d), F32),
        compiler_params=_params("parallel", "parallel"),
        name="ada_modulation",
    )(cond, ada_w, ada_b.reshape(depth, 1, 3 * d))


def _rms(x, g):
    return x * lax.rsqrt(jnp.mean(x * x, axis=-1, keepdims=True) + EPS) * g


def _modnorm_mm_kernel(x_ref, mod_ref, g_ref, w_ref, o_ref, z_ref):
    @pl.when(pl.program_id(2) == 0)
    def _():
        y = _rms(x_ref[...], g_ref[...])
        z_ref[...] = (y * (1.0 + mod_ref[1:2, :]) + mod_ref[0:1, :]).astype(BF16)

    o_ref[...] = _dot(z_ref[...], w_ref[...]).astype(o_ref.dtype)


def _modnorm_mm(x, mod, g, w, tm, tn, name):
    bx, r, d = x.shape
    n = w.shape[1]
    tm = min(tm, r)
    return pl.pallas_call(
        _modnorm_mm_kernel,
        grid=(bx, r // tm, n // tn),
        in_specs=[
            pl.BlockSpec((None, tm, d), lambda b, i, j: (b, i, 0)),
            pl.BlockSpec((None, 3, d), lambda b, i, j: (b, 0, 0)),
            pl.BlockSpec((1, d), lambda b, i, j: (0, 0)),
            pl.BlockSpec((d, tn), lambda b, i, j: (0, j)),
        ],
        out_specs=pl.BlockSpec((None, tm, tn), lambda b, i, j: (b, i, j)),
        out_shape=jax.ShapeDtypeStruct((bx, r, n), F32),
        scratch_shapes=[pltpu.VMEM((tm, d), BF16)],
        compiler_params=_params("parallel", "parallel", "arbitrary"),
        name=name,
    )(x, mod, g, w)


def _decay_bounds_kernel(a_ref, o_ref, *, layer):
    slots = a_ref[...]
    e = jnp.exp(slots - jnp.max(slots, axis=1, keepdims=True))
    o_ref[...] = jnp.sum(e[:, :layer + 1], axis=1) / jnp.sum(e, axis=1)


def _decay_bounds(hg_lb, layer):
    ndir, nslot, w = hg_lb.shape
    return pl.pallas_call(
        functools.partial(_decay_bounds_kernel, layer=layer),
        out_shape=jax.ShapeDtypeStruct((ndir, 1, w), F32),
        name="decay_bounds",
    )(hg_lb.reshape(ndir, nslot, 1, w))


def _hgrn_levels(c):
    w = c // 2
    out = []
    while w >= 1:
        out.append(w)
        w //= 2
    return tuple(out)


def _hgrn_constants(c):
    t = np.arange(c)
    tri = np.tril(np.ones((c, c), np.float32))
    masks = []
    for w in _hgrn_levels(c):
        blk = t // (2 * w)
        first = (t % (2 * w)) < w
        masks.append(((blk[:, None] == blk[None, :]) & (~first[:, None]) & first[None, :]).astype(np.float32))
    masks = np.stack(masks)
    tri2 = np.stack([tri, tri[::-1, ::-1]])
    m2 = np.stack([masks, masks[:, ::-1, ::-1]])
    return jnp.asarray(tri2, BF16), jnp.asarray(m2, F32)


def _hgrn_level_operand(b, g, k, q, w, d):
    c, width = b.shape
    row = lax.broadcasted_iota(jnp.int32, (c, 1), 0)
    keys_first = d == 0
    if w >= SUBLANES:
        ref_off = w - 1 if d == 0 else w
        pieces = []
        for r0 in range(0, c, 2 * w):
            bref = jnp.broadcast_to(b[r0 + ref_off:r0 + ref_off + 1, :], (w, width))
            for half in range(2):
                sl = slice(r0 + half * w, r0 + (half + 1) * w)
                if (half == 0) == keys_first:
                    pieces.append(k[sl] * jnp.exp2(bref - b[sl]))
                else:
                    pieces.append(q[sl] * jnp.exp2(b[sl] - bref))
        return jnp.concatenate(pieces, axis=0)
    before = (row % (2 * w)) < w
    kq = jnp.where(before == keys_first, k, q)
    if w == 1:
        moving = (row % 2 == 1) if d == 0 else (row % 2 == 0)
        return kq * jnp.exp2(jnp.where(moving, g, 0.0))
    ref_off = w - 1 if d == 0 else w
    sub = lax.broadcasted_iota(jnp.int32, (SUBLANES, 1), 0)
    pieces = []
    for r0 in range(0, c, SUBLANES):
        lo = jnp.broadcast_to(b[r0 + ref_off:r0 + ref_off + 1, :], (SUBLANES, width))
        if 2 * w == SUBLANES:
            pieces.append(lo)
        else:
            hi = jnp.broadcast_to(b[r0 + 2 * w + ref_off:r0 + 2 * w + ref_off + 1, :], (SUBLANES, width))
            pieces.append(jnp.where(sub < 2 * w, lo, hi))
    bref = jnp.concatenate(pieces, axis=0)
    return kq * jnp.exp2(-jnp.abs(b - bref))


def _hgrn_wide(q_ref, f_ref, v_ref, lb, tri, d, r0, *, c):
    last = c - 1 if d == 0 else 0
    rows = pl.ds(r0, c)
    half = 0.5 * (1.0 - lb)
    f = (lb + half) + half * jnp.tanh(0.5 * f_ref[rows, :])
    g = jnp.log2(f)
    g_hi, g_lo = _split_bf16(g)
    b = _dot(tri, g_hi) + _dot(tri, g_lo)
    q = _silu(q_ref[rows, :])
    k = 1.0 - f
    v = v_ref[rows, :]
    bl = b[last:last + 1, :]
    return dict(d=d, rows=rows, g=g, b=b, q=q, k=k, v=v, vb=v.astype(BF16), qk=q * k,
                qe=(q * jnp.exp2(b)).astype(BF16), kend=(k * jnp.exp2(bl - b)).astype(BF16), ebl=jnp.exp2(bl))


def _head(x, h):
    return x[:, h * HG_DK:(h + 1) * HG_DK]


def _hgrn_pairs(s, h, mk_ref, *, c):
    d = s["d"]
    att = None
    for l, w in enumerate(_hgrn_levels(c)):
        x = _hgrn_level_operand(_head(s["b"], h), _head(s["g"], h), _head(s["k"], h), _head(s["q"], h), w, d)
        x = x.astype(BF16)
        t = mk_ref[d, l] * _dot_nt(x, x)
        att = t if att is None else att + t
    return att


def _hgrn_finish(s, h, att, o_ref, st_ref):
    d = s["d"]
    vb = _head(s["vb"], h)
    inter = _dot_nt(_head(s["qe"], h), st_ref[d, h].astype(BF16))
    diag = jnp.sum(_head(s["qk"], h), axis=-1, keepdims=True)
    o_ref[s["rows"], h * HG_DK:(h + 1) * HG_DK] = inter + _dot(att.astype(BF16), vb) + diag * _head(s["v"], h)
    st_ref[d, h] = _head(s["ebl"], h) * st_ref[d, h] + _dot_tn(vb, _head(s["kend"], h))


def _hgrn_kernel(qf_ref, ff_ref, vf_ref, qb_ref, fb_ref, vb_ref, lb_ref, tri_ref, mk_ref, s0_ref,
                 of_ref, ob_ref, sout_ref, st_ref, *, tb, c, nh):
    nchunk = tb // c

    @pl.when(pl.program_id(1) == 0)
    def _():
        st_ref[...] = s0_ref[...]

    def body(cc, carry):
        rf = pl.multiple_of(cc * c, c)
        rb = pl.multiple_of((nchunk - 1 - cc) * c, c)
        sides = ((_hgrn_wide(qf_ref, ff_ref, vf_ref, lb_ref[0], tri_ref[0], 0, rf, c=c), of_ref),
                 (_hgrn_wide(qb_ref, fb_ref, vb_ref, lb_ref[1], tri_ref[1], 1, rb, c=c), ob_ref))
        pending = [None, None]
        for h in range(nh + 1):
            cur = [_hgrn_pairs(s, h, mk_ref, c=c) if h < nh else None for s, _ in sides]
            for (s, o_ref), p in zip(sides, pending):
                if p is not None:
                    _hgrn_finish(s, h - 1, p, o_ref, st_ref)
            pending = cur
        return carry

    lax.fori_loop(0, nchunk, body, 0, unroll=True)

    @pl.when(pl.program_id(1) == pl.num_programs(1) - 1)
    def _():
        sout_ref[...] = st_ref[...]


def _hgrn(p, lb, s0, consts):
    bsz, r, _ = p.shape
    w = lb.shape[-1]
    nh = w // HG_DK
    tb = min(TOK_TILE, r)
    c = HG_CHUNK
    nb = r // tb
    tri2, m2 = consts
    fwd = lambda col: pl.BlockSpec((None, tb, w), lambda b, s: (b, s, col))
    bwd = lambda col: pl.BlockSpec((None, tb, w), lambda b, s: (b, nb - 1 - s, col))
    const = lambda arr: pl.BlockSpec(arr.shape, lambda b, s: (0,) * arr.ndim)
    st_spec = pl.BlockSpec((None, 2, nh, HG_DK, HG_DK), lambda b, s: (b, 0, 0, 0, 0))
    kern = functools.partial(_hgrn_kernel, tb=tb, c=c, nh=nh)
    return pl.pallas_call(
        kern,
        grid=(bsz, nb),
        in_specs=[fwd(0), fwd(1), fwd(3), bwd(0), bwd(2), bwd(3), const(lb), const(tri2), const(m2),
                  st_spec],
        out_specs=[
            pl.BlockSpec((None, tb, w), lambda b, s: (b, s, 0)),
            pl.BlockSpec((None, tb, w), lambda b, s: (b, nb - 1 - s, 0)),
            st_spec,
        ],
        out_shape=[
            jax.ShapeDtypeStruct((bsz, r, w), F32),
            jax.ShapeDtypeStruct((bsz, r, w), F32),
            jax.ShapeDtypeStruct(s0.shape, F32),
        ],
        scratch_shapes=[pltpu.VMEM((2, nh, HG_DK, HG_DK), F32)],
        compiler_params=_params("parallel", "arbitrary"),
        name="hgrn2_scan",
    )(p, p, p, p, p, p, lb, tri2, m2, s0)


def _even_post_kernel(of_ref, ob_ref, ga_ref, u_ref, gb_ref, up_ref, un_ref, h_ref, mod_ref, hgn_ref, pw_ref,
                      ps_ref, ow_ref, modn_ref, gn_ref, wn_ref, o_ref, pn_ref, ext_ref, y_ref, z_ref, *, tb, seq, nh):
    step = pl.program_id(1)
    last = pl.num_programs(1) - 2
    j = jnp.minimum(step, last)
    w = nh * HG_DK

    @pl.when(step == 0)
    def _():
        z_ref[...] = jnp.zeros(z_ref.shape, BF16)

    n_parts = nh + len(POOL_WINDOWS)
    n_blk = pn_ref.shape[1] // LANES
    per = -(-n_blk // n_parts)

    def in_proj_part(i):
        c0 = min(i * per, n_blk) * LANES
        c1 = min((i + 1) * per, n_blk) * LANES
        if c0 < c1:
            pn_ref[:, c0:c1] = _dot(z_ref[...], wn_ref[:, c0:c1])

    o = of_ref[...] + ob_ref[...]
    for h in range(nh):
        in_proj_part(h)
        sl = slice(h * HG_DK, (h + 1) * HG_DK)
        y_ref[:, sl] = (_rms(o[:, sl], hgn_ref[:, sl]) * _silu(ga_ref[:, sl])).astype(BF16)
    u = u_ref[...]
    ext_ref[0:POOL_HALO, :] = jnp.where(j > 0, up_ref[...], 0.0)
    ext_ref[POOL_HALO:POOL_HALO + tb, :] = u
    ext_ref[POOL_HALO + tb:, :] = jnp.where(j < last, un_ref[...], 0.0)
    t = j * tb + lax.broadcasted_iota(jnp.int32, (tb, 1), 0)
    grp = w // len(POOL_WINDOWS)
    for gi, win in enumerate(POOL_WINDOWS):
        in_proj_part(nh + gi)
        sl = slice(gi * grp, (gi + 1) * grp)
        acc = ext_ref[POOL_HALO - win // 2:POOL_HALO - win // 2 + tb, sl]
        for off in range(-win // 2 + 1, win // 2):
            acc = acc + ext_ref[POOL_HALO + off:POOL_HALO + off + tb, sl]
        cnt = (jnp.minimum(t + win // 2, seq) - jnp.maximum(t - win // 2, 0)).astype(F32)
        yp = acc * (1.0 / cnt) - u[:, sl]
        yb = _dot(yp.astype(BF16), pw_ref[gi]) * ps_ref[:, sl]
        y_ref[:, w + gi * grp:w + (gi + 1) * grp] = (yb * _silu(gb_ref[:, sl])).astype(BF16)
    hn = h_ref[...] + mod_ref[2:3, :] * _dot(y_ref[...], ow_ref[...])
    o_ref[...] = hn
    z_ref[...] = (_rms(hn, gn_ref[...]) * (1.0 + modn_ref[1:2, :]) + modn_ref[0:1, :]).astype(BF16)


def _even_post(o_f, o_b, p, h, mod, hgn, pool_w, pool_scale, out_w, mod_n, g_n, w_n):
    bsz, r, w = o_f.shape
    d = h.shape[-1]
    tb = min(TOK_TILE, r)
    nb = r // tb
    hb = tb // POOL_HALO
    nh = w // HG_DK
    n_next = w_n.shape[1]
    cur = lambda s: jnp.minimum(s, nb - 1)
    tok = lambda col: pl.BlockSpec((None, tb, w), lambda b, s: (b, cur(s), col))
    const = lambda arr: pl.BlockSpec(arr.shape, lambda b, s: (0,) * arr.ndim)
    kern = functools.partial(_even_post_kernel, tb=tb, seq=r, nh=nh)
    return pl.pallas_call(
        kern,
        grid=(bsz, nb + 1),
        in_specs=[
            tok(0), tok(0), tok(4), tok(5), tok(6),
            pl.BlockSpec((None, POOL_HALO, w), lambda b, s: (b, jnp.maximum(cur(s) * hb - 1, 0), 5)),
            pl.BlockSpec((None, POOL_HALO, w), lambda b, s: (b, jnp.minimum((cur(s) + 1) * hb, nb * hb - 1), 5)),
            pl.BlockSpec((None, tb, d), lambda b, s: (b, cur(s), 0)),
            pl.BlockSpec((None, 3, d), lambda b, s: (b, 0, 0)),
            const(hgn), const(pool_w), const(pool_scale), const(out_w),
            pl.BlockSpec((None, 3, d), lambda b, s: (b, 0, 0)), const(g_n), const(w_n),
        ],
        out_specs=[pl.BlockSpec((None, tb, d), lambda b, s: (b, cur(s), 0)),
                   pl.BlockSpec((None, tb, n_next), lambda b, s: (b, jnp.maximum(s - 1, 0), 0))],
        out_shape=[jax.ShapeDtypeStruct((bsz, r, d), F32), jax.ShapeDtypeStruct((bsz, r, n_next), F32)],
        scratch_shapes=[pltpu.VMEM((tb + 2 * POOL_HALO, w), F32), pltpu.VMEM((tb, 2 * w), BF16),
                        pltpu.VMEM((tb, d), BF16)],
        compiler_params=_params("parallel", "arbitrary"),
        name="even_post",
    )(o_f, o_b, p, p, p, p, p, h, mod, hgn, pool_w, pool_scale, out_w, mod_n, g_n, w_n)


def _mla_kv_kernel(*refs, rope):
    if rope:
        ckv_ref, kr_ref, g_ref, wuk_ref, wuvt_ref, cos_ref, sin_ref, kcat_ref, vt_ref = refs
    else:
        ckv_ref, kr_ref, g_ref, wuk_ref, wuvt_ref, kcat_ref, vt_ref = refs
    cn = _rms(ckv_ref[...], g_ref[...]).astype(BF16)
    kn = _dot(cn, wuk_ref[...])
    kr = kr_ref[...]
    if rope:
        lane = lax.broadcasted_iota(jnp.int32, kr.shape, 1)
        swapped = jnp.where((lane % (2 * ROPE_FREQ)) < ROPE_FREQ,
                            pltpu.roll(kr, LANES - ROPE_FREQ, 1), pltpu.roll(kr, ROPE_FREQ, 1))
        kr = kr * cos_ref[...] + swapped * sin_ref[...]
    kr = kr.astype(BF16)
    ones_rows = (lax.broadcasted_iota(jnp.int32, (VT_ROWS - MLA_V, kr.shape[0]), 0) == 0).astype(BF16)
    vt = _dot_nt(wuvt_ref[...], cn)
    for h in range(MLA_HEADS):
        kcat_ref[h, :, 0:MLA_NOPE] = kn[:, h * MLA_NOPE:(h + 1) * MLA_NOPE].astype(BF16)
        kcat_ref[h, :, MLA_NOPE:] = kr
        vt_ref[h, 0:MLA_V, :] = vt[h * MLA_V:(h + 1) * MLA_V].astype(BF16)
        vt_ref[h, MLA_V:, :] = ones_rows


def _mla_kv(p, ckv_blk, kr_blk, g, wuk, wuvt, tables, tb):
    bsz, r, _ = p.shape
    rank = g.shape[-1]
    nb = r // tb
    const = lambda arr: pl.BlockSpec(arr.shape, lambda b, j: (0,) * arr.ndim)
    in_specs = [
        pl.BlockSpec((None, tb, rank), lambda b, j: (b, j, ckv_blk)),
        pl.BlockSpec((None, tb, LANES), lambda b, j: (b, j, kr_blk)),
        const(g), const(wuk), const(wuvt),
    ]
    args = [p, p, g, wuk, wuvt]
    if tables is not None:
        in_specs += [pl.BlockSpec((tb, LANES), lambda b, j: (j, 0))] * 2
        args += list(tables)
    return pl.pallas_call(
        functools.partial(_mla_kv_kernel, rope=tables is not None),
        grid=(bsz, nb),
        in_specs=in_specs,
        out_specs=[
            pl.BlockSpec((None, MLA_HEADS, None, tb, QK_PAD), lambda b, j: (b, 0, j, 0, 0)),
            pl.BlockSpec((None, MLA_HEADS, None, VT_ROWS, tb), lambda b, j: (b, 0, j, 0, 0)),
        ],
        out_shape=[
            jax.ShapeDtypeStruct((bsz, MLA_HEADS, nb, tb, QK_PAD), BF16),
            jax.ShapeDtypeStruct((bsz, MLA_HEADS, nb, VT_ROWS, tb), BF16),
        ],
        compiler_params=_params("parallel", "parallel"),
        name="mla_kv_rope" if tables is not None else "mla_kv",
    )(*args)


def _mla_q_kernel(cq_ref, g_ref, wqt_ref, cos_ref, sin_ref, qt_ref):
    cn = _rms(cq_ref[...], g_ref[...]).astype(BF16)
    f = ROPE_FREQ
    qt_all = _dot_nt(wqt_ref[...], cn) * (MLA_SCALE * LOG2_E)
    for h in range(MLA_HEADS):
        qt = qt_all[h * MLA_QK:(h + 1) * MLA_QK]
        qt_ref[h, 0:MLA_NOPE, :] = qt[0:MLA_NOPE].astype(BF16)
        for ax in range(2):
            r0 = MLA_NOPE + ax * 2 * f
            x1 = qt[r0:r0 + f]
            x2 = qt[r0 + f:r0 + 2 * f]
            co = cos_ref[ax]
            si = sin_ref[ax]
            qt_ref[h, r0:r0 + f, :] = (x1 * co - x2 * si).astype(BF16)
            qt_ref[h, r0 + f:r0 + 2 * f, :] = (x2 * co + x1 * si).astype(BF16)
        qt_ref[h, MLA_QK:, :] = jnp.zeros((QK_PAD - MLA_QK, cn.shape[0]), BF16)


def _mla_q(p, cq_blk, g, wqt, cos_t, sin_t):
    bsz, t, _ = p.shape
    rank = g.shape[-1]
    tm = min(2 * TOK_TILE, t)
    const = lambda arr: pl.BlockSpec(arr.shape, lambda b, j: (0,) * arr.ndim)
    tab = pl.BlockSpec((2, ROPE_FREQ, tm), lambda b, j: (0, 0, j))
    return pl.pallas_call(
        _mla_q_kernel,
        grid=(bsz, t // tm),
        in_specs=[pl.BlockSpec((None, tm, rank), lambda b, j: (b, j, cq_blk)), const(g), const(wqt), tab, tab],
        out_specs=pl.BlockSpec((None, MLA_HEADS, QK_PAD, tm), lambda b, j: (b, 0, 0, j)),
        out_shape=jax.ShapeDtypeStruct((bsz, MLA_HEADS, QK_PAD, t), BF16),
        compiler_params=_params("parallel", "parallel"),
        name="mla_q",
    )(p, g, wqt, cos_t, sin_t)


def _attn_kernel(qt_ref, kc_ref, vtc_ref, kl_ref, vtl_ref, g_ref, o_ref, m_ref, acc_ref, s_ref, mx_ref, *, tq, n_lat):
    nsub = tq // Q_SUB
    m_ref[...] = jnp.full(m_ref.shape, -jnp.inf, F32)
    acc_ref[...] = jnp.zeros(acc_ref.shape, F32)

    def scores(k, nxt, g):
        s = _dot(k, qt_ref[:, g * Q_SUB:(g + 1) * Q_SUB])
        s_ref[nxt, g, 0:k.shape[0], :] = s
        mx_ref[nxt, g] = jnp.max(s, axis=0, keepdims=True)

    def substep(k_next, vt_cur, cur, nxt):
        rows = vt_cur.shape[1]
        for g in range(nsub):
            sl = slice(g * Q_SUB, (g + 1) * Q_SUB)
            scores(k_next, nxt, g)
            m_old = m_ref[:, sl]
            m_new = jnp.maximum(m_old, mx_ref[cur, g])
            alpha = jnp.exp2(m_old - m_new)
            p = jnp.exp2(s_ref[cur, g, 0:rows, :] - m_new)
            acc_ref[:, sl] = alpha * acc_ref[:, sl] + _dot(vt_cur, p.astype(BF16))
            m_ref[:, sl] = m_new

    kc = kc_ref[...]
    for g in range(nsub):
        scores(kc, 0, g)
    substep(kl_ref[0], vtc_ref[...], 0, 1)

    def body(j, carry):
        for u in range(2):
            a = 2 * j + u
            substep(kl_ref[jnp.minimum(a + 1, n_lat - 1)], vtl_ref[a], (1 + u) % 2, u % 2)
        return carry

    lax.fori_loop(0, n_lat // 2, body, 0)
    o = (acc_ref[0:MLA_V, :] * (1.0 / acc_ref[MLA_V:MLA_V + 1, :])).T
    o_ref[...] = (o * _silu(g_ref[...])).astype(o_ref.dtype)


def _attn(qt, kc, vtc, kl, vtl, p):
    bsz, nh, _, t = qt.shape
    lc = kc.shape[3]
    tq = min(Q_TILE, t)
    n_lat = kl.shape[2]
    kv = kl.shape[3]
    assert kc.shape[2] == 1 and lc <= kv and n_lat % 2 == 0
    kern = functools.partial(_attn_kernel, tq=tq, n_lat=n_lat)
    ctx5 = lambda arr: pl.BlockSpec((None, None, None) + arr.shape[3:], lambda b, h, i: (b, h, 0, 0, 0))
    full5 = lambda arr: pl.BlockSpec((None, None) + arr.shape[2:], lambda b, h, i: (b, h, 0, 0, 0))
    return pl.pallas_call(
        kern,
        grid=(bsz, nh, t // tq),
        in_specs=[
            pl.BlockSpec((None, None, QK_PAD, tq), lambda b, h, i: (b, h, 0, i)),
            ctx5(kc), ctx5(vtc), full5(kl), full5(vtl),
            pl.BlockSpec((None, tq, MLA_V), lambda b, h, i: (b, i, h)),
        ],
        out_specs=pl.BlockSpec((None, tq, MLA_V), lambda b, h, i: (b, i, h)),
        out_shape=jax.ShapeDtypeStruct((bsz, t, nh * MLA_V), BF16),
        scratch_shapes=[pltpu.VMEM((1, tq), F32), pltpu.VMEM((VT_ROWS, tq), F32),
                        pltpu.VMEM((2, tq // Q_SUB, kv, Q_SUB), F32), pltpu.VMEM((2, tq // Q_SUB, 1, Q_SUB), F32)],
        compiler_params=_params("parallel", "parallel", "arbitrary"),
        name="mla_attention",
    )(qt, kc, vtc, kl, vtl, p)


def _out_final_kernel(y_ref, h_ref, mod_ref, ow_ref, g_ref, o_ref):
    hn = h_ref[...] + mod_ref[2:3, :] * _dot(y_ref[...], ow_ref[...])
    o_ref[...] = _rms(hn, g_ref[...])


def _out_final(y, h, mod, out_w, g):
    bsz, t, d = h.shape
    wi = y.shape[-1]
    tm = min(2 * TOK_TILE, t)
    return pl.pallas_call(
        _out_final_kernel,
        grid=(bsz, t // tm),
        in_specs=[
            pl.BlockSpec((None, tm, wi), lambda b, j: (b, j, 0)),
            pl.BlockSpec((None, tm, d), lambda b, j: (b, j, 0)),
            pl.BlockSpec((None, 3, d), lambda b, j: (b, 0, 0)),
            pl.BlockSpec((wi, d), lambda b, j: (0, 0)),
            pl.BlockSpec((1, d), lambda b, j: (0, 0)),
        ],
        out_specs=pl.BlockSpec((None, tm, d), lambda b, j: (b, j, 0)),
        out_shape=jax.ShapeDtypeStruct((bsz, t, d), F32),
        compiler_params=_params("parallel", "parallel"),
        name="out_final",
    )(y, h, mod, out_w, g)


def _rope_tables(n_tokens):
    rows = n_tokens // GRID_W
    pos_r = jnp.repeat(jnp.arange(rows), GRID_W).astype(F32)
    pos_c = jnp.tile(jnp.arange(GRID_W), rows).astype(F32)
    inv = ROPE_BASE ** (-2.0 * jnp.arange(ROPE_FREQ, dtype=F32) / (MLA_ROPE // 2))
    ang = jnp.stack([pos_r[:, None] * inv, pos_c[:, None] * inv], axis=1)
    cos, sin = jnp.cos(ang), jnp.sin(ang)
    pad = LANES - MLA_ROPE
    cos_k = jnp.pad(jnp.stack([cos, cos], axis=2).reshape(n_tokens, MLA_ROPE), ((0, 0), (0, pad)))
    sin_k = jnp.pad(jnp.stack([-sin, sin], axis=2).reshape(n_tokens, MLA_ROPE), ((0, 0), (0, pad)))
    cos_q = jnp.transpose(cos, (1, 2, 0))
    sin_q = jnp.transpose(sin, (1, 2, 0))
    return (cos_k, sin_k), (cos_q, sin_q)


def kernel(x, c, ctx, c_ctx, ada_w, ada_b, norm_g, out_w, ev_in_w, hg_lb, hg_norm_g, pool_w, pool_scale,
           od_in_w, qa_norm_g, qb_w, kva_norm_g, kvb_w, final_norm_g):
    bsz, t, d = x.shape
    lc = ctx.shape[1]
    depth = ada_w.shape[0]
    assert depth == 2 and t % (2 * TOK_TILE) == 0 and lc % TOK_TILE == 0 and t % GRID_W == 0
    w = hg_norm_g.shape[-1]
    nh = w // HG_DK
    q_rank = qa_norm_g.shape[-1]
    kv_rank = kva_norm_g.shape[-1]
    d_inner = out_w.shape[1]

    n_cond = -(-(bsz + 1) // SUBLANES) * SUBLANES
    cond = jnp.zeros((n_cond, d), F32).at[:bsz].set(c).at[bsz].set(c_ctx)
    mods = _ada(cond, ada_w, ada_b).reshape(depth, n_cond, 3, d)
    mod_l = [mods[l, :bsz] for l in range(depth)]
    mod_c = [mods[l, bsz:bsz + 1] for l in range(depth)]

    lb = _decay_bounds(hg_lb, 0)
    w_in0 = ev_in_w[0].astype(BF16)
    g0 = norm_g[0].reshape(1, d)
    ctx_flat = ctx.reshape(1, bsz * lc, d)
    n_in0 = w_in0.shape[1]
    p_c = _modnorm_mm(ctx_flat, mod_c[0], g0, w_in0, 4 * TOK_TILE, n_in0 // 4, "in_proj0_ctx").reshape(bsz, lc, n_in0)
    p_l = _modnorm_mm(x, mod_l[0], g0, w_in0, 4 * TOK_TILE, n_in0 // 4, "in_proj0")
    consts = _hgrn_constants(HG_CHUNK)
    s0 = jnp.zeros((bsz, 2, nh, HG_DK, HG_DK), F32)
    of_c, ob_c, s_c = _hgrn(p_c, lb, s0, consts)
    of_l, ob_l, _ = _hgrn(p_l, lb, s_c, consts)
    hgn = hg_norm_g[0].reshape(1, w)
    pw = pool_w[0].astype(BF16)
    ps = pool_scale[0].reshape(1, w)
    ow0 = out_w[0].astype(BF16)
    o1 = q_rank
    o2 = o1 + kv_rank
    o3 = o2 + MLA_ROPE
    w1 = od_in_w[0]
    kr_pad = jnp.zeros((d, LANES - MLA_ROPE), F32)
    w_in1 = jnp.concatenate([w1[:, o3:], w1[:, :o1], w1[:, o1:o2], w1[:, o2:o3], kr_pad], axis=1).astype(BF16)
    w_in1c = w_in1[:, d_inner + q_rank:]
    g1 = norm_g[1].reshape(1, d)
    bcast = lambda m: jnp.broadcast_to(m, (bsz, 3, d))
    _, p1_c = _even_post(of_c, ob_c, p_c, ctx, bcast(mod_c[0]), hgn, pw, ps, ow0, bcast(mod_c[1]), g1, w_in1c)
    hl1, p1_l = _even_post(of_l, ob_l, p_l, x, mod_l[0], hgn, pw, ps, ow0, mod_l[1], g1, w_in1)

    kvw = kvb_w[0].reshape(kv_rank, MLA_HEADS, MLA_NOPE + MLA_V)
    wuk = kvw[..., :MLA_NOPE].reshape(kv_rank, MLA_HEADS * MLA_NOPE).astype(BF16)
    wuvt = jnp.transpose(kvw[..., MLA_NOPE:], (1, 2, 0)).reshape(MLA_HEADS * MLA_V, kv_rank).astype(BF16)
    wqt = jnp.transpose(qb_w[0]).astype(BF16)
    kvg = kva_norm_g[0].reshape(1, kv_rank)
    qag = qa_norm_g[0].reshape(1, q_rank)
    tab_k, tab_q = _rope_tables(t)
    kc, vtc = _mla_kv(p1_c, 0, kv_rank // LANES, kvg, wuk, wuvt, None, lc)
    kl, vtl = _mla_kv(p1_l, (d_inner + q_rank) // kv_rank, (d_inner + q_rank + kv_rank) // LANES, kvg, wuk, wuvt,
                      tab_k, min(KV_CHUNK, t))
    qt = _mla_q(p1_l, d_inner // q_rank, qag, wqt, *tab_q)
    y = _attn(qt, kc, vtc, kl, vtl, p1_l)
    return _out_final(y, hl1, mod_l[1], out_w[1].astype(BF16), final_norm_g.reshape(1, d))
```

```python
import functools

import numpy as np
import jax
import jax.numpy as jnp
from jax import lax
from jax.experimental import pallas as pl
from jax.experimental.pallas import tpu as pltpu

F32 = jnp.float32
BF16 = jnp.bfloat16

EPS = 1e-6
GRID_W = 64
HG_DK = 128
POOL_WINDOWS = (2, 4, 8, 16)
MLA_HEADS = 16
MLA_NOPE = 128
MLA_ROPE = 64
MLA_V = 128
MLA_QK = MLA_NOPE + MLA_ROPE
QK_PAD = 256
BF16_ROWS = 16
VT_ROWS = MLA_V + BF16_ROWS
MLA_SCALE = MLA_QK ** -0.5
LOG2_E = 1.4426950408889634
ROPE_FREQ = MLA_ROPE // 4
ROPE_BASE = 10000.0

LANES = 128
SUBLANES = 8
VMEM_LIMIT = 48 * 1024 * 1024

HG_CHUNK = 64
TOK_TILE = 256
IN_PROJ_ROWS = 1024
IN_PROJ_COL_PARTS = 4
KV_CHUNK = 512
Q_TILE = 4096
Q_SUB = 256
POOL_HALO = 8


def _dot(a, b):
    return jnp.dot(a, b, preferred_element_type=F32)


def _dot_nt(a, b):
    return lax.dot_general(a, b, (((1,), (1,)), ((), ())), preferred_element_type=F32)


def _dot_tn(a, b):
    return lax.dot_general(a, b, (((0,), (0,)), ((), ())), preferred_element_type=F32)


def _silu(x):
    h = 0.5 * x
    return h + h * jnp.tanh(h)


def _split_bf16(x):
    hi = x.astype(BF16)
    lo = (x - hi.astype(F32)).astype(BF16)
    return hi, lo


def _params(*sem):
    return pltpu.CompilerParams(dimension_semantics=sem, vmem_limit_bytes=VMEM_LIMIT)


def _ada_kernel(c_ref, w_ref, b_ref, o_ref):
    c = c_ref[...]
    s_hi, s_lo = _split_bf16(_silu(c))
    w_hi, w_lo = _split_bf16(w_ref[...])
    o_ref[...] = _dot(s_hi, w_hi) + _dot(s_lo, w_hi) + _dot(s_hi, w_lo) + b_ref[...]


def _ada(cond, ada_w, ada_b):
    depth, d, _ = ada_w.shape
    r = cond.shape[0]
    return pl.pallas_call(
        _ada_kernel,
        grid=(depth, 3),
        in_specs=[
            pl.BlockSpec((r, d), lambda l, j: (0, 0)),
            pl.BlockSpec((None, d, d), lambda l, j: (l, 0, j)),
            pl.BlockSpec((None, 1, d), lambda l, j: (l, 0, j)),
        ],
        out_specs=pl.BlockSpec((None, r, d), lambda l, j: (l, 0, j)),
        out_shape=jax.ShapeDtypeStruct((depth, r, 3 * d), F32),
        compiler_params=_params("parallel", "parallel"),
        name="ada_modulation",
    )(cond, ada_w, ada_b.reshape(depth, 1, 3 * d))


def _rms(x, g):
    return x * lax.rsqrt(jnp.mean(x * x, axis=-1, keepdims=True) + EPS) * g


def _modnorm_mm_kernel(x_ref, mod_ref, g_ref, w_ref, o_ref, z_ref):
    @pl.when(pl.program_id(2) == 0)
    def _():
        y = _rms(x_ref[...], g_ref[...])
        z_ref[...] = (y * (1.0 + mod_ref[1:2, :]) + mod_ref[0:1, :]).astype(BF16)

    o_ref[...] = _dot(z_ref[...], w_ref[...])


def _modnorm_mm(x, mod, g, w, tm, tn, name):
    bx, r, d = x.shape
    n = w.shape[1]
    tm = min(tm, r)
    return pl.pallas_call(
        _modnorm_mm_kernel,
        grid=(bx, r // tm, n // tn),
        in_specs=[
            pl.BlockSpec((None, tm, d), lambda b, i, j: (b, i, 0)),
            pl.BlockSpec((None, 3, d), lambda b, i, j: (b, 0, 0)),
            pl.BlockSpec((1, d), lambda b, i, j: (0, 0)),
            pl.BlockSpec((d, tn), lambda b, i, j: (0, j)),
        ],
        out_specs=pl.BlockSpec((None, tm, tn), lambda b, i, j: (b, i, j)),
        out_shape=jax.ShapeDtypeStruct((bx, r, n), F32),
        scratch_shapes=[pltpu.VMEM((tm, d), BF16)],
        compiler_params=_params("parallel", "parallel", "arbitrary"),
        name=name,
    )(x, mod, g, w)


def _decay_bounds_kernel(a_ref, o_ref, *, layer):
    slots = a_ref[...]
    e = jnp.exp(slots - jnp.max(slots, axis=1, keepdims=True))
    o_ref[...] = jnp.sum(e[:, :layer + 1], axis=1) / jnp.sum(e, axis=1)


def _decay_bounds(hg_lb, layer):
    ndir, nslot, w = hg_lb.shape
    return pl.pallas_call(
        functools.partial(_decay_bounds_kernel, layer=layer),
        out_shape=jax.ShapeDtypeStruct((ndir, 1, w), F32),
        name="decay_bounds",
    )(hg_lb.reshape(ndir, nslot, 1, w))


def _hgrn_levels(c):
    w = c // 2
    out = []
    while w >= 1:
        out.append(w)
        w //= 2
    return tuple(out)


def _hgrn_constants(c):
    t = np.arange(c)
    tri = np.tril(np.ones((c, c), np.float32))
    masks = []
    for w in _hgrn_levels(c):
        blk = t // (2 * w)
        first = (t % (2 * w)) < w
        masks.append(((blk[:, None] == blk[None, :]) & (~first[:, None]) & first[None, :]).astype(np.float32))
    masks = np.stack(masks)
    tri2 = np.stack([tri, tri[::-1, ::-1]])
    m2 = np.stack([masks, masks[:, ::-1, ::-1]])
    return jnp.asarray(tri2, BF16), jnp.asarray(m2, F32)


def _hgrn_level_operand(b, g, k, q, w, d):
    c, width = b.shape
    row = lax.broadcasted_iota(jnp.int32, (c, 1), 0)
    keys_first = d == 0
    if w >= SUBLANES:
        ref_off = w - 1 if d == 0 else w
        pieces = []
        for r0 in range(0, c, 2 * w):
            bref = jnp.broadcast_to(b[r0 + ref_off:r0 + ref_off + 1, :], (w, width))
            for half in range(2):
                sl = slice(r0 + half * w, r0 + (half + 1) * w)
                if (half == 0) == keys_first:
                    pieces.append(k[sl] * jnp.exp2(bref - b[sl]))
                else:
                    pieces.append(q[sl] * jnp.exp2(b[sl] - bref))
        return jnp.concatenate(pieces, axis=0)
    before = (row % (2 * w)) < w
    kq = jnp.where(before == keys_first, k, q)
    if w == 1:
        moving = (row % 2 == 1) if d == 0 else (row % 2 == 0)
        return kq * jnp.exp2(jnp.where(moving, g, 0.0))
    ref_off = w - 1 if d == 0 else w
    sub = lax.broadcasted_iota(jnp.int32, (SUBLANES, 1), 0)
    pieces = []
    for r0 in range(0, c, SUBLANES):
        lo = jnp.broadcast_to(b[r0 + ref_off:r0 + ref_off + 1, :], (SUBLANES, width))
        if 2 * w == SUBLANES:
            pieces.append(lo)
        else:
            hi = jnp.broadcast_to(b[r0 + 2 * w + ref_off:r0 + 2 * w + ref_off + 1, :], (SUBLANES, width))
            pieces.append(jnp.where(sub < 2 * w, lo, hi))
    bref = jnp.concatenate(pieces, axis=0)
    return kq * jnp.exp2(-jnp.abs(b - bref))


def _hgrn_wide(q_ref, f_ref, v_ref, lb, tri, d, r0, *, c):
    last = c - 1 if d == 0 else 0
    rows = pl.ds(r0, c)
    half = 0.5 * (1.0 - lb)
    f = (lb + half) + half * jnp.tanh(0.5 * f_ref[rows, :])
    g = jnp.log2(f)
    g_hi, g_lo = _split_bf16(g)
    b = _dot(tri, g_hi) + _dot(tri, g_lo)
    q = _silu(q_ref[rows, :])
    k = 1.0 - f
    v = v_ref[rows, :]
    bl = b[last:last + 1, :]
    return dict(d=d, rows=rows, g=g, b=b, q=q, k=k, v=v, vb=v.astype(BF16), qk=q * k,
                qe=(q * jnp.exp2(b)).astype(BF16), kend=(k * jnp.exp2(bl - b)).astype(BF16), ebl=jnp.exp2(bl))


def _head(x, h):
    return x[:, h * HG_DK:(h + 1) * HG_DK]


def _hgrn_pairs(s, h, mk_ref, *, c):
    d = s["d"]
    att = None
    for l, w in enumerate(_hgrn_levels(c)):
        x = _hgrn_level_operand(_head(s["b"], h), _head(s["g"], h), _head(s["k"], h), _head(s["q"], h), w, d)
        x = x.astype(BF16)
        t = mk_ref[d, l] * _dot_nt(x, x)
        att = t if att is None else att + t
    return att


def _hgrn_finish(s, h, att, o_ref, st_ref):
    d = s["d"]
    vb = _head(s["vb"], h)
    inter = _dot_nt(_head(s["qe"], h), st_ref[d, h].astype(BF16))
    diag = jnp.sum(_head(s["qk"], h), axis=-1, keepdims=True)
    o_ref[s["rows"], h * HG_DK:(h + 1) * HG_DK] = inter + _dot(att.astype(BF16), vb) + diag * _head(s["v"], h)
    st_ref[d, h] = _head(s["ebl"], h) * st_ref[d, h] + _dot_tn(vb, _head(s["kend"], h))


def _hgrn_kernel(qf_ref, ff_ref, vf_ref, qb_ref, fb_ref, vb_ref, lb_ref, tri_ref, mk_ref, s0_ref,
                 of_ref, ob_ref, sout_ref, st_ref, *, tb, c, nh):
    nchunk = tb // c

    @pl.when(pl.program_id(1) == 0)
    def _():
        st_ref[...] = s0_ref[...]

    def body(cc, carry):
        rf = pl.multiple_of(cc * c, c)
        rb = pl.multiple_of((nchunk - 1 - cc) * c, c)
        sides = ((_hgrn_wide(qf_ref, ff_ref, vf_ref, lb_ref[0], tri_ref[0], 0, rf, c=c), of_ref),
                 (_hgrn_wide(qb_ref, fb_ref, vb_ref, lb_ref[1], tri_ref[1], 1, rb, c=c), ob_ref))
        pending = [None, None]
        for h in range(nh + 1):
            cur = [_hgrn_pairs(s, h, mk_ref, c=c) if h < nh else None for s, _ in sides]
            for (s, o_ref), p in zip(sides, pending):
                if p is not None:
                    _hgrn_finish(s, h - 1, p, o_ref, st_ref)
            pending = cur
        return carry

    lax.fori_loop(0, nchunk, body, 0, unroll=True)

    @pl.when(pl.program_id(1) == pl.num_programs(1) - 1)
    def _():
        sout_ref[...] = st_ref[...]


def _hgrn(p, lb, s0, consts):
    bsz, r, _ = p.shape
    w = lb.shape[-1]
    nh = w // HG_DK
    tb = min(TOK_TILE, r)
    c = HG_CHUNK
    nb = r // tb
    tri2, m2 = consts
    fwd = lambda col: pl.BlockSpec((None, tb, w), lambda b, s: (b, s, col))
    bwd = lambda col: pl.BlockSpec((None, tb, w), lambda b, s: (b, nb - 1 - s, col))
    const = lambda arr: pl.BlockSpec(arr.shape, lambda b, s: (0,) * arr.ndim)
    st_spec = pl.BlockSpec((None, 2, nh, HG_DK, HG_DK), lambda b, s: (b, 0, 0, 0, 0))
    kern = functools.partial(_hgrn_kernel, tb=tb, c=c, nh=nh)
    return pl.pallas_call(
        kern,
        grid=(bsz, nb),
        in_specs=[fwd(0), fwd(1), fwd(3), bwd(0), bwd(2), bwd(3), const(lb), const(tri2), const(m2),
                  st_spec],
        out_specs=[
            pl.BlockSpec((None, tb, w), lambda b, s: (b, s, 0)),
            pl.BlockSpec((None, tb, w), lambda b, s: (b, nb - 1 - s, 0)),
            st_spec,
        ],
        out_shape=[
            jax.ShapeDtypeStruct((bsz, r, w), F32),
            jax.ShapeDtypeStruct((bsz, r, w), F32),
            jax.ShapeDtypeStruct(s0.shape, F32),
        ],
        scratch_shapes=[pltpu.VMEM((2, nh, HG_DK, HG_DK), F32)],
        compiler_params=_params("parallel", "arbitrary"),
        name="hgrn2_scan",
    )(p, p, p, p, p, p, lb, tri2, m2, s0)


def _even_post_kernel(of_ref, ob_ref, ga_ref, u_ref, gb_ref, up_ref, un_ref, h_ref, mod_ref, hgn_ref, pw_ref,
                      ps_ref, ow_ref, modn_ref, gn_ref, wn_ref, o_ref, pn_ref, ext_ref, y_ref, z_ref, *, tb, seq, nh):
    step = pl.program_id(1)
    last = pl.num_programs(1) - 2
    j = jnp.minimum(step, last)
    w = nh * HG_DK

    @pl.when(step == 0)
    def _():
        z_ref[...] = jnp.zeros(z_ref.shape, BF16)

    n_parts = nh + len(POOL_WINDOWS)
    n_blk = pn_ref.shape[1] // LANES
    per = -(-n_blk // n_parts)

    def in_proj_part(i):
        c0 = min(i * per, n_blk) * LANES
        c1 = min((i + 1) * per, n_blk) * LANES
        if c0 < c1:
            pn_ref[:, c0:c1] = _dot(z_ref[...], wn_ref[:, c0:c1])

    o = of_ref[...] + ob_ref[...]
    for h in range(nh):
        in_proj_part(h)
        sl = slice(h * HG_DK, (h + 1) * HG_DK)
        y_ref[:, sl] = (_rms(o[:, sl], hgn_ref[:, sl]) * _silu(ga_ref[:, sl])).astype(BF16)
    u = u_ref[...]
    ext_ref[0:POOL_HALO, :] = jnp.where(j > 0, up_ref[...], 0.0)
    ext_ref[POOL_HALO:POOL_HALO + tb, :] = u
    ext_ref[POOL_HALO + tb:, :] = jnp.where(j < last, un_ref[...], 0.0)
    t = j * tb + lax.broadcasted_iota(jnp.int32, (tb, 1), 0)
    grp = w // len(POOL_WINDOWS)
    for gi, win in enumerate(POOL_WINDOWS):
        in_proj_part(nh + gi)
        sl = slice(gi * grp, (gi + 1) * grp)
        acc = ext_ref[POOL_HALO - win // 2:POOL_HALO - win // 2 + tb, sl]
        for off in range(-win // 2 + 1, win // 2):
            acc = acc + ext_ref[POOL_HALO + off:POOL_HALO + off + tb, sl]
        cnt = (jnp.minimum(t + win // 2, seq) - jnp.maximum(t - win // 2, 0)).astype(F32)
        yp = acc * (1.0 / cnt) - u[:, sl]
        yb = _dot(yp.astype(BF16), pw_ref[gi]) * ps_ref[:, sl]
        y_ref[:, w + gi * grp:w + (gi + 1) * grp] = (yb * _silu(gb_ref[:, sl])).astype(BF16)
    hn = h_ref[...] + mod_ref[2:3, :] * _dot(y_ref[...], ow_ref[...])
    o_ref[...] = hn
    z_ref[...] = (_rms(hn, gn_ref[...]) * (1.0 + modn_ref[1:2, :]) + modn_ref[0:1, :]).astype(BF16)


def _even_post(o_f, o_b, p, h, mod, hgn, pool_w, pool_scale, out_w, mod_n, g_n, w_n):
    bsz, r, w = o_f.shape
    d = h.shape[-1]
    tb = min(TOK_TILE, r)
    nb = r // tb
    hb = tb // POOL_HALO
    nh = w // HG_DK
    n_next = w_n.shape[1]
    cur = lambda s: jnp.minimum(s, nb - 1)
    tok = lambda col: pl.BlockSpec((None, tb, w), lambda b, s: (b, cur(s), col))
    const = lambda arr: pl.BlockSpec(arr.shape, lambda b, s: (0,) * arr.ndim)
    kern = functools.partial(_even_post_kernel, tb=tb, seq=r, nh=nh)
    return pl.pallas_call(
        kern,
        grid=(bsz, nb + 1),
        in_specs=[
            tok(0), tok(0), tok(4), tok(5), tok(6),
            pl.BlockSpec((None, POOL_HALO, w), lambda b, s: (b, jnp.maximum(cur(s) * hb - 1, 0), 5)),
            pl.BlockSpec((None, POOL_HALO, w), lambda b, s: (b, jnp.minimum((cur(s) + 1) * hb, nb * hb - 1), 5)),
            pl.BlockSpec((None, tb, d), lambda b, s: (b, cur(s), 0)),
            pl.BlockSpec((None, 3, d), lambda b, s: (b, 0, 0)),
            const(hgn), const(pool_w), const(pool_scale), const(out_w),
            pl.BlockSpec((None, 3, d), lambda b, s: (b, 0, 0)), const(g_n), const(w_n),
        ],
        out_specs=[pl.BlockSpec((None, tb, d), lambda b, s: (b, cur(s), 0)),
                   pl.BlockSpec((None, tb, n_next), lambda b, s: (b, jnp.maximum(s - 1, 0), 0))],
        out_shape=[jax.ShapeDtypeStruct((bsz, r, d), F32), jax.ShapeDtypeStruct((bsz, r, n_next), F32)],
        scratch_shapes=[pltpu.VMEM((tb + 2 * POOL_HALO, w), F32), pltpu.VMEM((tb, 2 * w), BF16),
                        pltpu.VMEM((tb, d), BF16)],
        compiler_params=_params("parallel", "arbitrary"),
        name="even_post",
    )(o_f, o_b, p, p, p, p, p, h, mod, hgn, pool_w, pool_scale, out_w, mod_n, g_n, w_n)


def _mla_kv_kernel(*refs, rope):
    if rope:
        ckv_ref, kr_ref, g_ref, wuk_ref, wuvt_ref, cos_ref, sin_ref, kcat_ref, vt_ref = refs
    else:
        ckv_ref, kr_ref, g_ref, wuk_ref, wuvt_ref, kcat_ref, vt_ref = refs
    cn = _rms(ckv_ref[...], g_ref[...]).astype(BF16)
    kn = _dot(cn, wuk_ref[...])
    kr = kr_ref[...]
    if rope:
        lane = lax.broadcasted_iota(jnp.int32, kr.shape, 1)
        swapped = jnp.where((lane % (2 * ROPE_FREQ)) < ROPE_FREQ,
                            pltpu.roll(kr, LANES - ROPE_FREQ, 1), pltpu.roll(kr, ROPE_FREQ, 1))
        kr = kr * cos_ref[...] + swapped * sin_ref[...]
    kr = kr.astype(BF16)
    ones_rows = (lax.broadcasted_iota(jnp.int32, (VT_ROWS - MLA_V, kr.shape[0]), 0) == 0).astype(BF16)
    vt = _dot_nt(wuvt_ref[...], cn)
    for h in range(MLA_HEADS):
        kcat_ref[h, :, 0:MLA_NOPE] = kn[:, h * MLA_NOPE:(h + 1) * MLA_NOPE].astype(BF16)
        kcat_ref[h, :, MLA_NOPE:] = kr
        vt_ref[h, 0:MLA_V, :] = vt[h * MLA_V:(h + 1) * MLA_V].astype(BF16)
        vt_ref[h, MLA_V:, :] = ones_rows


def _mla_kv(p, ckv_blk, kr_blk, g, wuk, wuvt, tables, tb):
    bsz, r, _ = p.shape
    rank = g.shape[-1]
    nb = r // tb
    const = lambda arr: pl.BlockSpec(arr.shape, lambda b, j: (0,) * arr.ndim)
    in_specs = [
        pl.BlockSpec((None, tb, rank), lambda b, j: (b, j, ckv_blk)),
        pl.BlockSpec((None, tb, LANES), lambda b, j: (b, j, kr_blk)),
        const(g), const(wuk), const(wuvt),
    ]
    args = [p, p, g, wuk, wuvt]
    if tables is not None:
        in_specs += [pl.BlockSpec((tb, LANES), lambda b, j: (j, 0))] * 2
        args += list(tables)
    return pl.pallas_call(
        functools.partial(_mla_kv_kernel, rope=tables is not None),
        grid=(bsz, nb),
        in_specs=in_specs,
        out_specs=[
            pl.BlockSpec((None, MLA_HEADS, None, tb, QK_PAD), lambda b, j: (b, 0, j, 0, 0)),
            pl.BlockSpec((None, MLA_HEADS, None, VT_ROWS, tb), lambda b, j: (b, 0, j, 0, 0)),
        ],
        out_shape=[
            jax.ShapeDtypeStruct((bsz, MLA_HEADS, nb, tb, QK_PAD), BF16),
            jax.ShapeDtypeStruct((bsz, MLA_HEADS, nb, VT_ROWS, tb), BF16),
        ],
        compiler_params=_params("parallel", "parallel"),
        name="mla_kv_rope" if tables is not None else "mla_kv",
    )(*args)


def _mla_q_kernel(cq_ref, g_ref, wqt_ref, cos_ref, sin_ref, qt_ref):
    cn = _rms(cq_ref[...], g_ref[...]).astype(BF16)
    f = ROPE_FREQ
    qt_all = _dot_nt(wqt_ref[...], cn) * (MLA_SCALE * LOG2_E)
    for h in range(MLA_HEADS):
        qt = qt_all[h * MLA_QK:(h + 1) * MLA_QK]
        qt_ref[h, 0:MLA_NOPE, :] = qt[0:MLA_NOPE].astype(BF16)
        for ax in range(2):
            r0 = MLA_NOPE + ax * 2 * f
            x1 = qt[r0:r0 + f]
            x2 = qt[r0 + f:r0 + 2 * f]
            co = cos_ref[ax]
            si = sin_ref[ax]
            qt_ref[h, r0:r0 + f, :] = (x1 * co - x2 * si).astype(BF16)
            qt_ref[h, r0 + f:r0 + 2 * f, :] = (x2 * co + x1 * si).astype(BF16)
        qt_ref[h, MLA_QK:, :] = jnp.zeros((QK_PAD - MLA_QK, cn.shape[0]), BF16)


def _mla_q(p, cq_blk, g, wqt, cos_t, sin_t):
    bsz, t, _ = p.shape
    rank = g.shape[-1]
    tm = min(2 * TOK_TILE, t)
    const = lambda arr: pl.BlockSpec(arr.shape, lambda b, j: (0,) * arr.ndim)
    tab = pl.BlockSpec((2, ROPE_FREQ, tm), lambda b, j: (0, 0, j))
    return pl.pallas_call(
        _mla_q_kernel,
        grid=(bsz, t // tm),
        in_specs=[pl.BlockSpec((None, tm, rank), lambda b, j: (b, j, cq_blk)), const(g), const(wqt), tab, tab],
        out_specs=pl.BlockSpec((None, MLA_HEADS, QK_PAD, tm), lambda b, j: (b, 0, 0, j)),
        out_shape=jax.ShapeDtypeStruct((bsz, MLA_HEADS, QK_PAD, t), BF16),
        compiler_params=_params("parallel", "parallel"),
        name="mla_q",
    )(p, g, wqt, cos_t, sin_t)


def _attn_kernel(qt_ref, kc_ref, vtc_ref, kl_ref, vtl_ref, g_ref, o_ref, m_ref, acc_ref, s_ref, mx_ref, *, tq, n_lat):
    nsub = tq // Q_SUB
    m_ref[...] = jnp.full(m_ref.shape, -jnp.inf, F32)
    acc_ref[...] = jnp.zeros(acc_ref.shape, F32)

    def scores(k, nxt, g):
        s = _dot(k, qt_ref[:, g * Q_SUB:(g + 1) * Q_SUB])
        s_ref[nxt, g, 0:k.shape[0], :] = s
        mx_ref[nxt, g] = jnp.max(s, axis=0, keepdims=True)

    def substep(k_next, vt_cur, cur, nxt):
        rows = vt_cur.shape[1]
        for g in range(nsub):
            sl = slice(g * Q_SUB, (g + 1) * Q_SUB)
            scores(k_next, nxt, g)
            m_old = m_ref[:, sl]
            m_new = jnp.maximum(m_old, mx_ref[cur, g])
            alpha = jnp.exp2(m_old - m_new)
            p = jnp.exp2(s_ref[cur, g, 0:rows, :] - m_new)
            acc_ref[:, sl] = alpha * acc_ref[:, sl] + _dot(vt_cur, p.astype(BF16))
            m_ref[:, sl] = m_new

    kc = kc_ref[...]
    for g in range(nsub):
        scores(kc, 0, g)
    substep(kl_ref[0], vtc_ref[...], 0, 1)

    def body(j, carry):
        for u in range(2):
            a = 2 * j + u
            substep(kl_ref[jnp.minimum(a + 1, n_lat - 1)], vtl_ref[a], (1 + u) % 2, u % 2)
        return carry

    lax.fori_loop(0, n_lat // 2, body, 0)
    o = (acc_ref[0:MLA_V, :] * (1.0 / acc_ref[MLA_V:MLA_V + 1, :])).T
    o_ref[...] = (o * _silu(g_ref[...])).astype(o_ref.dtype)


def _attn(qt, kc, vtc, kl, vtl, p):
    bsz, nh, _, t = qt.shape
    lc = kc.shape[3]
    tq = min(Q_TILE, t)
    n_lat = kl.shape[2]
    kv = kl.shape[3]
    assert kc.shape[2] == 1 and lc <= kv and n_lat % 2 == 0
    kern = functools.partial(_attn_kernel, tq=tq, n_lat=n_lat)
    ctx5 = lambda arr: pl.BlockSpec((None, None, None) + arr.shape[3:], lambda b, h, i: (b, h, 0, 0, 0))
    full5 = lambda arr: pl.BlockSpec((None, None) + arr.shape[2:], lambda b, h, i: (b, h, 0, 0, 0))
    return pl.pallas_call(
        kern,
        grid=(bsz, nh, t // tq),
        in_specs=[
            pl.BlockSpec((None, None, QK_PAD, tq), lambda b, h, i: (b, h, 0, i)),
            ctx5(kc), ctx5(vtc), full5(kl), full5(vtl),
            pl.BlockSpec((None, tq, MLA_V), lambda b, h, i: (b, i, h)),
        ],
        out_specs=pl.BlockSpec((None, tq, MLA_V), lambda b, h, i: (b, i, h)),
        out_shape=jax.ShapeDtypeStruct((bsz, t, nh * MLA_V), BF16),
        scratch_shapes=[pltpu.VMEM((1, tq), F32), pltpu.VMEM((VT_ROWS, tq), F32),
                        pltpu.VMEM((2, tq // Q_SUB, kv, Q_SUB), F32), pltpu.VMEM((2, tq // Q_SUB, 1, Q_SUB), F32)],
        compiler_params=_params("parallel", "parallel", "arbitrary"),
        name="mla_attention",
    )(qt, kc, vtc, kl, vtl, p)


def _out_final_kernel(y_ref, h_ref, mod_ref, ow_ref, g_ref, o_ref):
    hn = h_ref[...] + mod_ref[2:3, :] * _dot(y_ref[...], ow_ref[...])
    o_ref[...] = _rms(hn, g_ref[...])


def _out_final(y, h, mod, out_w, g):
    bsz, t, d = h.shape
    wi = y.shape[-1]
    tm = min(2 * TOK_TILE, t)
    return pl.pallas_call(
        _out_final_kernel,
        grid=(bsz, t // tm),
        in_specs=[
            pl.BlockSpec((None, tm, wi), lambda b, j: (b, j, 0)),
            pl.BlockSpec((None, tm, d), lambda b, j: (b, j, 0)),
            pl.BlockSpec((None, 3, d), lambda b, j: (b, 0, 0)),
            pl.BlockSpec((wi, d), lambda b, j: (0, 0)),
            pl.BlockSpec((1, d), lambda b, j: (0, 0)),
        ],
        out_specs=pl.BlockSpec((None, tm, d), lambda b, j: (b, j, 0)),
        out_shape=jax.ShapeDtypeStruct((bsz, t, d), F32),
        compiler_params=_params("parallel", "parallel"),
        name="out_final",
    )(y, h, mod, out_w, g)


def _rope_tables(n_tokens):
    rows = n_tokens // GRID_W
    pos_r = jnp.repeat(jnp.arange(rows), GRID_W).astype(F32)
    pos_c = jnp.tile(jnp.arange(GRID_W), rows).astype(F32)
    inv = ROPE_BASE ** (-2.0 * jnp.arange(ROPE_FREQ, dtype=F32) / (MLA_ROPE // 2))
    ang = jnp.stack([pos_r[:, None] * inv, pos_c[:, None] * inv], axis=1)
    cos, sin = jnp.cos(ang), jnp.sin(ang)
    pad = LANES - MLA_ROPE
    cos_k = jnp.pad(jnp.stack([cos, cos], axis=2).reshape(n_tokens, MLA_ROPE), ((0, 0), (0, pad)))
    sin_k = jnp.pad(jnp.stack([-sin, sin], axis=2).reshape(n_tokens, MLA_ROPE), ((0, 0), (0, pad)))
    cos_q = jnp.transpose(cos, (1, 2, 0))
    sin_q = jnp.transpose(sin, (1, 2, 0))
    return (cos_k, sin_k), (cos_q, sin_q)


def kernel(x, c, ctx, c_ctx, ada_w, ada_b, norm_g, out_w, ev_in_w, hg_lb, hg_norm_g, pool_w, pool_scale,
           od_in_w, qa_norm_g, qb_w, kva_norm_g, kvb_w, final_norm_g):
    bsz, t, d = x.shape
    lc = ctx.shape[1]
    depth = ada_w.shape[0]
    assert depth == 2 and t % (2 * TOK_TILE) == 0 and lc % TOK_TILE == 0 and t % GRID_W == 0
    w = hg_norm_g.shape[-1]
    nh = w // HG_DK
    q_rank = qa_norm_g.shape[-1]
    kv_rank = kva_norm_g.shape[-1]
    d_inner = out_w.shape[1]

    n_cond = -(-(bsz + 1) // SUBLANES) * SUBLANES
    cond = jnp.zeros((n_cond, d), F32).at[:bsz].set(c).at[bsz].set(c_ctx)
    mods = _ada(cond, ada_w, ada_b).reshape(depth, n_cond, 3, d)
    mod_l = [mods[l, :bsz] for l in range(depth)]
    mod_c = [mods[l, bsz:bsz + 1] for l in range(depth)]

    lb = _decay_bounds(hg_lb, 0)
    w_in0 = ev_in_w[0].astype(BF16)
    g0 = norm_g[0].reshape(1, d)
    ctx_flat = ctx.reshape(1, bsz * lc, d)
    n_in0 = w_in0.shape[1]
    tn0 = n_in0 // IN_PROJ_COL_PARTS
    p_c = _modnorm_mm(ctx_flat, mod_c[0], g0, w_in0, IN_PROJ_ROWS, tn0, "in_proj0_ctx").reshape(bsz, lc, n_in0)
    p_l = _modnorm_mm(x, mod_l[0], g0, w_in0, IN_PROJ_ROWS, tn0, "in_proj0")
    consts = _hgrn_constants(HG_CHUNK)
    s0 = jnp.zeros((bsz, 2, nh, HG_DK, HG_DK), F32)
    of_c, ob_c, s_c = _hgrn(p_c, lb, s0, consts)
    of_l, ob_l, _ = _hgrn(p_l, lb, s_c, consts)
    hgn = hg_norm_g[0].reshape(1, w)
    pw = pool_w[0].astype(BF16)
    ps = pool_scale[0].reshape(1, w)
    ow0 = out_w[0].astype(BF16)
    o1 = q_rank
    o2 = o1 + kv_rank
    o3 = o2 + MLA_ROPE
    w1 = od_in_w[0]
    kr_pad = jnp.zeros((d, LANES - MLA_ROPE), F32)
    w_in1 = jnp.concatenate([w1[:, o3:], w1[:, :o1], w1[:, o1:o2], w1[:, o2:o3], kr_pad], axis=1).astype(BF16)
    w_in1c = w_in1[:, d_inner + q_rank:]
    g1 = norm_g[1].reshape(1, d)
    bcast = lambda m: jnp.broadcast_to(m, (bsz, 3, d))
    _, p1_c = _even_post(of_c, ob_c, p_c, ctx, bcast(mod_c[0]), hgn, pw, ps, ow0, bcast(mod_c[1]), g1, w_in1c)
    hl1, p1_l = _even_post(of_l, ob_l, p_l, x, mod_l[0], hgn, pw, ps, ow0, mod_l[1], g1, w_in1)

    kvw = kvb_w[0].reshape(kv_rank, MLA_HEADS, MLA_NOPE + MLA_V)
    wuk = kvw[..., :MLA_NOPE].reshape(kv_rank, MLA_HEADS * MLA_NOPE).astype(BF16)
    wuvt = jnp.transpose(kvw[..., MLA_NOPE:], (1, 2, 0)).reshape(MLA_HEADS * MLA_V, kv_rank).astype(BF16)
    wqt = jnp.transpose(qb_w[0]).astype(BF16)
    kvg = kva_norm_g[0].reshape(1, kv_rank)
    qag = qa_norm_g[0].reshape(1, q_rank)
    tab_k, tab_q = _rope_tables(t)
    kc, vtc = _mla_kv(p1_c, 0, kv_rank // LANES, kvg, wuk, wuvt, None, lc)
    kl, vtl = _mla_kv(p1_l, (d_inner + q_rank) // kv_rank, (d_inner + q_rank + kv_rank) // LANES, kvg, wuk, wuvt,
                      tab_k, min(KV_CHUNK, t))
    qt = _mla_q(p1_l, d_inner // q_rank, qag, wqt, *tab_q)
    y = _attn(qt, kc, vtc, kl, vtl, p1_l)
    return _out_final(y, hl1, mod_l[1], out_w[1].astype(BF16), final_norm_g.reshape(1, d))
```

```python
import functools

import numpy as np
import jax
import jax.numpy as jnp
from jax import lax
from jax.experimental import pallas as pl
from jax.experimental.pallas import tpu as pltpu

F32 = jnp.float32
BF16 = jnp.bfloat16

EPS = 1e-6
GRID_W = 64
HG_DK = 128
POOL_WINDOWS = (2, 4, 8, 16)
MLA_HEADS = 16
MLA_NOPE = 128
MLA_ROPE = 64
MLA_V = 128
MLA_QK = MLA_NOPE + MLA_ROPE
QK_PAD = 256
BF16_ROWS = 16
VT_ROWS = MLA_V + BF16_ROWS
MLA_SCALE = MLA_QK ** -0.5
LOG2_E = 1.4426950408889634
ROPE_FREQ = MLA_ROPE // 4
ROPE_BASE = 10000.0

LANES = 128
SUBLANES = 8
VMEM_LIMIT = 48 * 1024 * 1024

HG_CHUNK = 64
TOK_TILE = 256
IN_PROJ_ROWS = 2048
IN_PROJ_COL_PARTS = 8
KV_CHUNK = 512
Q_TILE = 4096
Q_SUB = 256
POOL_HALO = 8


def _dot(a, b):
    return jnp.dot(a, b, preferred_element_type=F32)


def _dot_nt(a, b):
    return lax.dot_general(a, b, (((1,), (1,)), ((), ())), preferred_element_type=F32)


def _dot_tn(a, b):
    return lax.dot_general(a, b, (((0,), (0,)), ((), ())), preferred_element_type=F32)


def _silu(x):
    h = 0.5 * x
    return h + h * jnp.tanh(h)


def _split_bf16(x):
    hi = x.astype(BF16)
    lo = (x - hi.astype(F32)).astype(BF16)
    return hi, lo


def _params(*sem):
    return pltpu.CompilerParams(dimension_semantics=sem, vmem_limit_bytes=VMEM_LIMIT)


def _ada_kernel(c_ref, w_ref, b_ref, o_ref):
    c = c_ref[...]
    s_hi, s_lo = _split_bf16(_silu(c))
    w_hi, w_lo = _split_bf16(w_ref[...])
    o_ref[...] = _dot(s_hi, w_hi) + _dot(s_lo, w_hi) + _dot(s_hi, w_lo) + b_ref[...]


def _ada(cond, ada_w, ada_b):
    depth, d, _ = ada_w.shape
    r = cond.shape[0]
    return pl.pallas_call(
        _ada_kernel,
        grid=(depth, 3),
        in_specs=[
            pl.BlockSpec((r, d), lambda l, j: (0, 0)),
            pl.BlockSpec((None, d, d), lambda l, j: (l, 0, j)),
            pl.BlockSpec((None, 1, d), lambda l, j: (l, 0, j)),
        ],
        out_specs=pl.BlockSpec((None, r, d), lambda l, j: (l, 0, j)),
        out_shape=jax.ShapeDtypeStruct((depth, r, 3 * d), F32),
        compiler_params=_params("parallel", "parallel"),
        name="ada_modulation",
    )(cond, ada_w, ada_b.reshape(depth, 1, 3 * d))


def _rms(x, g):
    return x * lax.rsqrt(jnp.mean(x * x, axis=-1, keepdims=True) + EPS) * g


def _modnorm_mm_kernel(x_ref, mod_ref, g_ref, w_ref, o_ref, z_ref):
    @pl.when(pl.program_id(2) == 0)
    def _():
        y = _rms(x_ref[...], g_ref[...])
        z_ref[...] = (y * (1.0 + mod_ref[1:2, :]) + mod_ref[0:1, :]).astype(BF16)

    o_ref[...] = _dot(z_ref[...], w_ref[...])


def _modnorm_mm(x, mod, g, w, tm, tn, name):
    bx, r, d = x.shape
    n = w.shape[1]
    tm = min(tm, r)
    return pl.pallas_call(
        _modnorm_mm_kernel,
        grid=(bx, r // tm, n // tn),
        in_specs=[
            pl.BlockSpec((None, tm, d), lambda b, i, j: (b, i, 0)),
            pl.BlockSpec((None, 3, d), lambda b, i, j: (b, 0, 0)),
            pl.BlockSpec((1, d), lambda b, i, j: (0, 0)),
            pl.BlockSpec((d, tn), lambda b, i, j: (0, j)),
        ],
        out_specs=pl.BlockSpec((None, tm, tn), lambda b, i, j: (b, i, j)),
        out_shape=jax.ShapeDtypeStruct((bx, r, n), F32),
        scratch_shapes=[pltpu.VMEM((tm, d), BF16)],
        compiler_params=_params("parallel", "parallel", "arbitrary"),
        name=name,
    )(x, mod, g, w)


def _decay_bounds_kernel(a_ref, o_ref, *, layer):
    slots = a_ref[...]
    e = jnp.exp(slots - jnp.max(slots, axis=1, keepdims=True))
    o_ref[...] = jnp.sum(e[:, :layer + 1], axis=1) / jnp.sum(e, axis=1)


def _decay_bounds(hg_lb, layer):
    ndir, nslot, w = hg_lb.shape
    return pl.pallas_call(
        functools.partial(_decay_bounds_kernel, layer=layer),
        out_shape=jax.ShapeDtypeStruct((ndir, 1, w), F32),
        name="decay_bounds",
    )(hg_lb.reshape(ndir, nslot, 1, w))


def _hgrn_levels(c):
    w = c // 2
    out = []
    while w >= 1:
        out.append(w)
        w //= 2
    return tuple(out)


def _hgrn_constants(c):
    t = np.arange(c)
    tri = np.tril(np.ones((c, c), np.float32))
    masks = []
    for w in _hgrn_levels(c):
        blk = t // (2 * w)
        first = (t % (2 * w)) < w
        masks.append(((blk[:, None] == blk[None, :]) & (~first[:, None]) & first[None, :]).astype(np.float32))
    masks = np.stack(masks)
    tri2 = np.stack([tri, tri[::-1, ::-1]])
    m2 = np.stack([masks, masks[:, ::-1, ::-1]])
    return jnp.asarray(tri2, BF16), jnp.asarray(m2, F32)


def _hgrn_level_operand(b, g, k, q, w, d):
    c, width = b.shape
    row = lax.broadcasted_iota(jnp.int32, (c, 1), 0)
    keys_first = d == 0
    if w >= SUBLANES:
        ref_off = w - 1 if d == 0 else w
        pieces = []
        for r0 in range(0, c, 2 * w):
            bref = jnp.broadcast_to(b[r0 + ref_off:r0 + ref_off + 1, :], (w, width))
            for half in range(2):
                sl = slice(r0 + half * w, r0 + (half + 1) * w)
                if (half == 0) == keys_first:
                    pieces.append(k[sl] * jnp.exp2(bref - b[sl]))
                else:
                    pieces.append(q[sl] * jnp.exp2(b[sl] - bref))
        return jnp.concatenate(pieces, axis=0)
    before = (row % (2 * w)) < w
    kq = jnp.where(before == keys_first, k, q)
    if w == 1:
        moving = (row % 2 == 1) if d == 0 else (row % 2 == 0)
        return kq * jnp.exp2(jnp.where(moving, g, 0.0))
    ref_off = w - 1 if d == 0 else w
    sub = lax.broadcasted_iota(jnp.int32, (SUBLANES, 1), 0)
    pieces = []
    for r0 in range(0, c, SUBLANES):
        lo = jnp.broadcast_to(b[r0 + ref_off:r0 + ref_off + 1, :], (SUBLANES, width))
        if 2 * w == SUBLANES:
            pieces.append(lo)
        else:
            hi = jnp.broadcast_to(b[r0 + 2 * w + ref_off:r0 + 2 * w + ref_off + 1, :], (SUBLANES, width))
            pieces.append(jnp.where(sub < 2 * w, lo, hi))
    bref = jnp.concatenate(pieces, axis=0)
    return kq * jnp.exp2(-jnp.abs(b - bref))


def _hgrn_wide(q_ref, f_ref, v_ref, lb, tri, d, r0, *, c):
    last = c - 1 if d == 0 else 0
    rows = pl.ds(r0, c)
    half = 0.5 * (1.0 - lb)
    f = (lb + half) + half * jnp.tanh(0.5 * f_ref[rows, :])
    g = jnp.log2(f)
    g_hi, g_lo = _split_bf16(g)
    b = _dot(tri, g_hi) + _dot(tri, g_lo)
    q = _silu(q_ref[rows, :])
    k = 1.0 - f
    v = v_ref[rows, :]
    bl = b[last:last + 1, :]
    return dict(d=d, rows=rows, g=g, b=b, q=q, k=k, v=v, vb=v.astype(BF16), qk=q * k,
                qe=(q * jnp.exp2(b)).astype(BF16), kend=(k * jnp.exp2(bl - b)).astype(BF16), ebl=jnp.exp2(bl))


def _head(x, h):
    return x[:, h * HG_DK:(h + 1) * HG_DK]


def _hgrn_pairs(s, h, mk_ref, *, c):
    d = s["d"]
    att = None
    for l, w in enumerate(_hgrn_levels(c)):
        x = _hgrn_level_operand(_head(s["b"], h), _head(s["g"], h), _head(s["k"], h), _head(s["q"], h), w, d)
        x = x.astype(BF16)
        t = mk_ref[d, l] * _dot_nt(x, x)
        att = t if att is None else att + t
    return att


def _hgrn_finish(s, h, att, o_ref, st_ref):
    d = s["d"]
    vb = _head(s["vb"], h)
    inter = _dot_nt(_head(s["qe"], h), st_ref[d, h].astype(BF16))
    diag = jnp.sum(_head(s["qk"], h), axis=-1, keepdims=True)
    o_ref[s["rows"], h * HG_DK:(h + 1) * HG_DK] = inter + _dot(att.astype(BF16), vb) + diag * _head(s["v"], h)
    st_ref[d, h] = _head(s["ebl"], h) * st_ref[d, h] + _dot_tn(vb, _head(s["kend"], h))


def _hgrn_kernel(qf_ref, ff_ref, vf_ref, qb_ref, fb_ref, vb_ref, lb_ref, tri_ref, mk_ref, s0_ref,
                 of_ref, ob_ref, sout_ref, st_ref, *, tb, c, nh):
    nchunk = tb // c

    @pl.when(pl.program_id(1) == 0)
    def _():
        st_ref[...] = s0_ref[...]

    def body(cc, carry):
        rf = pl.multiple_of(cc * c, c)
        rb = pl.multiple_of((nchunk - 1 - cc) * c, c)
        sides = ((_hgrn_wide(qf_ref, ff_ref, vf_ref, lb_ref[0], tri_ref[0], 0, rf, c=c), of_ref),
                 (_hgrn_wide(qb_ref, fb_ref, vb_ref, lb_ref[1], tri_ref[1], 1, rb, c=c), ob_ref))
        pending = [None, None]
        for h in range(nh + 1):
            cur = [_hgrn_pairs(s, h, mk_ref, c=c) if h < nh else None for s, _ in sides]
            for (s, o_ref), p in zip(sides, pending):
                if p is not None:
                    _hgrn_finish(s, h - 1, p, o_ref, st_ref)
            pending = cur
        return carry

    lax.fori_loop(0, nchunk, body, 0, unroll=True)

    @pl.when(pl.program_id(1) == pl.num_programs(1) - 1)
    def _():
        sout_ref[...] = st_ref[...]


def _hgrn(p, lb, s0, consts):
    bsz, r, _ = p.shape
    w = lb.shape[-1]
    nh = w // HG_DK
    tb = min(TOK_TILE, r)
    c = HG_CHUNK
    nb = r // tb
    tri2, m2 = consts
    fwd = lambda col: pl.BlockSpec((None, tb, w), lambda b, s: (b, s, col))
    bwd = lambda col: pl.BlockSpec((None, tb, w), lambda b, s: (b, nb - 1 - s, col))
    const = lambda arr: pl.BlockSpec(arr.shape, lambda b, s: (0,) * arr.ndim)
    st_spec = pl.BlockSpec((None, 2, nh, HG_DK, HG_DK), lambda b, s: (b, 0, 0, 0, 0))
    kern = functools.partial(_hgrn_kernel, tb=tb, c=c, nh=nh)
    return pl.pallas_call(
        kern,
        grid=(bsz, nb),
        in_specs=[fwd(0), fwd(1), fwd(3), bwd(0), bwd(2), bwd(3), const(lb), const(tri2), const(m2),
                  st_spec],
        out_specs=[
            pl.BlockSpec((None, tb, w), lambda b, s: (b, s, 0)),
            pl.BlockSpec((None, tb, w), lambda b, s: (b, nb - 1 - s, 0)),
            st_spec,
        ],
        out_shape=[
            jax.ShapeDtypeStruct((bsz, r, w), F32),
            jax.ShapeDtypeStruct((bsz, r, w), F32),
            jax.ShapeDtypeStruct(s0.shape, F32),
        ],
        scratch_shapes=[pltpu.VMEM((2, nh, HG_DK, HG_DK), F32)],
        compiler_params=_params("parallel", "arbitrary"),
        name="hgrn2_scan",
    )(p, p, p, p, p, p, lb, tri2, m2, s0)


def _even_post_kernel(of_ref, ob_ref, ga_ref, u_ref, gb_ref, up_ref, un_ref, h_ref, mod_ref, hgn_ref, pw_ref,
                      ps_ref, ow_ref, modn_ref, gn_ref, wn_ref, o_ref, pn_ref, ext_ref, y_ref, z_ref, *, tb, seq, nh):
    step = pl.program_id(1)
    last = pl.num_programs(1) - 2
    j = jnp.minimum(step, last)
    w = nh * HG_DK

    @pl.when(step == 0)
    def _():
        z_ref[...] = jnp.zeros(z_ref.shape, BF16)

    n_parts = nh + len(POOL_WINDOWS)
    n_blk = pn_ref.shape[1] // LANES
    per = -(-n_blk // n_parts)

    def in_proj_part(i):
        c0 = min(i * per, n_blk) * LANES
        c1 = min((i + 1) * per, n_blk) * LANES
        if c0 < c1:
            pn_ref[:, c0:c1] = _dot(z_ref[...], wn_ref[:, c0:c1])

    o = of_ref[...] + ob_ref[...]
    for h in range(nh):
        in_proj_part(h)
        sl = slice(h * HG_DK, (h + 1) * HG_DK)
        y_ref[:, sl] = (_rms(o[:, sl], hgn_ref[:, sl]) * _silu(ga_ref[:, sl])).astype(BF16)
    u = u_ref[...]
    ext_ref[0:POOL_HALO, :] = jnp.where(j > 0, up_ref[...], 0.0)
    ext_ref[POOL_HALO:POOL_HALO + tb, :] = u
    ext_ref[POOL_HALO + tb:, :] = jnp.where(j < last, un_ref[...], 0.0)
    t = j * tb + lax.broadcasted_iota(jnp.int32, (tb, 1), 0)
    grp = w // len(POOL_WINDOWS)
    for gi, win in enumerate(POOL_WINDOWS):
        in_proj_part(nh + gi)
        sl = slice(gi * grp, (gi + 1) * grp)
        acc = ext_ref[POOL_HALO - win // 2:POOL_HALO - win // 2 + tb, sl]
        for off in range(-win // 2 + 1, win // 2):
            acc = acc + ext_ref[POOL_HALO + off:POOL_HALO + off + tb, sl]
        cnt = (jnp.minimum(t + win // 2, seq) - jnp.maximum(t - win // 2, 0)).astype(F32)
        yp = acc * (1.0 / cnt) - u[:, sl]
        yb = _dot(yp.astype(BF16), pw_ref[gi]) * ps_ref[:, sl]
        y_ref[:, w + gi * grp:w + (gi + 1) * grp] = (yb * _silu(gb_ref[:, sl])).astype(BF16)
    hn = h_ref[...] + mod_ref[2:3, :] * _dot(y_ref[...], ow_ref[...])
    o_ref[...] = hn
    z_ref[...] = (_rms(hn, gn_ref[...]) * (1.0 + modn_ref[1:2, :]) + modn_ref[0:1, :]).astype(BF16)


def _even_post(o_f, o_b, p, h, mod, hgn, pool_w, pool_scale, out_w, mod_n, g_n, w_n):
    bsz, r, w = o_f.shape
    d = h.shape[-1]
    tb = min(TOK_TILE, r)
    nb = r // tb
    hb = tb // POOL_HALO
    nh = w // HG_DK
    n_next = w_n.shape[1]
    cur = lambda s: jnp.minimum(s, nb - 1)
    tok = lambda col: pl.BlockSpec((None, tb, w), lambda b, s: (b, cur(s), col))
    const = lambda arr: pl.BlockSpec(arr.shape, lambda b, s: (0,) * arr.ndim)
    kern = functools.partial(_even_post_kernel, tb=tb, seq=r, nh=nh)
    return pl.pallas_call(
        kern,
        grid=(bsz, nb + 1),
        in_specs=[
            tok(0), tok(0), tok(4), tok(5), tok(6),
            pl.BlockSpec((None, POOL_HALO, w), lambda b, s: (b, jnp.maximum(cur(s) * hb - 1, 0), 5)),
            pl.BlockSpec((None, POOL_HALO, w), lambda b, s: (b, jnp.minimum((cur(s) + 1) * hb, nb * hb - 1), 5)),
            pl.BlockSpec((None, tb, d), lambda b, s: (b, cur(s), 0)),
            pl.BlockSpec((None, 3, d), lambda b, s: (b, 0, 0)),
            const(hgn), const(pool_w), const(pool_scale), const(out_w),
            pl.BlockSpec((None, 3, d), lambda b, s: (b, 0, 0)), const(g_n), const(w_n),
        ],
        out_specs=[pl.BlockSpec((None, tb, d), lambda b, s: (b, cur(s), 0)),
                   pl.BlockSpec((None, tb, n_next), lambda b, s: (b, jnp.maximum(s - 1, 0), 0))],
        out_shape=[jax.ShapeDtypeStruct((bsz, r, d), F32), jax.ShapeDtypeStruct((bsz, r, n_next), F32)],
        scratch_shapes=[pltpu.VMEM((tb + 2 * POOL_HALO, w), F32), pltpu.VMEM((tb, 2 * w), BF16),
                        pltpu.VMEM((tb, d), BF16)],
        compiler_params=_params("parallel", "arbitrary"),
        name="even_post",
    )(o_f, o_b, p, p, p, p, p, h, mod, hgn, pool_w, pool_scale, out_w, mod_n, g_n, w_n)


def _mla_kv_kernel(*refs, rope):
    if rope:
        ckv_ref, kr_ref, g_ref, wuk_ref, wuvt_ref, cos_ref, sin_ref, kcat_ref, vt_ref = refs
    else:
        ckv_ref, kr_ref, g_ref, wuk_ref, wuvt_ref, kcat_ref, vt_ref = refs
    cn = _rms(ckv_ref[...], g_ref[...]).astype(BF16)
    kn = _dot(cn, wuk_ref[...])
    kr = kr_ref[...]
    if rope:
        lane = lax.broadcasted_iota(jnp.int32, kr.shape, 1)
        swapped = jnp.where((lane % (2 * ROPE_FREQ)) < ROPE_FREQ,
                            pltpu.roll(kr, LANES - ROPE_FREQ, 1), pltpu.roll(kr, ROPE_FREQ, 1))
        kr = kr * cos_ref[...] + swapped * sin_ref[...]
    kr = kr.astype(BF16)
    ones_rows = (lax.broadcasted_iota(jnp.int32, (VT_ROWS - MLA_V, kr.shape[0]), 0) == 0).astype(BF16)
    vt = _dot_nt(wuvt_ref[...], cn)
    for h in range(MLA_HEADS):
        kcat_ref[h, :, 0:MLA_NOPE] = kn[:, h * MLA_NOPE:(h + 1) * MLA_NOPE].astype(BF16)
        kcat_ref[h, :, MLA_NOPE:] = kr
        vt_ref[h, 0:MLA_V, :] = vt[h * MLA_V:(h + 1) * MLA_V].astype(BF16)
        vt_ref[h, MLA_V:, :] = ones_rows


def _mla_kv(p, ckv_blk, kr_blk, g, wuk, wuvt, tables, tb):
    bsz, r, _ = p.shape
    rank = g.shape[-1]
    nb = r // tb
    const = lambda arr: pl.BlockSpec(arr.shape, lambda b, j: (0,) * arr.ndim)
    in_specs = [
        pl.BlockSpec((None, tb, rank), lambda b, j: (b, j, ckv_blk)),
        pl.BlockSpec((None, tb, LANES), lambda b, j: (b, j, kr_blk)),
        const(g), const(wuk), const(wuvt),
    ]
    args = [p, p, g, wuk, wuvt]
    if tables is not None:
        in_specs += [pl.BlockSpec((tb, LANES), lambda b, j: (j, 0))] * 2
        args += list(tables)
    return pl.pallas_call(
        functools.partial(_mla_kv_kernel, rope=tables is not None),
        grid=(bsz, nb),
        in_specs=in_specs,
        out_specs=[
            pl.BlockSpec((None, MLA_HEADS, None, tb, QK_PAD), lambda b, j: (b, 0, j, 0, 0)),
            pl.BlockSpec((None, MLA_HEADS, None, VT_ROWS, tb), lambda b, j: (b, 0, j, 0, 0)),
        ],
        out_shape=[
            jax.ShapeDtypeStruct((bsz, MLA_HEADS, nb, tb, QK_PAD), BF16),
            jax.ShapeDtypeStruct((bsz, MLA_HEADS, nb, VT_ROWS, tb), BF16),
        ],
        compiler_params=_params("parallel", "parallel"),
        name="mla_kv_rope" if tables is not None else "mla_kv",
    )(*args)


def _mla_q_kernel(cq_ref, g_ref, wqt_ref, cos_ref, sin_ref, qt_ref):
    cn = _rms(cq_ref[...], g_ref[...]).astype(BF16)
    f = ROPE_FREQ
    qt_all = _dot_nt(wqt_ref[...], cn) * (MLA_SCALE * LOG2_E)
    for h in range(MLA_HEADS):
        qt = qt_all[h * MLA_QK:(h + 1) * MLA_QK]
        qt_ref[h, 0:MLA_NOPE, :] = qt[0:MLA_NOPE].astype(BF16)
        for ax in range(2):
            r0 = MLA_NOPE + ax * 2 * f
            x1 = qt[r0:r0 + f]
            x2 = qt[r0 + f:r0 + 2 * f]
            co = cos_ref[ax]
            si = sin_ref[ax]
            qt_ref[h, r0:r0 + f, :] = (x1 * co - x2 * si).astype(BF16)
            qt_ref[h, r0 + f:r0 + 2 * f, :] = (x2 * co + x1 * si).astype(BF16)
        qt_ref[h, MLA_QK:, :] = jnp.zeros((QK_PAD - MLA_QK, cn.shape[0]), BF16)


def _mla_q(p, cq_blk, g, wqt, cos_t, sin_t):
    bsz, t, _ = p.shape
    rank = g.shape[-1]
    tm = min(2 * TOK_TILE, t)
    const = lambda arr: pl.BlockSpec(arr.shape, lambda b, j: (0,) * arr.ndim)
    tab = pl.BlockSpec((2, ROPE_FREQ, tm), lambda b, j: (0, 0, j))
    return pl.pallas_call(
        _mla_q_kernel,
        grid=(bsz, t // tm),
        in_specs=[pl.BlockSpec((None, tm, rank), lambda b, j: (b, j, cq_blk)), const(g), const(wqt), tab, tab],
        out_specs=pl.BlockSpec((None, MLA_HEADS, QK_PAD, tm), lambda b, j: (b, 0, 0, j)),
        out_shape=jax.ShapeDtypeStruct((bsz, MLA_HEADS, QK_PAD, t), BF16),
        compiler_params=_params("parallel", "parallel"),
        name="mla_q",
    )(p, g, wqt, cos_t, sin_t)


def _attn_kernel(qt_ref, kc_ref, vtc_ref, kl_ref, vtl_ref, g_ref, o_ref, m_ref, acc_ref, s_ref, mx_ref, *, tq, n_lat):
    nsub = tq // Q_SUB
    m_ref[...] = jnp.full(m_ref.shape, -jnp.inf, F32)
    acc_ref[...] = jnp.zeros(acc_ref.shape, F32)

    def scores(k, nxt, g):
        s = _dot(k, qt_ref[:, g * Q_SUB:(g + 1) * Q_SUB])
        s_ref[nxt, g, 0:k.shape[0], :] = s
        mx_ref[nxt, g] = jnp.max(s, axis=0, keepdims=True)

    def substep(k_next, vt_cur, cur, nxt):
        rows = vt_cur.shape[1]
        for g in range(nsub):
            sl = slice(g * Q_SUB, (g + 1) * Q_SUB)
            scores(k_next, nxt, g)
            m_old = m_ref[:, sl]
            m_new = jnp.maximum(m_old, mx_ref[cur, g])
            alpha = jnp.exp2(m_old - m_new)
            p = jnp.exp2(s_ref[cur, g, 0:rows, :] - m_new)
            acc_ref[:, sl] = alpha * acc_ref[:, sl] + _dot(vt_cur, p.astype(BF16))
            m_ref[:, sl] = m_new

    kc = kc_ref[...]
    for g in range(nsub):
        scores(kc, 0, g)
    substep(kl_ref[0], vtc_ref[...], 0, 1)

    def body(j, carry):
        for u in range(2):
            a = 2 * j + u
            substep(kl_ref[jnp.minimum(a + 1, n_lat - 1)], vtl_ref[a], (1 + u) % 2, u % 2)
        return carry

    lax.fori_loop(0, n_lat // 2, body, 0)
    o = (acc_ref[0:MLA_V, :] * (1.0 / acc_ref[MLA_V:MLA_V + 1, :])).T
    o_ref[...] = (o * _silu(g_ref[...])).astype(o_ref.dtype)


def _attn(qt, kc, vtc, kl, vtl, p):
    bsz, nh, _, t = qt.shape
    lc = kc.shape[3]
    tq = min(Q_TILE, t)
    n_lat = kl.shape[2]
    kv = kl.shape[3]
    assert kc.shape[2] == 1 and lc <= kv and n_lat % 2 == 0
    kern = functools.partial(_attn_kernel, tq=tq, n_lat=n_lat)
    ctx5 = lambda arr: pl.BlockSpec((None, None, None) + arr.shape[3:], lambda b, h, i: (b, h, 0, 0, 0))
    full5 = lambda arr: pl.BlockSpec((None, None) + arr.shape[2:], lambda b, h, i: (b, h, 0, 0, 0))
    return pl.pallas_call(
        kern,
        grid=(bsz, nh, t // tq),
        in_specs=[
            pl.BlockSpec((None, None, QK_PAD, tq), lambda b, h, i: (b, h, 0, i)),
            ctx5(kc), ctx5(vtc), full5(kl), full5(vtl),
            pl.BlockSpec((None, tq, MLA_V), lambda b, h, i: (b, i, h)),
        ],
        out_specs=pl.BlockSpec((None, tq, MLA_V), lambda b, h, i: (b, i, h)),
        out_shape=jax.ShapeDtypeStruct((bsz, t, nh * MLA_V), BF16),
        scratch_shapes=[pltpu.VMEM((1, tq), F32), pltpu.VMEM((VT_ROWS, tq), F32),
                        pltpu.VMEM((2, tq // Q_SUB, kv, Q_SUB), F32), pltpu.VMEM((2, tq // Q_SUB, 1, Q_SUB), F32)],
        compiler_params=_params("parallel", "parallel", "arbitrary"),
        name="mla_attention",
    )(qt, kc, vtc, kl, vtl, p)


def _out_final_kernel(y_ref, h_ref, mod_ref, ow_ref, g_ref, o_ref):
    hn = h_ref[...] + mod_ref[2:3, :] * _dot(y_ref[...], ow_ref[...])
    o_ref[...] = _rms(hn, g_ref[...])


def _out_final(y, h, mod, out_w, g):
    bsz, t, d = h.shape
    wi = y.shape[-1]
    tm = min(2 * TOK_TILE, t)
    return pl.pallas_call(
        _out_final_kernel,
        grid=(bsz, t // tm),
        in_specs=[
            pl.BlockSpec((None, tm, wi), lambda b, j: (b, j, 0)),
            pl.BlockSpec((None, tm, d), lambda b, j: (b, j, 0)),
            pl.BlockSpec((None, 3, d), lambda b, j: (b, 0, 0)),
            pl.BlockSpec((wi, d), lambda b, j: (0, 0)),
            pl.BlockSpec((1, d), lambda b, j: (0, 0)),
        ],
        out_specs=pl.BlockSpec((None, tm, d), lambda b, j: (b, j, 0)),
        out_shape=jax.ShapeDtypeStruct((bsz, t, d), F32),
        compiler_params=_params("parallel", "parallel"),
        name="out_final",
    )(y, h, mod, out_w, g)


def _rope_tables(n_tokens):
    rows = n_tokens // GRID_W
    pos_r = jnp.repeat(jnp.arange(rows), GRID_W).astype(F32)
    pos_c = jnp.tile(jnp.arange(GRID_W), rows).astype(F32)
    inv = ROPE_BASE ** (-2.0 * jnp.arange(ROPE_FREQ, dtype=F32) / (MLA_ROPE // 2))
    ang = jnp.stack([pos_r[:, None] * inv, pos_c[:, None] * inv], axis=1)
    cos, sin = jnp.cos(ang), jnp.sin(ang)
    pad = LANES - MLA_ROPE
    cos_k = jnp.pad(jnp.stack([cos, cos], axis=2).reshape(n_tokens, MLA_ROPE), ((0, 0), (0, pad)))
    sin_k = jnp.pad(jnp.stack([-sin, sin], axis=2).reshape(n_tokens, MLA_ROPE), ((0, 0), (0, pad)))
    cos_q = jnp.transpose(cos, (1, 2, 0))
    sin_q = jnp.transpose(sin, (1, 2, 0))
    return (cos_k, sin_k), (cos_q, sin_q)


def kernel(x, c, ctx, c_ctx, ada_w, ada_b, norm_g, out_w, ev_in_w, hg_lb, hg_norm_g, pool_w, pool_scale,
           od_in_w, qa_norm_g, qb_w, kva_norm_g, kvb_w, final_norm_g):
    bsz, t, d = x.shape
    lc = ctx.shape[1]
    depth = ada_w.shape[0]
    assert depth == 2 and t % (2 * TOK_TILE) == 0 and lc % TOK_TILE == 0 and t % GRID_W == 0
    w = hg_norm_g.shape[-1]
    nh = w // HG_DK
    q_rank = qa_norm_g.shape[-1]
    kv_rank = kva_norm_g.shape[-1]
    d_inner = out_w.shape[1]

    n_cond = -(-(bsz + 1) // SUBLANES) * SUBLANES
    cond = jnp.zeros((n_cond, d), F32).at[:bsz].set(c).at[bsz].set(c_ctx)
    mods = _ada(cond, ada_w, ada_b).reshape(depth, n_cond, 3, d)
    mod_l = [mods[l, :bsz] for l in range(depth)]
    mod_c = [mods[l, bsz:bsz + 1] for l in range(depth)]

    lb = _decay_bounds(hg_lb, 0)
    w_in0 = ev_in_w[0].astype(BF16)
    g0 = norm_g[0].reshape(1, d)
    ctx_flat = ctx.reshape(1, bsz * lc, d)
    n_in0 = w_in0.shape[1]
    tn0 = n_in0 // IN_PROJ_COL_PARTS
    p_c = _modnorm_mm(ctx_flat, mod_c[0], g0, w_in0, IN_PROJ_ROWS, tn0, "in_proj0_ctx").reshape(bsz, lc, n_in0)
    p_l = _modnorm_mm(x, mod_l[0], g0, w_in0, IN_PROJ_ROWS, tn0, "in_proj0")
    consts = _hgrn_constants(HG_CHUNK)
    s0 = jnp.zeros((bsz, 2, nh, HG_DK, HG_DK), F32)
    of_c, ob_c, s_c = _hgrn(p_c, lb, s0, consts)
    of_l, ob_l, _ = _hgrn(p_l, lb, s_c, consts)
    hgn = hg_norm_g[0].reshape(1, w)
    pw = pool_w[0].astype(BF16)
    ps = pool_scale[0].reshape(1, w)
    ow0 = out_w[0].astype(BF16)
    o1 = q_rank
    o2 = o1 + kv_rank
    o3 = o2 + MLA_ROPE
    w1 = od_in_w[0]
    kr_pad = jnp.zeros((d, LANES - MLA_ROPE), F32)
    w_in1 = jnp.concatenate([w1[:, o3:], w1[:, :o1], w1[:, o1:o2], w1[:, o2:o3], kr_pad], axis=1).astype(BF16)
    w_in1c = w_in1[:, d_inner + q_rank:]
    g1 = norm_g[1].reshape(1, d)
    bcast = lambda m: jnp.broadcast_to(m, (bsz, 3, d))
    _, p1_c = _even_post(of_c, ob_c, p_c, ctx, bcast(mod_c[0]), hgn, pw, ps, ow0, bcast(mod_c[1]), g1, w_in1c)
    hl1, p1_l = _even_post(of_l, ob_l, p_l, x, mod_l[0], hgn, pw, ps, ow0, mod_l[1], g1, w_in1)

    kvw = kvb_w[0].reshape(kv_rank, MLA_HEADS, MLA_NOPE + MLA_V)
    wuk = kvw[..., :MLA_NOPE].reshape(kv_rank, MLA_HEADS * MLA_NOPE).astype(BF16)
    wuvt = jnp.transpose(kvw[..., MLA_NOPE:], (1, 2, 0)).reshape(MLA_HEADS * MLA_V, kv_rank).astype(BF16)
    wqt = jnp.transpose(qb_w[0]).astype(BF16)
    kvg = kva_norm_g[0].reshape(1, kv_rank)
    qag = qa_norm_g[0].reshape(1, q_rank)
    tab_k, tab_q = _rope_tables(t)
    kc, vtc = _mla_kv(p1_c, 0, kv_rank // LANES, kvg, wuk, wuvt, None, lc)
    kl, vtl = _mla_kv(p1_l, (d_inner + q_rank) // kv_rank, (d_inner + q_rank + kv_rank) // LANES, kvg, wuk, wuvt,
                      tab_k, min(KV_CHUNK, t))
    qt = _mla_q(p1_l, d_inner // q_rank, qag, wqt, *tab_q)
    y = _attn(qt, kc, vtc, kl, vtl, p1_l)
    return _out_final(y, hl1, mod_l[1], out_w[1].astype(BF16), final_norm_g.reshape(1, d))
```

```python
import functools

import numpy as np
import jax
import jax.numpy as jnp
from jax import lax
from jax.experimental import pallas as pl
from jax.experimental.pallas import tpu as pltpu

F32 = jnp.float32
BF16 = jnp.bfloat16

EPS = 1e-6
GRID_W = 64
HG_DK = 128
POOL_WINDOWS = (2, 4, 8, 16)
MLA_HEADS = 16
MLA_NOPE = 128
MLA_ROPE = 64
MLA_V = 128
MLA_QK = MLA_NOPE + MLA_ROPE
QK_PAD = 256
BF16_ROWS = 16
VT_ROWS = MLA_V + BF16_ROWS
MLA_SCALE = MLA_QK ** -0.5
LOG2_E = 1.4426950408889634
ROPE_FREQ = MLA_ROPE // 4
ROPE_BASE = 10000.0

LANES = 128
SUBLANES = 8
VMEM_LIMIT = 48 * 1024 * 1024

HG_CHUNK = 64
TOK_TILE = 256
IN_PROJ_ROWS = 1024
IN_PROJ_COL_PARTS = 4
KV_CHUNK = 512
Q_TILE = 4096
Q_SUB = 256
POOL_HALO = 8


def _dot(a, b):
    return jnp.dot(a, b, preferred_element_type=F32)


def _dot_nt(a, b):
    return lax.dot_general(a, b, (((1,), (1,)), ((), ())), preferred_element_type=F32)


def _dot_tn(a, b):
    return lax.dot_general(a, b, (((0,), (0,)), ((), ())), preferred_element_type=F32)


def _silu(x):
    h = 0.5 * x
    return h + h * jnp.tanh(h)


def _split_bf16(x):
    hi = x.astype(BF16)
    lo = (x - hi.astype(F32)).astype(BF16)
    return hi, lo


def _params(*sem):
    return pltpu.CompilerParams(dimension_semantics=sem, vmem_limit_bytes=VMEM_LIMIT)


def _ada_kernel(c_ref, w_ref, b_ref, o_ref):
    c = c_ref[...]
    s_hi, s_lo = _split_bf16(_silu(c))
    w_hi, w_lo = _split_bf16(w_ref[...])
    o_ref[...] = _dot(s_hi, w_hi) + _dot(s_lo, w_hi) + _dot(s_hi, w_lo) + b_ref[...]


def _ada(cond, ada_w, ada_b):
    depth, d, _ = ada_w.shape
    r = cond.shape[0]
    return pl.pallas_call(
        _ada_kernel,
        grid=(depth, 3),
        in_specs=[
            pl.BlockSpec((r, d), lambda l, j: (0, 0)),
            pl.BlockSpec((None, d, d), lambda l, j: (l, 0, j)),
            pl.BlockSpec((None, 1, d), lambda l, j: (l, 0, j)),
        ],
        out_specs=pl.BlockSpec((None, r, d), lambda l, j: (l, 0, j)),
        out_shape=jax.ShapeDtypeStruct((depth, r, 3 * d), F32),
        compiler_params=_params("parallel", "parallel"),
        name="ada_modulation",
    )(cond, ada_w, ada_b.reshape(depth, 1, 3 * d))


def _rms(x, g):
    return x * lax.rsqrt(jnp.mean(x * x, axis=-1, keepdims=True) + EPS) * g


def _modnorm_mm_kernel(x_ref, mod_ref, g_ref, w_ref, o_ref, z_ref):
    @pl.when(pl.program_id(2) == 0)
    def _():
        y = _rms(x_ref[...], g_ref[...])
        z_ref[...] = (y * (1.0 + mod_ref[1:2, :]) + mod_ref[0:1, :]).astype(BF16)

    o_ref[...] = _dot(z_ref[...], w_ref[...])


def _modnorm_mm(x, mod, g, w, tm, tn, name):
    bx, r, d = x.shape
    n = w.shape[1]
    tm = min(tm, r)
    return pl.pallas_call(
        _modnorm_mm_kernel,
        grid=(bx, r // tm, n // tn),
        in_specs=[
            pl.BlockSpec((None, tm, d), lambda b, i, j: (b, i, 0)),
            pl.BlockSpec((None, 3, d), lambda b, i, j: (b, 0, 0)),
            pl.BlockSpec((1, d), lambda b, i, j: (0, 0)),
            pl.BlockSpec((d, tn), lambda b, i, j: (0, j)),
        ],
        out_specs=pl.BlockSpec((None, tm, tn), lambda b, i, j: (b, i, j)),
        out_shape=jax.ShapeDtypeStruct((bx, r, n), F32),
        scratch_shapes=[pltpu.VMEM((tm, d), BF16)],
        compiler_params=_params("parallel", "parallel", "arbitrary"),
        name=name,
    )(x, mod, g, w)


def _decay_bounds_kernel(a_ref, o_ref, *, layer):
    slots = a_ref[...]
    e = jnp.exp(slots - jnp.max(slots, axis=1, keepdims=True))
    o_ref[...] = jnp.sum(e[:, :layer + 1], axis=1) / jnp.sum(e, axis=1)


def _decay_bounds(hg_lb, layer):
    ndir, nslot, w = hg_lb.shape
    return pl.pallas_call(
        functools.partial(_decay_bounds_kernel, layer=layer),
        out_shape=jax.ShapeDtypeStruct((ndir, 1, w), F32),
        name="decay_bounds",
    )(hg_lb.reshape(ndir, nslot, 1, w))


def _hgrn_levels(c):
    w = c // 2
    out = []
    while w >= 1:
        out.append(w)
        w //= 2
    return tuple(out)


def _hgrn_constants(c):
    t = np.arange(c)
    tri = np.tril(np.ones((c, c), np.float32))
    masks = []
    for w in _hgrn_levels(c):
        blk = t // (2 * w)
        first = (t % (2 * w)) < w
        masks.append(((blk[:, None] == blk[None, :]) & (~first[:, None]) & first[None, :]).astype(np.float32))
    masks = np.stack(masks)
    tri2 = np.stack([tri, tri[::-1, ::-1]])
    m2 = np.stack([masks, masks[:, ::-1, ::-1]])
    return jnp.asarray(tri2, BF16), jnp.asarray(m2, F32)


def _hgrn_level_operand(b, g, k, q, w, d):
    c, width = b.shape
    row = lax.broadcasted_iota(jnp.int32, (c, 1), 0)
    keys_first = d == 0
    if w >= SUBLANES:
        ref_off = w - 1 if d == 0 else w
        pieces = []
        for r0 in range(0, c, 2 * w):
            bref = jnp.broadcast_to(b[r0 + ref_off:r0 + ref_off + 1, :], (w, width))
            for half in range(2):
                sl = slice(r0 + half * w, r0 + (half + 1) * w)
                if (half == 0) == keys_first:
                    pieces.append(k[sl] * jnp.exp2(bref - b[sl]))
                else:
                    pieces.append(q[sl] * jnp.exp2(b[sl] - bref))
        return jnp.concatenate(pieces, axis=0)
    before = (row % (2 * w)) < w
    kq = jnp.where(before == keys_first, k, q)
    if w == 1:
        moving = (row % 2 == 1) if d == 0 else (row % 2 == 0)
        return kq * jnp.exp2(jnp.where(moving, g, 0.0))
    ref_off = w - 1 if d == 0 else w
    sub = lax.broadcasted_iota(jnp.int32, (SUBLANES, 1), 0)
    pieces = []
    for r0 in range(0, c, SUBLANES):
        lo = jnp.broadcast_to(b[r0 + ref_off:r0 + ref_off + 1, :], (SUBLANES, width))
        if 2 * w == SUBLANES:
            pieces.append(lo)
        else:
            hi = jnp.broadcast_to(b[r0 + 2 * w + ref_off:r0 + 2 * w + ref_off + 1, :], (SUBLANES, width))
            pieces.append(jnp.where(sub < 2 * w, lo, hi))
    bref = jnp.concatenate(pieces, axis=0)
    return kq * jnp.exp2(-jnp.abs(b - bref))


def _hgrn_wide(q_ref, f_ref, v_ref, lb, tri, d, r0, *, c):
    last = c - 1 if d == 0 else 0
    rows = pl.ds(r0, c)
    half = 0.5 * (1.0 - lb)
    f = (lb + half) + half * jnp.tanh(0.5 * f_ref[rows, :])
    g = jnp.log2(f)
    g_hi, g_lo = _split_bf16(g)
    b = _dot(tri, g_hi) + _dot(tri, g_lo)
    q = _silu(q_ref[rows, :])
    k = 1.0 - f
    v = v_ref[rows, :]
    bl = b[last:last + 1, :]
    return dict(d=d, rows=rows, g=g, b=b, q=q, k=k, v=v, vb=v.astype(BF16), qk=q * k,
                qe=(q * jnp.exp2(b)).astype(BF16), kend=(k * jnp.exp2(bl - b)).astype(BF16), ebl=jnp.exp2(bl))


def _head(x, h):
    return x[:, h * HG_DK:(h + 1) * HG_DK]


def _hgrn_pairs(s, h, mk_ref, *, c):
    d = s["d"]
    att = None
    for l, w in enumerate(_hgrn_levels(c)):
        x = _hgrn_level_operand(_head(s["b"], h), _head(s["g"], h), _head(s["k"], h), _head(s["q"], h), w, d)
        x = x.astype(BF16)
        t = mk_ref[d, l] * _dot_nt(x, x)
        att = t if att is None else att + t
    return att


def _hgrn_finish(s, h, att, o_ref, st_ref):
    d = s["d"]
    vb = _head(s["vb"], h)
    inter = _dot_nt(_head(s["qe"], h), st_ref[d, h].astype(BF16))
    diag = jnp.sum(_head(s["qk"], h), axis=-1, keepdims=True)
    o_ref[s["rows"], h * HG_DK:(h + 1) * HG_DK] = inter + _dot(att.astype(BF16), vb) + diag * _head(s["v"], h)
    st_ref[d, h] = _head(s["ebl"], h) * st_ref[d, h] + _dot_tn(vb, _head(s["kend"], h))


def _hgrn_kernel(qf_ref, ff_ref, vf_ref, qb_ref, fb_ref, vb_ref, lb_ref, tri_ref, mk_ref, s0_ref,
                 of_ref, ob_ref, sout_ref, st_ref, *, tb, c, nh):
    nchunk = tb // c

    @pl.when(pl.program_id(1) == 0)
    def _():
        st_ref[...] = s0_ref[...]

    def body(cc, carry):
        rf = pl.multiple_of(cc * c, c)
        rb = pl.multiple_of((nchunk - 1 - cc) * c, c)
        sides = ((_hgrn_wide(qf_ref, ff_ref, vf_ref, lb_ref[0], tri_ref[0], 0, rf, c=c), of_ref),
                 (_hgrn_wide(qb_ref, fb_ref, vb_ref, lb_ref[1], tri_ref[1], 1, rb, c=c), ob_ref))
        pending = [None, None]
        for h in range(nh + 1):
            cur = [_hgrn_pairs(s, h, mk_ref, c=c) if h < nh else None for s, _ in sides]
            for (s, o_ref), p in zip(sides, pending):
                if p is not None:
                    _hgrn_finish(s, h - 1, p, o_ref, st_ref)
            pending = cur
        return carry

    lax.fori_loop(0, nchunk, body, 0, unroll=True)

    @pl.when(pl.program_id(1) == pl.num_programs(1) - 1)
    def _():
        sout_ref[...] = st_ref[...]


def _hgrn(p, lb, s0, consts):
    bsz, r, _ = p.shape
    w = lb.shape[-1]
    nh = w // HG_DK
    tb = min(TOK_TILE, r)
    c = HG_CHUNK
    nb = r // tb
    tri2, m2 = consts
    fwd = lambda col: pl.BlockSpec((None, tb, w), lambda b, s: (b, s, col))
    bwd = lambda col: pl.BlockSpec((None, tb, w), lambda b, s: (b, nb - 1 - s, col))
    const = lambda arr: pl.BlockSpec(arr.shape, lambda b, s: (0,) * arr.ndim)
    st_spec = pl.BlockSpec((None, 2, nh, HG_DK, HG_DK), lambda b, s: (b, 0, 0, 0, 0))
    kern = functools.partial(_hgrn_kernel, tb=tb, c=c, nh=nh)
    return pl.pallas_call(
        kern,
        grid=(bsz, nb),
        in_specs=[fwd(0), fwd(1), fwd(3), bwd(0), bwd(2), bwd(3), const(lb), const(tri2), const(m2),
                  st_spec],
        out_specs=[
            pl.BlockSpec((None, tb, w), lambda b, s: (b, s, 0)),
            pl.BlockSpec((None, tb, w), lambda b, s: (b, nb - 1 - s, 0)),
            st_spec,
        ],
        out_shape=[
            jax.ShapeDtypeStruct((bsz, r, w), F32),
            jax.ShapeDtypeStruct((bsz, r, w), F32),
            jax.ShapeDtypeStruct(s0.shape, F32),
        ],
        scratch_shapes=[pltpu.VMEM((2, nh, HG_DK, HG_DK), F32)],
        compiler_params=_params("parallel", "arbitrary"),
        name="hgrn2_scan",
    )(p, p, p, p, p, p, lb, tri2, m2, s0)


def _even_post_kernel(of_ref, ob_ref, ga_ref, u_ref, gb_ref, up_ref, un_ref, h_ref, mod_ref, hgn_ref, pw_ref,
                      ps_ref, ow_ref, modn_ref, gn_ref, wn_ref, o_ref, pn_ref, ext_ref, y_ref, z_ref, *, tb, seq, nh):
    step = pl.program_id(1)
    last = pl.num_programs(1) - 2
    j = jnp.minimum(step, last)
    w = nh * HG_DK

    @pl.when(step == 0)
    def _():
        z_ref[...] = jnp.zeros(z_ref.shape, BF16)

    n_parts = nh + len(POOL_WINDOWS)
    n_blk = pn_ref.shape[1] // LANES
    per = -(-n_blk // n_parts)

    def in_proj_part(i):
        c0 = min(i * per, n_blk) * LANES
        c1 = min((i + 1) * per, n_blk) * LANES
        if c0 < c1:
            pn_ref[:, c0:c1] = _dot(z_ref[...], wn_ref[:, c0:c1])

    o = of_ref[...] + ob_ref[...]
    for h in range(nh):
        in_proj_part(h)
        sl = slice(h * HG_DK, (h + 1) * HG_DK)
        y_ref[:, sl] = (_rms(o[:, sl], hgn_ref[:, sl]) * _silu(ga_ref[:, sl])).astype(BF16)
    u = u_ref[...]
    ext_ref[0:POOL_HALO, :] = jnp.where(j > 0, up_ref[...], 0.0)
    ext_ref[POOL_HALO:POOL_HALO + tb, :] = u
    ext_ref[POOL_HALO + tb:, :] = jnp.where(j < last, un_ref[...], 0.0)
    t = j * tb + lax.broadcasted_iota(jnp.int32, (tb, 1), 0)
    grp = w // len(POOL_WINDOWS)
    for gi, win in enumerate(POOL_WINDOWS):
        in_proj_part(nh + gi)
        sl = slice(gi * grp, (gi + 1) * grp)
        acc = ext_ref[POOL_HALO - win // 2:POOL_HALO - win // 2 + tb, sl]
        for off in range(-win // 2 + 1, win // 2):
            acc = acc + ext_ref[POOL_HALO + off:POOL_HALO + off + tb, sl]
        cnt = (jnp.minimum(t + win // 2, seq) - jnp.maximum(t - win // 2, 0)).astype(F32)
        yp = acc * (1.0 / cnt) - u[:, sl]
        yb = _dot(yp.astype(BF16), pw_ref[gi]) * ps_ref[:, sl]
        y_ref[:, w + gi * grp:w + (gi + 1) * grp] = (yb * _silu(gb_ref[:, sl])).astype(BF16)
    hn = h_ref[...] + mod_ref[2:3, :] * _dot(y_ref[...], ow_ref[...])
    o_ref[...] = hn
    z_ref[...] = (_rms(hn, gn_ref[...]) * (1.0 + modn_ref[1:2, :]) + modn_ref[0:1, :]).astype(BF16)


def _even_post(o_f, o_b, p, h, mod, hgn, pool_w, pool_scale, out_w, mod_n, g_n, w_n):
    bsz, r, w = o_f.shape
    d = h.shape[-1]
    tb = min(TOK_TILE, r)
    nb = r // tb
    hb = tb // POOL_HALO
    nh = w // HG_DK
    n_next = w_n.shape[1]
    cur = lambda s: jnp.minimum(s, nb - 1)
    tok = lambda col: pl.BlockSpec((None, tb, w), lambda b, s: (b, cur(s), col))
    const = lambda arr: pl.BlockSpec(arr.shape, lambda b, s: (0,) * arr.ndim)
    kern = functools.partial(_even_post_kernel, tb=tb, seq=r, nh=nh)
    return pl.pallas_call(
        kern,
        grid=(bsz, nb + 1),
        in_specs=[
            tok(0), tok(0), tok(4), tok(5), tok(6),
            pl.BlockSpec((None, POOL_HALO, w), lambda b, s: (b, jnp.maximum(cur(s) * hb - 1, 0), 5)),
            pl.BlockSpec((None, POOL_HALO, w), lambda b, s: (b, jnp.minimum((cur(s) + 1) * hb, nb * hb - 1), 5)),
            pl.BlockSpec((None, tb, d), lambda b, s: (b, cur(s), 0)),
            pl.BlockSpec((None, 3, d), lambda b, s: (b, 0, 0)),
            const(hgn), const(pool_w), const(pool_scale), const(out_w),
            pl.BlockSpec((None, 3, d), lambda b, s: (b, 0, 0)), const(g_n), const(w_n),
        ],
        out_specs=[pl.BlockSpec((None, tb, d), lambda b, s: (b, cur(s), 0)),
                   pl.BlockSpec((None, tb, n_next), lambda b, s: (b, jnp.maximum(s - 1, 0), 0))],
        out_shape=[jax.ShapeDtypeStruct((bsz, r, d), F32), jax.ShapeDtypeStruct((bsz, r, n_next), F32)],
        scratch_shapes=[pltpu.VMEM((tb + 2 * POOL_HALO, w), F32), pltpu.VMEM((tb, 2 * w), BF16),
                        pltpu.VMEM((tb, d), BF16)],
        compiler_params=_params("parallel", "arbitrary"),
        name="even_post",
    )(o_f, o_b, p, p, p, p, p, h, mod, hgn, pool_w, pool_scale, out_w, mod_n, g_n, w_n)


def _mla_kv_kernel(*refs, rope):
    if rope:
        ckv_ref, kr_ref, g_ref, wuk_ref, wuvt_ref, cos_ref, sin_ref, kcat_ref, vt_ref = refs
    else:
        ckv_ref, kr_ref, g_ref, wuk_ref, wuvt_ref, kcat_ref, vt_ref = refs
    cn = _rms(ckv_ref[...], g_ref[...]).astype(BF16)
    kn = _dot(cn, wuk_ref[...])
    kr = kr_ref[...]
    if rope:
        lane = lax.broadcasted_iota(jnp.int32, kr.shape, 1)
        swapped = jnp.where((lane % (2 * ROPE_FREQ)) < ROPE_FREQ,
                            pltpu.roll(kr, LANES - ROPE_FREQ, 1), pltpu.roll(kr, ROPE_FREQ, 1))
        kr = kr * cos_ref[...] + swapped * sin_ref[...]
    kr = kr.astype(BF16)
    ones_rows = (lax.broadcasted_iota(jnp.int32, (VT_ROWS - MLA_V, kr.shape[0]), 0) == 0).astype(BF16)
    vt = _dot_nt(wuvt_ref[...], cn)
    for h in range(MLA_HEADS):
        kcat_ref[h, :, 0:MLA_NOPE] = kn[:, h * MLA_NOPE:(h + 1) * MLA_NOPE].astype(BF16)
        kcat_ref[h, :, MLA_NOPE:] = kr
        vt_ref[h, 0:MLA_V, :] = vt[h * MLA_V:(h + 1) * MLA_V].astype(BF16)
        vt_ref[h, MLA_V:, :] = ones_rows


def _mla_kv(p, ckv_blk, kr_blk, g, wuk, wuvt, tables, tb):
    bsz, r, _ = p.shape
    rank = g.shape[-1]
    nb = r // tb
    const = lambda arr: pl.BlockSpec(arr.shape, lambda b, j: (0,) * arr.ndim)
    in_specs = [
        pl.BlockSpec((None, tb, rank), lambda b, j: (b, j, ckv_blk)),
        pl.BlockSpec((None, tb, LANES), lambda b, j: (b, j, kr_blk)),
        const(g), const(wuk), const(wuvt),
    ]
    args = [p, p, g, wuk, wuvt]
    if tables is not None:
        in_specs += [pl.BlockSpec((tb, LANES), lambda b, j: (j, 0))] * 2
        args += list(tables)
    return pl.pallas_call(
        functools.partial(_mla_kv_kernel, rope=tables is not None),
        grid=(bsz, nb),
        in_specs=in_specs,
        out_specs=[
            pl.BlockSpec((None, MLA_HEADS, None, tb, QK_PAD), lambda b, j: (b, 0, j, 0, 0)),
            pl.BlockSpec((None, MLA_HEADS, None, VT_ROWS, tb), lambda b, j: (b, 0, j, 0, 0)),
        ],
        out_shape=[
            jax.ShapeDtypeStruct((bsz, MLA_HEADS, nb, tb, QK_PAD), BF16),
            jax.ShapeDtypeStruct((bsz, MLA_HEADS, nb, VT_ROWS, tb), BF16),
        ],
        compiler_params=_params("parallel", "parallel"),
        name="mla_kv_rope" if tables is not None else "mla_kv",
    )(*args)


def _mla_q_kernel(cq_ref, g_ref, wqt_ref, cos_ref, sin_ref, qt_ref):
    cn = _rms(cq_ref[...], g_ref[...]).astype(BF16)
    f = ROPE_FREQ
    qt_all = _dot_nt(wqt_ref[...], cn) * (MLA_SCALE * LOG2_E)
    for h in range(MLA_HEADS):
        qt = qt_all[h * MLA_QK:(h + 1) * MLA_QK]
        qt_ref[h, 0:MLA_NOPE, :] = qt[0:MLA_NOPE].astype(BF16)
        for ax in range(2):
            r0 = MLA_NOPE + ax * 2 * f
            x1 = qt[r0:r0 + f]
            x2 = qt[r0 + f:r0 + 2 * f]
            co = cos_ref[ax]
            si = sin_ref[ax]
            qt_ref[h, r0:r0 + f, :] = (x1 * co - x2 * si).astype(BF16)
            qt_ref[h, r0 + f:r0 + 2 * f, :] = (x2 * co + x1 * si).astype(BF16)
        qt_ref[h, MLA_QK:, :] = jnp.zeros((QK_PAD - MLA_QK, cn.shape[0]), BF16)


def _mla_q(p, cq_blk, g, wqt, cos_t, sin_t):
    bsz, t, _ = p.shape
    rank = g.shape[-1]
    tm = min(2 * TOK_TILE, t)
    const = lambda arr: pl.BlockSpec(arr.shape, lambda b, j: (0,) * arr.ndim)
    tab = pl.BlockSpec((2, ROPE_FREQ, tm), lambda b, j: (0, 0, j))
    return pl.pallas_call(
        _mla_q_kernel,
        grid=(bsz, t // tm),
        in_specs=[pl.BlockSpec((None, tm, rank), lambda b, j: (b, j, cq_blk)), const(g), const(wqt), tab, tab],
        out_specs=pl.BlockSpec((None, MLA_HEADS, QK_PAD, tm), lambda b, j: (b, 0, 0, j)),
        out_shape=jax.ShapeDtypeStruct((bsz, MLA_HEADS, QK_PAD, t), BF16),
        compiler_params=_params("parallel", "parallel"),
        name="mla_q",
    )(p, g, wqt, cos_t, sin_t)


def _attn_kernel(qt_ref, kc_ref, vtc_ref, kl_ref, vtl_ref, g_ref, o_ref, m_ref, acc_ref, s_ref, mx_ref, *, tq, n_lat):
    nsub = tq // Q_SUB
    m_ref[...] = jnp.full(m_ref.shape, -jnp.inf, F32)
    acc_ref[...] = jnp.zeros(acc_ref.shape, F32)

    def scores(k, nxt, g):
        s = _dot(k, qt_ref[:, g * Q_SUB:(g + 1) * Q_SUB])
        s_ref[nxt, g, 0:k.shape[0], :] = s
        mx_ref[nxt, g] = jnp.max(s, axis=0, keepdims=True)

    def substep(k_next, vt_cur, cur, nxt):
        rows = vt_cur.shape[1]
        for g in range(nsub):
            sl = slice(g * Q_SUB, (g + 1) * Q_SUB)
            scores(k_next, nxt, g)
            m_old = m_ref[:, sl]
            m_new = jnp.maximum(m_old, mx_ref[cur, g])
            alpha = jnp.exp2(m_old - m_new)
            p = jnp.exp2(s_ref[cur, g, 0:rows, :] - m_new)
            acc_ref[:, sl] = alpha * acc_ref[:, sl] + _dot(vt_cur, p.astype(BF16))
            m_ref[:, sl] = m_new

    kc = kc_ref[...]
    for g in range(nsub):
        scores(kc, 0, g)
    substep(kl_ref[0], vtc_ref[...], 0, 1)

    per_trip = 4 if n_lat % 4 == 0 else 2

    def body(j, carry):
        for u in range(per_trip):
            a = per_trip * j + u
            substep(kl_ref[jnp.minimum(a + 1, n_lat - 1)], vtl_ref[a], (1 + u) % 2, u % 2)
        return carry

    lax.fori_loop(0, n_lat // per_trip, body, 0)
    o = (acc_ref[0:MLA_V, :] * (1.0 / acc_ref[MLA_V:MLA_V + 1, :])).T
    o_ref[...] = (o * _silu(g_ref[...])).astype(o_ref.dtype)


def _attn(qt, kc, vtc, kl, vtl, p):
    bsz, nh, _, t = qt.shape
    lc = kc.shape[3]
    tq = min(Q_TILE, t)
    n_lat = kl.shape[2]
    kv = kl.shape[3]
    assert kc.shape[2] == 1 and lc <= kv and n_lat % 2 == 0
    kern = functools.partial(_attn_kernel, tq=tq, n_lat=n_lat)
    ctx5 = lambda arr: pl.BlockSpec((None, None, None) + arr.shape[3:], lambda b, h, i: (b, h, 0, 0, 0))
    full5 = lambda arr: pl.BlockSpec((None, None) + arr.shape[2:], lambda b, h, i: (b, h, 0, 0, 0))
    return pl.pallas_call(
        kern,
        grid=(bsz, nh, t // tq),
        in_specs=[
            pl.BlockSpec((None, None, QK_PAD, tq), lambda b, h, i: (b, h, 0, i)),
            ctx5(kc), ctx5(vtc), full5(kl), full5(vtl),
            pl.BlockSpec((None, tq, MLA_V), lambda b, h, i: (b, i, h)),
        ],
        out_specs=pl.BlockSpec((None, tq, MLA_V), lambda b, h, i: (b, i, h)),
        out_shape=jax.ShapeDtypeStruct((bsz, t, nh * MLA_V), BF16),
        scratch_shapes=[pltpu.VMEM((1, tq), F32), pltpu.VMEM((VT_ROWS, tq), F32),
                        pltpu.VMEM((2, tq // Q_SUB, kv, Q_SUB), F32), pltpu.VMEM((2, tq // Q_SUB, 1, Q_SUB), F32)],
        compiler_params=_params("parallel", "parallel", "arbitrary"),
        name="mla_attention",
    )(qt, kc, vtc, kl, vtl, p)


def _out_final_kernel(y_ref, h_ref, mod_ref, ow_ref, g_ref, o_ref):
    hn = h_ref[...] + mod_ref[2:3, :] * _dot(y_ref[...], ow_ref[...])
    o_ref[...] = _rms(hn, g_ref[...])


def _out_final(y, h, mod, out_w, g):
    bsz, t, d = h.shape
    wi = y.shape[-1]
    tm = min(2 * TOK_TILE, t)
    return pl.pallas_call(
        _out_final_kernel,
        grid=(bsz, t // tm),
        in_specs=[
            pl.BlockSpec((None, tm, wi), lambda b, j: (b, j, 0)),
            pl.BlockSpec((None, tm, d), lambda b, j: (b, j, 0)),
            pl.BlockSpec((None, 3, d), lambda b, j: (b, 0, 0)),
            pl.BlockSpec((wi, d), lambda b, j: (0, 0)),
            pl.BlockSpec((1, d), lambda b, j: (0, 0)),
        ],
        out_specs=pl.BlockSpec((None, tm, d), lambda b, j: (b, j, 0)),
        out_shape=jax.ShapeDtypeStruct((bsz, t, d), F32),
        compiler_params=_params("parallel", "parallel"),
        name="out_final",
    )(y, h, mod, out_w, g)


def _rope_tables(n_tokens):
    rows = n_tokens // GRID_W
    pos_r = jnp.repeat(jnp.arange(rows), GRID_W).astype(F32)
    pos_c = jnp.tile(jnp.arange(GRID_W), rows).astype(F32)
    inv = ROPE_BASE ** (-2.0 * jnp.arange(ROPE_FREQ, dtype=F32) / (MLA_ROPE // 2))
    ang = jnp.stack([pos_r[:, None] * inv, pos_c[:, None] * inv], axis=1)
    cos, sin = jnp.cos(ang), jnp.sin(ang)
    pad = LANES - MLA_ROPE
    cos_k = jnp.pad(jnp.stack([cos, cos], axis=2).reshape(n_tokens, MLA_ROPE), ((0, 0), (0, pad)))
    sin_k = jnp.pad(jnp.stack([-sin, sin], axis=2).reshape(n_tokens, MLA_ROPE), ((0, 0), (0, pad)))
    cos_q = jnp.transpose(cos, (1, 2, 0))
    sin_q = jnp.transpose(sin, (1, 2, 0))
    return (cos_k, sin_k), (cos_q, sin_q)


def kernel(x, c, ctx, c_ctx, ada_w, ada_b, norm_g, out_w, ev_in_w, hg_lb, hg_norm_g, pool_w, pool_scale,
           od_in_w, qa_norm_g, qb_w, kva_norm_g, kvb_w, final_norm_g):
    bsz, t, d = x.shape
    lc = ctx.shape[1]
    depth = ada_w.shape[0]
    assert depth == 2 and t % (2 * TOK_TILE) == 0 and lc % TOK_TILE == 0 and t % GRID_W == 0
    w = hg_norm_g.shape[-1]
    nh = w // HG_DK
    q_rank = qa_norm_g.shape[-1]
    kv_rank = kva_norm_g.shape[-1]
    d_inner = out_w.shape[1]

    n_cond = -(-(bsz + 1) // SUBLANES) * SUBLANES
    cond = jnp.zeros((n_cond, d), F32).at[:bsz].set(c).at[bsz].set(c_ctx)
    mods = _ada(cond, ada_w, ada_b).reshape(depth, n_cond, 3, d)
    mod_l = [mods[l, :bsz] for l in range(depth)]
    mod_c = [mods[l, bsz:bsz + 1] for l in range(depth)]

    lb = _decay_bounds(hg_lb, 0)
    w_in0 = ev_in_w[0].astype(BF16)
    g0 = norm_g[0].reshape(1, d)
    ctx_flat = ctx.reshape(1, bsz * lc, d)
    n_in0 = w_in0.shape[1]
    tn0 = n_in0 // IN_PROJ_COL_PARTS
    p_c = _modnorm_mm(ctx_flat, mod_c[0], g0, w_in0, IN_PROJ_ROWS, tn0, "in_proj0_ctx").reshape(bsz, lc, n_in0)
    p_l = _modnorm_mm(x, mod_l[0], g0, w_in0, IN_PROJ_ROWS, tn0, "in_proj0")
    consts = _hgrn_constants(HG_CHUNK)
    s0 = jnp.zeros((bsz, 2, nh, HG_DK, HG_DK), F32)
    of_c, ob_c, s_c = _hgrn(p_c, lb, s0, consts)
    of_l, ob_l, _ = _hgrn(p_l, lb, s_c, consts)
    hgn = hg_norm_g[0].reshape(1, w)
    pw = pool_w[0].astype(BF16)
    ps = pool_scale[0].reshape(1, w)
    ow0 = out_w[0].astype(BF16)
    o1 = q_rank
    o2 = o1 + kv_rank
    o3 = o2 + MLA_ROPE
    w1 = od_in_w[0]
    kr_pad = jnp.zeros((d, LANES - MLA_ROPE), F32)
    w_in1 = jnp.concatenate([w1[:, o3:], w1[:, :o1], w1[:, o1:o2], w1[:, o2:o3], kr_pad], axis=1).astype(BF16)
    w_in1c = w_in1[:, d_inner + q_rank:]
    g1 = norm_g[1].reshape(1, d)
    bcast = lambda m: jnp.broadcast_to(m, (bsz, 3, d))
    _, p1_c = _even_post(of_c, ob_c, p_c, ctx, bcast(mod_c[0]), hgn, pw, ps, ow0, bcast(mod_c[1]), g1, w_in1c)
    hl1, p1_l = _even_post(of_l, ob_l, p_l, x, mod_l[0], hgn, pw, ps, ow0, mod_l[1], g1, w_in1)

    kvw = kvb_w[0].reshape(kv_rank, MLA_HEADS, MLA_NOPE + MLA_V)
    wuk = kvw[..., :MLA_NOPE].reshape(kv_rank, MLA_HEADS * MLA_NOPE).astype(BF16)
    wuvt = jnp.transpose(kvw[..., MLA_NOPE:], (1, 2, 0)).reshape(MLA_HEADS * MLA_V, kv_rank).astype(BF16)
    wqt = jnp.transpose(qb_w[0]).astype(BF16)
    kvg = kva_norm_g[0].reshape(1, kv_rank)
    qag = qa_norm_g[0].reshape(1, q_rank)
    tab_k, tab_q = _rope_tables(t)
    kc, vtc = _mla_kv(p1_c, 0, kv_rank // LANES, kvg, wuk, wuvt, None, lc)
    kl, vtl = _mla_kv(p1_l, (d_inner + q_rank) // kv_rank, (d_inner + q_rank + kv_rank) // LANES, kvg, wuk, wuvt,
                      tab_k, min(KV_CHUNK, t))
    qt = _mla_q(p1_l, d_inner // q_rank, qag, wqt, *tab_q)
    y = _attn(qt, kc, vtc, kl, vtl, p1_l)
    return _out_final(y, hl1, mod_l[1], out_w[1].astype(BF16), final_norm_g.reshape(1, d))
```

```python
import functools

import numpy as np
import jax
import jax.numpy as jnp
from jax import lax
from jax.experimental import pallas as pl
from jax.experimental.pallas import tpu as pltpu

F32 = jnp.float32
BF16 = jnp.bfloat16

EPS = 1e-6
GRID_W = 64
HG_DK = 128
POOL_WINDOWS = (2, 4, 8, 16)
MLA_HEADS = 16
MLA_NOPE = 128
MLA_ROPE = 64
MLA_V = 128
MLA_QK = MLA_NOPE + MLA_ROPE
QK_PAD = 256
BF16_ROWS = 16
VT_ROWS = MLA_V + BF16_ROWS
MLA_SCALE = MLA_QK ** -0.5
LOG2_E = 1.4426950408889634
ROPE_FREQ = MLA_ROPE // 4
ROPE_BASE = 10000.0

LANES = 128
SUBLANES = 8
VMEM_LIMIT = 48 * 1024 * 1024

HG_CHUNK = 64
TOK_TILE = 256
IN_PROJ_ROWS = 1024
IN_PROJ_COL_PARTS = 4
KV_CHUNK = 512
Q_TILE = 4096
Q_SUB = 256
POOL_HALO = 8


def _dot(a, b):
    return jnp.dot(a, b, preferred_element_type=F32)


def _dot_nt(a, b):
    return lax.dot_general(a, b, (((1,), (1,)), ((), ())), preferred_element_type=F32)


def _dot_tn(a, b):
    return lax.dot_general(a, b, (((0,), (0,)), ((), ())), preferred_element_type=F32)


def _silu(x):
    h = 0.5 * x
    return h + h * jnp.tanh(h)


def _split_bf16(x):
    hi = x.astype(BF16)
    lo = (x - hi.astype(F32)).astype(BF16)
    return hi, lo


def _params(*sem):
    return pltpu.CompilerParams(dimension_semantics=sem, vmem_limit_bytes=VMEM_LIMIT)


def _ada_kernel(c_ref, w_ref, b_ref, o_ref):
    c = c_ref[...]
    s_hi, s_lo = _split_bf16(_silu(c))
    w_hi, w_lo = _split_bf16(w_ref[...])
    o_ref[...] = _dot(s_hi, w_hi) + _dot(s_lo, w_hi) + _dot(s_hi, w_lo) + b_ref[...]


def _ada(cond, ada_w, ada_b):
    depth, d, _ = ada_w.shape
    r = cond.shape[0]
    return pl.pallas_call(
        _ada_kernel,
        grid=(depth, 3),
        in_specs=[
            pl.BlockSpec((r, d), lambda l, j: (0, 0)),
            pl.BlockSpec((None, d, d), lambda l, j: (l, 0, j)),
            pl.BlockSpec((None, 1, d), lambda l, j: (l, 0, j)),
        ],
        out_specs=pl.BlockSpec((None, r, d), lambda l, j: (l, 0, j)),
        out_shape=jax.ShapeDtypeStruct((depth, r, 3 * d), F32),
        compiler_params=_params("parallel", "parallel"),
        name="ada_modulation",
    )(cond, ada_w, ada_b.reshape(depth, 1, 3 * d))


def _rms(x, g):
    return x * lax.rsqrt(jnp.mean(x * x, axis=-1, keepdims=True) + EPS) * g


def _modnorm_mm_kernel(x_ref, mod_ref, g_ref, w_ref, o_ref, z_ref):
    @pl.when(pl.program_id(2) == 0)
    def _():
        y = _rms(x_ref[...], g_ref[...])
        z_ref[...] = (y * (1.0 + mod_ref[1:2, :]) + mod_ref[0:1, :]).astype(BF16)

    o_ref[...] = _dot(z_ref[...], w_ref[...])


def _modnorm_mm(x, mod, g, w, tm, tn, name):
    bx, r, d = x.shape
    n = w.shape[1]
    tm = min(tm, r)
    return pl.pallas_call(
        _modnorm_mm_kernel,
        grid=(bx, r // tm, n // tn),
        in_specs=[
            pl.BlockSpec((None, tm, d), lambda b, i, j: (b, i, 0)),
            pl.BlockSpec((None, 3, d), lambda b, i, j: (b, 0, 0)),
            pl.BlockSpec((1, d), lambda b, i, j: (0, 0)),
            pl.BlockSpec((d, tn), lambda b, i, j: (0, j)),
        ],
        out_specs=pl.BlockSpec((None, tm, tn), lambda b, i, j: (b, i, j)),
        out_shape=jax.ShapeDtypeStruct((bx, r, n), F32),
        scratch_shapes=[pltpu.VMEM((tm, d), BF16)],
        compiler_params=_params("parallel", "parallel", "arbitrary"),
        name=name,
    )(x, mod, g, w)


def _decay_bounds_kernel(a_ref, o_ref, *, layer):
    slots = a_ref[...]
    e = jnp.exp(slots - jnp.max(slots, axis=1, keepdims=True))
    o_ref[...] = jnp.sum(e[:, :layer + 1], axis=1) / jnp.sum(e, axis=1)


def _decay_bounds(hg_lb, layer):
    ndir, nslot, w = hg_lb.shape
    return pl.pallas_call(
        functools.partial(_decay_bounds_kernel, layer=layer),
        out_shape=jax.ShapeDtypeStruct((ndir, 1, w), F32),
        name="decay_bounds",
    )(hg_lb.reshape(ndir, nslot, 1, w))


def _hgrn_levels(c):
    w = c // 2
    out = []
    while w >= 1:
        out.append(w)
        w //= 2
    return tuple(out)


def _hgrn_constants(c):
    t = np.arange(c)
    tri = np.tril(np.ones((c, c), np.float32))
    masks = []
    for w in _hgrn_levels(c):
        blk = t // (2 * w)
        first = (t % (2 * w)) < w
        masks.append(((blk[:, None] == blk[None, :]) & (~first[:, None]) & first[None, :]).astype(np.float32))
    masks = np.stack(masks)
    tri2 = np.stack([tri, tri[::-1, ::-1]])
    m2 = np.stack([masks, masks[:, ::-1, ::-1]])
    return jnp.asarray(tri2, BF16), jnp.asarray(m2, F32)


def _hgrn_level_operand(b, g, k, q, w, d):
    c, width = b.shape
    row = lax.broadcasted_iota(jnp.int32, (c, 1), 0)
    keys_first = d == 0
    if w >= SUBLANES:
        ref_off = w - 1 if d == 0 else w
        pieces = []
        for r0 in range(0, c, 2 * w):
            bref = jnp.broadcast_to(b[r0 + ref_off:r0 + ref_off + 1, :], (w, width))
            for half in range(2):
                sl = slice(r0 + half * w, r0 + (half + 1) * w)
                if (half == 0) == keys_first:
                    pieces.append(k[sl] * jnp.exp2(bref - b[sl]))
                else:
                    pieces.append(q[sl] * jnp.exp2(b[sl] - bref))
        return jnp.concatenate(pieces, axis=0)
    before = (row % (2 * w)) < w
    kq = jnp.where(before == keys_first, k, q)
    if w == 1:
        moving = (row % 2 == 1) if d == 0 else (row % 2 == 0)
        return kq * jnp.exp2(jnp.where(moving, g, 0.0))
    ref_off = w - 1 if d == 0 else w
    sub = lax.broadcasted_iota(jnp.int32, (SUBLANES, 1), 0)
    pieces = []
    for r0 in range(0, c, SUBLANES):
        lo = jnp.broadcast_to(b[r0 + ref_off:r0 + ref_off + 1, :], (SUBLANES, width))
        if 2 * w == SUBLANES:
            pieces.append(lo)
        else:
            hi = jnp.broadcast_to(b[r0 + 2 * w + ref_off:r0 + 2 * w + ref_off + 1, :], (SUBLANES, width))
            pieces.append(jnp.where(sub < 2 * w, lo, hi))
    bref = jnp.concatenate(pieces, axis=0)
    return kq * jnp.exp2(-jnp.abs(b - bref))


def _hgrn_wide(q_ref, f_ref, v_ref, lb, tri, d, r0, *, c):
    last = c - 1 if d == 0 else 0
    rows = pl.ds(r0, c)
    half = 0.5 * (1.0 - lb)
    f = (lb + half) + half * jnp.tanh(0.5 * f_ref[rows, :])
    g = jnp.log2(f)
    g_hi, g_lo = _split_bf16(g)
    b = _dot(tri, g_hi) + _dot(tri, g_lo)
    q = _silu(q_ref[rows, :])
    k = 1.0 - f
    v = v_ref[rows, :]
    bl = b[last:last + 1, :]
    return dict(d=d, rows=rows, g=g, b=b, q=q, k=k, v=v, vb=v.astype(BF16), qk=q * k,
                qe=(q * jnp.exp2(b)).astype(BF16), kend=(k * jnp.exp2(bl - b)).astype(BF16), ebl=jnp.exp2(bl))


def _head(x, h):
    return x[:, h * HG_DK:(h + 1) * HG_DK]


def _hgrn_pairs(s, h, mk_ref, *, c):
    d = s["d"]
    att = None
    for l, w in enumerate(_hgrn_levels(c)):
        x = _hgrn_level_operand(_head(s["b"], h), _head(s["g"], h), _head(s["k"], h), _head(s["q"], h), w, d)
        x = x.astype(BF16)
        t = mk_ref[d, l] * _dot_nt(x, x)
        att = t if att is None else att + t
    return att


def _hgrn_finish(s, h, att, o_ref, st_ref):
    d = s["d"]
    vb = _head(s["vb"], h)
    inter = _dot_nt(_head(s["qe"], h), st_ref[d, h].astype(BF16))
    diag = jnp.sum(_head(s["qk"], h), axis=-1, keepdims=True)
    o_ref[s["rows"], h * HG_DK:(h + 1) * HG_DK] = inter + _dot(att.astype(BF16), vb) + diag * _head(s["v"], h)
    st_ref[d, h] = _head(s["ebl"], h) * st_ref[d, h] + _dot_tn(vb, _head(s["kend"], h))


def _hgrn_kernel(qf_ref, ff_ref, vf_ref, qb_ref, fb_ref, vb_ref, lb_ref, tri_ref, mk_ref, s0_ref,
                 of_ref, ob_ref, sout_ref, st_ref, *, tb, c, nh):
    nchunk = tb // c

    @pl.when(pl.program_id(1) == 0)
    def _():
        st_ref[...] = s0_ref[...]

    def body(cc, carry):
        rf = pl.multiple_of(cc * c, c)
        rb = pl.multiple_of((nchunk - 1 - cc) * c, c)
        sides = ((_hgrn_wide(qf_ref, ff_ref, vf_ref, lb_ref[0], tri_ref[0], 0, rf, c=c), of_ref),
                 (_hgrn_wide(qb_ref, fb_ref, vb_ref, lb_ref[1], tri_ref[1], 1, rb, c=c), ob_ref))
        pending = [None, None]
        for h in range(nh + 1):
            cur = [_hgrn_pairs(s, h, mk_ref, c=c) if h < nh else None for s, _ in sides]
            for (s, o_ref), p in zip(sides, pending):
                if p is not None:
                    _hgrn_finish(s, h - 1, p, o_ref, st_ref)
            pending = cur
        return carry

    lax.fori_loop(0, nchunk, body, 0, unroll=True)

    @pl.when(pl.program_id(1) == pl.num_programs(1) - 1)
    def _():
        sout_ref[...] = st_ref[...]


def _hgrn(p, lb, s0, consts):
    bsz, r, _ = p.shape
    w = lb.shape[-1]
    nh = w // HG_DK
    tb = min(TOK_TILE, r)
    c = HG_CHUNK
    nb = r // tb
    tri2, m2 = consts
    fwd = lambda col: pl.BlockSpec((None, tb, w), lambda b, s: (b, s, col))
    bwd = lambda col: pl.BlockSpec((None, tb, w), lambda b, s: (b, nb - 1 - s, col))
    const = lambda arr: pl.BlockSpec(arr.shape, lambda b, s: (0,) * arr.ndim)
    st_spec = pl.BlockSpec((None, 2, nh, HG_DK, HG_DK), lambda b, s: (b, 0, 0, 0, 0))
    kern = functools.partial(_hgrn_kernel, tb=tb, c=c, nh=nh)
    return pl.pallas_call(
        kern,
        grid=(bsz, nb),
        in_specs=[fwd(0), fwd(1), fwd(3), bwd(0), bwd(2), bwd(3), const(lb), const(tri2), const(m2),
                  st_spec],
        out_specs=[
            pl.BlockSpec((None, tb, w), lambda b, s: (b, s, 0)),
            pl.BlockSpec((None, tb, w), lambda b, s: (b, nb - 1 - s, 0)),
            st_spec,
        ],
        out_shape=[
            jax.ShapeDtypeStruct((bsz, r, w), F32),
            jax.ShapeDtypeStruct((bsz, r, w), F32),
            jax.ShapeDtypeStruct(s0.shape, F32),
        ],
        scratch_shapes=[pltpu.VMEM((2, nh, HG_DK, HG_DK), F32)],
        compiler_params=_params("parallel", "arbitrary"),
        name="hgrn2_scan",
    )(p, p, p, p, p, p, lb, tri2, m2, s0)


def _even_post_kernel(of_ref, ob_ref, ga_ref, u_ref, gb_ref, up_ref, un_ref, h_ref, mod_ref, hgn_ref, pw_ref,
                      ps_ref, ow_ref, modn_ref, gn_ref, wn_ref, o_ref, pn_ref, ext_ref, y_ref, z_ref, *, tb, seq, nh):
    step = pl.program_id(1)
    last = pl.num_programs(1) - 2
    j = jnp.minimum(step, last)
    w = nh * HG_DK

    @pl.when(step == 0)
    def _():
        z_ref[...] = jnp.zeros(z_ref.shape, BF16)

    n_parts = nh + len(POOL_WINDOWS)
    n_blk = pn_ref.shape[1] // LANES
    per = -(-n_blk // n_parts)

    def in_proj_part(i):
        c0 = min(i * per, n_blk) * LANES
        c1 = min((i + 1) * per, n_blk) * LANES
        if c0 < c1:
            pn_ref[:, c0:c1] = _dot(z_ref[...], wn_ref[:, c0:c1])

    o = of_ref[...] + ob_ref[...]
    for h in range(nh):
        in_proj_part(h)
        sl = slice(h * HG_DK, (h + 1) * HG_DK)
        y_ref[:, sl] = (_rms(o[:, sl], hgn_ref[:, sl]) * _silu(ga_ref[:, sl])).astype(BF16)
    u = u_ref[...]
    ext_ref[0:POOL_HALO, :] = jnp.where(j > 0, up_ref[...], 0.0)
    ext_ref[POOL_HALO:POOL_HALO + tb, :] = u
    ext_ref[POOL_HALO + tb:, :] = jnp.where(j < last, un_ref[...], 0.0)
    t = j * tb + lax.broadcasted_iota(jnp.int32, (tb, 1), 0)
    grp = w // len(POOL_WINDOWS)
    for gi, win in enumerate(POOL_WINDOWS):
        in_proj_part(nh + gi)
        sl = slice(gi * grp, (gi + 1) * grp)
        acc = ext_ref[POOL_HALO - win // 2:POOL_HALO - win // 2 + tb, sl]
        for off in range(-win // 2 + 1, win // 2):
            acc = acc + ext_ref[POOL_HALO + off:POOL_HALO + off + tb, sl]
        cnt = (jnp.minimum(t + win // 2, seq) - jnp.maximum(t - win // 2, 0)).astype(F32)
        yp = acc * (1.0 / cnt) - u[:, sl]
        yb = _dot(yp.astype(BF16), pw_ref[gi]) * ps_ref[:, sl]
        y_ref[:, w + gi * grp:w + (gi + 1) * grp] = (yb * _silu(gb_ref[:, sl])).astype(BF16)
    hn = h_ref[...] + mod_ref[2:3, :] * _dot(y_ref[...], ow_ref[...])
    o_ref[...] = hn
    z_ref[...] = (_rms(hn, gn_ref[...]) * (1.0 + modn_ref[1:2, :]) + modn_ref[0:1, :]).astype(BF16)


def _even_post(o_f, o_b, p, h, mod, hgn, pool_w, pool_scale, out_w, mod_n, g_n, w_n):
    bsz, r, w = o_f.shape
    d = h.shape[-1]
    tb = min(TOK_TILE, r)
    nb = r // tb
    hb = tb // POOL_HALO
    nh = w // HG_DK
    n_next = w_n.shape[1]
    cur = lambda s: jnp.minimum(s, nb - 1)
    tok = lambda col: pl.BlockSpec((None, tb, w), lambda b, s: (b, cur(s), col))
    const = lambda arr: pl.BlockSpec(arr.shape, lambda b, s: (0,) * arr.ndim)
    kern = functools.partial(_even_post_kernel, tb=tb, seq=r, nh=nh)
    return pl.pallas_call(
        kern,
        grid=(bsz, nb + 1),
        in_specs=[
            tok(0), tok(0), tok(4), tok(5), tok(6),
            pl.BlockSpec((None, POOL_HALO, w), lambda b, s: (b, jnp.maximum(cur(s) * hb - 1, 0), 5)),
            pl.BlockSpec((None, POOL_HALO, w), lambda b, s: (b, jnp.minimum((cur(s) + 1) * hb, nb * hb - 1), 5)),
            pl.BlockSpec((None, tb, d), lambda b, s: (b, cur(s), 0)),
            pl.BlockSpec((None, 3, d), lambda b, s: (b, 0, 0)),
            const(hgn), const(pool_w), const(pool_scale), const(out_w),
            pl.BlockSpec((None, 3, d), lambda b, s: (b, 0, 0)), const(g_n), const(w_n),
        ],
        out_specs=[pl.BlockSpec((None, tb, d), lambda b, s: (b, cur(s), 0)),
                   pl.BlockSpec((None, tb, n_next), lambda b, s: (b, jnp.maximum(s - 1, 0), 0))],
        out_shape=[jax.ShapeDtypeStruct((bsz, r, d), F32), jax.ShapeDtypeStruct((bsz, r, n_next), F32)],
        scratch_shapes=[pltpu.VMEM((tb + 2 * POOL_HALO, w), F32), pltpu.VMEM((tb, 2 * w), BF16),
                        pltpu.VMEM((tb, d), BF16)],
        compiler_params=_params("parallel", "arbitrary"),
        name="even_post",
    )(o_f, o_b, p, p, p, p, p, h, mod, hgn, pool_w, pool_scale, out_w, mod_n, g_n, w_n)


def _mla_kv_kernel(*refs, rope):
    if rope:
        ckv_ref, kr_ref, g_ref, wuk_ref, wuvt_ref, cos_ref, sin_ref, kcat_ref, vt_ref = refs
    else:
        ckv_ref, kr_ref, g_ref, wuk_ref, wuvt_ref, kcat_ref, vt_ref = refs
    cn = _rms(ckv_ref[...], g_ref[...]).astype(BF16)
    kn = _dot(cn, wuk_ref[...])
    kr = kr_ref[...]
    if rope:
        lane = lax.broadcasted_iota(jnp.int32, kr.shape, 1)
        swapped = jnp.where((lane % (2 * ROPE_FREQ)) < ROPE_FREQ,
                            pltpu.roll(kr, LANES - ROPE_FREQ, 1), pltpu.roll(kr, ROPE_FREQ, 1))
        kr = kr * cos_ref[...] + swapped * sin_ref[...]
    kr = kr.astype(BF16)
    ones_rows = (lax.broadcasted_iota(jnp.int32, (VT_ROWS - MLA_V, kr.shape[0]), 0) == 0).astype(BF16)
    vt = _dot_nt(wuvt_ref[...], cn)
    for h in range(MLA_HEADS):
        kcat_ref[h, :, 0:MLA_NOPE] = kn[:, h * MLA_NOPE:(h + 1) * MLA_NOPE].astype(BF16)
        kcat_ref[h, :, MLA_NOPE:] = kr
        vt_ref[h, 0:MLA_V, :] = vt[h * MLA_V:(h + 1) * MLA_V].astype(BF16)
        vt_ref[h, MLA_V:, :] = ones_rows


def _mla_kv(p, ckv_blk, kr_blk, g, wuk, wuvt, tables, tb):
    bsz, r, _ = p.shape
    rank = g.shape[-1]
    nb = r // tb
    const = lambda arr: pl.BlockSpec(arr.shape, lambda b, j: (0,) * arr.ndim)
    in_specs = [
        pl.BlockSpec((None, tb, rank), lambda b, j: (b, j, ckv_blk)),
        pl.BlockSpec((None, tb, LANES), lambda b, j: (b, j, kr_blk)),
        const(g), const(wuk), const(wuvt),
    ]
    args = [p, p, g, wuk, wuvt]
    if tables is not None:
        in_specs += [pl.BlockSpec((tb, LANES), lambda b, j: (j, 0))] * 2
        args += list(tables)
    return pl.pallas_call(
        functools.partial(_mla_kv_kernel, rope=tables is not None),
        grid=(bsz, nb),
        in_specs=in_specs,
        out_specs=[
            pl.BlockSpec((None, MLA_HEADS, None, tb, QK_PAD), lambda b, j: (b, 0, j, 0, 0)),
            pl.BlockSpec((None, MLA_HEADS, None, VT_ROWS, tb), lambda b, j: (b, 0, j, 0, 0)),
        ],
        out_shape=[
            jax.ShapeDtypeStruct((bsz, MLA_HEADS, nb, tb, QK_PAD), BF16),
            jax.ShapeDtypeStruct((bsz, MLA_HEADS, nb, VT_ROWS, tb), BF16),
        ],
        compiler_params=_params("parallel", "parallel"),
        name="mla_kv_rope" if tables is not None else "mla_kv",
    )(*args)


def _mla_q_kernel(cq_ref, g_ref, wqt_ref, cos_ref, sin_ref, qt_ref):
    cn = _rms(cq_ref[...], g_ref[...]).astype(BF16)
    f = ROPE_FREQ
    qt_all = _dot_nt(wqt_ref[...], cn) * (MLA_SCALE * LOG2_E)
    for h in range(MLA_HEADS):
        qt = qt_all[h * MLA_QK:(h + 1) * MLA_QK]
        qt_ref[h, 0:MLA_NOPE, :] = qt[0:MLA_NOPE].astype(BF16)
        for ax in range(2):
            r0 = MLA_NOPE + ax * 2 * f
            x1 = qt[r0:r0 + f]
            x2 = qt[r0 + f:r0 + 2 * f]
            co = cos_ref[ax]
            si = sin_ref[ax]
            qt_ref[h, r0:r0 + f, :] = (x1 * co - x2 * si).astype(BF16)
            qt_ref[h, r0 + f:r0 + 2 * f, :] = (x2 * co + x1 * si).astype(BF16)
        qt_ref[h, MLA_QK:, :] = jnp.zeros((QK_PAD - MLA_QK, cn.shape[0]), BF16)


def _mla_q(p, cq_blk, g, wqt, cos_t, sin_t):
    bsz, t, _ = p.shape
    rank = g.shape[-1]
    tm = min(2 * TOK_TILE, t)
    const = lambda arr: pl.BlockSpec(arr.shape, lambda b, j: (0,) * arr.ndim)
    tab = pl.BlockSpec((2, ROPE_FREQ, tm), lambda b, j: (0, 0, j))
    return pl.pallas_call(
        _mla_q_kernel,
        grid=(bsz, t // tm),
        in_specs=[pl.BlockSpec((None, tm, rank), lambda b, j: (b, j, cq_blk)), const(g), const(wqt), tab, tab],
        out_specs=pl.BlockSpec((None, MLA_HEADS, QK_PAD, tm), lambda b, j: (b, 0, 0, j)),
        out_shape=jax.ShapeDtypeStruct((bsz, MLA_HEADS, QK_PAD, t), BF16),
        compiler_params=_params("parallel", "parallel"),
        name="mla_q",
    )(p, g, wqt, cos_t, sin_t)


def _attn_kernel(qt_ref, kc_ref, vtc_ref, kl_ref, vtl_ref, g_ref, o_ref, m_ref, acc_ref, s_ref, mx_ref, *, tq, n_lat):
    nsub = tq // Q_SUB
    m_ref[...] = jnp.full(m_ref.shape, -jnp.inf, F32)
    acc_ref[...] = jnp.zeros(acc_ref.shape, F32)

    def scores(k, nxt, g):
        s = _dot(k, qt_ref[:, g * Q_SUB:(g + 1) * Q_SUB])
        s_ref[nxt, g, 0:k.shape[0], :] = s
        mx_ref[nxt, g] = jnp.max(s, axis=0, keepdims=True)

    def substep(k_next, vt_cur, cur, nxt):
        rows = vt_cur.shape[1]
        for g in range(nsub):
            sl = slice(g * Q_SUB, (g + 1) * Q_SUB)
            scores(k_next, nxt, g)
            m_old = m_ref[:, sl]
            m_new = jnp.maximum(m_old, mx_ref[cur, g])
            alpha = jnp.exp2(m_old - m_new)
            p = jnp.exp2(s_ref[cur, g, 0:rows, :] - m_new)
            acc_ref[:, sl] = alpha * acc_ref[:, sl] + _dot(vt_cur, p.astype(BF16))
            m_ref[:, sl] = m_new

    kc = kc_ref[...]
    for g in range(nsub):
        scores(kc, 0, g)
    substep(kl_ref[0], vtc_ref[...], 0, 1)

    per_trip = max(u for u in (2, 4, 8) if n_lat % u == 0)

    def body(j, carry):
        for u in range(per_trip):
            a = per_trip * j + u
            substep(kl_ref[jnp.minimum(a + 1, n_lat - 1)], vtl_ref[a], (1 + u) % 2, u % 2)
        return carry

    lax.fori_loop(0, n_lat // per_trip, body, 0)
    o = (acc_ref[0:MLA_V, :] * (1.0 / acc_ref[MLA_V:MLA_V + 1, :])).T
    o_ref[...] = (o * _silu(g_ref[...])).astype(o_ref.dtype)


def _attn(qt, kc, vtc, kl, vtl, p):
    bsz, nh, _, t = qt.shape
    lc = kc.shape[3]
    tq = min(Q_TILE, t)
    n_lat = kl.shape[2]
    kv = kl.shape[3]
    assert kc.shape[2] == 1 and lc <= kv and n_lat % 2 == 0
    kern = functools.partial(_attn_kernel, tq=tq, n_lat=n_lat)
    ctx5 = lambda arr: pl.BlockSpec((None, None, None) + arr.shape[3:], lambda b, h, i: (b, h, 0, 0, 0))
    full5 = lambda arr: pl.BlockSpec((None, None) + arr.shape[2:], lambda b, h, i: (b, h, 0, 0, 0))
    return pl.pallas_call(
        kern,
        grid=(bsz, nh, t // tq),
        in_specs=[
            pl.BlockSpec((None, None, QK_PAD, tq), lambda b, h, i: (b, h, 0, i)),
            ctx5(kc), ctx5(vtc), full5(kl), full5(vtl),
            pl.BlockSpec((None, tq, MLA_V), lambda b, h, i: (b, i, h)),
        ],
        out_specs=pl.BlockSpec((None, tq, MLA_V), lambda b, h, i: (b, i, h)),
        out_shape=jax.ShapeDtypeStruct((bsz, t, nh * MLA_V), BF16),
        scratch_shapes=[pltpu.VMEM((1, tq), F32), pltpu.VMEM((VT_ROWS, tq), F32),
                        pltpu.VMEM((2, tq // Q_SUB, kv, Q_SUB), F32), pltpu.VMEM((2, tq // Q_SUB, 1, Q_SUB), F32)],
        compiler_params=_params("parallel", "parallel", "arbitrary"),
        name="mla_attention",
    )(qt, kc, vtc, kl, vtl, p)


def _out_final_kernel(y_ref, h_ref, mod_ref, ow_ref, g_ref, o_ref):
    hn = h_ref[...] + mod_ref[2:3, :] * _dot(y_ref[...], ow_ref[...])
    o_ref[...] = _rms(hn, g_ref[...])


def _out_final(y, h, mod, out_w, g):
    bsz, t, d = h.shape
    wi = y.shape[-1]
    tm = min(2 * TOK_TILE, t)
    return pl.pallas_call(
        _out_final_kernel,
        grid=(bsz, t // tm),
        in_specs=[
            pl.BlockSpec((None, tm, wi), lambda b, j: (b, j, 0)),
            pl.BlockSpec((None, tm, d), lambda b, j: (b, j, 0)),
            pl.BlockSpec((None, 3, d), lambda b, j: (b, 0, 0)),
            pl.BlockSpec((wi, d), lambda b, j: (0, 0)),
            pl.BlockSpec((1, d), lambda b, j: (0, 0)),
        ],
        out_specs=pl.BlockSpec((None, tm, d), lambda b, j: (b, j, 0)),
        out_shape=jax.ShapeDtypeStruct((bsz, t, d), F32),
        compiler_params=_params("parallel", "parallel"),
        name="out_final",
    )(y, h, mod, out_w, g)


def _rope_tables(n_tokens):
    rows = n_tokens // GRID_W
    pos_r = jnp.repeat(jnp.arange(rows), GRID_W).astype(F32)
    pos_c = jnp.tile(jnp.arange(GRID_W), rows).astype(F32)
    inv = ROPE_BASE ** (-2.0 * jnp.arange(ROPE_FREQ, dtype=F32) / (MLA_ROPE // 2))
    ang = jnp.stack([pos_r[:, None] * inv, pos_c[:, None] * inv], axis=1)
    cos, sin = jnp.cos(ang), jnp.sin(ang)
    pad = LANES - MLA_ROPE
    cos_k = jnp.pad(jnp.stack([cos, cos], axis=2).reshape(n_tokens, MLA_ROPE), ((0, 0), (0, pad)))
    sin_k = jnp.pad(jnp.stack([-sin, sin], axis=2).reshape(n_tokens, MLA_ROPE), ((0, 0), (0, pad)))
    cos_q = jnp.transpose(cos, (1, 2, 0))
    sin_q = jnp.transpose(sin, (1, 2, 0))
    return (cos_k, sin_k), (cos_q, sin_q)


def kernel(x, c, ctx, c_ctx, ada_w, ada_b, norm_g, out_w, ev_in_w, hg_lb, hg_norm_g, pool_w, pool_scale,
           od_in_w, qa_norm_g, qb_w, kva_norm_g, kvb_w, final_norm_g):
    bsz, t, d = x.shape
    lc = ctx.shape[1]
    depth = ada_w.shape[0]
    assert depth == 2 and t % (2 * TOK_TILE) == 0 and lc % TOK_TILE == 0 and t % GRID_W == 0
    w = hg_norm_g.shape[-1]
    nh = w // HG_DK
    q_rank = qa_norm_g.shape[-1]
    kv_rank = kva_norm_g.shape[-1]
    d_inner = out_w.shape[1]

    n_cond = -(-(bsz + 1) // SUBLANES) * SUBLANES
    cond = jnp.zeros((n_cond, d), F32).at[:bsz].set(c).at[bsz].set(c_ctx)
    mods = _ada(cond, ada_w, ada_b).reshape(depth, n_cond, 3, d)
    mod_l = [mods[l, :bsz] for l in range(depth)]
    mod_c = [mods[l, bsz:bsz + 1] for l in range(depth)]

    lb = _decay_bounds(hg_lb, 0)
    w_in0 = ev_in_w[0].astype(BF16)
    g0 = norm_g[0].reshape(1, d)
    ctx_flat = ctx.reshape(1, bsz * lc, d)
    n_in0 = w_in0.shape[1]
    tn0 = n_in0 // IN_PROJ_COL_PARTS
    p_c = _modnorm_mm(ctx_flat, mod_c[0], g0, w_in0, IN_PROJ_ROWS, tn0, "in_proj0_ctx").reshape(bsz, lc, n_in0)
    p_l = _modnorm_mm(x, mod_l[0], g0, w_in0, IN_PROJ_ROWS, tn0, "in_proj0")
    consts = _hgrn_constants(HG_CHUNK)
    s0 = jnp.zeros((bsz, 2, nh, HG_DK, HG_DK), F32)
    of_c, ob_c, s_c = _hgrn(p_c, lb, s0, consts)
    of_l, ob_l, _ = _hgrn(p_l, lb, s_c, consts)
    hgn = hg_norm_g[0].reshape(1, w)
    pw = pool_w[0].astype(BF16)
    ps = pool_scale[0].reshape(1, w)
    ow0 = out_w[0].astype(BF16)
    o1 = q_rank
    o2 = o1 + kv_rank
    o3 = o2 + MLA_ROPE
    w1 = od_in_w[0]
    kr_pad = jnp.zeros((d, LANES - MLA_ROPE), F32)
    w_in1 = jnp.concatenate([w1[:, o3:], w1[:, :o1], w1[:, o1:o2], w1[:, o2:o3], kr_pad], axis=1).astype(BF16)
    w_in1c = w_in1[:, d_inner + q_rank:]
    g1 = norm_g[1].reshape(1, d)
    bcast = lambda m: jnp.broadcast_to(m, (bsz, 3, d))
    _, p1_c = _even_post(of_c, ob_c, p_c, ctx, bcast(mod_c[0]), hgn, pw, ps, ow0, bcast(mod_c[1]), g1, w_in1c)
    hl1, p1_l = _even_post(of_l, ob_l, p_l, x, mod_l[0], hgn, pw, ps, ow0, mod_l[1], g1, w_in1)

    kvw = kvb_w[0].reshape(kv_rank, MLA_HEADS, MLA_NOPE + MLA_V)
    wuk = kvw[..., :MLA_NOPE].reshape(kv_rank, MLA_HEADS * MLA_NOPE).astype(BF16)
    wuvt = jnp.transpose(kvw[..., MLA_NOPE:], (1, 2, 0)).reshape(MLA_HEADS * MLA_V, kv_rank).astype(BF16)
    wqt = jnp.transpose(qb_w[0]).astype(BF16)
    kvg = kva_norm_g[0].reshape(1, kv_rank)
    qag = qa_norm_g[0].reshape(1, q_rank)
    tab_k, tab_q = _rope_tables(t)
    kc, vtc = _mla_kv(p1_c, 0, kv_rank // LANES, kvg, wuk, wuvt, None, lc)
    kl, vtl = _mla_kv(p1_l, (d_inner + q_rank) // kv_rank, (d_inner + q_rank + kv_rank) // LANES, kvg, wuk, wuvt,
                      tab_k, min(KV_CHUNK, t))
    qt = _mla_q(p1_l, d_inner // q_rank, qag, wqt, *tab_q)
    y = _attn(qt, kc, vtc, kl, vtl, p1_l)
    return _out_final(y, hl1, mod_l[1], out_w[1].astype(BF16), final_norm_g.reshape(1, d))
```

```python
import functools

import numpy as np
import jax
import jax.numpy as jnp
from jax import lax
from jax.experimental import pallas as pl
from jax.experimental.pallas import tpu as pltpu

F32 = jnp.float32
BF16 = jnp.bfloat16

EPS = 1e-6
GRID_W = 64
HG_DK = 128
POOL_WINDOWS = (2, 4, 8, 16)
MLA_HEADS = 16
MLA_NOPE = 128
MLA_ROPE = 64
MLA_V = 128
MLA_QK = MLA_NOPE + MLA_ROPE
QK_PAD = 256
BF16_ROWS = 16
VT_ROWS = MLA_V + BF16_ROWS
MLA_SCALE = MLA_QK ** -0.5
LOG2_E = 1.4426950408889634
ROPE_FREQ = MLA_ROPE // 4
ROPE_BASE = 10000.0

LANES = 128
SUBLANES = 8
VMEM_LIMIT = 48 * 1024 * 1024

HG_CHUNK = 64
HG_TILE = 512
TOK_TILE = 256
IN_PROJ_ROWS = 1024
IN_PROJ_COL_PARTS = 4
KV_CHUNK = 512
Q_TILE = 4096
Q_SUB = 256
POOL_HALO = 8


def _dot(a, b):
    return jnp.dot(a, b, preferred_element_type=F32)


def _dot_nt(a, b):
    return lax.dot_general(a, b, (((1,), (1,)), ((), ())), preferred_element_type=F32)


def _dot_tn(a, b):
    return lax.dot_general(a, b, (((0,), (0,)), ((), ())), preferred_element_type=F32)


def _silu(x):
    h = 0.5 * x
    return h + h * jnp.tanh(h)


def _split_bf16(x):
    hi = x.astype(BF16)
    lo = (x - hi.astype(F32)).astype(BF16)
    return hi, lo


def _params(*sem):
    return pltpu.CompilerParams(dimension_semantics=sem, vmem_limit_bytes=VMEM_LIMIT)


def _ada_kernel(c_ref, w_ref, b_ref, o_ref):
    c = c_ref[...]
    s_hi, s_lo = _split_bf16(_silu(c))
    w_hi, w_lo = _split_bf16(w_ref[...])
    o_ref[...] = _dot(s_hi, w_hi) + _dot(s_lo, w_hi) + _dot(s_hi, w_lo) + b_ref[...]


def _ada(cond, ada_w, ada_b):
    depth, d, _ = ada_w.shape
    r = cond.shape[0]
    return pl.pallas_call(
        _ada_kernel,
        grid=(depth, 3),
        in_specs=[
            pl.BlockSpec((r, d), lambda l, j: (0, 0)),
            pl.BlockSpec((None, d, d), lambda l, j: (l, 0, j)),
            pl.BlockSpec((None, 1, d), lambda l, j: (l, 0, j)),
        ],
        out_specs=pl.BlockSpec((None, r, d), lambda l, j: (l, 0, j)),
        out_shape=jax.ShapeDtypeStruct((depth, r, 3 * d), F32),
        compiler_params=_params("parallel", "parallel"),
        name="ada_modulation",
    )(cond, ada_w, ada_b.reshape(depth, 1, 3 * d))


def _rms(x, g):
    return x * lax.rsqrt(jnp.mean(x * x, axis=-1, keepdims=True) + EPS) * g


def _modnorm_mm_kernel(x_ref, mod_ref, g_ref, w_ref, o_ref, z_ref):
    @pl.when(pl.program_id(2) == 0)
    def _():
        y = _rms(x_ref[...], g_ref[...])
        z_ref[...] = (y * (1.0 + mod_ref[1:2, :]) + mod_ref[0:1, :]).astype(BF16)

    o_ref[...] = _dot(z_ref[...], w_ref[...])


def _modnorm_mm(x, mod, g, w, tm, tn, name):
    bx, r, d = x.shape
    n = w.shape[1]
    tm = min(tm, r)
    return pl.pallas_call(
        _modnorm_mm_kernel,
        grid=(bx, r // tm, n // tn),
        in_specs=[
            pl.BlockSpec((None, tm, d), lambda b, i, j: (b, i, 0)),
            pl.BlockSpec((None, 3, d), lambda b, i, j: (b, 0, 0)),
            pl.BlockSpec((1, d), lambda b, i, j: (0, 0)),
            pl.BlockSpec((d, tn), lambda b, i, j: (0, j)),
        ],
        out_specs=pl.BlockSpec((None, tm, tn), lambda b, i, j: (b, i, j)),
        out_shape=jax.ShapeDtypeStruct((bx, r, n), F32),
        scratch_shapes=[pltpu.VMEM((tm, d), BF16)],
        compiler_params=_params("parallel", "parallel", "arbitrary"),
        name=name,
    )(x, mod, g, w)


def _decay_bounds_kernel(a_ref, o_ref, *, layer):
    slots = a_ref[...]
    e = jnp.exp(slots - jnp.max(slots, axis=1, keepdims=True))
    o_ref[...] = jnp.sum(e[:, :layer + 1], axis=1) / jnp.sum(e, axis=1)


def _decay_bounds(hg_lb, layer):
    ndir, nslot, w = hg_lb.shape
    return pl.pallas_call(
        functools.partial(_decay_bounds_kernel, layer=layer),
        out_shape=jax.ShapeDtypeStruct((ndir, 1, w), F32),
        name="decay_bounds",
    )(hg_lb.reshape(ndir, nslot, 1, w))


def _hgrn_levels(c):
    w = c // 2
    out = []
    while w >= 1:
        out.append(w)
        w //= 2
    return tuple(out)


def _hgrn_constants(c):
    t = np.arange(c)
    tri = np.tril(np.ones((c, c), np.float32))
    masks = []
    for w in _hgrn_levels(c):
        blk = t // (2 * w)
        first = (t % (2 * w)) < w
        masks.append(((blk[:, None] == blk[None, :]) & (~first[:, None]) & first[None, :]).astype(np.float32))
    masks = np.stack(masks)
    tri2 = np.stack([tri, tri[::-1, ::-1]])
    m2 = np.stack([masks, masks[:, ::-1, ::-1]])
    return jnp.asarray(tri2, BF16), jnp.asarray(m2, F32)


def _hgrn_level_operand(b, g, k, q, w, d):
    c, width = b.shape
    row = lax.broadcasted_iota(jnp.int32, (c, 1), 0)
    keys_first = d == 0
    if w >= SUBLANES:
        ref_off = w - 1 if d == 0 else w
        pieces = []
        for r0 in range(0, c, 2 * w):
            bref = jnp.broadcast_to(b[r0 + ref_off:r0 + ref_off + 1, :], (w, width))
            for half in range(2):
                sl = slice(r0 + half * w, r0 + (half + 1) * w)
                if (half == 0) == keys_first:
                    pieces.append(k[sl] * jnp.exp2(bref - b[sl]))
                else:
                    pieces.append(q[sl] * jnp.exp2(b[sl] - bref))
        return jnp.concatenate(pieces, axis=0)
    before = (row % (2 * w)) < w
    kq = jnp.where(before == keys_first, k, q)
    if w == 1:
        moving = (row % 2 == 1) if d == 0 else (row % 2 == 0)
        return kq * jnp.exp2(jnp.where(moving, g, 0.0))
    ref_off = w - 1 if d == 0 else w
    sub = lax.broadcasted_iota(jnp.int32, (SUBLANES, 1), 0)
    pieces = []
    for r0 in range(0, c, SUBLANES):
        lo = jnp.broadcast_to(b[r0 + ref_off:r0 + ref_off + 1, :], (SUBLANES, width))
        if 2 * w == SUBLANES:
            pieces.append(lo)
        else:
            hi = jnp.broadcast_to(b[r0 + 2 * w + ref_off:r0 + 2 * w + ref_off + 1, :], (SUBLANES, width))
            pieces.append(jnp.where(sub < 2 * w, lo, hi))
    bref = jnp.concatenate(pieces, axis=0)
    return kq * jnp.exp2(-jnp.abs(b - bref))


def _hgrn_wide(q_ref, f_ref, v_ref, lb, tri, d, r0, *, c):
    last = c - 1 if d == 0 else 0
    rows = pl.ds(r0, c)
    half = 0.5 * (1.0 - lb)
    f = (lb + half) + half * jnp.tanh(0.5 * f_ref[rows, :])
    g = jnp.log2(f)
    g_hi, g_lo = _split_bf16(g)
    b = _dot(tri, g_hi) + _dot(tri, g_lo)
    q = _silu(q_ref[rows, :])
    k = 1.0 - f
    v = v_ref[rows, :]
    bl = b[last:last + 1, :]
    return dict(d=d, rows=rows, g=g, b=b, q=q, k=k, v=v, vb=v.astype(BF16), qk=q * k,
                qe=(q * jnp.exp2(b)).astype(BF16), kend=(k * jnp.exp2(bl - b)).astype(BF16), ebl=jnp.exp2(bl))


def _head(x, h):
    return x[:, h * HG_DK:(h + 1) * HG_DK]


def _hgrn_pairs(s, h, mk_ref, *, c):
    d = s["d"]
    att = None
    for l, w in enumerate(_hgrn_levels(c)):
        x = _hgrn_level_operand(_head(s["b"], h), _head(s["g"], h), _head(s["k"], h), _head(s["q"], h), w, d)
        x = x.astype(BF16)
        t = mk_ref[d, l] * _dot_nt(x, x)
        att = t if att is None else att + t
    return att


def _hgrn_finish(s, h, att, o_ref, st_ref):
    d = s["d"]
    vb = _head(s["vb"], h)
    inter = _dot_nt(_head(s["qe"], h), st_ref[d, h].astype(BF16))
    diag = jnp.sum(_head(s["qk"], h), axis=-1, keepdims=True)
    o_ref[s["rows"], h * HG_DK:(h + 1) * HG_DK] = inter + _dot(att.astype(BF16), vb) + diag * _head(s["v"], h)
    st_ref[d, h] = _head(s["ebl"], h) * st_ref[d, h] + _dot_tn(vb, _head(s["kend"], h))


def _hgrn_kernel(qf_ref, ff_ref, vf_ref, qb_ref, fb_ref, vb_ref, lb_ref, tri_ref, mk_ref, s0_ref,
                 of_ref, ob_ref, sout_ref, st_ref, *, tb, c, nh):
    nchunk = tb // c

    @pl.when(pl.program_id(1) == 0)
    def _():
        st_ref[...] = s0_ref[...]

    def body(cc, carry):
        rf = pl.multiple_of(cc * c, c)
        rb = pl.multiple_of((nchunk - 1 - cc) * c, c)
        sides = ((_hgrn_wide(qf_ref, ff_ref, vf_ref, lb_ref[0], tri_ref[0], 0, rf, c=c), of_ref),
                 (_hgrn_wide(qb_ref, fb_ref, vb_ref, lb_ref[1], tri_ref[1], 1, rb, c=c), ob_ref))
        pending = [None, None]
        for h in range(nh + 1):
            cur = [_hgrn_pairs(s, h, mk_ref, c=c) if h < nh else None for s, _ in sides]
            for (s, o_ref), p in zip(sides, pending):
                if p is not None:
                    _hgrn_finish(s, h - 1, p, o_ref, st_ref)
            pending = cur
        return carry

    lax.fori_loop(0, nchunk, body, 0, unroll=True)

    @pl.when(pl.program_id(1) == pl.num_programs(1) - 1)
    def _():
        sout_ref[...] = st_ref[...]


def _hgrn(p, lb, s0, consts):
    bsz, r, _ = p.shape
    w = lb.shape[-1]
    nh = w // HG_DK
    tb = min(HG_TILE, r)
    c = HG_CHUNK
    nb = r // tb
    tri2, m2 = consts
    fwd = lambda col: pl.BlockSpec((None, tb, w), lambda b, s: (b, s, col))
    bwd = lambda col: pl.BlockSpec((None, tb, w), lambda b, s: (b, nb - 1 - s, col))
    const = lambda arr: pl.BlockSpec(arr.shape, lambda b, s: (0,) * arr.ndim)
    st_spec = pl.BlockSpec((None, 2, nh, HG_DK, HG_DK), lambda b, s: (b, 0, 0, 0, 0))
    kern = functools.partial(_hgrn_kernel, tb=tb, c=c, nh=nh)
    return pl.pallas_call(
        kern,
        grid=(bsz, nb),
        in_specs=[fwd(0), fwd(1), fwd(3), bwd(0), bwd(2), bwd(3), const(lb), const(tri2), const(m2),
                  st_spec],
        out_specs=[
            pl.BlockSpec((None, tb, w), lambda b, s: (b, s, 0)),
            pl.BlockSpec((None, tb, w), lambda b, s: (b, nb - 1 - s, 0)),
            st_spec,
        ],
        out_shape=[
            jax.ShapeDtypeStruct((bsz, r, w), F32),
            jax.ShapeDtypeStruct((bsz, r, w), F32),
            jax.ShapeDtypeStruct(s0.shape, F32),
        ],
        scratch_shapes=[pltpu.VMEM((2, nh, HG_DK, HG_DK), F32)],
        compiler_params=_params("parallel", "arbitrary"),
        name="hgrn2_scan",
    )(p, p, p, p, p, p, lb, tri2, m2, s0)


def _even_post_kernel(of_ref, ob_ref, ga_ref, u_ref, gb_ref, up_ref, un_ref, h_ref, mod_ref, hgn_ref, pw_ref,
                      ps_ref, ow_ref, modn_ref, gn_ref, wn_ref, o_ref, pn_ref, ext_ref, y_ref, z_ref, *, tb, seq, nh):
    step = pl.program_id(1)
    last = pl.num_programs(1) - 2
    j = jnp.minimum(step, last)
    w = nh * HG_DK

    @pl.when(step == 0)
    def _():
        z_ref[...] = jnp.zeros(z_ref.shape, BF16)

    n_parts = nh + len(POOL_WINDOWS)
    n_blk = pn_ref.shape[1] // LANES
    per = -(-n_blk // n_parts)

    def in_proj_part(i):
        c0 = min(i * per, n_blk) * LANES
        c1 = min((i + 1) * per, n_blk) * LANES
        if c0 < c1:
            pn_ref[:, c0:c1] = _dot(z_ref[...], wn_ref[:, c0:c1])

    o = of_ref[...] + ob_ref[...]
    for h in range(nh):
        in_proj_part(h)
        sl = slice(h * HG_DK, (h + 1) * HG_DK)
        y_ref[:, sl] = (_rms(o[:, sl], hgn_ref[:, sl]) * _silu(ga_ref[:, sl])).astype(BF16)
    u = u_ref[...]
    ext_ref[0:POOL_HALO, :] = jnp.where(j > 0, up_ref[...], 0.0)
    ext_ref[POOL_HALO:POOL_HALO + tb, :] = u
    ext_ref[POOL_HALO + tb:, :] = jnp.where(j < last, un_ref[...], 0.0)
    t = j * tb + lax.broadcasted_iota(jnp.int32, (tb, 1), 0)
    grp = w // len(POOL_WINDOWS)
    for gi, win in enumerate(POOL_WINDOWS):
        in_proj_part(nh + gi)
        sl = slice(gi * grp, (gi + 1) * grp)
        acc = ext_ref[POOL_HALO - win // 2:POOL_HALO - win // 2 + tb, sl]
        for off in range(-win // 2 + 1, win // 2):
            acc = acc + ext_ref[POOL_HALO + off:POOL_HALO + off + tb, sl]
        cnt = (jnp.minimum(t + win // 2, seq) - jnp.maximum(t - win // 2, 0)).astype(F32)
        yp = acc * (1.0 / cnt) - u[:, sl]
        yb = _dot(yp.astype(BF16), pw_ref[gi]) * ps_ref[:, sl]
        y_ref[:, w + gi * grp:w + (gi + 1) * grp] = (yb * _silu(gb_ref[:, sl])).astype(BF16)
    hn = h_ref[...] + mod_ref[2:3, :] * _dot(y_ref[...], ow_ref[...])
    o_ref[...] = hn
    z_ref[...] = (_rms(hn, gn_ref[...]) * (1.0 + modn_ref[1:2, :]) + modn_ref[0:1, :]).astype(BF16)


def _even_post(o_f, o_b, p, h, mod, hgn, pool_w, pool_scale, out_w, mod_n, g_n, w_n):
    bsz, r, w = o_f.shape
    d = h.shape[-1]
    tb = min(TOK_TILE, r)
    nb = r // tb
    hb = tb // POOL_HALO
    nh = w // HG_DK
    n_next = w_n.shape[1]
    cur = lambda s: jnp.minimum(s, nb - 1)
    tok = lambda col: pl.BlockSpec((None, tb, w), lambda b, s: (b, cur(s), col))
    const = lambda arr: pl.BlockSpec(arr.shape, lambda b, s: (0,) * arr.ndim)
    kern = functools.partial(_even_post_kernel, tb=tb, seq=r, nh=nh)
    return pl.pallas_call(
        kern,
        grid=(bsz, nb + 1),
        in_specs=[
            tok(0), tok(0), tok(4), tok(5), tok(6),
            pl.BlockSpec((None, POOL_HALO, w), lambda b, s: (b, jnp.maximum(cur(s) * hb - 1, 0), 5)),
            pl.BlockSpec((None, POOL_HALO, w), lambda b, s: (b, jnp.minimum((cur(s) + 1) * hb, nb * hb - 1), 5)),
            pl.BlockSpec((None, tb, d), lambda b, s: (b, cur(s), 0)),
            pl.BlockSpec((None, 3, d), lambda b, s: (b, 0, 0)),
            const(hgn), const(pool_w), const(pool_scale), const(out_w),
            pl.BlockSpec((None, 3, d), lambda b, s: (b, 0, 0)), const(g_n), const(w_n),
        ],
        out_specs=[pl.BlockSpec((None, tb, d), lambda b, s: (b, cur(s), 0)),
                   pl.BlockSpec((None, tb, n_next), lambda b, s: (b, jnp.maximum(s - 1, 0), 0))],
        out_shape=[jax.ShapeDtypeStruct((bsz, r, d), F32), jax.ShapeDtypeStruct((bsz, r, n_next), F32)],
        scratch_shapes=[pltpu.VMEM((tb + 2 * POOL_HALO, w), F32), pltpu.VMEM((tb, 2 * w), BF16),
                        pltpu.VMEM((tb, d), BF16)],
        compiler_params=_params("parallel", "arbitrary"),
        name="even_post",
    )(o_f, o_b, p, p, p, p, p, h, mod, hgn, pool_w, pool_scale, out_w, mod_n, g_n, w_n)


def _mla_kv_kernel(*refs, rope):
    if rope:
        ckv_ref, kr_ref, g_ref, wuk_ref, wuvt_ref, cos_ref, sin_ref, kcat_ref, vt_ref = refs
    else:
        ckv_ref, kr_ref, g_ref, wuk_ref, wuvt_ref, kcat_ref, vt_ref = refs
    cn = _rms(ckv_ref[...], g_ref[...]).astype(BF16)
    kn = _dot(cn, wuk_ref[...])
    kr = kr_ref[...]
    if rope:
        lane = lax.broadcasted_iota(jnp.int32, kr.shape, 1)
        swapped = jnp.where((lane % (2 * ROPE_FREQ)) < ROPE_FREQ,
                            pltpu.roll(kr, LANES - ROPE_FREQ, 1), pltpu.roll(kr, ROPE_FREQ, 1))
        kr = kr * cos_ref[...] + swapped * sin_ref[...]
    kr = kr.astype(BF16)
    ones_rows = (lax.broadcasted_iota(jnp.int32, (VT_ROWS - MLA_V, kr.shape[0]), 0) == 0).astype(BF16)
    vt = _dot_nt(wuvt_ref[...], cn)
    for h in range(MLA_HEADS):
        kcat_ref[h, :, 0:MLA_NOPE] = kn[:, h * MLA_NOPE:(h + 1) * MLA_NOPE].astype(BF16)
        kcat_ref[h, :, MLA_NOPE:] = kr
        vt_ref[h, 0:MLA_V, :] = vt[h * MLA_V:(h + 1) * MLA_V].astype(BF16)
        vt_ref[h, MLA_V:, :] = ones_rows


def _mla_kv(p, ckv_blk, kr_blk, g, wuk, wuvt, tables, tb):
    bsz, r, _ = p.shape
    rank = g.shape[-1]
    nb = r // tb
    const = lambda arr: pl.BlockSpec(arr.shape, lambda b, j: (0,) * arr.ndim)
    in_specs = [
        pl.BlockSpec((None, tb, rank), lambda b, j: (b, j, ckv_blk)),
        pl.BlockSpec((None, tb, LANES), lambda b, j: (b, j, kr_blk)),
        const(g), const(wuk), const(wuvt),
    ]
    args = [p, p, g, wuk, wuvt]
    if tables is not None:
        in_specs += [pl.BlockSpec((tb, LANES), lambda b, j: (j, 0))] * 2
        args += list(tables)
    return pl.pallas_call(
        functools.partial(_mla_kv_kernel, rope=tables is not None),
        grid=(bsz, nb),
        in_specs=in_specs,
        out_specs=[
            pl.BlockSpec((None, MLA_HEADS, None, tb, QK_PAD), lambda b, j: (b, 0, j, 0, 0)),
            pl.BlockSpec((None, MLA_HEADS, None, VT_ROWS, tb), lambda b, j: (b, 0, j, 0, 0)),
        ],
        out_shape=[
            jax.ShapeDtypeStruct((bsz, MLA_HEADS, nb, tb, QK_PAD), BF16),
            jax.ShapeDtypeStruct((bsz, MLA_HEADS, nb, VT_ROWS, tb), BF16),
        ],
        compiler_params=_params("parallel", "parallel"),
        name="mla_kv_rope" if tables is not None else "mla_kv",
    )(*args)


def _mla_q_kernel(cq_ref, g_ref, wqt_ref, cos_ref, sin_ref, qt_ref):
    cn = _rms(cq_ref[...], g_ref[...]).astype(BF16)
    f = ROPE_FREQ
    qt_all = _dot_nt(wqt_ref[...], cn) * (MLA_SCALE * LOG2_E)
    for h in range(MLA_HEADS):
        qt = qt_all[h * MLA_QK:(h + 1) * MLA_QK]
        qt_ref[h, 0:MLA_NOPE, :] = qt[0:MLA_NOPE].astype(BF16)
        for ax in range(2):
            r0 = MLA_NOPE + ax * 2 * f
            x1 = qt[r0:r0 + f]
            x2 = qt[r0 + f:r0 + 2 * f]
            co = cos_ref[ax]
            si = sin_ref[ax]
            qt_ref[h, r0:r0 + f, :] = (x1 * co - x2 * si).astype(BF16)
            qt_ref[h, r0 + f:r0 + 2 * f, :] = (x2 * co + x1 * si).astype(BF16)
        qt_ref[h, MLA_QK:, :] = jnp.zeros((QK_PAD - MLA_QK, cn.shape[0]), BF16)


def _mla_q(p, cq_blk, g, wqt, cos_t, sin_t):
    bsz, t, _ = p.shape
    rank = g.shape[-1]
    tm = min(2 * TOK_TILE, t)
    const = lambda arr: pl.BlockSpec(arr.shape, lambda b, j: (0,) * arr.ndim)
    tab = pl.BlockSpec((2, ROPE_FREQ, tm), lambda b, j: (0, 0, j))
    return pl.pallas_call(
        _mla_q_kernel,
        grid=(bsz, t // tm),
        in_specs=[pl.BlockSpec((None, tm, rank), lambda b, j: (b, j, cq_blk)), const(g), const(wqt), tab, tab],
        out_specs=pl.BlockSpec((None, MLA_HEADS, QK_PAD, tm), lambda b, j: (b, 0, 0, j)),
        out_shape=jax.ShapeDtypeStruct((bsz, MLA_HEADS, QK_PAD, t), BF16),
        compiler_params=_params("parallel", "parallel"),
        name="mla_q",
    )(p, g, wqt, cos_t, sin_t)


def _attn_kernel(qt_ref, kc_ref, vtc_ref, kl_ref, vtl_ref, g_ref, o_ref, m_ref, acc_ref, s_ref, mx_ref, *, tq, n_lat):
    nsub = tq // Q_SUB
    m_ref[...] = jnp.full(m_ref.shape, -jnp.inf, F32)
    acc_ref[...] = jnp.zeros(acc_ref.shape, F32)

    def scores(k, nxt, g):
        s = _dot(k, qt_ref[:, g * Q_SUB:(g + 1) * Q_SUB])
        s_ref[nxt, g, 0:k.shape[0], :] = s
        mx_ref[nxt, g] = jnp.max(s, axis=0, keepdims=True)

    def substep(k_next, vt_cur, cur, nxt):
        rows = vt_cur.shape[1]
        for g in range(nsub):
            sl = slice(g * Q_SUB, (g + 1) * Q_SUB)
            scores(k_next, nxt, g)
            m_old = m_ref[:, sl]
            m_new = jnp.maximum(m_old, mx_ref[cur, g])
            alpha = jnp.exp2(m_old - m_new)
            p = jnp.exp2(s_ref[cur, g, 0:rows, :] - m_new)
            acc_ref[:, sl] = alpha * acc_ref[:, sl] + _dot(vt_cur, p.astype(BF16))
            m_ref[:, sl] = m_new

    kc = kc_ref[...]
    for g in range(nsub):
        scores(kc, 0, g)
    substep(kl_ref[0], vtc_ref[...], 0, 1)

    per_trip = max(u for u in (2, 4, 8) if n_lat % u == 0)

    def body(j, carry):
        for u in range(per_trip):
            a = per_trip * j + u
            substep(kl_ref[jnp.minimum(a + 1, n_lat - 1)], vtl_ref[a], (1 + u) % 2, u % 2)
        return carry

    lax.fori_loop(0, n_lat // per_trip, body, 0)
    o = (acc_ref[0:MLA_V, :] * (1.0 / acc_ref[MLA_V:MLA_V + 1, :])).T
    o_ref[...] = (o * _silu(g_ref[...])).astype(o_ref.dtype)


def _attn(qt, kc, vtc, kl, vtl, p):
    bsz, nh, _, t = qt.shape
    lc = kc.shape[3]
    tq = min(Q_TILE, t)
    n_lat = kl.shape[2]
    kv = kl.shape[3]
    assert kc.shape[2] == 1 and lc <= kv and n_lat % 2 == 0
    kern = functools.partial(_attn_kernel, tq=tq, n_lat=n_lat)
    ctx5 = lambda arr: pl.BlockSpec((None, None, None) + arr.shape[3:], lambda b, h, i: (b, h, 0, 0, 0))
    full5 = lambda arr: pl.BlockSpec((None, None) + arr.shape[2:], lambda b, h, i: (b, h, 0, 0, 0))
    return pl.pallas_call(
        kern,
        grid=(bsz, nh, t // tq),
        in_specs=[
            pl.BlockSpec((None, None, QK_PAD, tq), lambda b, h, i: (b, h, 0, i)),
            ctx5(kc), ctx5(vtc), full5(kl), full5(vtl),
            pl.BlockSpec((None, tq, MLA_V), lambda b, h, i: (b, i, h)),
        ],
        out_specs=pl.BlockSpec((None, tq, MLA_V), lambda b, h, i: (b, i, h)),
        out_shape=jax.ShapeDtypeStruct((bsz, t, nh * MLA_V), BF16),
        scratch_shapes=[pltpu.VMEM((1, tq), F32), pltpu.VMEM((VT_ROWS, tq), F32),
                        pltpu.VMEM((2, tq // Q_SUB, kv, Q_SUB), F32), pltpu.VMEM((2, tq // Q_SUB, 1, Q_SUB), F32)],
        compiler_params=_params("parallel", "parallel", "arbitrary"),
        name="mla_attention",
    )(qt, kc, vtc, kl, vtl, p)


def _out_final_kernel(y_ref, h_ref, mod_ref, ow_ref, g_ref, o_ref):
    hn = h_ref[...] + mod_ref[2:3, :] * _dot(y_ref[...], ow_ref[...])
    o_ref[...] = _rms(hn, g_ref[...])


def _out_final(y, h, mod, out_w, g):
    bsz, t, d = h.shape
    wi = y.shape[-1]
    tm = min(2 * TOK_TILE, t)
    return pl.pallas_call(
        _out_final_kernel,
        grid=(bsz, t // tm),
        in_specs=[
            pl.BlockSpec((None, tm, wi), lambda b, j: (b, j, 0)),
            pl.BlockSpec((None, tm, d), lambda b, j: (b, j, 0)),
            pl.BlockSpec((None, 3, d), lambda b, j: (b, 0, 0)),
            pl.BlockSpec((wi, d), lambda b, j: (0, 0)),
            pl.BlockSpec((1, d), lambda b, j: (0, 0)),
        ],
        out_specs=pl.BlockSpec((None, tm, d), lambda b, j: (b, j, 0)),
        out_shape=jax.ShapeDtypeStruct((bsz, t, d), F32),
        compiler_params=_params("parallel", "parallel"),
        name="out_final",
    )(y, h, mod, out_w, g)


def _rope_tables(n_tokens):
    rows = n_tokens // GRID_W
    pos_r = jnp.repeat(jnp.arange(rows), GRID_W).astype(F32)
    pos_c = jnp.tile(jnp.arange(GRID_W), rows).astype(F32)
    inv = ROPE_BASE ** (-2.0 * jnp.arange(ROPE_FREQ, dtype=F32) / (MLA_ROPE // 2))
    ang = jnp.stack([pos_r[:, None] * inv, pos_c[:, None] * inv], axis=1)
    cos, sin = jnp.cos(ang), jnp.sin(ang)
    pad = LANES - MLA_ROPE
    cos_k = jnp.pad(jnp.stack([cos, cos], axis=2).reshape(n_tokens, MLA_ROPE), ((0, 0), (0, pad)))
    sin_k = jnp.pad(jnp.stack([-sin, sin], axis=2).reshape(n_tokens, MLA_ROPE), ((0, 0), (0, pad)))
    cos_q = jnp.transpose(cos, (1, 2, 0))
    sin_q = jnp.transpose(sin, (1, 2, 0))
    return (cos_k, sin_k), (cos_q, sin_q)


def kernel(x, c, ctx, c_ctx, ada_w, ada_b, norm_g, out_w, ev_in_w, hg_lb, hg_norm_g, pool_w, pool_scale,
           od_in_w, qa_norm_g, qb_w, kva_norm_g, kvb_w, final_norm_g):
    bsz, t, d = x.shape
    lc = ctx.shape[1]
    depth = ada_w.shape[0]
    assert depth == 2 and t % (2 * TOK_TILE) == 0 and lc % TOK_TILE == 0 and t % GRID_W == 0
    w = hg_norm_g.shape[-1]
    nh = w // HG_DK
    q_rank = qa_norm_g.shape[-1]
    kv_rank = kva_norm_g.shape[-1]
    d_inner = out_w.shape[1]

    n_cond = -(-(bsz + 1) // SUBLANES) * SUBLANES
    cond = jnp.zeros((n_cond, d), F32).at[:bsz].set(c).at[bsz].set(c_ctx)
    mods = _ada(cond, ada_w, ada_b).reshape(depth, n_cond, 3, d)
    mod_l = [mods[l, :bsz] for l in range(depth)]
    mod_c = [mods[l, bsz:bsz + 1] for l in range(depth)]

    lb = _decay_bounds(hg_lb, 0)
    w_in0 = ev_in_w[0].astype(BF16)
    g0 = norm_g[0].reshape(1, d)
    ctx_flat = ctx.reshape(1, bsz * lc, d)
    n_in0 = w_in0.shape[1]
    tn0 = n_in0 // IN_PROJ_COL_PARTS
    p_c = _modnorm_mm(ctx_flat, mod_c[0], g0, w_in0, IN_PROJ_ROWS, tn0, "in_proj0_ctx").reshape(bsz, lc, n_in0)
    p_l = _modnorm_mm(x, mod_l[0], g0, w_in0, IN_PROJ_ROWS, tn0, "in_proj0")
    consts = _hgrn_constants(HG_CHUNK)
    s0 = jnp.zeros((bsz, 2, nh, HG_DK, HG_DK), F32)
    of_c, ob_c, s_c = _hgrn(p_c, lb, s0, consts)
    of_l, ob_l, _ = _hgrn(p_l, lb, s_c, consts)
    hgn = hg_norm_g[0].reshape(1, w)
    pw = pool_w[0].astype(BF16)
    ps = pool_scale[0].reshape(1, w)
    ow0 = out_w[0].astype(BF16)
    o1 = q_rank
    o2 = o1 + kv_rank
    o3 = o2 + MLA_ROPE
    w1 = od_in_w[0]
    kr_pad = jnp.zeros((d, LANES - MLA_ROPE), F32)
    w_in1 = jnp.concatenate([w1[:, o3:], w1[:, :o1], w1[:, o1:o2], w1[:, o2:o3], kr_pad], axis=1).astype(BF16)
    w_in1c = w_in1[:, d_inner + q_rank:]
    g1 = norm_g[1].reshape(1, d)
    bcast = lambda m: jnp.broadcast_to(m, (bsz, 3, d))
    _, p1_c = _even_post(of_c, ob_c, p_c, ctx, bcast(mod_c[0]), hgn, pw, ps, ow0, bcast(mod_c[1]), g1, w_in1c)
    hl1, p1_l = _even_post(of_l, ob_l, p_l, x, mod_l[0], hgn, pw, ps, ow0, mod_l[1], g1, w_in1)

    kvw = kvb_w[0].reshape(kv_rank, MLA_HEADS, MLA_NOPE + MLA_V)
    wuk = kvw[..., :MLA_NOPE].reshape(kv_rank, MLA_HEADS * MLA_NOPE).astype(BF16)
    wuvt = jnp.transpose(kvw[..., MLA_NOPE:], (1, 2, 0)).reshape(MLA_HEADS * MLA_V, kv_rank).astype(BF16)
    wqt = jnp.transpose(qb_w[0]).astype(BF16)
    kvg = kva_norm_g[0].reshape(1, kv_rank)
    qag = qa_norm_g[0].reshape(1, q_rank)
    tab_k, tab_q = _rope_tables(t)
    kc, vtc = _mla_kv(p1_c, 0, kv_rank // LANES, kvg, wuk, wuvt, None, lc)
    kl, vtl = _mla_kv(p1_l, (d_inner + q_rank) // kv_rank, (d_inner + q_rank + kv_rank) // LANES, kvg, wuk, wuvt,
                      tab_k, min(KV_CHUNK, t))
    qt = _mla_q(p1_l, d_inner // q_rank, qag, wqt, *tab_q)
    y = _attn(qt, kc, vtc, kl, vtl, p1_l)
    return _out_final(y, hl1, mod_l[1], out_w[1].astype(BF16), final_norm_g.reshape(1, d))
```

```python
import functools

import numpy as np
import jax
import jax.numpy as jnp
from jax import lax
from jax.experimental import pallas as pl
from jax.experimental.pallas import tpu as pltpu

F32 = jnp.float32
BF16 = jnp.bfloat16

EPS = 1e-6
GRID_W = 64
HG_DK = 128
POOL_WINDOWS = (2, 4, 8, 16)
MLA_HEADS = 16
MLA_NOPE = 128
MLA_ROPE = 64
MLA_V = 128
MLA_QK = MLA_NOPE + MLA_ROPE
QK_PAD = 256
BF16_ROWS = 16
VT_ROWS = MLA_V + BF16_ROWS
MLA_SCALE = MLA_QK ** -0.5
LOG2_E = 1.4426950408889634
ROPE_FREQ = MLA_ROPE // 4
ROPE_BASE = 10000.0

LANES = 128
SUBLANES = 8
VMEM_LIMIT = 48 * 1024 * 1024

HG_CHUNK = 64
HG_TILE = 512
TOK_TILE = 256
IN_PROJ_ROWS = 1024
IN_PROJ_COL_PARTS = 4
KV_CHUNK = 512
Q_TILE = 4096
Q_SUB = 256
POOL_HALO = 8
IN_PROJ_SHIFT = 5


def _dot(a, b):
    return jnp.dot(a, b, preferred_element_type=F32)


def _dot_nt(a, b):
    return lax.dot_general(a, b, (((1,), (1,)), ((), ())), preferred_element_type=F32)


def _dot_tn(a, b):
    return lax.dot_general(a, b, (((0,), (0,)), ((), ())), preferred_element_type=F32)


def _silu(x):
    h = 0.5 * x
    return h + h * jnp.tanh(h)


def _split_bf16(x):
    hi = x.astype(BF16)
    lo = (x - hi.astype(F32)).astype(BF16)
    return hi, lo


def _params(*sem):
    return pltpu.CompilerParams(dimension_semantics=sem, vmem_limit_bytes=VMEM_LIMIT)


def _ada_kernel(c_ref, w_ref, b_ref, o_ref):
    c = c_ref[...]
    s_hi, s_lo = _split_bf16(_silu(c))
    w_hi, w_lo = _split_bf16(w_ref[...])
    o_ref[...] = _dot(s_hi, w_hi) + _dot(s_lo, w_hi) + _dot(s_hi, w_lo) + b_ref[...]


def _ada(cond, ada_w, ada_b):
    depth, d, _ = ada_w.shape
    r = cond.shape[0]
    return pl.pallas_call(
        _ada_kernel,
        grid=(depth, 3),
        in_specs=[
            pl.BlockSpec((r, d), lambda l, j: (0, 0)),
            pl.BlockSpec((None, d, d), lambda l, j: (l, 0, j)),
            pl.BlockSpec((None, 1, d), lambda l, j: (l, 0, j)),
        ],
        out_specs=pl.BlockSpec((None, r, d), lambda l, j: (l, 0, j)),
        out_shape=jax.ShapeDtypeStruct((depth, r, 3 * d), F32),
        compiler_params=_params("parallel", "parallel"),
        name="ada_modulation",
    )(cond, ada_w, ada_b.reshape(depth, 1, 3 * d))


def _rms(x, g):
    return x * lax.rsqrt(jnp.mean(x * x, axis=-1, keepdims=True) + EPS) * g


def _modnorm_mm_kernel(x_ref, mod_ref, g_ref, w_ref, o_ref, z_ref):
    @pl.when(pl.program_id(2) == 0)
    def _():
        y = _rms(x_ref[...], g_ref[...])
        z_ref[...] = (y * (1.0 + mod_ref[1:2, :]) + mod_ref[0:1, :]).astype(BF16)

    o_ref[...] = _dot(z_ref[...], w_ref[...])


def _modnorm_mm(x, mod, g, w, tm, tn, name):
    bx, r, d = x.shape
    n = w.shape[1]
    tm = min(tm, r)
    return pl.pallas_call(
        _modnorm_mm_kernel,
        grid=(bx, r // tm, n // tn),
        in_specs=[
            pl.BlockSpec((None, tm, d), lambda b, i, j: (b, i, 0)),
            pl.BlockSpec((None, 3, d), lambda b, i, j: (b, 0, 0)),
            pl.BlockSpec((1, d), lambda b, i, j: (0, 0)),
            pl.BlockSpec((d, tn), lambda b, i, j: (0, j)),
        ],
        out_specs=pl.BlockSpec((None, tm, tn), lambda b, i, j: (b, i, j)),
        out_shape=jax.ShapeDtypeStruct((bx, r, n), F32),
        scratch_shapes=[pltpu.VMEM((tm, d), BF16)],
        compiler_params=_params("parallel", "parallel", "arbitrary"),
        name=name,
    )(x, mod, g, w)


def _decay_bounds_kernel(a_ref, o_ref, *, layer):
    slots = a_ref[...]
    e = jnp.exp(slots - jnp.max(slots, axis=1, keepdims=True))
    o_ref[...] = jnp.sum(e[:, :layer + 1], axis=1) / jnp.sum(e, axis=1)


def _decay_bounds(hg_lb, layer):
    ndir, nslot, w = hg_lb.shape
    return pl.pallas_call(
        functools.partial(_decay_bounds_kernel, layer=layer),
        out_shape=jax.ShapeDtypeStruct((ndir, 1, w), F32),
        name="decay_bounds",
    )(hg_lb.reshape(ndir, nslot, 1, w))


def _hgrn_levels(c):
    w = c // 2
    out = []
    while w >= 1:
        out.append(w)
        w //= 2
    return tuple(out)


def _hgrn_constants(c):
    t = np.arange(c)
    tri = np.tril(np.ones((c, c), np.float32))
    masks = []
    for w in _hgrn_levels(c):
        blk = t // (2 * w)
        first = (t % (2 * w)) < w
        masks.append(((blk[:, None] == blk[None, :]) & (~first[:, None]) & first[None, :]).astype(np.float32))
    masks = np.stack(masks)
    tri2 = np.stack([tri, tri[::-1, ::-1]])
    m2 = np.stack([masks, masks[:, ::-1, ::-1]])
    return jnp.asarray(tri2, BF16), jnp.asarray(m2, F32)


def _hgrn_level_operand(b, g, k, q, w, d):
    c, width = b.shape
    row = lax.broadcasted_iota(jnp.int32, (c, 1), 0)
    keys_first = d == 0
    if w >= SUBLANES:
        ref_off = w - 1 if d == 0 else w
        pieces = []
        for r0 in range(0, c, 2 * w):
            bref = jnp.broadcast_to(b[r0 + ref_off:r0 + ref_off + 1, :], (w, width))
            for half in range(2):
                sl = slice(r0 + half * w, r0 + (half + 1) * w)
                if (half == 0) == keys_first:
                    pieces.append(k[sl] * jnp.exp2(bref - b[sl]))
                else:
                    pieces.append(q[sl] * jnp.exp2(b[sl] - bref))
        return jnp.concatenate(pieces, axis=0)
    before = (row % (2 * w)) < w
    kq = jnp.where(before == keys_first, k, q)
    if w == 1:
        moving = (row % 2 == 1) if d == 0 else (row % 2 == 0)
        return kq * jnp.exp2(jnp.where(moving, g, 0.0))
    ref_off = w - 1 if d == 0 else w
    sub = lax.broadcasted_iota(jnp.int32, (SUBLANES, 1), 0)
    pieces = []
    for r0 in range(0, c, SUBLANES):
        lo = jnp.broadcast_to(b[r0 + ref_off:r0 + ref_off + 1, :], (SUBLANES, width))
        if 2 * w == SUBLANES:
            pieces.append(lo)
        else:
            hi = jnp.broadcast_to(b[r0 + 2 * w + ref_off:r0 + 2 * w + ref_off + 1, :], (SUBLANES, width))
            pieces.append(jnp.where(sub < 2 * w, lo, hi))
    bref = jnp.concatenate(pieces, axis=0)
    return kq * jnp.exp2(-jnp.abs(b - bref))


def _hgrn_wide(q_ref, f_ref, v_ref, lb, tri, d, r0, *, c):
    last = c - 1 if d == 0 else 0
    rows = pl.ds(r0, c)
    half = 0.5 * (1.0 - lb)
    f = (lb + half) + half * jnp.tanh(0.5 * f_ref[rows, :])
    g = jnp.log2(f)
    g_hi, g_lo = _split_bf16(g)
    b = _dot(tri, g_hi) + _dot(tri, g_lo)
    q = _silu(q_ref[rows, :])
    k = 1.0 - f
    v = v_ref[rows, :]
    bl = b[last:last + 1, :]
    return dict(d=d, rows=rows, g=g, b=b, q=q, k=k, v=v, vb=v.astype(BF16), qk=q * k,
                qe=(q * jnp.exp2(b)).astype(BF16), kend=(k * jnp.exp2(bl - b)).astype(BF16), ebl=jnp.exp2(bl))


def _head(x, h):
    return x[:, h * HG_DK:(h + 1) * HG_DK]


def _hgrn_pairs(s, h, mk_ref, *, c):
    d = s["d"]
    att = None
    for l, w in enumerate(_hgrn_levels(c)):
        x = _hgrn_level_operand(_head(s["b"], h), _head(s["g"], h), _head(s["k"], h), _head(s["q"], h), w, d)
        x = x.astype(BF16)
        t = mk_ref[d, l] * _dot_nt(x, x)
        att = t if att is None else att + t
    return att


def _hgrn_finish(s, h, att, o_ref, st_ref):
    d = s["d"]
    vb = _head(s["vb"], h)
    inter = _dot_nt(_head(s["qe"], h), st_ref[d, h].astype(BF16))
    diag = jnp.sum(_head(s["qk"], h), axis=-1, keepdims=True)
    o_ref[s["rows"], h * HG_DK:(h + 1) * HG_DK] = inter + _dot(att.astype(BF16), vb) + diag * _head(s["v"], h)
    st_ref[d, h] = _head(s["ebl"], h) * st_ref[d, h] + _dot_tn(vb, _head(s["kend"], h))


def _hgrn_kernel(qf_ref, ff_ref, vf_ref, qb_ref, fb_ref, vb_ref, lb_ref, tri_ref, mk_ref, s0_ref,
                 of_ref, ob_ref, sout_ref, st_ref, *, tb, c, nh):
    nchunk = tb // c

    @pl.when(pl.program_id(1) == 0)
    def _():
        st_ref[...] = s0_ref[...]

    def body(cc, carry):
        rf = pl.multiple_of(cc * c, c)
        rb = pl.multiple_of((nchunk - 1 - cc) * c, c)
        sides = ((_hgrn_wide(qf_ref, ff_ref, vf_ref, lb_ref[0], tri_ref[0], 0, rf, c=c), of_ref),
                 (_hgrn_wide(qb_ref, fb_ref, vb_ref, lb_ref[1], tri_ref[1], 1, rb, c=c), ob_ref))
        pending = [None, None]
        for h in range(nh + 1):
            cur = [_hgrn_pairs(s, h, mk_ref, c=c) if h < nh else None for s, _ in sides]
            for (s, o_ref), p in zip(sides, pending):
                if p is not None:
                    _hgrn_finish(s, h - 1, p, o_ref, st_ref)
            pending = cur
        return carry

    lax.fori_loop(0, nchunk, body, 0, unroll=True)

    @pl.when(pl.program_id(1) == pl.num_programs(1) - 1)
    def _():
        sout_ref[...] = st_ref[...]


def _hgrn(p, lb, s0, consts):
    bsz, r, _ = p.shape
    w = lb.shape[-1]
    nh = w // HG_DK
    tb = min(HG_TILE, r)
    c = HG_CHUNK
    nb = r // tb
    tri2, m2 = consts
    fwd = lambda col: pl.BlockSpec((None, tb, w), lambda b, s: (b, s, col))
    bwd = lambda col: pl.BlockSpec((None, tb, w), lambda b, s: (b, nb - 1 - s, col))
    const = lambda arr: pl.BlockSpec(arr.shape, lambda b, s: (0,) * arr.ndim)
    st_spec = pl.BlockSpec((None, 2, nh, HG_DK, HG_DK), lambda b, s: (b, 0, 0, 0, 0))
    kern = functools.partial(_hgrn_kernel, tb=tb, c=c, nh=nh)
    return pl.pallas_call(
        kern,
        grid=(bsz, nb),
        in_specs=[fwd(0), fwd(1), fwd(3), bwd(0), bwd(2), bwd(3), const(lb), const(tri2), const(m2),
                  st_spec],
        out_specs=[
            pl.BlockSpec((None, tb, w), lambda b, s: (b, s, 0)),
            pl.BlockSpec((None, tb, w), lambda b, s: (b, nb - 1 - s, 0)),
            st_spec,
        ],
        out_shape=[
            jax.ShapeDtypeStruct((bsz, r, w), F32),
            jax.ShapeDtypeStruct((bsz, r, w), F32),
            jax.ShapeDtypeStruct(s0.shape, F32),
        ],
        scratch_shapes=[pltpu.VMEM((2, nh, HG_DK, HG_DK), F32)],
        compiler_params=_params("parallel", "arbitrary"),
        name="hgrn2_scan",
    )(p, p, p, p, p, p, lb, tri2, m2, s0)


def _even_post_kernel(of_ref, ob_ref, ga_ref, u_ref, gb_ref, up_ref, un_ref, h_ref, mod_ref, hgn_ref, pw_ref,
                      ps_ref, ow_ref, modn_ref, gn_ref, wn_ref, o_ref, pn_ref, ext_ref, y_ref, z_ref, zo_ref, *,
                      tb, seq, nh):
    step = pl.program_id(1)
    last = pl.num_programs(1) - 2
    j = jnp.minimum(step, last)
    w = nh * HG_DK

    @pl.when(step == 0)
    def _():
        z_ref[...] = jnp.zeros(z_ref.shape, BF16)

    zo_ref[...] = z_ref[...]
    n_parts = nh + len(POOL_WINDOWS)
    n_blk = pn_ref.shape[1] // LANES
    per = -(-n_blk // n_parts)

    def in_proj_part(i):
        c0 = min(i * per, n_blk) * LANES
        c1 = min((i + 1) * per, n_blk) * LANES
        if 0 <= i and c0 < c1:
            pn_ref[:, c0:c1] = _dot(zo_ref[...], wn_ref[:, c0:c1])

    o = of_ref[...] + ob_ref[...]
    for h in range(nh):
        sl = slice(h * HG_DK, (h + 1) * HG_DK)
        y_ref[:, sl] = (_rms(o[:, sl], hgn_ref[:, sl]) * _silu(ga_ref[:, sl])).astype(BF16)
        in_proj_part(h - IN_PROJ_SHIFT)
    u = u_ref[...]
    ext_ref[0:POOL_HALO, :] = jnp.where(j > 0, up_ref[...], 0.0)
    ext_ref[POOL_HALO:POOL_HALO + tb, :] = u
    ext_ref[POOL_HALO + tb:, :] = jnp.where(j < last, un_ref[...], 0.0)
    t = j * tb + lax.broadcasted_iota(jnp.int32, (tb, 1), 0)
    grp = w // len(POOL_WINDOWS)
    for gi, win in enumerate(POOL_WINDOWS):
        sl = slice(gi * grp, (gi + 1) * grp)
        acc = ext_ref[POOL_HALO - win // 2:POOL_HALO - win // 2 + tb, sl]
        for off in range(-win // 2 + 1, win // 2):
            acc = acc + ext_ref[POOL_HALO + off:POOL_HALO + off + tb, sl]
        cnt = (jnp.minimum(t + win // 2, seq) - jnp.maximum(t - win // 2, 0)).astype(F32)
        yp = acc * (1.0 / cnt) - u[:, sl]
        yb = _dot(yp.astype(BF16), pw_ref[gi]) * ps_ref[:, sl]
        y_ref[:, w + gi * grp:w + (gi + 1) * grp] = (yb * _silu(gb_ref[:, sl])).astype(BF16)
        in_proj_part(nh + gi - IN_PROJ_SHIFT)
    hn = h_ref[...] + mod_ref[2:3, :] * _dot(y_ref[...], ow_ref[...])
    o_ref[...] = hn
    for i in range(n_parts - IN_PROJ_SHIFT, n_parts):
        in_proj_part(i)
    z_ref[...] = (_rms(hn, gn_ref[...]) * (1.0 + modn_ref[1:2, :]) + modn_ref[0:1, :]).astype(BF16)


def _even_post(o_f, o_b, p, h, mod, hgn, pool_w, pool_scale, out_w, mod_n, g_n, w_n):
    bsz, r, w = o_f.shape
    d = h.shape[-1]
    tb = min(TOK_TILE, r)
    nb = r // tb
    hb = tb // POOL_HALO
    nh = w // HG_DK
    n_next = w_n.shape[1]
    cur = lambda s: jnp.minimum(s, nb - 1)
    tok = lambda col: pl.BlockSpec((None, tb, w), lambda b, s: (b, cur(s), col))
    const = lambda arr: pl.BlockSpec(arr.shape, lambda b, s: (0,) * arr.ndim)
    kern = functools.partial(_even_post_kernel, tb=tb, seq=r, nh=nh)
    return pl.pallas_call(
        kern,
        grid=(bsz, nb + 1),
        in_specs=[
            tok(0), tok(0), tok(4), tok(5), tok(6),
            pl.BlockSpec((None, POOL_HALO, w), lambda b, s: (b, jnp.maximum(cur(s) * hb - 1, 0), 5)),
            pl.BlockSpec((None, POOL_HALO, w), lambda b, s: (b, jnp.minimum((cur(s) + 1) * hb, nb * hb - 1), 5)),
            pl.BlockSpec((None, tb, d), lambda b, s: (b, cur(s), 0)),
            pl.BlockSpec((None, 3, d), lambda b, s: (b, 0, 0)),
            const(hgn), const(pool_w), const(pool_scale), const(out_w),
            pl.BlockSpec((None, 3, d), lambda b, s: (b, 0, 0)), const(g_n), const(w_n),
        ],
        out_specs=[pl.BlockSpec((None, tb, d), lambda b, s: (b, cur(s), 0)),
                   pl.BlockSpec((None, tb, n_next), lambda b, s: (b, jnp.maximum(s - 1, 0), 0))],
        out_shape=[jax.ShapeDtypeStruct((bsz, r, d), F32), jax.ShapeDtypeStruct((bsz, r, n_next), F32)],
        scratch_shapes=[pltpu.VMEM((tb + 2 * POOL_HALO, w), F32), pltpu.VMEM((tb, 2 * w), BF16),
                        pltpu.VMEM((tb, d), BF16), pltpu.VMEM((tb, d), BF16)],
        compiler_params=_params("parallel", "arbitrary"),
        name="even_post",
    )(o_f, o_b, p, p, p, p, p, h, mod, hgn, pool_w, pool_scale, out_w, mod_n, g_n, w_n)


def _mla_kv_kernel(*refs, rope):
    if rope:
        ckv_ref, kr_ref, g_ref, wuk_ref, wuvt_ref, cos_ref, sin_ref, kcat_ref, vt_ref = refs
    else:
        ckv_ref, kr_ref, g_ref, wuk_ref, wuvt_ref, kcat_ref, vt_ref = refs
    cn = _rms(ckv_ref[...], g_ref[...]).astype(BF16)
    kn = _dot(cn, wuk_ref[...])
    kr = kr_ref[...]
    if rope:
        lane = lax.broadcasted_iota(jnp.int32, kr.shape, 1)
        swapped = jnp.where((lane % (2 * ROPE_FREQ)) < ROPE_FREQ,
                            pltpu.roll(kr, LANES - ROPE_FREQ, 1), pltpu.roll(kr, ROPE_FREQ, 1))
        kr = kr * cos_ref[...] + swapped * sin_ref[...]
    kr = kr.astype(BF16)
    ones_rows = (lax.broadcasted_iota(jnp.int32, (VT_ROWS - MLA_V, kr.shape[0]), 0) == 0).astype(BF16)
    vt = _dot_nt(wuvt_ref[...], cn)
    for h in range(MLA_HEADS):
        kcat_ref[h, :, 0:MLA_NOPE] = kn[:, h * MLA_NOPE:(h + 1) * MLA_NOPE].astype(BF16)
        kcat_ref[h, :, MLA_NOPE:] = kr
        vt_ref[h, 0:MLA_V, :] = vt[h * MLA_V:(h + 1) * MLA_V].astype(BF16)
        vt_ref[h, MLA_V:, :] = ones_rows


def _mla_kv(p, ckv_blk, kr_blk, g, wuk, wuvt, tables, tb):
    bsz, r, _ = p.shape
    rank = g.shape[-1]
    nb = r // tb
    const = lambda arr: pl.BlockSpec(arr.shape, lambda b, j: (0,) * arr.ndim)
    in_specs = [
        pl.BlockSpec((None, tb, rank), lambda b, j: (b, j, ckv_blk)),
        pl.BlockSpec((None, tb, LANES), lambda b, j: (b, j, kr_blk)),
        const(g), const(wuk), const(wuvt),
    ]
    args = [p, p, g, wuk, wuvt]
    if tables is not None:
        in_specs += [pl.BlockSpec((tb, LANES), lambda b, j: (j, 0))] * 2
        args += list(tables)
    return pl.pallas_call(
        functools.partial(_mla_kv_kernel, rope=tables is not None),
        grid=(bsz, nb),
        in_specs=in_specs,
        out_specs=[
            pl.BlockSpec((None, MLA_HEADS, None, tb, QK_PAD), lambda b, j: (b, 0, j, 0, 0)),
            pl.BlockSpec((None, MLA_HEADS, None, VT_ROWS, tb), lambda b, j: (b, 0, j, 0, 0)),
        ],
        out_shape=[
            jax.ShapeDtypeStruct((bsz, MLA_HEADS, nb, tb, QK_PAD), BF16),
            jax.ShapeDtypeStruct((bsz, MLA_HEADS, nb, VT_ROWS, tb), BF16),
        ],
        compiler_params=_params("parallel", "parallel"),
        name="mla_kv_rope" if tables is not None else "mla_kv",
    )(*args)


def _mla_q_kernel(cq_ref, g_ref, wqt_ref, cos_ref, sin_ref, qt_ref):
    cn = _rms(cq_ref[...], g_ref[...]).astype(BF16)
    f = ROPE_FREQ
    qt_all = _dot_nt(wqt_ref[...], cn) * (MLA_SCALE * LOG2_E)
    for h in range(MLA_HEADS):
        qt = qt_all[h * MLA_QK:(h + 1) * MLA_QK]
        qt_ref[h, 0:MLA_NOPE, :] = qt[0:MLA_NOPE].astype(BF16)
        for ax in range(2):
            r0 = MLA_NOPE + ax * 2 * f
            x1 = qt[r0:r0 + f]
            x2 = qt[r0 + f:r0 + 2 * f]
            co = cos_ref[ax]
            si = sin_ref[ax]
            qt_ref[h, r0:r0 + f, :] = (x1 * co - x2 * si).astype(BF16)
            qt_ref[h, r0 + f:r0 + 2 * f, :] = (x2 * co + x1 * si).astype(BF16)
        qt_ref[h, MLA_QK:, :] = jnp.zeros((QK_PAD - MLA_QK, cn.shape[0]), BF16)


def _mla_q(p, cq_blk, g, wqt, cos_t, sin_t):
    bsz, t, _ = p.shape
    rank = g.shape[-1]
    tm = min(2 * TOK_TILE, t)
    const = lambda arr: pl.BlockSpec(arr.shape, lambda b, j: (0,) * arr.ndim)
    tab = pl.BlockSpec((2, ROPE_FREQ, tm), lambda b, j: (0, 0, j))
    return pl.pallas_call(
        _mla_q_kernel,
        grid=(bsz, t // tm),
        in_specs=[pl.BlockSpec((None, tm, rank), lambda b, j: (b, j, cq_blk)), const(g), const(wqt), tab, tab],
        out_specs=pl.BlockSpec((None, MLA_HEADS, QK_PAD, tm), lambda b, j: (b, 0, 0, j)),
        out_shape=jax.ShapeDtypeStruct((bsz, MLA_HEADS, QK_PAD, t), BF16),
        compiler_params=_params("parallel", "parallel"),
        name="mla_q",
    )(p, g, wqt, cos_t, sin_t)


def _attn_kernel(qt_ref, kc_ref, vtc_ref, kl_ref, vtl_ref, g_ref, o_ref, m_ref, acc_ref, s_ref, mx_ref, *, tq, n_lat):
    nsub = tq // Q_SUB
    m_ref[...] = jnp.full(m_ref.shape, -jnp.inf, F32)
    acc_ref[...] = jnp.zeros(acc_ref.shape, F32)

    def scores(k, nxt, g):
        s = _dot(k, qt_ref[:, g * Q_SUB:(g + 1) * Q_SUB])
        s_ref[nxt, g, 0:k.shape[0], :] = s
        mx_ref[nxt, g] = jnp.max(s, axis=0, keepdims=True)

    def substep(k_next, vt_cur, cur, nxt):
        rows = vt_cur.shape[1]
        for g in range(nsub):
            sl = slice(g * Q_SUB, (g + 1) * Q_SUB)
            scores(k_next, nxt, g)
            m_old = m_ref[:, sl]
            m_new = jnp.maximum(m_old, mx_ref[cur, g])
            alpha = jnp.exp2(m_old - m_new)
            p = jnp.exp2(s_ref[cur, g, 0:rows, :] - m_new)
            acc_ref[:, sl] = alpha * acc_ref[:, sl] + _dot(vt_cur, p.astype(BF16))
            m_ref[:, sl] = m_new

    kc = kc_ref[...]
    for g in range(nsub):
        scores(kc, 0, g)
    substep(kl_ref[0], vtc_ref[...], 0, 1)

    per_trip = max(u for u in (2, 4, 8) if n_lat % u == 0)

    def body(j, carry):
        for u in range(per_trip):
            a = per_trip * j + u
            substep(kl_ref[jnp.minimum(a + 1, n_lat - 1)], vtl_ref[a], (1 + u) % 2, u % 2)
        return carry

    lax.fori_loop(0, n_lat // per_trip, body, 0)
    o = (acc_ref[0:MLA_V, :] * (1.0 / acc_ref[MLA_V:MLA_V + 1, :])).T
    o_ref[...] = (o * _silu(g_ref[...])).astype(o_ref.dtype)


def _attn(qt, kc, vtc, kl, vtl, p):
    bsz, nh, _, t = qt.shape
    lc = kc.shape[3]
    tq = min(Q_TILE, t)
    n_lat = kl.shape[2]
    kv = kl.shape[3]
    assert kc.shape[2] == 1 and lc <= kv and n_lat % 2 == 0
    kern = functools.partial(_attn_kernel, tq=tq, n_lat=n_lat)
    ctx5 = lambda arr: pl.BlockSpec((None, None, None) + arr.shape[3:], lambda b, h, i: (b, h, 0, 0, 0))
    full5 = lambda arr: pl.BlockSpec((None, None) + arr.shape[2:], lambda b, h, i: (b, h, 0, 0, 0))
    return pl.pallas_call(
        kern,
        grid=(bsz, nh, t // tq),
        in_specs=[
            pl.BlockSpec((None, None, QK_PAD, tq), lambda b, h, i: (b, h, 0, i)),
            ctx5(kc), ctx5(vtc), full5(kl), full5(vtl),
            pl.BlockSpec((None, tq, MLA_V), lambda b, h, i: (b, i, h)),
        ],
        out_specs=pl.BlockSpec((None, tq, MLA_V), lambda b, h, i: (b, i, h)),
        out_shape=jax.ShapeDtypeStruct((bsz, t, nh * MLA_V), BF16),
        scratch_shapes=[pltpu.VMEM((1, tq), F32), pltpu.VMEM((VT_ROWS, tq), F32),
                        pltpu.VMEM((2, tq // Q_SUB, kv, Q_SUB), F32), pltpu.VMEM((2, tq // Q_SUB, 1, Q_SUB), F32)],
        compiler_params=_params("parallel", "parallel", "arbitrary"),
        name="mla_attention",
    )(qt, kc, vtc, kl, vtl, p)


def _out_final_kernel(y_ref, h_ref, mod_ref, ow_ref, g_ref, o_ref):
    hn = h_ref[...] + mod_ref[2:3, :] * _dot(y_ref[...], ow_ref[...])
    o_ref[...] = _rms(hn, g_ref[...])


def _out_final(y, h, mod, out_w, g):
    bsz, t, d = h.shape
    wi = y.shape[-1]
    tm = min(2 * TOK_TILE, t)
    return pl.pallas_call(
        _out_final_kernel,
        grid=(bsz, t // tm),
        in_specs=[
            pl.BlockSpec((None, tm, wi), lambda b, j: (b, j, 0)),
            pl.BlockSpec((None, tm, d), lambda b, j: (b, j, 0)),
            pl.BlockSpec((None, 3, d), lambda b, j: (b, 0, 0)),
            pl.BlockSpec((wi, d), lambda b, j: (0, 0)),
            pl.BlockSpec((1, d), lambda b, j: (0, 0)),
        ],
        out_specs=pl.BlockSpec((None, tm, d), lambda b, j: (b, j, 0)),
        out_shape=jax.ShapeDtypeStruct((bsz, t, d), F32),
        compiler_params=_params("parallel", "parallel"),
        name="out_final",
    )(y, h, mod, out_w, g)


def _rope_tables(n_tokens):
    rows = n_tokens // GRID_W
    pos_r = jnp.repeat(jnp.arange(rows), GRID_W).astype(F32)
    pos_c = jnp.tile(jnp.arange(GRID_W), rows).astype(F32)
    inv = ROPE_BASE ** (-2.0 * jnp.arange(ROPE_FREQ, dtype=F32) / (MLA_ROPE // 2))
    ang = jnp.stack([pos_r[:, None] * inv, pos_c[:, None] * inv], axis=1)
    cos, sin = jnp.cos(ang), jnp.sin(ang)
    pad = LANES - MLA_ROPE
    cos_k = jnp.pad(jnp.stack([cos, cos], axis=2).reshape(n_tokens, MLA_ROPE), ((0, 0), (0, pad)))
    sin_k = jnp.pad(jnp.stack([-sin, sin], axis=2).reshape(n_tokens, MLA_ROPE), ((0, 0), (0, pad)))
    cos_q = jnp.transpose(cos, (1, 2, 0))
    sin_q = jnp.transpose(sin, (1, 2, 0))
    return (cos_k, sin_k), (cos_q, sin_q)


def kernel(x, c, ctx, c_ctx, ada_w, ada_b, norm_g, out_w, ev_in_w, hg_lb, hg_norm_g, pool_w, pool_scale,
           od_in_w, qa_norm_g, qb_w, kva_norm_g, kvb_w, final_norm_g):
    bsz, t, d = x.shape
    lc = ctx.shape[1]
    depth = ada_w.shape[0]
    assert depth == 2 and t % (2 * TOK_TILE) == 0 and lc % TOK_TILE == 0 and t % GRID_W == 0
    w = hg_norm_g.shape[-1]
    nh = w // HG_DK
    q_rank = qa_norm_g.shape[-1]
    kv_rank = kva_norm_g.shape[-1]
    d_inner = out_w.shape[1]

    n_cond = -(-(bsz + 1) // SUBLANES) * SUBLANES
    cond = jnp.zeros((n_cond, d), F32).at[:bsz].set(c).at[bsz].set(c_ctx)
    mods = _ada(cond, ada_w, ada_b).reshape(depth, n_cond, 3, d)
    mod_l = [mods[l, :bsz] for l in range(depth)]
    mod_c = [mods[l, bsz:bsz + 1] for l in range(depth)]

    lb = _decay_bounds(hg_lb, 0)
    w_in0 = ev_in_w[0].astype(BF16)
    g0 = norm_g[0].reshape(1, d)
    ctx_flat = ctx.reshape(1, bsz * lc, d)
    n_in0 = w_in0.shape[1]
    tn0 = n_in0 // IN_PROJ_COL_PARTS
    p_c = _modnorm_mm(ctx_flat, mod_c[0], g0, w_in0, IN_PROJ_ROWS, tn0, "in_proj0_ctx").reshape(bsz, lc, n_in0)
    p_l = _modnorm_mm(x, mod_l[0], g0, w_in0, IN_PROJ_ROWS, tn0, "in_proj0")
    consts = _hgrn_constants(HG_CHUNK)
    s0 = jnp.zeros((bsz, 2, nh, HG_DK, HG_DK), F32)
    of_c, ob_c, s_c = _hgrn(p_c, lb, s0, consts)
    of_l, ob_l, _ = _hgrn(p_l, lb, s_c, consts)
    hgn = hg_norm_g[0].reshape(1, w)
    pw = pool_w[0].astype(BF16)
    ps = pool_scale[0].reshape(1, w)
    ow0 = out_w[0].astype(BF16)
    o1 = q_rank
    o2 = o1 + kv_rank
    o3 = o2 + MLA_ROPE
    w1 = od_in_w[0]
    kr_pad = jnp.zeros((d, LANES - MLA_ROPE), F32)
    w_in1 = jnp.concatenate([w1[:, o3:], w1[:, :o1], w1[:, o1:o2], w1[:, o2:o3], kr_pad], axis=1).astype(BF16)
    w_in1c = w_in1[:, d_inner + q_rank:]
    g1 = norm_g[1].reshape(1, d)
    bcast = lambda m: jnp.broadcast_to(m, (bsz, 3, d))
    _, p1_c = _even_post(of_c, ob_c, p_c, ctx, bcast(mod_c[0]), hgn, pw, ps, ow0, bcast(mod_c[1]), g1, w_in1c)
    hl1, p1_l = _even_post(of_l, ob_l, p_l, x, mod_l[0], hgn, pw, ps, ow0, mod_l[1], g1, w_in1)

    kvw = kvb_w[0].reshape(kv_rank, MLA_HEADS, MLA_NOPE + MLA_V)
    wuk = kvw[..., :MLA_NOPE].reshape(kv_rank, MLA_HEADS * MLA_NOPE).astype(BF16)
    wuvt = jnp.transpose(kvw[..., MLA_NOPE:], (1, 2, 0)).reshape(MLA_HEADS * MLA_V, kv_rank).astype(BF16)
    wqt = jnp.transpose(qb_w[0]).astype(BF16)
    kvg = kva_norm_g[0].reshape(1, kv_rank)
    qag = qa_norm_g[0].reshape(1, q_rank)
    tab_k, tab_q = _rope_tables(t)
    kc, vtc = _mla_kv(p1_c, 0, kv_rank // LANES, kvg, wuk, wuvt, None, lc)
    kl, vtl = _mla_kv(p1_l, (d_inner + q_rank) // kv_rank, (d_inner + q_rank + kv_rank) // LANES, kvg, wuk, wuvt,
                      tab_k, min(KV_CHUNK, t))
    qt = _mla_q(p1_l, d_inner // q_rank, qag, wqt, *tab_q)
    y = _attn(qt, kc, vtc, kl, vtl, p1_l)
    return _out_final(y, hl1, mod_l[1], out_w[1].astype(BF16), final_norm_g.reshape(1, d))
```

```python
import functools

import numpy as np
import jax
import jax.numpy as jnp
from jax import lax
from jax.experimental import pallas as pl
from jax.experimental.pallas import tpu as pltpu

F32 = jnp.float32
BF16 = jnp.bfloat16

EPS = 1e-6
GRID_W = 64
HG_DK = 128
POOL_WINDOWS = (2, 4, 8, 16)
MLA_HEADS = 16
MLA_NOPE = 128
MLA_ROPE = 64
MLA_V = 128
MLA_QK = MLA_NOPE + MLA_ROPE
QK_PAD = 256
BF16_ROWS = 16
VT_ROWS = MLA_V + BF16_ROWS
MLA_SCALE = MLA_QK ** -0.5
LOG2_E = 1.4426950408889634
ROPE_FREQ = MLA_ROPE // 4
ROPE_BASE = 10000.0

LANES = 128
SUBLANES = 8
VMEM_LIMIT = 48 * 1024 * 1024

HG_CHUNK = 64
HG_TILE = 512
TOK_TILE = 256
IN_PROJ_ROWS = 1024
IN_PROJ_COL_PARTS = 4
KV_CHUNK = 512
Q_TILE = 4096
Q_SUB = 256
POOL_HALO = 8
IN_PROJ_SHIFT = 5


def _dot(a, b):
    return jnp.dot(a, b, preferred_element_type=F32)


def _dot_nt(a, b):
    return lax.dot_general(a, b, (((1,), (1,)), ((), ())), preferred_element_type=F32)


def _dot_tn(a, b):
    return lax.dot_general(a, b, (((0,), (0,)), ((), ())), preferred_element_type=F32)


def _silu(x):
    h = 0.5 * x
    return h + h * jnp.tanh(h)


def _split_bf16(x):
    hi = x.astype(BF16)
    lo = (x - hi.astype(F32)).astype(BF16)
    return hi, lo


def _params(*sem):
    return pltpu.CompilerParams(dimension_semantics=sem, vmem_limit_bytes=VMEM_LIMIT)


def _ada_kernel(c_ref, w_ref, b_ref, o_ref):
    c = c_ref[...]
    s_hi, s_lo = _split_bf16(_silu(c))
    w_hi, w_lo = _split_bf16(w_ref[...])
    o_ref[...] = _dot(s_hi, w_hi) + _dot(s_lo, w_hi) + _dot(s_hi, w_lo) + b_ref[...]


def _ada(cond, ada_w, ada_b):
    depth, d, _ = ada_w.shape
    r = cond.shape[0]
    return pl.pallas_call(
        _ada_kernel,
        grid=(depth, 3),
        in_specs=[
            pl.BlockSpec((r, d), lambda l, j: (0, 0)),
            pl.BlockSpec((None, d, d), lambda l, j: (l, 0, j)),
            pl.BlockSpec((None, 1, d), lambda l, j: (l, 0, j)),
        ],
        out_specs=pl.BlockSpec((None, r, d), lambda l, j: (l, 0, j)),
        out_shape=jax.ShapeDtypeStruct((depth, r, 3 * d), F32),
        compiler_params=_params("parallel", "parallel"),
        name="ada_modulation",
    )(cond, ada_w, ada_b.reshape(depth, 1, 3 * d))


def _rms(x, g):
    return x * lax.rsqrt(jnp.mean(x * x, axis=-1, keepdims=True) + EPS) * g


def _modnorm_mm_kernel(x_ref, mod_ref, g_ref, w_ref, o_ref, z_ref):
    @pl.when(pl.program_id(2) == 0)
    def _():
        y = _rms(x_ref[...], g_ref[...])
        z_ref[...] = (y * (1.0 + mod_ref[1:2, :]) + mod_ref[0:1, :]).astype(BF16)

    o_ref[...] = _dot(z_ref[...], w_ref[...])


def _modnorm_mm(x, mod, g, w, tm, tn, name):
    bx, r, d = x.shape
    n = w.shape[1]
    tm = min(tm, r)
    return pl.pallas_call(
        _modnorm_mm_kernel,
        grid=(bx, r // tm, n // tn),
        in_specs=[
            pl.BlockSpec((None, tm, d), lambda b, i, j: (b, i, 0)),
            pl.BlockSpec((None, 3, d), lambda b, i, j: (b, 0, 0)),
            pl.BlockSpec((1, d), lambda b, i, j: (0, 0)),
            pl.BlockSpec((d, tn), lambda b, i, j: (0, j)),
        ],
        out_specs=pl.BlockSpec((None, tm, tn), lambda b, i, j: (b, i, j)),
        out_shape=jax.ShapeDtypeStruct((bx, r, n), F32),
        scratch_shapes=[pltpu.VMEM((tm, d), BF16)],
        compiler_params=_params("parallel", "parallel", "arbitrary"),
        name=name,
    )(x, mod, g, w)


def _decay_bounds_kernel(a_ref, o_ref, *, layer):
    slots = a_ref[...]
    e = jnp.exp(slots - jnp.max(slots, axis=1, keepdims=True))
    o_ref[...] = jnp.sum(e[:, :layer + 1], axis=1) / jnp.sum(e, axis=1)


def _decay_bounds(hg_lb, layer):
    ndir, nslot, w = hg_lb.shape
    return pl.pallas_call(
        functools.partial(_decay_bounds_kernel, layer=layer),
        out_shape=jax.ShapeDtypeStruct((ndir, 1, w), F32),
        name="decay_bounds",
    )(hg_lb.reshape(ndir, nslot, 1, w))


def _hgrn_levels(c):
    w = c // 2
    out = []
    while w >= 1:
        out.append(w)
        w //= 2
    return tuple(out)


def _hgrn_constants(c):
    t = np.arange(c)
    tri = np.tril(np.ones((c, c), np.float32))
    masks = []
    for w in _hgrn_levels(c):
        blk = t // (2 * w)
        first = (t % (2 * w)) < w
        masks.append(((blk[:, None] == blk[None, :]) & (~first[:, None]) & first[None, :]).astype(np.float32))
    masks = np.stack(masks)
    tri2 = np.stack([tri, tri[::-1, ::-1]])
    m2 = np.stack([masks, masks[:, ::-1, ::-1]])
    return jnp.asarray(tri2, BF16), jnp.asarray(m2, F32)


def _hgrn_level_operand(b, g, k, q, w, d):
    c, width = b.shape
    row = lax.broadcasted_iota(jnp.int32, (c, 1), 0)
    keys_first = d == 0
    if w >= SUBLANES:
        ref_off = w - 1 if d == 0 else w
        pieces = []
        for r0 in range(0, c, 2 * w):
            bref = jnp.broadcast_to(b[r0 + ref_off:r0 + ref_off + 1, :], (w, width))
            for half in range(2):
                sl = slice(r0 + half * w, r0 + (half + 1) * w)
                if (half == 0) == keys_first:
                    pieces.append(k[sl] * jnp.exp2(bref - b[sl]))
                else:
                    pieces.append(q[sl] * jnp.exp2(b[sl] - bref))
        return jnp.concatenate(pieces, axis=0)
    before = (row % (2 * w)) < w
    kq = jnp.where(before == keys_first, k, q)
    if w == 1:
        moving = (row % 2 == 1) if d == 0 else (row % 2 == 0)
        return kq * jnp.exp2(jnp.where(moving, g, 0.0))
    ref_off = w - 1 if d == 0 else w
    sub = lax.broadcasted_iota(jnp.int32, (SUBLANES, 1), 0)
    pieces = []
    for r0 in range(0, c, SUBLANES):
        lo = jnp.broadcast_to(b[r0 + ref_off:r0 + ref_off + 1, :], (SUBLANES, width))
        if 2 * w == SUBLANES:
            pieces.append(lo)
        else:
            hi = jnp.broadcast_to(b[r0 + 2 * w + ref_off:r0 + 2 * w + ref_off + 1, :], (SUBLANES, width))
            pieces.append(jnp.where(sub < 2 * w, lo, hi))
    bref = jnp.concatenate(pieces, axis=0)
    return kq * jnp.exp2(-jnp.abs(b - bref))


def _hgrn_wide(q_ref, f_ref, v_ref, lb, tri, d, r0, *, c):
    last = c - 1 if d == 0 else 0
    rows = pl.ds(r0, c)
    half = 0.5 * (1.0 - lb)
    f = (lb + half) + half * jnp.tanh(0.5 * f_ref[rows, :])
    g = jnp.log2(f)
    g_hi, g_lo = _split_bf16(g)
    b = _dot(tri, g_hi) + _dot(tri, g_lo)
    q = _silu(q_ref[rows, :])
    k = 1.0 - f
    v = v_ref[rows, :]
    bl = b[last:last + 1, :]
    return dict(d=d, rows=rows, g=g, b=b, q=q, k=k, v=v, vb=v.astype(BF16), qk=q * k,
                qe=(q * jnp.exp2(b)).astype(BF16), kend=(k * jnp.exp2(bl - b)).astype(BF16), ebl=jnp.exp2(bl))


def _head(x, h):
    return x[:, h * HG_DK:(h + 1) * HG_DK]


def _hgrn_pairs(s, h, mk_ref, *, c):
    d = s["d"]
    att = None
    for l, w in enumerate(_hgrn_levels(c)):
        x = _hgrn_level_operand(_head(s["b"], h), _head(s["g"], h), _head(s["k"], h), _head(s["q"], h), w, d)
        x = x.astype(BF16)
        t = mk_ref[d, l] * _dot_nt(x, x)
        att = t if att is None else att + t
    return att


def _hgrn_finish(s, h, att, o_ref, st_ref):
    d = s["d"]
    vb = _head(s["vb"], h)
    inter = _dot_nt(_head(s["qe"], h), st_ref[d, h].astype(BF16))
    diag = jnp.sum(_head(s["qk"], h), axis=-1, keepdims=True)
    o_ref[s["rows"], h * HG_DK:(h + 1) * HG_DK] = inter + _dot(att.astype(BF16), vb) + diag * _head(s["v"], h)
    st_ref[d, h] = _head(s["ebl"], h) * st_ref[d, h] + _dot_tn(vb, _head(s["kend"], h))


def _hgrn_kernel(qf_ref, ff_ref, vf_ref, qb_ref, fb_ref, vb_ref, lb_ref, tri_ref, mk_ref, s0_ref,
                 of_ref, ob_ref, sout_ref, st_ref, *, tb, c, nh):
    nchunk = tb // c

    @pl.when(pl.program_id(1) == 0)
    def _():
        st_ref[...] = s0_ref[...]

    def body(cc, carry):
        rf = pl.multiple_of(cc * c, c)
        rb = pl.multiple_of((nchunk - 1 - cc) * c, c)
        sides = ((_hgrn_wide(qf_ref, ff_ref, vf_ref, lb_ref[0], tri_ref[0], 0, rf, c=c), of_ref),
                 (_hgrn_wide(qb_ref, fb_ref, vb_ref, lb_ref[1], tri_ref[1], 1, rb, c=c), ob_ref))
        pending = [None, None]
        for h in range(nh + 1):
            cur = [_hgrn_pairs(s, h, mk_ref, c=c) if h < nh else None for s, _ in sides]
            for (s, o_ref), p in zip(sides, pending):
                if p is not None:
                    _hgrn_finish(s, h - 1, p, o_ref, st_ref)
            pending = cur
        return carry

    lax.fori_loop(0, nchunk, body, 0, unroll=True)

    @pl.when(pl.program_id(1) == pl.num_programs(1) - 1)
    def _():
        sout_ref[...] = st_ref[...]


def _hgrn(p, lb, s0, consts):
    bsz, r, _ = p.shape
    w = lb.shape[-1]
    nh = w // HG_DK
    tb = min(HG_TILE, r)
    c = HG_CHUNK
    nb = r // tb
    tri2, m2 = consts
    fwd = lambda col: pl.BlockSpec((None, tb, w), lambda b, s: (b, s, col))
    bwd = lambda col: pl.BlockSpec((None, tb, w), lambda b, s: (b, nb - 1 - s, col))
    const = lambda arr: pl.BlockSpec(arr.shape, lambda b, s: (0,) * arr.ndim)
    st_spec = pl.BlockSpec((None, 2, nh, HG_DK, HG_DK), lambda b, s: (b, 0, 0, 0, 0))
    kern = functools.partial(_hgrn_kernel, tb=tb, c=c, nh=nh)
    return pl.pallas_call(
        kern,
        grid=(bsz, nb),
        in_specs=[fwd(0), fwd(1), fwd(3), bwd(0), bwd(2), bwd(3), const(lb), const(tri2), const(m2),
                  st_spec],
        out_specs=[
            pl.BlockSpec((None, tb, w), lambda b, s: (b, s, 0)),
            pl.BlockSpec((None, tb, w), lambda b, s: (b, nb - 1 - s, 0)),
            st_spec,
        ],
        out_shape=[
            jax.ShapeDtypeStruct((bsz, r, w), F32),
            jax.ShapeDtypeStruct((bsz, r, w), F32),
            jax.ShapeDtypeStruct(s0.shape, F32),
        ],
        scratch_shapes=[pltpu.VMEM((2, nh, HG_DK, HG_DK), F32)],
        compiler_params=_params("parallel", "arbitrary"),
        name="hgrn2_scan",
    )(p, p, p, p, p, p, lb, tri2, m2, s0)


def _even_post_kernel(of_ref, ob_ref, ga_ref, u_ref, gb_ref, up_ref, un_ref, h_ref, mod_ref, hgn_ref, pw_ref,
                      ps_ref, ow_ref, modn_ref, gn_ref, wn_ref, o_ref, pn_ref, ext_ref, y_ref, z_ref, zo_ref, *,
                      tb, seq, nh):
    step = pl.program_id(1)
    last = pl.num_programs(1) - 2
    j = jnp.minimum(step, last)
    w = nh * HG_DK

    @pl.when(step == 0)
    def _():
        z_ref[...] = jnp.zeros(z_ref.shape, BF16)

    zo_ref[...] = z_ref[...]
    n_parts = nh + len(POOL_WINDOWS)
    n_blk = pn_ref.shape[1] // LANES
    per = -(-n_blk // n_parts)

    def in_proj_part(i):
        c0 = min(i * per, n_blk) * LANES
        c1 = min((i + 1) * per, n_blk) * LANES
        if 0 <= i and c0 < c1:
            pn_ref[:, c0:c1] = _dot(zo_ref[...], wn_ref[:, c0:c1]).astype(pn_ref.dtype)

    o = of_ref[...] + ob_ref[...]
    for h in range(nh):
        sl = slice(h * HG_DK, (h + 1) * HG_DK)
        y_ref[:, sl] = (_rms(o[:, sl], hgn_ref[:, sl]) * _silu(ga_ref[:, sl])).astype(BF16)
        in_proj_part(h - IN_PROJ_SHIFT)
    u = u_ref[...]
    ext_ref[0:POOL_HALO, :] = jnp.where(j > 0, up_ref[...], 0.0)
    ext_ref[POOL_HALO:POOL_HALO + tb, :] = u
    ext_ref[POOL_HALO + tb:, :] = jnp.where(j < last, un_ref[...], 0.0)
    t = j * tb + lax.broadcasted_iota(jnp.int32, (tb, 1), 0)
    grp = w // len(POOL_WINDOWS)
    for gi, win in enumerate(POOL_WINDOWS):
        sl = slice(gi * grp, (gi + 1) * grp)
        acc = ext_ref[POOL_HALO - win // 2:POOL_HALO - win // 2 + tb, sl]
        for off in range(-win // 2 + 1, win // 2):
            acc = acc + ext_ref[POOL_HALO + off:POOL_HALO + off + tb, sl]
        cnt = (jnp.minimum(t + win // 2, seq) - jnp.maximum(t - win // 2, 0)).astype(F32)
        yp = acc * (1.0 / cnt) - u[:, sl]
        yb = _dot(yp.astype(BF16), pw_ref[gi]) * ps_ref[:, sl]
        y_ref[:, w + gi * grp:w + (gi + 1) * grp] = (yb * _silu(gb_ref[:, sl])).astype(BF16)
        in_proj_part(nh + gi - IN_PROJ_SHIFT)
    hn = h_ref[...] + mod_ref[2:3, :] * _dot(y_ref[...], ow_ref[...])
    o_ref[...] = hn
    for i in range(n_parts - IN_PROJ_SHIFT, n_parts):
        in_proj_part(i)
    z_ref[...] = (_rms(hn, gn_ref[...]) * (1.0 + modn_ref[1:2, :]) + modn_ref[0:1, :]).astype(BF16)


def _even_post(o_f, o_b, p, h, mod, hgn, pool_w, pool_scale, out_w, mod_n, g_n, w_n):
    bsz, r, w = o_f.shape
    d = h.shape[-1]
    tb = min(TOK_TILE, r)
    nb = r // tb
    hb = tb // POOL_HALO
    nh = w // HG_DK
    n_next = w_n.shape[1]
    cur = lambda s: jnp.minimum(s, nb - 1)
    tok = lambda col: pl.BlockSpec((None, tb, w), lambda b, s: (b, cur(s), col))
    const = lambda arr: pl.BlockSpec(arr.shape, lambda b, s: (0,) * arr.ndim)
    kern = functools.partial(_even_post_kernel, tb=tb, seq=r, nh=nh)
    return pl.pallas_call(
        kern,
        grid=(bsz, nb + 1),
        in_specs=[
            tok(0), tok(0), tok(4), tok(5), tok(6),
            pl.BlockSpec((None, POOL_HALO, w), lambda b, s: (b, jnp.maximum(cur(s) * hb - 1, 0), 5)),
            pl.BlockSpec((None, POOL_HALO, w), lambda b, s: (b, jnp.minimum((cur(s) + 1) * hb, nb * hb - 1), 5)),
            pl.BlockSpec((None, tb, d), lambda b, s: (b, cur(s), 0)),
            pl.BlockSpec((None, 3, d), lambda b, s: (b, 0, 0)),
            const(hgn), const(pool_w), const(pool_scale), const(out_w),
            pl.BlockSpec((None, 3, d), lambda b, s: (b, 0, 0)), const(g_n), const(w_n),
        ],
        out_specs=[pl.BlockSpec((None, tb, d), lambda b, s: (b, cur(s), 0)),
                   pl.BlockSpec((None, tb, n_next), lambda b, s: (b, jnp.maximum(s - 1, 0), 0))],
        out_shape=[jax.ShapeDtypeStruct((bsz, r, d), F32), jax.ShapeDtypeStruct((bsz, r, n_next), BF16)],
        scratch_shapes=[pltpu.VMEM((tb + 2 * POOL_HALO, w), F32), pltpu.VMEM((tb, 2 * w), BF16),
                        pltpu.VMEM((tb, d), BF16), pltpu.VMEM((tb, d), BF16)],
        compiler_params=_params("parallel", "arbitrary"),
        name="even_post",
    )(o_f, o_b, p, p, p, p, p, h, mod, hgn, pool_w, pool_scale, out_w, mod_n, g_n, w_n)


def _mla_kv_kernel(*refs, rope):
    if rope:
        ckv_ref, kr_ref, g_ref, wuk_ref, wuvt_ref, cos_ref, sin_ref, kcat_ref, vt_ref = refs
    else:
        ckv_ref, kr_ref, g_ref, wuk_ref, wuvt_ref, kcat_ref, vt_ref = refs
    cn = _rms(ckv_ref[...].astype(F32), g_ref[...]).astype(BF16)
    kn = _dot(cn, wuk_ref[...])
    kr = kr_ref[...].astype(F32)
    if rope:
        lane = lax.broadcasted_iota(jnp.int32, kr.shape, 1)
        swapped = jnp.where((lane % (2 * ROPE_FREQ)) < ROPE_FREQ,
                            pltpu.roll(kr, LANES - ROPE_FREQ, 1), pltpu.roll(kr, ROPE_FREQ, 1))
        kr = kr * cos_ref[...] + swapped * sin_ref[...]
    kr = kr.astype(BF16)
    ones_rows = (lax.broadcasted_iota(jnp.int32, (VT_ROWS - MLA_V, kr.shape[0]), 0) == 0).astype(BF16)
    vt = _dot_nt(wuvt_ref[...], cn)
    for h in range(MLA_HEADS):
        kcat_ref[h, :, 0:MLA_NOPE] = kn[:, h * MLA_NOPE:(h + 1) * MLA_NOPE].astype(BF16)
        kcat_ref[h, :, MLA_NOPE:] = kr
        vt_ref[h, 0:MLA_V, :] = vt[h * MLA_V:(h + 1) * MLA_V].astype(BF16)
        vt_ref[h, MLA_V:, :] = ones_rows


def _mla_kv(p, ckv_blk, kr_blk, g, wuk, wuvt, tables, tb):
    bsz, r, _ = p.shape
    rank = g.shape[-1]
    nb = r // tb
    const = lambda arr: pl.BlockSpec(arr.shape, lambda b, j: (0,) * arr.ndim)
    in_specs = [
        pl.BlockSpec((None, tb, rank), lambda b, j: (b, j, ckv_blk)),
        pl.BlockSpec((None, tb, LANES), lambda b, j: (b, j, kr_blk)),
        const(g), const(wuk), const(wuvt),
    ]
    args = [p, p, g, wuk, wuvt]
    if tables is not None:
        in_specs += [pl.BlockSpec((tb, LANES), lambda b, j: (j, 0))] * 2
        args += list(tables)
    return pl.pallas_call(
        functools.partial(_mla_kv_kernel, rope=tables is not None),
        grid=(bsz, nb),
        in_specs=in_specs,
        out_specs=[
            pl.BlockSpec((None, MLA_HEADS, None, tb, QK_PAD), lambda b, j: (b, 0, j, 0, 0)),
            pl.BlockSpec((None, MLA_HEADS, None, VT_ROWS, tb), lambda b, j: (b, 0, j, 0, 0)),
        ],
        out_shape=[
            jax.ShapeDtypeStruct((bsz, MLA_HEADS, nb, tb, QK_PAD), BF16),
            jax.ShapeDtypeStruct((bsz, MLA_HEADS, nb, VT_ROWS, tb), BF16),
        ],
        compiler_params=_params("parallel", "parallel"),
        name="mla_kv_rope" if tables is not None else "mla_kv",
    )(*args)


def _mla_q_kernel(cq_ref, g_ref, wqt_ref, cos_ref, sin_ref, qt_ref):
    cn = _rms(cq_ref[...].astype(F32), g_ref[...]).astype(BF16)
    f = ROPE_FREQ
    qt_all = _dot_nt(wqt_ref[...], cn) * (MLA_SCALE * LOG2_E)
    for h in range(MLA_HEADS):
        qt = qt_all[h * MLA_QK:(h + 1) * MLA_QK]
        qt_ref[h, 0:MLA_NOPE, :] = qt[0:MLA_NOPE].astype(BF16)
        for ax in range(2):
            r0 = MLA_NOPE + ax * 2 * f
            x1 = qt[r0:r0 + f]
            x2 = qt[r0 + f:r0 + 2 * f]
            co = cos_ref[ax]
            si = sin_ref[ax]
            qt_ref[h, r0:r0 + f, :] = (x1 * co - x2 * si).astype(BF16)
            qt_ref[h, r0 + f:r0 + 2 * f, :] = (x2 * co + x1 * si).astype(BF16)
        qt_ref[h, MLA_QK:, :] = jnp.zeros((QK_PAD - MLA_QK, cn.shape[0]), BF16)


def _mla_q(p, cq_blk, g, wqt, cos_t, sin_t):
    bsz, t, _ = p.shape
    rank = g.shape[-1]
    tm = min(2 * TOK_TILE, t)
    const = lambda arr: pl.BlockSpec(arr.shape, lambda b, j: (0,) * arr.ndim)
    tab = pl.BlockSpec((2, ROPE_FREQ, tm), lambda b, j: (0, 0, j))
    return pl.pallas_call(
        _mla_q_kernel,
        grid=(bsz, t // tm),
        in_specs=[pl.BlockSpec((None, tm, rank), lambda b, j: (b, j, cq_blk)), const(g), const(wqt), tab, tab],
        out_specs=pl.BlockSpec((None, MLA_HEADS, QK_PAD, tm), lambda b, j: (b, 0, 0, j)),
        out_shape=jax.ShapeDtypeStruct((bsz, MLA_HEADS, QK_PAD, t), BF16),
        compiler_params=_params("parallel", "parallel"),
        name="mla_q",
    )(p, g, wqt, cos_t, sin_t)


def _attn_kernel(qt_ref, kc_ref, vtc_ref, kl_ref, vtl_ref, g_ref, o_ref, m_ref, acc_ref, s_ref, mx_ref, *, tq, n_lat):
    nsub = tq // Q_SUB
    m_ref[...] = jnp.full(m_ref.shape, -jnp.inf, F32)
    acc_ref[...] = jnp.zeros(acc_ref.shape, F32)

    def scores(k, nxt, g):
        s = _dot(k, qt_ref[:, g * Q_SUB:(g + 1) * Q_SUB])
        s_ref[nxt, g, 0:k.shape[0], :] = s
        mx_ref[nxt, g] = jnp.max(s, axis=0, keepdims=True)

    def substep(k_next, vt_cur, cur, nxt):
        rows = vt_cur.shape[1]
        for g in range(nsub):
            sl = slice(g * Q_SUB, (g + 1) * Q_SUB)
            scores(k_next, nxt, g)
            m_old = m_ref[:, sl]
            m_new = jnp.maximum(m_old, mx_ref[cur, g])
            alpha = jnp.exp2(m_old - m_new)
            p = jnp.exp2(s_ref[cur, g, 0:rows, :] - m_new)
            acc_ref[:, sl] = alpha * acc_ref[:, sl] + _dot(vt_cur, p.astype(BF16))
            m_ref[:, sl] = m_new

    kc = kc_ref[...]
    for g in range(nsub):
        scores(kc, 0, g)
    substep(kl_ref[0], vtc_ref[...], 0, 1)

    per_trip = max(u for u in (2, 4, 8) if n_lat % u == 0)

    def body(j, carry):
        for u in range(per_trip):
            a = per_trip * j + u
            substep(kl_ref[jnp.minimum(a + 1, n_lat - 1)], vtl_ref[a], (1 + u) % 2, u % 2)
        return carry

    lax.fori_loop(0, n_lat // per_trip, body, 0)
    o = (acc_ref[0:MLA_V, :] * (1.0 / acc_ref[MLA_V:MLA_V + 1, :])).T
    o_ref[...] = (o * _silu(g_ref[...].astype(F32))).astype(o_ref.dtype)


def _attn(qt, kc, vtc, kl, vtl, p):
    bsz, nh, _, t = qt.shape
    lc = kc.shape[3]
    tq = min(Q_TILE, t)
    n_lat = kl.shape[2]
    kv = kl.shape[3]
    assert kc.shape[2] == 1 and lc <= kv and n_lat % 2 == 0
    kern = functools.partial(_attn_kernel, tq=tq, n_lat=n_lat)
    ctx5 = lambda arr: pl.BlockSpec((None, None, None) + arr.shape[3:], lambda b, h, i: (b, h, 0, 0, 0))
    full5 = lambda arr: pl.BlockSpec((None, None) + arr.shape[2:], lambda b, h, i: (b, h, 0, 0, 0))
    return pl.pallas_call(
        kern,
        grid=(bsz, nh, t // tq),
        in_specs=[
            pl.BlockSpec((None, None, QK_PAD, tq), lambda b, h, i: (b, h, 0, i)),
            ctx5(kc), ctx5(vtc), full5(kl), full5(vtl),
            pl.BlockSpec((None, tq, MLA_V), lambda b, h, i: (b, i, h)),
        ],
        out_specs=pl.BlockSpec((None, tq, MLA_V), lambda b, h, i: (b, i, h)),
        out_shape=jax.ShapeDtypeStruct((bsz, t, nh * MLA_V), BF16),
        scratch_shapes=[pltpu.VMEM((1, tq), F32), pltpu.VMEM((VT_ROWS, tq), F32),
                        pltpu.VMEM((2, tq // Q_SUB, kv, Q_SUB), F32), pltpu.VMEM((2, tq // Q_SUB, 1, Q_SUB), F32)],
        compiler_params=_params("parallel", "parallel", "arbitrary"),
        name="mla_attention",
    )(qt, kc, vtc, kl, vtl, p)


def _out_final_kernel(y_ref, h_ref, mod_ref, ow_ref, g_ref, o_ref):
    hn = h_ref[...] + mod_ref[2:3, :] * _dot(y_ref[...], ow_ref[...])
    o_ref[...] = _rms(hn, g_ref[...])


def _out_final(y, h, mod, out_w, g):
    bsz, t, d = h.shape
    wi = y.shape[-1]
    tm = min(2 * TOK_TILE, t)
    return pl.pallas_call(
        _out_final_kernel,
        grid=(bsz, t // tm),
        in_specs=[
            pl.BlockSpec((None, tm, wi), lambda b, j: (b, j, 0)),
            pl.BlockSpec((None, tm, d), lambda b, j: (b, j, 0)),
            pl.BlockSpec((None, 3, d), lambda b, j: (b, 0, 0)),
            pl.BlockSpec((wi, d), lambda b, j: (0, 0)),
            pl.BlockSpec((1, d), lambda b, j: (0, 0)),
        ],
        out_specs=pl.BlockSpec((None, tm, d), lambda b, j: (b, j, 0)),
        out_shape=jax.ShapeDtypeStruct((bsz, t, d), F32),
        compiler_params=_params("parallel", "parallel"),
        name="out_final",
    )(y, h, mod, out_w, g)


def _rope_tables(n_tokens):
    rows = n_tokens // GRID_W
    pos_r = jnp.repeat(jnp.arange(rows), GRID_W).astype(F32)
    pos_c = jnp.tile(jnp.arange(GRID_W), rows).astype(F32)
    inv = ROPE_BASE ** (-2.0 * jnp.arange(ROPE_FREQ, dtype=F32) / (MLA_ROPE // 2))
    ang = jnp.stack([pos_r[:, None] * inv, pos_c[:, None] * inv], axis=1)
    cos, sin = jnp.cos(ang), jnp.sin(ang)
    pad = LANES - MLA_ROPE
    cos_k = jnp.pad(jnp.stack([cos, cos], axis=2).reshape(n_tokens, MLA_ROPE), ((0, 0), (0, pad)))
    sin_k = jnp.pad(jnp.stack([-sin, sin], axis=2).reshape(n_tokens, MLA_ROPE), ((0, 0), (0, pad)))
    cos_q = jnp.transpose(cos, (1, 2, 0))
    sin_q = jnp.transpose(sin, (1, 2, 0))
    return (cos_k, sin_k), (cos_q, sin_q)


def kernel(x, c, ctx, c_ctx, ada_w, ada_b, norm_g, out_w, ev_in_w, hg_lb, hg_norm_g, pool_w, pool_scale,
           od_in_w, qa_norm_g, qb_w, kva_norm_g, kvb_w, final_norm_g):
    bsz, t, d = x.shape
    lc = ctx.shape[1]
    depth = ada_w.shape[0]
    assert depth == 2 and t % (2 * TOK_TILE) == 0 and lc % TOK_TILE == 0 and t % GRID_W == 0
    w = hg_norm_g.shape[-1]
    nh = w // HG_DK
    q_rank = qa_norm_g.shape[-1]
    kv_rank = kva_norm_g.shape[-1]
    d_inner = out_w.shape[1]

    n_cond = -(-(bsz + 1) // SUBLANES) * SUBLANES
    cond = jnp.zeros((n_cond, d), F32).at[:bsz].set(c).at[bsz].set(c_ctx)
    mods = _ada(cond, ada_w, ada_b).reshape(depth, n_cond, 3, d)
    mod_l = [mods[l, :bsz] for l in range(depth)]
    mod_c = [mods[l, bsz:bsz + 1] for l in range(depth)]

    lb = _decay_bounds(hg_lb, 0)
    w_in0 = ev_in_w[0].astype(BF16)
    g0 = norm_g[0].reshape(1, d)
    ctx_flat = ctx.reshape(1, bsz * lc, d)
    n_in0 = w_in0.shape[1]
    tn0 = n_in0 // IN_PROJ_COL_PARTS
    p_c = _modnorm_mm(ctx_flat, mod_c[0], g0, w_in0, IN_PROJ_ROWS, tn0, "in_proj0_ctx").reshape(bsz, lc, n_in0)
    p_l = _modnorm_mm(x, mod_l[0], g0, w_in0, IN_PROJ_ROWS, tn0, "in_proj0")
    consts = _hgrn_constants(HG_CHUNK)
    s0 = jnp.zeros((bsz, 2, nh, HG_DK, HG_DK), F32)
    of_c, ob_c, s_c = _hgrn(p_c, lb, s0, consts)
    of_l, ob_l, _ = _hgrn(p_l, lb, s_c, consts)
    hgn = hg_norm_g[0].reshape(1, w)
    pw = pool_w[0].astype(BF16)
    ps = pool_scale[0].reshape(1, w)
    ow0 = out_w[0].astype(BF16)
    o1 = q_rank
    o2 = o1 + kv_rank
    o3 = o2 + MLA_ROPE
    w1 = od_in_w[0]
    kr_pad = jnp.zeros((d, LANES - MLA_ROPE), F32)
    w_in1 = jnp.concatenate([w1[:, o3:], w1[:, :o1], w1[:, o1:o2], w1[:, o2:o3], kr_pad], axis=1).astype(BF16)
    w_in1c = w_in1[:, d_inner + q_rank:]
    g1 = norm_g[1].reshape(1, d)
    bcast = lambda m: jnp.broadcast_to(m, (bsz, 3, d))
    _, p1_c = _even_post(of_c, ob_c, p_c, ctx, bcast(mod_c[0]), hgn, pw, ps, ow0, bcast(mod_c[1]), g1, w_in1c)
    hl1, p1_l = _even_post(of_l, ob_l, p_l, x, mod_l[0], hgn, pw, ps, ow0, mod_l[1], g1, w_in1)

    kvw = kvb_w[0].reshape(kv_rank, MLA_HEADS, MLA_NOPE + MLA_V)
    wuk = kvw[..., :MLA_NOPE].reshape(kv_rank, MLA_HEADS * MLA_NOPE).astype(BF16)
    wuvt = jnp.transpose(kvw[..., MLA_NOPE:], (1, 2, 0)).reshape(MLA_HEADS * MLA_V, kv_rank).astype(BF16)
    wqt = jnp.transpose(qb_w[0]).astype(BF16)
    kvg = kva_norm_g[0].reshape(1, kv_rank)
    qag = qa_norm_g[0].reshape(1, q_rank)
    tab_k, tab_q = _rope_tables(t)
    kc, vtc = _mla_kv(p1_c, 0, kv_rank // LANES, kvg, wuk, wuvt, None, lc)
    kl, vtl = _mla_kv(p1_l, (d_inner + q_rank) // kv_rank, (d_inner + q_rank + kv_rank) // LANES, kvg, wuk, wuvt,
                      tab_k, min(KV_CHUNK, t))
    qt = _mla_q(p1_l, d_inner // q_rank, qag, wqt, *tab_q)
    y = _attn(qt, kc, vtc, kl, vtl, p1_l)
    return _out_final(y, hl1, mod_l[1], out_w[1].astype(BF16), final_norm_g.reshape(1, d))
```

```python
import functools

import numpy as np
import jax
import jax.numpy as jnp
from jax import lax
from jax.experimental import pallas as pl
from jax.experimental.pallas import tpu as pltpu

F32 = jnp.float32
BF16 = jnp.bfloat16

EPS = 1e-6
GRID_W = 64
HG_DK = 128
POOL_WINDOWS = (2, 4, 8, 16)
MLA_HEADS = 16
MLA_NOPE = 128
MLA_ROPE = 64
MLA_V = 128
MLA_QK = MLA_NOPE + MLA_ROPE
QK_PAD = 256
BF16_ROWS = 16
VT_ROWS = MLA_V + BF16_ROWS
MLA_SCALE = MLA_QK ** -0.5
LOG2_E = 1.4426950408889634
ROPE_FREQ = MLA_ROPE // 4
ROPE_BASE = 10000.0

LANES = 128
SUBLANES = 8
VMEM_LIMIT = 48 * 1024 * 1024

HG_CHUNK = 64
HG_TILE = 512
TOK_TILE = 256
IN_PROJ_ROWS = 1024
IN_PROJ_COL_PARTS = 4
KV_CHUNK = 512
Q_TILE = 4096
Q_SUB = 256
POOL_HALO = 8
IN_PROJ_SHIFT = 5


def _dot(a, b):
    return jnp.dot(a, b, preferred_element_type=F32)


def _dot_nt(a, b):
    return lax.dot_general(a, b, (((1,), (1,)), ((), ())), preferred_element_type=F32)


def _dot_tn(a, b):
    return lax.dot_general(a, b, (((0,), (0,)), ((), ())), preferred_element_type=F32)


def _silu(x):
    h = 0.5 * x
    return h + h * jnp.tanh(h)


def _split_bf16(x):
    hi = x.astype(BF16)
    lo = (x - hi.astype(F32)).astype(BF16)
    return hi, lo


def _params(*sem):
    return pltpu.CompilerParams(dimension_semantics=sem, vmem_limit_bytes=VMEM_LIMIT)


def _ada_kernel(c_ref, w_ref, b_ref, o_ref):
    c = c_ref[...]
    s_hi, s_lo = _split_bf16(_silu(c))
    w_hi, w_lo = _split_bf16(w_ref[...])
    o_ref[...] = _dot(s_hi, w_hi) + _dot(s_lo, w_hi) + _dot(s_hi, w_lo) + b_ref[...]


def _ada(cond, ada_w, ada_b):
    depth, d, _ = ada_w.shape
    r = cond.shape[0]
    return pl.pallas_call(
        _ada_kernel,
        grid=(depth, 3),
        in_specs=[
            pl.BlockSpec((r, d), lambda l, j: (0, 0)),
            pl.BlockSpec((None, d, d), lambda l, j: (l, 0, j)),
            pl.BlockSpec((None, 1, d), lambda l, j: (l, 0, j)),
        ],
        out_specs=pl.BlockSpec((None, r, d), lambda l, j: (l, 0, j)),
        out_shape=jax.ShapeDtypeStruct((depth, r, 3 * d), F32),
        compiler_params=_params("parallel", "parallel"),
        name="ada_modulation",
    )(cond, ada_w, ada_b.reshape(depth, 1, 3 * d))


def _rms(x, g):
    return x * lax.rsqrt(jnp.mean(x * x, axis=-1, keepdims=True) + EPS) * g


W_SLOTS = 3


def _modnorm_mm_kernel(x_ref, mod_ref, g_ref, w_hbm, o_ref, z_ref, w_ref, sem, *, tn):
    n_row, n_col = pl.num_programs(1), pl.num_programs(2)
    lin = (pl.program_id(0) * n_row + pl.program_id(1)) * n_col + pl.program_id(2)
    total = pl.num_programs(0) * n_row * n_col

    def w_copy(k):
        col = pl.multiple_of((k % n_col) * tn, tn)
        return pltpu.make_async_copy(w_hbm.at[:, pl.ds(col, tn)], w_ref.at[k % W_SLOTS], sem.at[k % W_SLOTS])

    @pl.when(lin == 0)
    def _():
        for k in range(W_SLOTS - 1):
            w_copy(k).start()

    @pl.when(pl.program_id(2) == 0)
    def _():
        y = _rms(x_ref[...], g_ref[...])
        z_ref[...] = (y * (1.0 + mod_ref[1:2, :]) + mod_ref[0:1, :]).astype(BF16)

    w_copy(lin).wait()

    @pl.when(lin + W_SLOTS - 1 < total)
    def _():
        w_copy(lin + W_SLOTS - 1).start()

    o_ref[...] = _dot(z_ref[...], w_ref[lin % W_SLOTS])


def _modnorm_mm(x, mod, g, w, tm, tn, name):
    bx, r, d = x.shape
    n = w.shape[1]
    tm = min(tm, r)
    grid = (bx, r // tm, n // tn)
    assert grid[0] * grid[1] * grid[2] >= W_SLOTS - 1
    return pl.pallas_call(
        functools.partial(_modnorm_mm_kernel, tn=tn),
        grid=grid,
        in_specs=[
            pl.BlockSpec((None, tm, d), lambda b, i, j: (b, i, 0)),
            pl.BlockSpec((None, 3, d), lambda b, i, j: (b, 0, 0)),
            pl.BlockSpec((1, d), lambda b, i, j: (0, 0)),
            pl.BlockSpec(memory_space=pl.ANY),
        ],
        out_specs=pl.BlockSpec((None, tm, tn), lambda b, i, j: (b, i, j)),
        out_shape=jax.ShapeDtypeStruct((bx, r, n), F32),
        scratch_shapes=[pltpu.VMEM((tm, d), BF16), pltpu.VMEM((W_SLOTS, d, tn), BF16),
                        pltpu.SemaphoreType.DMA((W_SLOTS,))],
        compiler_params=_params("arbitrary", "arbitrary", "arbitrary"),
        name=name,
    )(x, mod, g, w)


def _decay_bounds_kernel(a_ref, o_ref, *, layer):
    slots = a_ref[...]
    e = jnp.exp(slots - jnp.max(slots, axis=1, keepdims=True))
    o_ref[...] = jnp.sum(e[:, :layer + 1], axis=1) / jnp.sum(e, axis=1)


def _decay_bounds(hg_lb, layer):
    ndir, nslot, w = hg_lb.shape
    return pl.pallas_call(
        functools.partial(_decay_bounds_kernel, layer=layer),
        out_shape=jax.ShapeDtypeStruct((ndir, 1, w), F32),
        name="decay_bounds",
    )(hg_lb.reshape(ndir, nslot, 1, w))


def _hgrn_levels(c):
    w = c // 2
    out = []
    while w >= 1:
        out.append(w)
        w //= 2
    return tuple(out)


def _hgrn_constants(c):
    t = np.arange(c)
    tri = np.tril(np.ones((c, c), np.float32))
    masks = []
    for w in _hgrn_levels(c):
        blk = t // (2 * w)
        first = (t % (2 * w)) < w
        masks.append(((blk[:, None] == blk[None, :]) & (~first[:, None]) & first[None, :]).astype(np.float32))
    masks = np.stack(masks)
    tri2 = np.stack([tri, tri[::-1, ::-1]])
    m2 = np.stack([masks, masks[:, ::-1, ::-1]])
    return jnp.asarray(tri2, BF16), jnp.asarray(m2, F32)


def _hgrn_level_operand(b, g, k, q, w, d):
    c, width = b.shape
    row = lax.broadcasted_iota(jnp.int32, (c, 1), 0)
    keys_first = d == 0
    if w >= SUBLANES:
        ref_off = w - 1 if d == 0 else w
        pieces = []
        for r0 in range(0, c, 2 * w):
            bref = jnp.broadcast_to(b[r0 + ref_off:r0 + ref_off + 1, :], (w, width))
            for half in range(2):
                sl = slice(r0 + half * w, r0 + (half + 1) * w)
                if (half == 0) == keys_first:
                    pieces.append(k[sl] * jnp.exp2(bref - b[sl]))
                else:
                    pieces.append(q[sl] * jnp.exp2(b[sl] - bref))
        return jnp.concatenate(pieces, axis=0)
    before = (row % (2 * w)) < w
    kq = jnp.where(before == keys_first, k, q)
    if w == 1:
        moving = (row % 2 == 1) if d == 0 else (row % 2 == 0)
        return kq * jnp.exp2(jnp.where(moving, g, 0.0))
    ref_off = w - 1 if d == 0 else w
    sub = lax.broadcasted_iota(jnp.int32, (SUBLANES, 1), 0)
    pieces = []
    for r0 in range(0, c, SUBLANES):
        lo = jnp.broadcast_to(b[r0 + ref_off:r0 + ref_off + 1, :], (SUBLANES, width))
        if 2 * w == SUBLANES:
            pieces.append(lo)
        else:
            hi = jnp.broadcast_to(b[r0 + 2 * w + ref_off:r0 + 2 * w + ref_off + 1, :], (SUBLANES, width))
            pieces.append(jnp.where(sub < 2 * w, lo, hi))
    bref = jnp.concatenate(pieces, axis=0)
    return kq * jnp.exp2(-jnp.abs(b - bref))


def _hgrn_wide(q_ref, f_ref, v_ref, lb, tri, d, r0, *, c):
    last = c - 1 if d == 0 else 0
    rows = pl.ds(r0, c)
    half = 0.5 * (1.0 - lb)
    f = (lb + half) + half * jnp.tanh(0.5 * f_ref[rows, :])
    g = jnp.log2(f)
    g_hi, g_lo = _split_bf16(g)
    b = _dot(tri, g_hi) + _dot(tri, g_lo)
    q = _silu(q_ref[rows, :])
    k = 1.0 - f
    v = v_ref[rows, :]
    bl = b[last:last + 1, :]
    return dict(d=d, rows=rows, g=g, b=b, q=q, k=k, v=v, vb=v.astype(BF16), qk=q * k,
                qe=(q * jnp.exp2(b)).astype(BF16), kend=(k * jnp.exp2(bl - b)).astype(BF16), ebl=jnp.exp2(bl))


def _head(x, h):
    return x[:, h * HG_DK:(h + 1) * HG_DK]


def _hgrn_pairs(s, h, mk_ref, *, c):
    d = s["d"]
    att = None
    for l, w in enumerate(_hgrn_levels(c)):
        x = _hgrn_level_operand(_head(s["b"], h), _head(s["g"], h), _head(s["k"], h), _head(s["q"], h), w, d)
        x = x.astype(BF16)
        t = mk_ref[d, l] * _dot_nt(x, x)
        att = t if att is None else att + t
    return att


def _hgrn_finish(s, h, att, o_ref, st_ref):
    d = s["d"]
    vb = _head(s["vb"], h)
    inter = _dot_nt(_head(s["qe"], h), st_ref[d, h].astype(BF16))
    diag = jnp.sum(_head(s["qk"], h), axis=-1, keepdims=True)
    o_ref[s["rows"], h * HG_DK:(h + 1) * HG_DK] = inter + _dot(att.astype(BF16), vb) + diag * _head(s["v"], h)
    st_ref[d, h] = _head(s["ebl"], h) * st_ref[d, h] + _dot_tn(vb, _head(s["kend"], h))


def _hgrn_kernel(qf_ref, ff_ref, vf_ref, qb_ref, fb_ref, vb_ref, lb_ref, tri_ref, mk_ref, s0_ref,
                 of_ref, ob_ref, sout_ref, st_ref, *, tb, c, nh):
    nchunk = tb // c

    @pl.when(pl.program_id(1) == 0)
    def _():
        st_ref[...] = s0_ref[...]

    def body(cc, carry):
        rf = pl.multiple_of(cc * c, c)
        rb = pl.multiple_of((nchunk - 1 - cc) * c, c)
        sides = ((_hgrn_wide(qf_ref, ff_ref, vf_ref, lb_ref[0], tri_ref[0], 0, rf, c=c), of_ref),
                 (_hgrn_wide(qb_ref, fb_ref, vb_ref, lb_ref[1], tri_ref[1], 1, rb, c=c), ob_ref))
        pending = [None, None]
        for h in range(nh + 1):
            cur = [_hgrn_pairs(s, h, mk_ref, c=c) if h < nh else None for s, _ in sides]
            for (s, o_ref), p in zip(sides, pending):
                if p is not None:
                    _hgrn_finish(s, h - 1, p, o_ref, st_ref)
            pending = cur
        return carry

    lax.fori_loop(0, nchunk, body, 0, unroll=True)

    @pl.when(pl.program_id(1) == pl.num_programs(1) - 1)
    def _():
        sout_ref[...] = st_ref[...]


def _hgrn(p, lb, s0, consts):
    bsz, r, _ = p.shape
    w = lb.shape[-1]
    nh = w // HG_DK
    tb = min(HG_TILE, r)
    c = HG_CHUNK
    nb = r // tb
    tri2, m2 = consts
    fwd = lambda col: pl.BlockSpec((None, tb, w), lambda b, s: (b, s, col))
    bwd = lambda col: pl.BlockSpec((None, tb, w), lambda b, s: (b, nb - 1 - s, col))
    const = lambda arr: pl.BlockSpec(arr.shape, lambda b, s: (0,) * arr.ndim)
    st_spec = pl.BlockSpec((None, 2, nh, HG_DK, HG_DK), lambda b, s: (b, 0, 0, 0, 0))
    kern = functools.partial(_hgrn_kernel, tb=tb, c=c, nh=nh)
    return pl.pallas_call(
        kern,
        grid=(bsz, nb),
        in_specs=[fwd(0), fwd(1), fwd(3), bwd(0), bwd(2), bwd(3), const(lb), const(tri2), const(m2),
                  st_spec],
        out_specs=[
            pl.BlockSpec((None, tb, w), lambda b, s: (b, s, 0)),
            pl.BlockSpec((None, tb, w), lambda b, s: (b, nb - 1 - s, 0)),
            st_spec,
        ],
        out_shape=[
            jax.ShapeDtypeStruct((bsz, r, w), F32),
            jax.ShapeDtypeStruct((bsz, r, w), F32),
            jax.ShapeDtypeStruct(s0.shape, F32),
        ],
        scratch_shapes=[pltpu.VMEM((2, nh, HG_DK, HG_DK), F32)],
        compiler_params=_params("parallel", "arbitrary"),
        name="hgrn2_scan",
    )(p, p, p, p, p, p, lb, tri2, m2, s0)


def _even_post_kernel(of_ref, ob_ref, ga_ref, u_ref, gb_ref, up_ref, un_ref, h_ref, mod_ref, hgn_ref, pw_ref,
                      ps_ref, ow_ref, modn_ref, gn_ref, wn_ref, o_ref, pn_ref, ext_ref, y_ref, z_ref, zo_ref, *,
                      tb, seq, nh):
    step = pl.program_id(1)
    last = pl.num_programs(1) - 2
    j = jnp.minimum(step, last)
    w = nh * HG_DK

    @pl.when(step == 0)
    def _():
        z_ref[...] = jnp.zeros(z_ref.shape, BF16)

    zo_ref[...] = z_ref[...]
    n_parts = nh + len(POOL_WINDOWS)
    n_blk = pn_ref.shape[1] // LANES
    per = -(-n_blk // n_parts)

    def in_proj_part(i):
        c0 = min(i * per, n_blk) * LANES
        c1 = min((i + 1) * per, n_blk) * LANES
        if 0 <= i and c0 < c1:
            pn_ref[:, c0:c1] = _dot(zo_ref[...], wn_ref[:, c0:c1]).astype(pn_ref.dtype)

    o = of_ref[...] + ob_ref[...]
    for h in range(nh):
        sl = slice(h * HG_DK, (h + 1) * HG_DK)
        y_ref[:, sl] = (_rms(o[:, sl], hgn_ref[:, sl]) * _silu(ga_ref[:, sl])).astype(BF16)
        in_proj_part(h - IN_PROJ_SHIFT)
    u = u_ref[...]
    ext_ref[0:POOL_HALO, :] = jnp.where(j > 0, up_ref[...], 0.0)
    ext_ref[POOL_HALO:POOL_HALO + tb, :] = u
    ext_ref[POOL_HALO + tb:, :] = jnp.where(j < last, un_ref[...], 0.0)
    t = j * tb + lax.broadcasted_iota(jnp.int32, (tb, 1), 0)
    grp = w // len(POOL_WINDOWS)
    for gi, win in enumerate(POOL_WINDOWS):
        sl = slice(gi * grp, (gi + 1) * grp)
        acc = ext_ref[POOL_HALO - win // 2:POOL_HALO - win // 2 + tb, sl]
        for off in range(-win // 2 + 1, win // 2):
            acc = acc + ext_ref[POOL_HALO + off:POOL_HALO + off + tb, sl]
        cnt = (jnp.minimum(t + win // 2, seq) - jnp.maximum(t - win // 2, 0)).astype(F32)
        yp = acc * (1.0 / cnt) - u[:, sl]
        yb = _dot(yp.astype(BF16), pw_ref[gi]) * ps_ref[:, sl]
        y_ref[:, w + gi * grp:w + (gi + 1) * grp] = (yb * _silu(gb_ref[:, sl])).astype(BF16)
        in_proj_part(nh + gi - IN_PROJ_SHIFT)
    hn = h_ref[...] + mod_ref[2:3, :] * _dot(y_ref[...], ow_ref[...])
    o_ref[...] = hn
    for i in range(n_parts - IN_PROJ_SHIFT, n_parts):
        in_proj_part(i)
    z_ref[...] = (_rms(hn, gn_ref[...]) * (1.0 + modn_ref[1:2, :]) + modn_ref[0:1, :]).astype(BF16)


def _even_post(o_f, o_b, p, h, mod, hgn, pool_w, pool_scale, out_w, mod_n, g_n, w_n):
    bsz, r, w = o_f.shape
    d = h.shape[-1]
    tb = min(TOK_TILE, r)
    nb = r // tb
    hb = tb // POOL_HALO
    nh = w // HG_DK
    n_next = w_n.shape[1]
    cur = lambda s: jnp.minimum(s, nb - 1)
    tok = lambda col: pl.BlockSpec((None, tb, w), lambda b, s: (b, cur(s), col))
    const = lambda arr: pl.BlockSpec(arr.shape, lambda b, s: (0,) * arr.ndim)
    kern = functools.partial(_even_post_kernel, tb=tb, seq=r, nh=nh)
    return pl.pallas_call(
        kern,
        grid=(bsz, nb + 1),
        in_specs=[
            tok(0), tok(0), tok(4), tok(5), tok(6),
            pl.BlockSpec((None, POOL_HALO, w), lambda b, s: (b, jnp.maximum(cur(s) * hb - 1, 0), 5)),
            pl.BlockSpec((None, POOL_HALO, w), lambda b, s: (b, jnp.minimum((cur(s) + 1) * hb, nb * hb - 1), 5)),
            pl.BlockSpec((None, tb, d), lambda b, s: (b, cur(s), 0)),
            pl.BlockSpec((None, 3, d), lambda b, s: (b, 0, 0)),
            const(hgn), const(pool_w), const(pool_scale), const(out_w),
            pl.BlockSpec((None, 3, d), lambda b, s: (b, 0, 0)), const(g_n), const(w_n),
        ],
        out_specs=[pl.BlockSpec((None, tb, d), lambda b, s: (b, cur(s), 0)),
                   pl.BlockSpec((None, tb, n_next), lambda b, s: (b, jnp.maximum(s - 1, 0), 0))],
        out_shape=[jax.ShapeDtypeStruct((bsz, r, d), F32), jax.ShapeDtypeStruct((bsz, r, n_next), BF16)],
        scratch_shapes=[pltpu.VMEM((tb + 2 * POOL_HALO, w), F32), pltpu.VMEM((tb, 2 * w), BF16),
                        pltpu.VMEM((tb, d), BF16), pltpu.VMEM((tb, d), BF16)],
        compiler_params=_params("parallel", "arbitrary"),
        name="even_post",
    )(o_f, o_b, p, p, p, p, p, h, mod, hgn, pool_w, pool_scale, out_w, mod_n, g_n, w_n)


def _mla_kv_kernel(*refs, rope):
    if rope:
        ckv_ref, kr_ref, g_ref, wuk_ref, wuvt_ref, cos_ref, sin_ref, kcat_ref, vt_ref = refs
    else:
        ckv_ref, kr_ref, g_ref, wuk_ref, wuvt_ref, kcat_ref, vt_ref = refs
    cn = _rms(ckv_ref[...].astype(F32), g_ref[...]).astype(BF16)
    kn = _dot(cn, wuk_ref[...])
    kr = kr_ref[...].astype(F32)
    if rope:
        lane = lax.broadcasted_iota(jnp.int32, kr.shape, 1)
        swapped = jnp.where((lane % (2 * ROPE_FREQ)) < ROPE_FREQ,
                            pltpu.roll(kr, LANES - ROPE_FREQ, 1), pltpu.roll(kr, ROPE_FREQ, 1))
        kr = kr * cos_ref[...] + swapped * sin_ref[...]
    kr = kr.astype(BF16)
    ones_rows = (lax.broadcasted_iota(jnp.int32, (VT_ROWS - MLA_V, kr.shape[0]), 0) == 0).astype(BF16)
    vt = _dot_nt(wuvt_ref[...], cn)
    for h in range(MLA_HEADS):
        kcat_ref[h, :, 0:MLA_NOPE] = kn[:, h * MLA_NOPE:(h + 1) * MLA_NOPE].astype(BF16)
        kcat_ref[h, :, MLA_NOPE:] = kr
        vt_ref[h, 0:MLA_V, :] = vt[h * MLA_V:(h + 1) * MLA_V].astype(BF16)
        vt_ref[h, MLA_V:, :] = ones_rows


def _mla_kv(p, ckv_blk, kr_blk, g, wuk, wuvt, tables, tb):
    bsz, r, _ = p.shape
    rank = g.shape[-1]
    nb = r // tb
    const = lambda arr: pl.BlockSpec(arr.shape, lambda b, j: (0,) * arr.ndim)
    in_specs = [
        pl.BlockSpec((None, tb, rank), lambda b, j: (b, j, ckv_blk)),
        pl.BlockSpec((None, tb, LANES), lambda b, j: (b, j, kr_blk)),
        const(g), const(wuk), const(wuvt),
    ]
    args = [p, p, g, wuk, wuvt]
    if tables is not None:
        in_specs += [pl.BlockSpec((tb, LANES), lambda b, j: (j, 0))] * 2
        args += list(tables)
    return pl.pallas_call(
        functools.partial(_mla_kv_kernel, rope=tables is not None),
        grid=(bsz, nb),
        in_specs=in_specs,
        out_specs=[
            pl.BlockSpec((None, MLA_HEADS, None, tb, QK_PAD), lambda b, j: (b, 0, j, 0, 0)),
            pl.BlockSpec((None, MLA_HEADS, None, VT_ROWS, tb), lambda b, j: (b, 0, j, 0, 0)),
        ],
        out_shape=[
            jax.ShapeDtypeStruct((bsz, MLA_HEADS, nb, tb, QK_PAD), BF16),
            jax.ShapeDtypeStruct((bsz, MLA_HEADS, nb, VT_ROWS, tb), BF16),
        ],
        compiler_params=_params("parallel", "parallel"),
        name="mla_kv_rope" if tables is not None else "mla_kv",
    )(*args)


def _mla_q_kernel(cq_ref, g_ref, wqt_ref, cos_ref, sin_ref, qt_ref):
    cn = _rms(cq_ref[...].astype(F32), g_ref[...]).astype(BF16)
    f = ROPE_FREQ
    qt_all = _dot_nt(wqt_ref[...], cn) * (MLA_SCALE * LOG2_E)
    for h in range(MLA_HEADS):
        qt = qt_all[h * MLA_QK:(h + 1) * MLA_QK]
        qt_ref[h, 0:MLA_NOPE, :] = qt[0:MLA_NOPE].astype(BF16)
        for ax in range(2):
            r0 = MLA_NOPE + ax * 2 * f
            x1 = qt[r0:r0 + f]
            x2 = qt[r0 + f:r0 + 2 * f]
            co = cos_ref[ax]
            si = sin_ref[ax]
            qt_ref[h, r0:r0 + f, :] = (x1 * co - x2 * si).astype(BF16)
            qt_ref[h, r0 + f:r0 + 2 * f, :] = (x2 * co + x1 * si).astype(BF16)
        qt_ref[h, MLA_QK:, :] = jnp.zeros((QK_PAD - MLA_QK, cn.shape[0]), BF16)


def _mla_q(p, cq_blk, g, wqt, cos_t, sin_t):
    bsz, t, _ = p.shape
    rank = g.shape[-1]
    tm = min(2 * TOK_TILE, t)
    const = lambda arr: pl.BlockSpec(arr.shape, lambda b, j: (0,) * arr.ndim)
    tab = pl.BlockSpec((2, ROPE_FREQ, tm), lambda b, j: (0, 0, j))
    return pl.pallas_call(
        _mla_q_kernel,
        grid=(bsz, t // tm),
        in_specs=[pl.BlockSpec((None, tm, rank), lambda b, j: (b, j, cq_blk)), const(g), const(wqt), tab, tab],
        out_specs=pl.BlockSpec((None, MLA_HEADS, QK_PAD, tm), lambda b, j: (b, 0, 0, j)),
        out_shape=jax.ShapeDtypeStruct((bsz, MLA_HEADS, QK_PAD, t), BF16),
        compiler_params=_params("parallel", "parallel"),
        name="mla_q",
    )(p, g, wqt, cos_t, sin_t)


def _attn_kernel(qt_ref, kc_ref, vtc_ref, kl_ref, vtl_ref, g_ref, o_ref, m_ref, acc_ref, s_ref, mx_ref, *, tq, n_lat):
    nsub = tq // Q_SUB
    m_ref[...] = jnp.full(m_ref.shape, -jnp.inf, F32)
    acc_ref[...] = jnp.zeros(acc_ref.shape, F32)

    def scores(k, nxt, g):
        s = _dot(k, qt_ref[:, g * Q_SUB:(g + 1) * Q_SUB])
        s_ref[nxt, g, 0:k.shape[0], :] = s
        mx_ref[nxt, g] = jnp.max(s, axis=0, keepdims=True)

    def substep(k_next, vt_cur, cur, nxt):
        rows = vt_cur.shape[1]
        for g in range(nsub):
            sl = slice(g * Q_SUB, (g + 1) * Q_SUB)
            scores(k_next, nxt, g)
            m_old = m_ref[:, sl]
            m_new = jnp.maximum(m_old, mx_ref[cur, g])
            alpha = jnp.exp2(m_old - m_new)
            p = jnp.exp2(s_ref[cur, g, 0:rows, :] - m_new)
            acc_ref[:, sl] = alpha * acc_ref[:, sl] + _dot(vt_cur, p.astype(BF16))
            m_ref[:, sl] = m_new

    kc = kc_ref[...]
    for g in range(nsub):
        scores(kc, 0, g)
    substep(kl_ref[0], vtc_ref[...], 0, 1)

    per_trip = max(u for u in (2, 4, 8) if n_lat % u == 0)

    def body(j, carry):
        for u in range(per_trip):
            a = per_trip * j + u
            substep(kl_ref[jnp.minimum(a + 1, n_lat - 1)], vtl_ref[a], (1 + u) % 2, u % 2)
        return carry

    lax.fori_loop(0, n_lat // per_trip, body, 0)
    o = (acc_ref[0:MLA_V, :] * (1.0 / acc_ref[MLA_V:MLA_V + 1, :])).T
    o_ref[...] = (o * _silu(g_ref[...].astype(F32))).astype(o_ref.dtype)


def _attn(qt, kc, vtc, kl, vtl, p):
    bsz, nh, _, t = qt.shape
    lc = kc.shape[3]
    tq = min(Q_TILE, t)
    n_lat = kl.shape[2]
    kv = kl.shape[3]
    assert kc.shape[2] == 1 and lc <= kv and n_lat % 2 == 0
    kern = functools.partial(_attn_kernel, tq=tq, n_lat=n_lat)
    ctx5 = lambda arr: pl.BlockSpec((None, None, None) + arr.shape[3:], lambda b, h, i: (b, h, 0, 0, 0))
    full5 = lambda arr: pl.BlockSpec((None, None) + arr.shape[2:], lambda b, h, i: (b, h, 0, 0, 0))
    return pl.pallas_call(
        kern,
        grid=(bsz, nh, t // tq),
        in_specs=[
            pl.BlockSpec((None, None, QK_PAD, tq), lambda b, h, i: (b, h, 0, i)),
            ctx5(kc), ctx5(vtc), full5(kl), full5(vtl),
            pl.BlockSpec((None, tq, MLA_V), lambda b, h, i: (b, i, h)),
        ],
        out_specs=pl.BlockSpec((None, tq, MLA_V), lambda b, h, i: (b, i, h)),
        out_shape=jax.ShapeDtypeStruct((bsz, t, nh * MLA_V), BF16),
        scratch_shapes=[pltpu.VMEM((1, tq), F32), pltpu.VMEM((VT_ROWS, tq), F32),
                        pltpu.VMEM((2, tq // Q_SUB, kv, Q_SUB), F32), pltpu.VMEM((2, tq // Q_SUB, 1, Q_SUB), F32)],
        compiler_params=_params("parallel", "parallel", "arbitrary"),
        name="mla_attention",
    )(qt, kc, vtc, kl, vtl, p)


def _out_final_kernel(y_ref, h_ref, mod_ref, ow_ref, g_ref, o_ref):
    hn = h_ref[...] + mod_ref[2:3, :] * _dot(y_ref[...], ow_ref[...])
    o_ref[...] = _rms(hn, g_ref[...])


def _out_final(y, h, mod, out_w, g):
    bsz, t, d = h.shape
    wi = y.shape[-1]
    tm = min(2 * TOK_TILE, t)
    return pl.pallas_call(
        _out_final_kernel,
        grid=(bsz, t // tm),
        in_specs=[
            pl.BlockSpec((None, tm, wi), lambda b, j: (b, j, 0)),
            pl.BlockSpec((None, tm, d), lambda b, j: (b, j, 0)),
            pl.BlockSpec((None, 3, d), lambda b, j: (b, 0, 0)),
            pl.BlockSpec((wi, d), lambda b, j: (0, 0)),
            pl.BlockSpec((1, d), lambda b, j: (0, 0)),
        ],
        out_specs=pl.BlockSpec((None, tm, d), lambda b, j: (b, j, 0)),
        out_shape=jax.ShapeDtypeStruct((bsz, t, d), F32),
        compiler_params=_params("parallel", "parallel"),
        name="out_final",
    )(y, h, mod, out_w, g)


def _rope_tables(n_tokens):
    rows = n_tokens // GRID_W
    pos_r = jnp.repeat(jnp.arange(rows), GRID_W).astype(F32)
    pos_c = jnp.tile(jnp.arange(GRID_W), rows).astype(F32)
    inv = ROPE_BASE ** (-2.0 * jnp.arange(ROPE_FREQ, dtype=F32) / (MLA_ROPE // 2))
    ang = jnp.stack([pos_r[:, None] * inv, pos_c[:, None] * inv], axis=1)
    cos, sin = jnp.cos(ang), jnp.sin(ang)
    pad = LANES - MLA_ROPE
    cos_k = jnp.pad(jnp.stack([cos, cos], axis=2).reshape(n_tokens, MLA_ROPE), ((0, 0), (0, pad)))
    sin_k = jnp.pad(jnp.stack([-sin, sin], axis=2).reshape(n_tokens, MLA_ROPE), ((0, 0), (0, pad)))
    cos_q = jnp.transpose(cos, (1, 2, 0))
    sin_q = jnp.transpose(sin, (1, 2, 0))
    return (cos_k, sin_k), (cos_q, sin_q)


def kernel(x, c, ctx, c_ctx, ada_w, ada_b, norm_g, out_w, ev_in_w, hg_lb, hg_norm_g, pool_w, pool_scale,
           od_in_w, qa_norm_g, qb_w, kva_norm_g, kvb_w, final_norm_g):
    bsz, t, d = x.shape
    lc = ctx.shape[1]
    depth = ada_w.shape[0]
    assert depth == 2 and t % (2 * TOK_TILE) == 0 and lc % TOK_TILE == 0 and t % GRID_W == 0
    w = hg_norm_g.shape[-1]
    nh = w // HG_DK
    q_rank = qa_norm_g.shape[-1]
    kv_rank = kva_norm_g.shape[-1]
    d_inner = out_w.shape[1]

    n_cond = -(-(bsz + 1) // SUBLANES) * SUBLANES
    cond = jnp.zeros((n_cond, d), F32).at[:bsz].set(c).at[bsz].set(c_ctx)
    mods = _ada(cond, ada_w, ada_b).reshape(depth, n_cond, 3, d)
    mod_l = [mods[l, :bsz] for l in range(depth)]
    mod_c = [mods[l, bsz:bsz + 1] for l in range(depth)]

    lb = _decay_bounds(hg_lb, 0)
    w_in0 = ev_in_w[0].astype(BF16)
    g0 = norm_g[0].reshape(1, d)
    ctx_flat = ctx.reshape(1, bsz * lc, d)
    n_in0 = w_in0.shape[1]
    tn0 = n_in0 // IN_PROJ_COL_PARTS
    p_c = _modnorm_mm(ctx_flat, mod_c[0], g0, w_in0, IN_PROJ_ROWS, tn0, "in_proj0_ctx").reshape(bsz, lc, n_in0)
    p_l = _modnorm_mm(x, mod_l[0], g0, w_in0, IN_PROJ_ROWS, tn0, "in_proj0")
    consts = _hgrn_constants(HG_CHUNK)
    s0 = jnp.zeros((bsz, 2, nh, HG_DK, HG_DK), F32)
    of_c, ob_c, s_c = _hgrn(p_c, lb, s0, consts)
    of_l, ob_l, _ = _hgrn(p_l, lb, s_c, consts)
    hgn = hg_norm_g[0].reshape(1, w)
    pw = pool_w[0].astype(BF16)
    ps = pool_scale[0].reshape(1, w)
    ow0 = out_w[0].astype(BF16)
    o1 = q_rank
    o2 = o1 + kv_rank
    o3 = o2 + MLA_ROPE
    w1 = od_in_w[0]
    kr_pad = jnp.zeros((d, LANES - MLA_ROPE), F32)
    w_in1 = jnp.concatenate([w1[:, o3:], w1[:, :o1], w1[:, o1:o2], w1[:, o2:o3], kr_pad], axis=1).astype(BF16)
    w_in1c = w_in1[:, d_inner + q_rank:]
    g1 = norm_g[1].reshape(1, d)
    bcast = lambda m: jnp.broadcast_to(m, (bsz, 3, d))
    _, p1_c = _even_post(of_c, ob_c, p_c, ctx, bcast(mod_c[0]), hgn, pw, ps, ow0, bcast(mod_c[1]), g1, w_in1c)
    hl1, p1_l = _even_post(of_l, ob_l, p_l, x, mod_l[0], hgn, pw, ps, ow0, mod_l[1], g1, w_in1)

    kvw = kvb_w[0].reshape(kv_rank, MLA_HEADS, MLA_NOPE + MLA_V)
    wuk = kvw[..., :MLA_NOPE].reshape(kv_rank, MLA_HEADS * MLA_NOPE).astype(BF16)
    wuvt = jnp.transpose(kvw[..., MLA_NOPE:], (1, 2, 0)).reshape(MLA_HEADS * MLA_V, kv_rank).astype(BF16)
    wqt = jnp.transpose(qb_w[0]).astype(BF16)
    kvg = kva_norm_g[0].reshape(1, kv_rank)
    qag = qa_norm_g[0].reshape(1, q_rank)
    tab_k, tab_q = _rope_tables(t)
    kc, vtc = _mla_kv(p1_c, 0, kv_rank // LANES, kvg, wuk, wuvt, None, lc)
    kl, vtl = _mla_kv(p1_l, (d_inner + q_rank) // kv_rank, (d_inner + q_rank + kv_rank) // LANES, kvg, wuk, wuvt,
                      tab_k, min(KV_CHUNK, t))
    qt = _mla_q(p1_l, d_inner // q_rank, qag, wqt, *tab_q)
    y = _attn(qt, kc, vtc, kl, vtl, p1_l)
    return _out_final(y, hl1, mod_l[1], out_w[1].astype(BF16), final_norm_g.reshape(1, d))
```
